```python
import math
import jax, jax.numpy as jnp
from jax import lax
import numpy as np

D_MODEL = 1024
BATCH = 2
SEQ = 8192
DEPTH = 4
DEC_BATCH = 1
DEC_SEQ = 16384
PAST_LEN = 128

N_EVEN = (DEPTH + 1) // 2
N_ODD = DEPTH // 2
POOL_WIDTH = D_MODEL // 2
N_POOL_GROUPS = 4
POOL_GROUP_DIM = POOL_WIDTH // N_POOL_GROUPS
POOL_WINDOWS = (2, 4, 8, 16)
CONV_WIDTH_CH = D_MODEL // 2
CONV_K = 3
MIX_IN_COLS = POOL_WIDTH + 3 * CONV_WIDTH_CH
MIX_OUT_IN = POOL_WIDTH + CONV_WIDTH_CH
N_HEADS = 8
HEAD_DIM = D_MODEL // (2 * N_HEADS)
D_ATTN = 2 * N_HEADS * HEAD_DIM
ATTN_SCALE = HEAD_DIM ** -0.5
Q_BLOCK = 128
D_FF = 4 * D_MODEL
NORM_EPS = 1e-6

kernel_name = "hybrid_pool_shortconv_diffattn_encoder"


def rms_norm(x, g, eps=NORM_EPS):
    xf = x.astype(jnp.float32)
    y = xf * lax.rsqrt(jnp.mean(xf * xf, axis=-1, keepdims=True) + eps)
    return (y * g.astype(jnp.float32)).astype(x.dtype)


def centred_mean(u, w):
    S = u.shape[1]
    cs = jnp.cumsum(u.astype(jnp.float32), axis=1)
    cs = jnp.pad(cs, ((0, 0), (1, 0), (0, 0)))
    t = jnp.arange(S)
    lo = jnp.clip(t - w // 2, 0, S)
    hi = jnp.clip(t + w // 2, 0, S)
    total = cs[:, hi] - cs[:, lo]
    cnt = (hi - lo).astype(jnp.float32)
    return (total / cnt[None, :, None]).astype(u.dtype)


def pool_mixer(u, pool_w, pool_scale):
    B_, S, _ = u.shape
    ug = u.reshape(B_, S, N_POOL_GROUPS, POOL_GROUP_DIM)
    diffs = [centred_mean(ug[:, :, g], w) - ug[:, :, g] for g, w in enumerate(POOL_WINDOWS)]
    d = jnp.stack(diffs, axis=2)
    y = jnp.einsum('bsgc,gcd->bsgd', d, pool_w)
    return y.reshape(B_, S, POOL_WIDTH) * pool_scale


def short_conv_mixer(h, gate_b, gate_c, conv_w):
    z = gate_c * h
    zp = jnp.pad(z, ((0, 0), (1, 1), (0, 0)))
    conv = conv_w[0] * zp[:, :-2] + conv_w[1] * zp[:, 1:-1] + conv_w[2] * zp[:, 2:]
    return gate_b * conv


def even_mixer(h, mix_in_w, pool_w, pool_scale, conv_w, mix_out_w):
    proj = h @ mix_in_w
    u_pool = proj[..., :POOL_WIDTH]
    c0 = POOL_WIDTH
    h_conv = proj[..., c0:c0 + CONV_WIDTH_CH]
    g_b = proj[..., c0 + CONV_WIDTH_CH:c0 + 2 * CONV_WIDTH_CH]
    g_c = proj[..., c0 + 2 * CONV_WIDTH_CH:]
    a_out = pool_mixer(u_pool, pool_w, pool_scale)
    b_out = short_conv_mixer(h_conv, g_b, g_c, conv_w)
    return jnp.concatenate([a_out, b_out], axis=-1) @ mix_out_w


def diff_attention(h, qkv_w, out_w, lq1, lk1, lq2, lk2, subln_g, lambda_init):
    B_, S, _ = h.shape
    qkv = h @ qkv_w
    q, k, v = jnp.split(qkv, 3, axis=-1)
    q = q.reshape(B_, S, 2 * N_HEADS, HEAD_DIM)
    k = k.reshape(B_, S, 2 * N_HEADS, HEAD_DIM)
    v = v.reshape(B_, S, N_HEADS, 2 * HEAD_DIM)
    f32 = jnp.float32
    lam = (jnp.exp(jnp.sum(lq1.astype(f32) * lk1.astype(f32)))
           - jnp.exp(jnp.sum(lq2.astype(f32) * lk2.astype(f32))) + lambda_init)
    head_slopes = jnp.exp2(-8.0 * (jnp.arange(N_HEADS, dtype=f32) + 1.0) / N_HEADS)
    map_slopes = jnp.repeat(head_slopes, 2)
    nb = S // Q_BLOCK
    qb = q.reshape(B_, nb, Q_BLOCK, 2 * N_HEADS, HEAD_DIM).transpose(1, 0, 2, 3, 4)
    starts = jnp.arange(nb) * Q_BLOCK
    key_pos = jnp.arange(S)

    def block(args):
        qblk, t0 = args
        s = jnp.einsum('bqmd,bkmd->bmqk', qblk, k, preferred_element_type=f32) * ATTN_SCALE
        dist = jnp.abs((t0 + jnp.arange(Q_BLOCK))[:, None] - key_pos[None, :]).astype(f32)
        p = jax.nn.softmax(s - map_slopes[:, None, None] * dist, axis=-1)
        p = p.reshape(B_, N_HEADS, 2, Q_BLOCK, S)
        a = p[:, :, 0] - lam * p[:, :, 1]
        return jnp.einsum('bhqk,bkhe->bqhe', a.astype(v.dtype), v)

    o = lax.map(block, (qb, starts))
    o = o.transpose(1, 0, 2, 3, 4).reshape(B_, S, N_HEADS, 2 * HEAD_DIM)
    o = rms_norm(o, subln_g, eps=1e-5) * (1.0 - lambda_init)
    return o.reshape(B_, S, D_ATTN) @ out_w


def sq_relu_mlp(h, w1, w2):
    a = jax.nn.relu(h @ w1)
    return (a * a) @ w2


def setup_inputs(seed: int = 0) -> dict:
    key = jax.random.key(seed)
    ks = jax.random.split(key, 20)
    n = jax.random.normal
    f32 = jnp.float32
    return {
        "x_prompt": n(ks[0], (BATCH, SEQ, D_MODEL), f32),
        "x_sample": n(ks[1], (DEC_BATCH, DEC_SEQ, D_MODEL), f32),
        "norm1_g": 1.0 + 0.02 * n(ks[2], (DEPTH, D_MODEL), f32),
        "norm2_g": 1.0 + 0.02 * n(ks[3], (DEPTH, D_MODEL), f32),
        "final_g": 1.0 + 0.02 * n(ks[4], (D_MODEL,), f32),
        "mix_in_w": n(ks[5], (N_EVEN, D_MODEL, MIX_IN_COLS), f32) * D_MODEL ** -0.5,
        "pool_w": n(ks[6], (N_EVEN, N_POOL_GROUPS, POOL_GROUP_DIM, POOL_GROUP_DIM), f32) * POOL_GROUP_DIM ** -0.5,
        "pool_scale": 1.0 + 0.1 * n(ks[7], (N_EVEN, POOL_WIDTH), f32),
        "conv_w": n(ks[8], (N_EVEN, CONV_K, CONV_WIDTH_CH), f32) * CONV_K ** -0.5,
        "mix_out_w": n(ks[9], (N_EVEN, MIX_OUT_IN, D_MODEL), f32) * MIX_OUT_IN ** -0.5,
        "attn_qkv_w": n(ks[10], (N_ODD, D_MODEL, 3 * D_ATTN), f32) * D_MODEL ** -0.5,
        "attn_out_w": n(ks[11], (N_ODD, D_ATTN, D_MODEL), f32) * D_ATTN ** -0.5,
        "lambda_q1": 0.1 * n(ks[12], (N_ODD, HEAD_DIM), f32),
        "lambda_k1": 0.1 * n(ks[13], (N_ODD, HEAD_DIM), f32),
        "lambda_q2": 0.1 * n(ks[14], (N_ODD, HEAD_DIM), f32),
        "lambda_k2": 0.1 * n(ks[15], (N_ODD, HEAD_DIM), f32),
        "subln_g": 1.0 + 0.02 * n(ks[16], (N_ODD, 2 * HEAD_DIM), f32),
        "mlp_w1": n(ks[17], (DEPTH, D_MODEL, D_FF), f32) * D_MODEL ** -0.5,
        "mlp_w2": n(ks[18], (DEPTH, D_FF, D_MODEL), f32) * D_FF ** -0.5,
    }


def trunk(x, norm1_g, norm2_g, final_g, mix_in_w, pool_w, pool_scale, conv_w, mix_out_w,
          attn_qkv_w, attn_out_w, lambda_q1, lambda_k1, lambda_q2, lambda_k2, subln_g,
          mlp_w1, mlp_w2):
    for i in range(DEPTH):
        h = rms_norm(x, norm1_g[i])
        j = i // 2
        if i % 2 == 0:
            x = x + even_mixer(h, mix_in_w[j], pool_w[j], pool_scale[j], conv_w[j], mix_out_w[j])
        else:
            lambda_init = 0.8 - 0.6 * math.exp(-0.3 * i)
            x = x + diff_attention(h, attn_qkv_w[j], attn_out_w[j], lambda_q1[j], lambda_k1[j],
                                   lambda_q2[j], lambda_k2[j], subln_g[j], lambda_init)
        x = x + sq_relu_mlp(rms_norm(x, norm2_g[i]), mlp_w1[i], mlp_w2[i])
    return rms_norm(x, final_g)


def reference(x_prompt, x_sample, norm1_g, norm2_g, final_g, mix_in_w, pool_w, pool_scale,
              conv_w, mix_out_w, attn_qkv_w, attn_out_w, lambda_q1, lambda_k1, lambda_q2,
              lambda_k2, subln_g, mlp_w1, mlp_w2):
    y_prompt = trunk(x_prompt, norm1_g, norm2_g, final_g, mix_in_w, pool_w, pool_scale, conv_w,
                     mix_out_w, attn_qkv_w, attn_out_w, lambda_q1, lambda_k1, lambda_q2,
                     lambda_k2, subln_g, mlp_w1, mlp_w2)
    y_sample = trunk(x_sample, norm1_g, norm2_g, final_g, mix_in_w, pool_w, pool_scale, conv_w,
                     mix_out_w, attn_qkv_w, attn_out_w, lambda_q1, lambda_k1, lambda_q2,
                     lambda_k2, subln_g, mlp_w1, mlp_w2)
    return (y_prompt, y_sample)
```

```python
import functools
import math

import jax
import jax.numpy as jnp
from jax import lax
from jax.experimental import pallas as pl
from jax.experimental.pallas import tpu as pltpu

D_MODEL = 1024
DEPTH = 4
POOL_WIDTH = D_MODEL // 2
N_POOL_GROUPS = 4
POOL_GROUP_DIM = POOL_WIDTH // N_POOL_GROUPS
POOL_WINDOWS = (2, 4, 8, 16)
CONV_WIDTH_CH = D_MODEL // 2
MIX_IN_COLS = POOL_WIDTH + 3 * CONV_WIDTH_CH
N_HEADS = 8
HEAD_DIM = D_MODEL // (2 * N_HEADS)
V_DIM = 2 * HEAD_DIM
ATTN_SCALE = HEAD_DIM ** -0.5
D_FF = 4 * D_MODEL
NORM_EPS = 1e-6
SUBLN_EPS = 1e-5

F32 = jnp.float32
BF16 = jnp.bfloat16

TOKEN_TILE = 512
HALO = 16
FF_CHUNK = 1024
VMEM_LIMIT_BYTES = 48 * 1024 * 1024
POS_LO_MASK = 255
POS_HI_MASK = TOKEN_TILE - 1 - POS_LO_MASK


def _rms(x, g, eps):
    return x * lax.rsqrt(jnp.mean(x * x, axis=-1, keepdims=True) + eps) * g


def _dot(a, b):
    return jnp.dot(a, b, preferred_element_type=F32)


def _const_spec(shape):
    zeros = (0,) * len(shape)
    return pl.BlockSpec(shape, lambda *_: zeros)


def _params(n_axes):
    return pltpu.CompilerParams(
        dimension_semantics=("arbitrary",) * n_axes,
        vmem_limit_bytes=VMEM_LIMIT_BYTES,
    )


def _mixer_kernel(xp_ref, x_ref, xn_ref, g_ref, win_ref, pw_ref, ps_ref, cw_ref, wout_ref,
                  o_ref, u_ref, z_ref, *, bounds):
    tm = x_ref.shape[0]
    start = pl.program_id(0) * tm
    seq_start = jnp.int32(bounds[0])
    seq_end = jnp.int32(bounds[-1])
    for b in bounds[1:-1]:
        seq_start = jnp.where(start >= b, b, seq_start)
    for b in reversed(bounds[1:-1]):
        seq_end = jnp.where(start < b, b, seq_end)

    x = x_ref[...]
    xe = jnp.concatenate([xp_ref[...], x, xn_ref[...]], axis=0)
    pos = start - HALO + lax.broadcasted_iota(jnp.int32, (tm + 2 * HALO, 1), 0)
    valid = (pos >= seq_start) & (pos < seq_end)
    he = _rms(xe, g_ref[...], NORM_EPS).astype(BF16)
    proj = _dot(he, win_ref[...])
    c0 = POOL_WIDTH
    u_ref[...] = jnp.where(valid, proj[:, :c0], 0.0)
    z_ref[...] = jnp.where(valid, proj[:, c0 + 2 * CONV_WIDTH_CH:] * proj[:, c0:c0 + CONV_WIDTH_CH], 0.0)

    rel = start - seq_start + lax.broadcasted_iota(jnp.int32, (tm, 1), 0)
    seq_len = seq_end - seq_start
    ys = []
    for g, w in enumerate(POOL_WINDOWS):
        cols = slice(g * POOL_GROUP_DIM, (g + 1) * POOL_GROUP_DIM)
        tot = u_ref[HALO - w // 2:HALO - w // 2 + tm, cols]
        for o in range(-w // 2 + 1, w // 2):
            tot = tot + u_ref[HALO + o:HALO + o + tm, cols]
        cnt = (jnp.minimum(rel + w // 2, seq_len) - jnp.maximum(rel - w // 2, 0)).astype(F32)
        d = tot / cnt - u_ref[HALO:HALO + tm, cols]
        ys.append(_dot(d.astype(BF16), pw_ref[g]))
    a_out = jnp.concatenate(ys, axis=-1) * ps_ref[...]

    conv = (cw_ref[0:1, :] * z_ref[HALO - 1:HALO - 1 + tm, :]
            + cw_ref[1:2, :] * z_ref[HALO:HALO + tm, :]
            + cw_ref[2:3, :] * z_ref[HALO + 1:HALO + 1 + tm, :])
    b_out = proj[HALO:HALO + tm, c0 + CONV_WIDTH_CH:c0 + 2 * CONV_WIDTH_CH] * conv
    mixed = jnp.concatenate([a_out, b_out], axis=-1).astype(BF16)
    o_ref[...] = x + _dot(mixed, wout_ref[...])


def _mixer(x, g, win, pw, ps, cw, wout, bounds):
    t, d = x.shape
    tm = TOKEN_TILE
    per = tm // HALO
    last = t // HALO - 1
    return pl.pallas_call(
        functools.partial(_mixer_kernel, bounds=bounds),
        grid=(t // tm,),
        in_specs=[
            pl.BlockSpec((HALO, d), lambda i: (jnp.maximum(i * per - 1, 0), 0)),
            pl.BlockSpec((tm, d), lambda i: (i, 0)),
            pl.BlockSpec((HALO, d), lambda i: (jnp.minimum((i + 1) * per, last), 0)),
            _const_spec((1, d)),
            _const_spec(win.shape),
            _const_spec(pw.shape),
            _const_spec((1, POOL_WIDTH)),
            _const_spec(cw.shape),
            _const_spec(wout.shape),
        ],
        out_specs=pl.BlockSpec((tm, d), lambda i: (i, 0)),
        out_shape=jax.ShapeDtypeStruct((t, d), F32),
        scratch_shapes=[
            pltpu.VMEM((tm + 2 * HALO, POOL_WIDTH), F32),
            pltpu.VMEM((tm + 2 * HALO, CONV_WIDTH_CH), F32),
        ],
        compiler_params=_params(1),
        name="even_mixer",
    )(x, x, x, g.reshape(1, d), win, pw, ps.reshape(1, POOL_WIDTH), cw, wout)


def _mlp_kernel(*refs, has_attn, has_final):
    refs = list(refs)
    x_ref = refs.pop(0)
    if has_attn:
        a_ref = refs.pop(0)
        wo_ref = refs.pop(0)
    g_ref, w1_ref, w2_ref = refs[:3]
    refs = refs[3:]
    if has_final:
        fg_ref = refs.pop(0)
    o_ref = refs.pop(0)

    x = x_ref[...]
    if has_attn:
        x = x + _dot(a_ref[...], wo_ref[...])
    h = _rms(x, g_ref[...], NORM_EPS).astype(BF16)
    acc = x
    for c in range(D_FF // FF_CHUNK):
        cols = slice(c * FF_CHUNK, (c + 1) * FF_CHUNK)
        a = jnp.maximum(_dot(h, w1_ref[:, cols]), 0.0)
        acc = acc + _dot((a * a).astype(BF16), w2_ref[cols, :])
    if has_final:
        acc = _rms(acc, fg_ref[...], NORM_EPS)
    o_ref[...] = acc


def _mlp(x, g, w1, w2, attn=None, attn_w=None, final_g=None):
    t, d = x.shape
    tm = TOKEN_TILE
    row_spec = pl.BlockSpec((tm, d), lambda i: (i, 0))
    args = [x]
    in_specs = [row_spec]
    if attn is not None:
        args += [attn, attn_w]
        in_specs += [row_spec, _const_spec(attn_w.shape)]
    args += [g.reshape(1, d), w1, w2]
    in_specs += [_const_spec((1, d)), _const_spec(w1.shape), _const_spec(w2.shape)]
    if final_g is not None:
        args.append(final_g.reshape(1, d))
        in_specs.append(_const_spec((1, d)))
    return pl.pallas_call(
        functools.partial(_mlp_kernel, has_attn=attn is not None, has_final=final_g is not None),
        grid=(t // tm,),
        in_specs=in_specs,
        out_specs=row_spec,
        out_shape=jax.ShapeDtypeStruct((t, d), F32),
        compiler_params=_params(1),
        name="sq_relu_mlp",
    )(*args)


def _qkv_kernel(x_ref, g_ref, w_ref, qt_ref, kk_ref, vt_ref):
    tm = x_ref.shape[0]
    h = _rms(x_ref[...], g_ref[...], NORM_EPS).astype(BF16)
    qkv = _dot(h, w_ref[...])

    q_row = lax.broadcasted_iota(jnp.int32, (HEAD_DIM, tm), 0)
    q_tok = lax.broadcasted_iota(jnp.int32, (HEAD_DIM, tm), 1)
    q_lo = (q_tok & POS_LO_MASK).astype(F32)
    q_hi = (q_tok & POS_HI_MASK).astype(F32)
    k_lane = lax.broadcasted_iota(jnp.int32, (tm, V_DIM), 1)
    k_tok = lax.broadcasted_iota(jnp.int32, (tm, V_DIM), 0)
    k_lo = (k_tok & POS_LO_MASK).astype(F32)
    k_hi = (k_tok & POS_HI_MASK).astype(F32)

    def k_extra(base, slope):
        return jnp.where(
            (k_lane == base) | (k_lane == base + 1), 1.0,
            jnp.where(k_lane == base + 2, -slope * k_lo,
                      jnp.where(k_lane == base + 3, -slope * k_hi, 0.0)))

    for hd in range(N_HEADS):
        slope = 2.0 ** (-8.0 * (hd + 1) / N_HEADS)
        cols = slice(hd * V_DIM, (hd + 1) * V_DIM)
        q_t = (qkv[:, cols] * ATTN_SCALE).T
        top, bot = q_t[:HEAD_DIM], q_t[HEAD_DIM:]
        q_extra = jnp.where(q_row == 0, slope * q_lo,
                            jnp.where(q_row == 1, slope * q_hi,
                                      jnp.where(q_row < 4, 1.0, 0.0)))
        qt_ref[hd, 0] = jnp.concatenate([top, q_extra], axis=0).astype(BF16)
        qt_ref[hd, 1] = jnp.concatenate([top, -q_extra], axis=0).astype(BF16)
        qt_ref[hd, 2] = jnp.concatenate([q_extra, bot], axis=0).astype(BF16)
        qt_ref[hd, 3] = jnp.concatenate([-q_extra, bot], axis=0).astype(BF16)

        kh = qkv[:, D_MODEL + hd * V_DIM:D_MODEL + (hd + 1) * V_DIM]
        kk_ref[hd, 0, 0] = jnp.where(k_lane < HEAD_DIM, kh, k_extra(HEAD_DIM, slope)).astype(BF16)
        kk_ref[hd, 1, 0] = jnp.where(k_lane >= HEAD_DIM, kh, k_extra(0, slope)).astype(BF16)

        vh = qkv[:, 2 * D_MODEL + hd * V_DIM:2 * D_MODEL + (hd + 1) * V_DIM]
        vt_ref[hd, 0] = vh.T.astype(BF16)


def _qkv(x, g, w):
    t, d = x.shape
    tm = TOKEN_TILE
    nc = t // tm
    return pl.pallas_call(
        _qkv_kernel,
        grid=(nc,),
        in_specs=[
            pl.BlockSpec((tm, d), lambda i: (i, 0)),
            _const_spec((1, d)),
            _const_spec(w.shape),
        ],
        out_specs=[
            pl.BlockSpec((N_HEADS, 4, V_DIM, tm), lambda i: (0, 0, 0, i)),
            pl.BlockSpec((N_HEADS, 2, 1, tm, V_DIM), lambda i: (0, 0, i, 0, 0)),
            pl.BlockSpec((N_HEADS, 1, V_DIM, tm), lambda i: (0, i, 0, 0)),
        ],
        out_shape=[
            jax.ShapeDtypeStruct((N_HEADS, 4, V_DIM, t), BF16),
            jax.ShapeDtypeStruct((N_HEADS, 2, nc, tm, V_DIM), BF16),
            jax.ShapeDtypeStruct((N_HEADS, nc, V_DIM, tm), BF16),
        ],
        compiler_params=_params(1),
        name="attn_qkv",
    )(x, g.reshape(1, d), w)


def _flash_kernel(slope_ref, lq1_ref, lk1_ref, lq2_ref, lk2_ref, sg_ref, qt_ref, kk_ref, vt_ref,
                  o_ref, m_ref, l_ref, acc_ref, *, lambda_init):
    n_chunks, tkc = kk_ref.shape[1], kk_ref.shape[2]
    tq = qt_ref.shape[-1]
    slope = slope_ref[pl.program_id(1)]
    cq = pl.program_id(2)

    m_ref[...] = jnp.full(m_ref.shape, -jnp.inf, F32)
    l_ref[...] = jnp.zeros(l_ref.shape, F32)
    acc_ref[...] = jnp.zeros(acc_ref.shape, F32)

    def chunk(c, sign_idx, fixup):
        shift = -slope * (tkc * jnp.abs(cq - c)).astype(F32)
        for mp in range(2):
            s = _dot(kk_ref[mp, c], qt_ref[2 * mp + sign_idx])
            if fixup is not None:
                s = s + fixup
            m_old = m_ref[mp]
            m_new = jnp.maximum(m_old, jnp.max(s, axis=0, keepdims=True) + shift)
            p = jnp.exp(s - (m_new - shift))
            alpha = jnp.exp(m_old - m_new)
            l_ref[mp] = alpha * l_ref[mp] + jnp.sum(p, axis=0, keepdims=True)
            acc_ref[mp] = alpha * acc_ref[mp] + _dot(vt_ref[c], p.astype(BF16))
            m_ref[mp] = m_new

    def before(c, carry):
        chunk(c, 1, None)
        return carry

    def after(c, carry):
        chunk(c, 0, None)
        return carry

    lax.fori_loop(0, cq, before, 0)
    jj = lax.broadcasted_iota(jnp.int32, (tkc, tq), 0)
    ii = lax.broadcasted_iota(jnp.int32, (tkc, tq), 1)
    chunk(cq, 0, (-2.0 * slope) * jnp.maximum(ii - jj, 0).astype(F32))
    lax.fori_loop(cq + 1, n_chunks, after, 0)

    lam = (jnp.exp(jnp.sum(lq1_ref[...] * lk1_ref[...], keepdims=True))
           - jnp.exp(jnp.sum(lq2_ref[...] * lk2_ref[...], keepdims=True)) + lambda_init)
    o_t = acc_ref[0] / l_ref[0] - lam * (acc_ref[1] / l_ref[1])
    o = _rms(o_t.T, sg_ref[...], SUBLN_EPS) * (1.0 - lambda_init)
    o_ref[...] = o.astype(o_ref.dtype)


def _flash(slopes, lam_vecs, subln_g, qt, kk, vt, *, tok_start, n_seq, seq_len, lambda_init):
    tq = TOKEN_TILE
    nq = seq_len // tq
    blk0 = tok_start // tq
    seq0 = tok_start // seq_len
    smem = pl.BlockSpec(memory_space=pltpu.SMEM)
    vec = _const_spec((1, HEAD_DIM))
    return pl.pallas_call(
        functools.partial(_flash_kernel, lambda_init=lambda_init),
        grid=(n_seq, N_HEADS, nq),
        in_specs=[
            smem, vec, vec, vec, vec, _const_spec((1, V_DIM)),
            pl.BlockSpec((None, 4, V_DIM, tq), lambda b, h, i: (h, 0, 0, blk0 + b * nq + i)),
            pl.BlockSpec((None, 2, nq, tq, V_DIM), lambda b, h, i: (h, 0, seq0 + b, 0, 0)),
            pl.BlockSpec((None, nq, V_DIM, tq), lambda b, h, i: (h, seq0 + b, 0, 0)),
        ],
        out_specs=pl.BlockSpec((tq, V_DIM), lambda b, h, i: (b * nq + i, h)),
        out_shape=jax.ShapeDtypeStruct((n_seq * seq_len, N_HEADS * V_DIM), BF16),
        scratch_shapes=[
            pltpu.VMEM((2, 1, tq), F32),
            pltpu.VMEM((2, 1, tq), F32),
            pltpu.VMEM((2, V_DIM, tq), F32),
        ],
        compiler_params=_params(3),
        name="diff_flash",
    )(slopes, *lam_vecs, subln_g.reshape(1, V_DIM), qt, kk, vt)


def kernel(x_prompt, x_sample, norm1_g, norm2_g, final_g, mix_in_w, pool_w, pool_scale, conv_w,
           mix_out_w, attn_qkv_w, attn_out_w, lambda_q1, lambda_k1, lambda_q2, lambda_k2, subln_g,
           mlp_w1, mlp_w2):
    pb, ps, d = x_prompt.shape
    sb, ss, _ = x_sample.shape
    n_prompt = pb * ps
    x = jnp.concatenate([x_prompt.reshape(n_prompt, d), x_sample.reshape(sb * ss, d)], axis=0)
    bounds = tuple(ps * b for b in range(pb)) + tuple(n_prompt + ss * b for b in range(sb + 1))
    slopes = jnp.exp2(-8.0 * (jnp.arange(N_HEADS, dtype=F32) + 1.0) / N_HEADS)

    for i in range(DEPTH):
        j = i // 2
        w1 = mlp_w1[i].astype(BF16)
        w2 = mlp_w2[i].astype(BF16)
        fg = final_g if i == DEPTH - 1 else None
        if i % 2 == 0:
            x = _mixer(x, norm1_g[i], mix_in_w[j].astype(BF16), pool_w[j].astype(BF16),
                       pool_scale[j], conv_w[j], mix_out_w[j].astype(BF16), bounds)
            x = _mlp(x, norm2_g[i], w1, w2, final_g=fg)
        else:
            lambda_init = 0.8 - 0.6 * math.exp(-0.3 * i)
            qt, kk, vt = _qkv(x, norm1_g[i], attn_qkv_w[j].astype(BF16))
            lam_vecs = [v[j].reshape(1, HEAD_DIM) for v in (lambda_q1, lambda_k1, lambda_q2, lambda_k2)]
            flash = functools.partial(_flash, slopes, lam_vecs, subln_g[j], qt, kk, vt,
                                      lambda_init=lambda_init)
            attn = jnp.concatenate([
                flash(tok_start=0, n_seq=pb, seq_len=ps),
                flash(tok_start=n_prompt, n_seq=sb, seq_len=ss),
            ], axis=0)
            x = _mlp(x, norm2_g[i], w1, w2, attn=attn, attn_w=attn_out_w[j].astype(BF16), final_g=fg)

    return (x[:n_prompt].reshape(pb, ps, d), x[n_prompt:].reshape(sb, ss, d))
```

```python
import functools
import math

import jax
import jax.numpy as jnp
from jax import lax
from jax.experimental import pallas as pl
from jax.experimental.pallas import tpu as pltpu

D_MODEL = 1024
DEPTH = 4
POOL_WIDTH = D_MODEL // 2
N_POOL_GROUPS = 4
POOL_GROUP_DIM = POOL_WIDTH // N_POOL_GROUPS
POOL_WINDOWS = (2, 4, 8, 16)
CONV_WIDTH_CH = D_MODEL // 2
MIX_IN_COLS = POOL_WIDTH + 3 * CONV_WIDTH_CH
N_HEADS = 8
HEAD_DIM = D_MODEL // (2 * N_HEADS)
V_DIM = 2 * HEAD_DIM
ATTN_SCALE = HEAD_DIM ** -0.5
D_FF = 4 * D_MODEL
NORM_EPS = 1e-6
SUBLN_EPS = 1e-5

F32 = jnp.float32
BF16 = jnp.bfloat16

TOKEN_TILE = 512
HALO = 16
FF_CHUNK = 1024
VMEM_LIMIT_BYTES = 48 * 1024 * 1024
DENOM_ROWS = 16
V_ROWS = V_DIM + DENOM_ROWS
POS_LO_MASK = 255
POS_HI_MASK = TOKEN_TILE - 1 - POS_LO_MASK


def _rms(x, g, eps):
    return x * lax.rsqrt(jnp.mean(x * x, axis=-1, keepdims=True) + eps) * g


def _dot(a, b):
    return jnp.dot(a, b, preferred_element_type=F32)


def _const_spec(shape):
    zeros = (0,) * len(shape)
    return pl.BlockSpec(shape, lambda *_: zeros)


def _params(n_axes):
    return pltpu.CompilerParams(
        dimension_semantics=("arbitrary",) * n_axes,
        vmem_limit_bytes=VMEM_LIMIT_BYTES,
    )


def _mixer_kernel(xp_ref, x_ref, xn_ref, g_ref, win_ref, pw_ref, ps_ref, cw_ref, wout_ref,
                  o_ref, u_ref, z_ref, *, bounds):
    tm = x_ref.shape[0]
    start = pl.program_id(0) * tm
    seq_start = jnp.int32(bounds[0])
    seq_end = jnp.int32(bounds[-1])
    for b in bounds[1:-1]:
        seq_start = jnp.where(start >= b, b, seq_start)
    for b in reversed(bounds[1:-1]):
        seq_end = jnp.where(start < b, b, seq_end)

    x = x_ref[...]
    xe = jnp.concatenate([xp_ref[...], x, xn_ref[...]], axis=0)
    pos = start - HALO + lax.broadcasted_iota(jnp.int32, (tm + 2 * HALO, 1), 0)
    valid = (pos >= seq_start) & (pos < seq_end)
    he = _rms(xe, g_ref[...], NORM_EPS).astype(BF16)
    proj = _dot(he, win_ref[...])
    c0 = POOL_WIDTH
    u_ref[...] = jnp.where(valid, proj[:, :c0], 0.0)
    z_ref[...] = jnp.where(valid, proj[:, c0 + 2 * CONV_WIDTH_CH:] * proj[:, c0:c0 + CONV_WIDTH_CH], 0.0)

    rel = start - seq_start + lax.broadcasted_iota(jnp.int32, (tm, 1), 0)
    seq_len = seq_end - seq_start
    ys = []
    for g, w in enumerate(POOL_WINDOWS):
        cols = slice(g * POOL_GROUP_DIM, (g + 1) * POOL_GROUP_DIM)
        tot = u_ref[HALO - w // 2:HALO - w // 2 + tm, cols]
        for o in range(-w // 2 + 1, w // 2):
            tot = tot + u_ref[HALO + o:HALO + o + tm, cols]
        cnt = (jnp.minimum(rel + w // 2, seq_len) - jnp.maximum(rel - w // 2, 0)).astype(F32)
        d = tot / cnt - u_ref[HALO:HALO + tm, cols]
        ys.append(_dot(d.astype(BF16), pw_ref[g]))
    a_out = jnp.concatenate(ys, axis=-1) * ps_ref[...]

    conv = (cw_ref[0:1, :] * z_ref[HALO - 1:HALO - 1 + tm, :]
            + cw_ref[1:2, :] * z_ref[HALO:HALO + tm, :]
            + cw_ref[2:3, :] * z_ref[HALO + 1:HALO + 1 + tm, :])
    b_out = proj[HALO:HALO + tm, c0 + CONV_WIDTH_CH:c0 + 2 * CONV_WIDTH_CH] * conv
    mixed = jnp.concatenate([a_out, b_out], axis=-1).astype(BF16)
    o_ref[...] = x + _dot(mixed, wout_ref[...])


def _mixer(x, g, win, pw, ps, cw, wout, bounds):
    t, d = x.shape
    tm = TOKEN_TILE
    per = tm // HALO
    last = t // HALO - 1
    return pl.pallas_call(
        functools.partial(_mixer_kernel, bounds=bounds),
        grid=(t // tm,),
        in_specs=[
            pl.BlockSpec((HALO, d), lambda i: (jnp.maximum(i * per - 1, 0), 0)),
            pl.BlockSpec((tm, d), lambda i: (i, 0)),
            pl.BlockSpec((HALO, d), lambda i: (jnp.minimum((i + 1) * per, last), 0)),
            _const_spec((1, d)),
            _const_spec(win.shape),
            _const_spec(pw.shape),
            _const_spec((1, POOL_WIDTH)),
            _const_spec(cw.shape),
            _const_spec(wout.shape),
        ],
        out_specs=pl.BlockSpec((tm, d), lambda i: (i, 0)),
        out_shape=jax.ShapeDtypeStruct((t, d), F32),
        scratch_shapes=[
            pltpu.VMEM((tm + 2 * HALO, POOL_WIDTH), F32),
            pltpu.VMEM((tm + 2 * HALO, CONV_WIDTH_CH), F32),
        ],
        compiler_params=_params(1),
        name="even_mixer",
    )(x, x, x, g.reshape(1, d), win, pw, ps.reshape(1, POOL_WIDTH), cw, wout)


def _mlp_kernel(*refs, has_attn, has_final):
    refs = list(refs)
    x_ref = refs.pop(0)
    if has_attn:
        a_ref = refs.pop(0)
        wo_ref = refs.pop(0)
    g_ref, w1_ref, w2_ref = refs[:3]
    refs = refs[3:]
    if has_final:
        fg_ref = refs.pop(0)
    o_ref = refs.pop(0)

    x = x_ref[...]
    if has_attn:
        x = x + _dot(a_ref[...], wo_ref[...])
    h = _rms(x, g_ref[...], NORM_EPS).astype(BF16)
    acc = x
    for c in range(D_FF // FF_CHUNK):
        cols = slice(c * FF_CHUNK, (c + 1) * FF_CHUNK)
        a = jnp.maximum(_dot(h, w1_ref[:, cols]), 0.0)
        acc = acc + _dot((a * a).astype(BF16), w2_ref[cols, :])
    if has_final:
        acc = _rms(acc, fg_ref[...], NORM_EPS)
    o_ref[...] = acc


def _mlp(x, g, w1, w2, attn=None, attn_w=None, final_g=None):
    t, d = x.shape
    tm = TOKEN_TILE
    row_spec = pl.BlockSpec((tm, d), lambda i: (i, 0))
    args = [x]
    in_specs = [row_spec]
    if attn is not None:
        args += [attn, attn_w]
        in_specs += [row_spec, _const_spec(attn_w.shape)]
    args += [g.reshape(1, d), w1, w2]
    in_specs += [_const_spec((1, d)), _const_spec(w1.shape), _const_spec(w2.shape)]
    if final_g is not None:
        args.append(final_g.reshape(1, d))
        in_specs.append(_const_spec((1, d)))
    return pl.pallas_call(
        functools.partial(_mlp_kernel, has_attn=attn is not None, has_final=final_g is not None),
        grid=(t // tm,),
        in_specs=in_specs,
        out_specs=row_spec,
        out_shape=jax.ShapeDtypeStruct((t, d), F32),
        compiler_params=_params(1),
        name="sq_relu_mlp",
    )(*args)


def _qkv_kernel(x_ref, g_ref, w_ref, qt_ref, kk_ref, vt_ref):
    tm = x_ref.shape[0]
    h = _rms(x_ref[...], g_ref[...], NORM_EPS).astype(BF16)
    qkv = _dot(h, w_ref[...])

    q_row = lax.broadcasted_iota(jnp.int32, (HEAD_DIM, tm), 0)
    q_tok = lax.broadcasted_iota(jnp.int32, (HEAD_DIM, tm), 1)
    q_lo = (q_tok & POS_LO_MASK).astype(F32)
    q_hi = (q_tok & POS_HI_MASK).astype(F32)
    k_lane = lax.broadcasted_iota(jnp.int32, (tm, V_DIM), 1)
    k_tok = lax.broadcasted_iota(jnp.int32, (tm, V_DIM), 0)
    k_lo = (k_tok & POS_LO_MASK).astype(F32)
    k_hi = (k_tok & POS_HI_MASK).astype(F32)

    denom_rows = (lax.broadcasted_iota(jnp.int32, (DENOM_ROWS, tm), 0) == 0).astype(F32)

    def k_extra(base, slope):
        return jnp.where(
            (k_lane == base) | (k_lane == base + 1), 1.0,
            jnp.where(k_lane == base + 2, -slope * k_lo,
                      jnp.where(k_lane == base + 3, -slope * k_hi, 0.0)))

    for hd in range(N_HEADS):
        slope = 2.0 ** (-8.0 * (hd + 1) / N_HEADS)
        cols = slice(hd * V_DIM, (hd + 1) * V_DIM)
        q_t = (qkv[:, cols] * ATTN_SCALE).T
        top, bot = q_t[:HEAD_DIM], q_t[HEAD_DIM:]
        q_extra = jnp.where(q_row == 0, slope * q_lo,
                            jnp.where(q_row == 1, slope * q_hi,
                                      jnp.where(q_row < 4, 1.0, 0.0)))
        qt_ref[hd, 0] = jnp.concatenate([top, q_extra], axis=0).astype(BF16)
        qt_ref[hd, 1] = jnp.concatenate([top, -q_extra], axis=0).astype(BF16)
        qt_ref[hd, 2] = jnp.concatenate([q_extra, bot], axis=0).astype(BF16)
        qt_ref[hd, 3] = jnp.concatenate([-q_extra, bot], axis=0).astype(BF16)

        kh = qkv[:, D_MODEL + hd * V_DIM:D_MODEL + (hd + 1) * V_DIM]
        kk_ref[hd, 0, 0] = jnp.where(k_lane < HEAD_DIM, kh, k_extra(HEAD_DIM, slope)).astype(BF16)
        kk_ref[hd, 1, 0] = jnp.where(k_lane >= HEAD_DIM, kh, k_extra(0, slope)).astype(BF16)

        vh = qkv[:, 2 * D_MODEL + hd * V_DIM:2 * D_MODEL + (hd + 1) * V_DIM]
        vt_ref[hd, 0] = jnp.concatenate([vh.T, denom_rows], axis=0).astype(BF16)


def _qkv(x, g, w):
    t, d = x.shape
    tm = TOKEN_TILE
    nc = t // tm
    return pl.pallas_call(
        _qkv_kernel,
        grid=(nc,),
        in_specs=[
            pl.BlockSpec((tm, d), lambda i: (i, 0)),
            _const_spec((1, d)),
            _const_spec(w.shape),
        ],
        out_specs=[
            pl.BlockSpec((N_HEADS, 4, V_DIM, tm), lambda i: (0, 0, 0, i)),
            pl.BlockSpec((N_HEADS, 2, 1, tm, V_DIM), lambda i: (0, 0, i, 0, 0)),
            pl.BlockSpec((N_HEADS, 1, V_ROWS, tm), lambda i: (0, i, 0, 0)),
        ],
        out_shape=[
            jax.ShapeDtypeStruct((N_HEADS, 4, V_DIM, t), BF16),
            jax.ShapeDtypeStruct((N_HEADS, 2, nc, tm, V_DIM), BF16),
            jax.ShapeDtypeStruct((N_HEADS, nc, V_ROWS, tm), BF16),
        ],
        compiler_params=_params(1),
        name="attn_qkv",
    )(x, g.reshape(1, d), w)


def _flash_kernel(slope_ref, lq1_ref, lk1_ref, lq2_ref, lk2_ref, sg_ref, qt_ref, kk_ref, vt_ref,
                  o_ref, s_ref, mc_ref, m_ref, acc_ref, *, lambda_init):
    n_chunks, tkc = kk_ref.shape[1], kk_ref.shape[2]
    tq = qt_ref.shape[-1]
    slope = slope_ref[pl.program_id(1)]
    cq = pl.program_id(2)
    n_off = n_chunks - 1

    def chunk_of(t):
        return jnp.where(t < n_off, t + (t >= cq).astype(jnp.int32), cq)

    def scores(t, mp):
        c = chunk_of(t)
        return _dot(kk_ref[mp, c], qt_ref[2 * mp + (c < cq).astype(jnp.int32)])

    def issue(t, mp):
        s = scores(t, mp)
        s_ref[mp] = s
        mc_ref[mp] = jnp.max(s, axis=0, keepdims=True)

    def absorb(t, mp, s, mc):
        c = chunk_of(t)
        shift = -slope * (tkc * jnp.abs(cq - c)).astype(F32)
        m_old = m_ref[mp]
        m_new = jnp.maximum(m_old, mc + shift)
        p = jnp.exp(s - (m_new - shift)).astype(BF16)
        acc_ref[mp] = jnp.exp(m_old - m_new) * acc_ref[mp] + _dot(vt_ref[c], p)
        m_ref[mp] = m_new

    m_ref[...] = jnp.full(m_ref.shape, -jnp.inf, F32)
    acc_ref[...] = jnp.zeros(acc_ref.shape, F32)

    issue(0, 0)

    def trip(t, carry):
        issue(t, 1)
        absorb(t, 0, s_ref[0], mc_ref[0])
        issue(t + 1, 0)
        absorb(t, 1, s_ref[1], mc_ref[1])
        return carry

    lax.fori_loop(0, n_off, trip, 0, unroll=4)

    jj = lax.broadcasted_iota(jnp.int32, (tkc, tq), 0)
    ii = lax.broadcasted_iota(jnp.int32, (tkc, tq), 1)
    fixup = (-2.0 * slope) * jnp.maximum(ii - jj, 0).astype(F32)
    for mp in range(2):
        s = (s_ref[0] if mp == 0 else scores(n_off, 1)) + fixup
        absorb(n_off, mp, s, jnp.max(s, axis=0, keepdims=True))

    lam = (jnp.exp(jnp.sum(lq1_ref[...] * lk1_ref[...], keepdims=True))
           - jnp.exp(jnp.sum(lq2_ref[...] * lk2_ref[...], keepdims=True)) + lambda_init)
    o_t = (acc_ref[0, :V_DIM] / acc_ref[0, V_DIM:V_DIM + 1]
           - lam * (acc_ref[1, :V_DIM] / acc_ref[1, V_DIM:V_DIM + 1]))
    o = _rms(o_t.T, sg_ref[...], SUBLN_EPS) * (1.0 - lambda_init)
    o_ref[...] = o.astype(o_ref.dtype)


def _flash(slopes, lam_vecs, subln_g, qt, kk, vt, *, tok_start, n_seq, seq_len, lambda_init):
    tq = TOKEN_TILE
    nq = seq_len // tq
    blk0 = tok_start // tq
    seq0 = tok_start // seq_len
    smem = pl.BlockSpec(memory_space=pltpu.SMEM)
    vec = _const_spec((1, HEAD_DIM))
    return pl.pallas_call(
        functools.partial(_flash_kernel, lambda_init=lambda_init),
        grid=(n_seq, N_HEADS, nq),
        in_specs=[
            smem, vec, vec, vec, vec, _const_spec((1, V_DIM)),
            pl.BlockSpec((None, 4, V_DIM, tq), lambda b, h, i: (h, 0, 0, blk0 + b * nq + i)),
            pl.BlockSpec((None, 2, nq, tq, V_DIM), lambda b, h, i: (h, 0, seq0 + b, 0, 0)),
            pl.BlockSpec((None, nq, V_ROWS, tq), lambda b, h, i: (h, seq0 + b, 0, 0)),
        ],
        out_specs=pl.BlockSpec((tq, V_DIM), lambda b, h, i: (b * nq + i, h)),
        out_shape=jax.ShapeDtypeStruct((n_seq * seq_len, N_HEADS * V_DIM), BF16),
        scratch_shapes=[
            pltpu.VMEM((2, tq, tq), F32),
            pltpu.VMEM((2, 1, tq), F32),
            pltpu.VMEM((2, 1, tq), F32),
            pltpu.VMEM((2, V_ROWS, tq), F32),
        ],
        compiler_params=_params(3),
        name="diff_flash",
    )(slopes, *lam_vecs, subln_g.reshape(1, V_DIM), qt, kk, vt)


def kernel(x_prompt, x_sample, norm1_g, norm2_g, final_g, mix_in_w, pool_w, pool_scale, conv_w,
           mix_out_w, attn_qkv_w, attn_out_w, lambda_q1, lambda_k1, lambda_q2, lambda_k2, subln_g,
           mlp_w1, mlp_w2):
    pb, ps, d = x_prompt.shape
    sb, ss, _ = x_sample.shape
    n_prompt = pb * ps
    x = jnp.concatenate([x_prompt.reshape(n_prompt, d), x_sample.reshape(sb * ss, d)], axis=0)
    bounds = tuple(ps * b for b in range(pb)) + tuple(n_prompt + ss * b for b in range(sb + 1))
    slopes = jnp.exp2(-8.0 * (jnp.arange(N_HEADS, dtype=F32) + 1.0) / N_HEADS)

    for i in range(DEPTH):
        j = i // 2
        w1 = mlp_w1[i].astype(BF16)
        w2 = mlp_w2[i].astype(BF16)
        fg = final_g if i == DEPTH - 1 else None
        if i % 2 == 0:
            x = _mixer(x, norm1_g[i], mix_in_w[j].astype(BF16), pool_w[j].astype(BF16),
                       pool_scale[j], conv_w[j], mix_out_w[j].astype(BF16), bounds)
            x = _mlp(x, norm2_g[i], w1, w2, final_g=fg)
        else:
            lambda_init = 0.8 - 0.6 * math.exp(-0.3 * i)
            qt, kk, vt = _qkv(x, norm1_g[i], attn_qkv_w[j].astype(BF16))
            lam_vecs = [v[j].reshape(1, HEAD_DIM) for v in (lambda_q1, lambda_k1, lambda_q2, lambda_k2)]
            flash = functools.partial(_flash, slopes, lam_vecs, subln_g[j], qt, kk, vt,
                                      lambda_init=lambda_init)
            attn = jnp.concatenate([
                flash(tok_start=0, n_seq=pb, seq_len=ps),
                flash(tok_start=n_prompt, n_seq=sb, seq_len=ss),
            ], axis=0)
            x = _mlp(x, norm2_g[i], w1, w2, attn=attn, attn_w=attn_out_w[j].astype(BF16), final_g=fg)

    return (x[:n_prompt].reshape(pb, ps, d), x[n_prompt:].reshape(sb, ss, d))
```

```python
import functools
import math

import jax
import jax.numpy as jnp
from jax import lax
from jax.experimental import pallas as pl
from jax.experimental.pallas import tpu as pltpu

D_MODEL = 1024
DEPTH = 4
POOL_WIDTH = D_MODEL // 2
N_POOL_GROUPS = 4
POOL_GROUP_DIM = POOL_WIDTH // N_POOL_GROUPS
POOL_WINDOWS = (2, 4, 8, 16)
CONV_WIDTH_CH = D_MODEL // 2
MIX_IN_COLS = POOL_WIDTH + 3 * CONV_WIDTH_CH
N_HEADS = 8
HEAD_DIM = D_MODEL // (2 * N_HEADS)
V_DIM = 2 * HEAD_DIM
ATTN_SCALE = HEAD_DIM ** -0.5
D_FF = 4 * D_MODEL
NORM_EPS = 1e-6
SUBLN_EPS = 1e-5

F32 = jnp.float32
BF16 = jnp.bfloat16

TOKEN_TILE = 512
HALO = 16
FF_CHUNK = 1024
VMEM_LIMIT_BYTES = 48 * 1024 * 1024
DENOM_ROWS = 16
V_ROWS = V_DIM + DENOM_ROWS
PIPE_UNROLL = 4
EXP_ZERO_BELOW = 106.0
NORM_SLACK = 1.01
POS_LO_MASK = 255
POS_HI_MASK = TOKEN_TILE - 1 - POS_LO_MASK


def _rms(x, g, eps):
    return x * lax.rsqrt(jnp.mean(x * x, axis=-1, keepdims=True) + eps) * g


def _dot(a, b):
    return jnp.dot(a, b, preferred_element_type=F32)


def _const_spec(shape):
    zeros = (0,) * len(shape)
    return pl.BlockSpec(shape, lambda *_: zeros)


def _params(n_axes):
    return pltpu.CompilerParams(
        dimension_semantics=("arbitrary",) * n_axes,
        vmem_limit_bytes=VMEM_LIMIT_BYTES,
    )


def _mixer_kernel(xp_ref, x_ref, xn_ref, g_ref, win_ref, pw_ref, ps_ref, cw_ref, wout_ref,
                  o_ref, u_ref, z_ref, *, bounds):
    tm = x_ref.shape[0]
    start = pl.program_id(0) * tm
    seq_start = jnp.int32(bounds[0])
    seq_end = jnp.int32(bounds[-1])
    for b in bounds[1:-1]:
        seq_start = jnp.where(start >= b, b, seq_start)
    for b in reversed(bounds[1:-1]):
        seq_end = jnp.where(start < b, b, seq_end)

    x = x_ref[...]
    xe = jnp.concatenate([xp_ref[...], x, xn_ref[...]], axis=0)
    pos = start - HALO + lax.broadcasted_iota(jnp.int32, (tm + 2 * HALO, 1), 0)
    valid = (pos >= seq_start) & (pos < seq_end)
    he = _rms(xe, g_ref[...], NORM_EPS).astype(BF16)
    proj = _dot(he, win_ref[...])
    c0 = POOL_WIDTH
    u_ref[...] = jnp.where(valid, proj[:, :c0], 0.0)
    z_ref[...] = jnp.where(valid, proj[:, c0 + 2 * CONV_WIDTH_CH:] * proj[:, c0:c0 + CONV_WIDTH_CH], 0.0)

    rel = start - seq_start + lax.broadcasted_iota(jnp.int32, (tm, 1), 0)
    seq_len = seq_end - seq_start
    ys = []
    for g, w in enumerate(POOL_WINDOWS):
        cols = slice(g * POOL_GROUP_DIM, (g + 1) * POOL_GROUP_DIM)
        tot = u_ref[HALO - w // 2:HALO - w // 2 + tm, cols]
        for o in range(-w // 2 + 1, w // 2):
            tot = tot + u_ref[HALO + o:HALO + o + tm, cols]
        cnt = (jnp.minimum(rel + w // 2, seq_len) - jnp.maximum(rel - w // 2, 0)).astype(F32)
        d = tot / cnt - u_ref[HALO:HALO + tm, cols]
        ys.append(_dot(d.astype(BF16), pw_ref[g]))
    a_out = jnp.concatenate(ys, axis=-1) * ps_ref[...]

    conv = (cw_ref[0:1, :] * z_ref[HALO - 1:HALO - 1 + tm, :]
            + cw_ref[1:2, :] * z_ref[HALO:HALO + tm, :]
            + cw_ref[2:3, :] * z_ref[HALO + 1:HALO + 1 + tm, :])
    b_out = proj[HALO:HALO + tm, c0 + CONV_WIDTH_CH:c0 + 2 * CONV_WIDTH_CH] * conv
    mixed = jnp.concatenate([a_out, b_out], axis=-1).astype(BF16)
    o_ref[...] = x + _dot(mixed, wout_ref[...])


def _mixer(x, g, win, pw, ps, cw, wout, bounds):
    t, d = x.shape
    tm = TOKEN_TILE
    per = tm // HALO
    last = t // HALO - 1
    return pl.pallas_call(
        functools.partial(_mixer_kernel, bounds=bounds),
        grid=(t // tm,),
        in_specs=[
            pl.BlockSpec((HALO, d), lambda i: (jnp.maximum(i * per - 1, 0), 0)),
            pl.BlockSpec((tm, d), lambda i: (i, 0)),
            pl.BlockSpec((HALO, d), lambda i: (jnp.minimum((i + 1) * per, last), 0)),
            _const_spec((1, d)),
            _const_spec(win.shape),
            _const_spec(pw.shape),
            _const_spec((1, POOL_WIDTH)),
            _const_spec(cw.shape),
            _const_spec(wout.shape),
        ],
        out_specs=pl.BlockSpec((tm, d), lambda i: (i, 0)),
        out_shape=jax.ShapeDtypeStruct((t, d), F32),
        scratch_shapes=[
            pltpu.VMEM((tm + 2 * HALO, POOL_WIDTH), F32),
            pltpu.VMEM((tm + 2 * HALO, CONV_WIDTH_CH), F32),
        ],
        compiler_params=_params(1),
        name="even_mixer",
    )(x, x, x, g.reshape(1, d), win, pw, ps.reshape(1, POOL_WIDTH), cw, wout)


def _mlp_kernel(*refs, has_attn, has_final):
    refs = list(refs)
    x_ref = refs.pop(0)
    if has_attn:
        a_ref = refs.pop(0)
        wo_ref = refs.pop(0)
    g_ref, w1_ref, w2_ref = refs[:3]
    refs = refs[3:]
    if has_final:
        fg_ref = refs.pop(0)
    o_ref = refs.pop(0)

    x = x_ref[...]
    if has_attn:
        x = x + _dot(a_ref[...], wo_ref[...])
    h = _rms(x, g_ref[...], NORM_EPS).astype(BF16)
    acc = x
    for c in range(D_FF // FF_CHUNK):
        cols = slice(c * FF_CHUNK, (c + 1) * FF_CHUNK)
        a = jnp.maximum(_dot(h, w1_ref[:, cols]), 0.0)
        acc = acc + _dot((a * a).astype(BF16), w2_ref[cols, :])
    if has_final:
        acc = _rms(acc, fg_ref[...], NORM_EPS)
    o_ref[...] = acc


def _mlp(x, g, w1, w2, attn=None, attn_w=None, final_g=None):
    t, d = x.shape
    tm = TOKEN_TILE
    row_spec = pl.BlockSpec((tm, d), lambda i: (i, 0))
    args = [x]
    in_specs = [row_spec]
    if attn is not None:
        args += [attn, attn_w]
        in_specs += [row_spec, _const_spec(attn_w.shape)]
    args += [g.reshape(1, d), w1, w2]
    in_specs += [_const_spec((1, d)), _const_spec(w1.shape), _const_spec(w2.shape)]
    if final_g is not None:
        args.append(final_g.reshape(1, d))
        in_specs.append(_const_spec((1, d)))
    return pl.pallas_call(
        functools.partial(_mlp_kernel, has_attn=attn is not None, has_final=final_g is not None),
        grid=(t // tm,),
        in_specs=in_specs,
        out_specs=row_spec,
        out_shape=jax.ShapeDtypeStruct((t, d), F32),
        compiler_params=_params(1),
        name="sq_relu_mlp",
    )(*args)


def _qkv_kernel(x_ref, g_ref, w_ref, qt_ref, kk_ref, vt_ref, nrm_ref):
    tm = x_ref.shape[0]
    h = _rms(x_ref[...], g_ref[...], NORM_EPS).astype(BF16)
    qkv = _dot(h, w_ref[...])
    q_all = (qkv[:, :D_MODEL] * ATTN_SCALE).astype(BF16).astype(F32)
    k_all = qkv[:, D_MODEL:2 * D_MODEL].astype(BF16).astype(F32)

    sel = (lax.shift_right_logical(lax.broadcasted_iota(jnp.int32, (D_MODEL, V_DIM), 0),
                                   HEAD_DIM.bit_length() - 1)
           == lax.broadcasted_iota(jnp.int32, (D_MODEL, V_DIM), 1)).astype(BF16)

    def max_sq_norm(a):
        sq = a * a
        hi = sq.astype(BF16)
        lo = (sq - hi.astype(F32)).astype(BF16)
        return jnp.max(_dot(hi, sel) + _dot(lo, sel), axis=0, keepdims=True)

    nrm_ref[0] = jnp.concatenate(
        [max_sq_norm(q_all), max_sq_norm(k_all), jnp.zeros((6, V_DIM), F32)], axis=0)

    q_row = lax.broadcasted_iota(jnp.int32, (HEAD_DIM, tm), 0)
    q_tok = lax.broadcasted_iota(jnp.int32, (HEAD_DIM, tm), 1)
    q_lo = (q_tok & POS_LO_MASK).astype(F32)
    q_hi = (q_tok & POS_HI_MASK).astype(F32)
    k_lane = lax.broadcasted_iota(jnp.int32, (tm, V_DIM), 1)
    k_tok = lax.broadcasted_iota(jnp.int32, (tm, V_DIM), 0)
    k_lo = (k_tok & POS_LO_MASK).astype(F32)
    k_hi = (k_tok & POS_HI_MASK).astype(F32)

    denom_rows = (lax.broadcasted_iota(jnp.int32, (DENOM_ROWS, tm), 0) == 0).astype(F32)

    def k_extra(base, slope):
        return jnp.where(
            (k_lane == base) | (k_lane == base + 1), 1.0,
            jnp.where(k_lane == base + 2, -slope * k_lo,
                      jnp.where(k_lane == base + 3, -slope * k_hi, 0.0)))

    for hd in range(N_HEADS):
        slope = 2.0 ** (-8.0 * (hd + 1) / N_HEADS)
        cols = slice(hd * V_DIM, (hd + 1) * V_DIM)
        q_t = q_all[:, cols].T
        top, bot = q_t[:HEAD_DIM], q_t[HEAD_DIM:]
        q_extra = jnp.where(q_row == 0, slope * q_lo,
                            jnp.where(q_row == 1, slope * q_hi,
                                      jnp.where(q_row < 4, 1.0, 0.0)))
        qt_ref[hd, 0] = jnp.concatenate([top, q_extra], axis=0).astype(BF16)
        qt_ref[hd, 1] = jnp.concatenate([top, -q_extra], axis=0).astype(BF16)
        qt_ref[hd, 2] = jnp.concatenate([q_extra, bot], axis=0).astype(BF16)
        qt_ref[hd, 3] = jnp.concatenate([-q_extra, bot], axis=0).astype(BF16)

        kh = k_all[:, cols]
        kk_ref[hd, 0, 0] = jnp.where(k_lane < HEAD_DIM, kh, k_extra(HEAD_DIM, slope)).astype(BF16)
        kk_ref[hd, 1, 0] = jnp.where(k_lane >= HEAD_DIM, kh, k_extra(0, slope)).astype(BF16)

        vh = qkv[:, 2 * D_MODEL + hd * V_DIM:2 * D_MODEL + (hd + 1) * V_DIM]
        vt_ref[hd, 0] = jnp.concatenate([vh.T, denom_rows], axis=0).astype(BF16)


def _qkv(x, g, w):
    t, d = x.shape
    tm = TOKEN_TILE
    nc = t // tm
    return pl.pallas_call(
        _qkv_kernel,
        grid=(nc,),
        in_specs=[
            pl.BlockSpec((tm, d), lambda i: (i, 0)),
            _const_spec((1, d)),
            _const_spec(w.shape),
        ],
        out_specs=[
            pl.BlockSpec((N_HEADS, 4, V_DIM, tm), lambda i: (0, 0, 0, i)),
            pl.BlockSpec((N_HEADS, 2, 1, tm, V_DIM), lambda i: (0, 0, i, 0, 0)),
            pl.BlockSpec((N_HEADS, 1, V_ROWS, tm), lambda i: (0, i, 0, 0)),
            pl.BlockSpec((1, 8, V_DIM), lambda i: (i, 0, 0)),
        ],
        out_shape=[
            jax.ShapeDtypeStruct((N_HEADS, 4, V_DIM, t), BF16),
            jax.ShapeDtypeStruct((N_HEADS, 2, nc, tm, V_DIM), BF16),
            jax.ShapeDtypeStruct((N_HEADS, nc, V_ROWS, tm), BF16),
            jax.ShapeDtypeStruct((nc, 8, V_DIM), F32),
        ],
        compiler_params=_params(1),
        name="attn_qkv",
    )(x, g.reshape(1, d), w)


def _flash_kernel(slope_ref, qn_ref, kn_ref, lq1_ref, lk1_ref, lq2_ref, lk2_ref, sg_ref,
                  qt_ref, kk_ref, vt_ref, o_ref, s_ref, mc_ref, m_ref, acc_ref,
                  *, lambda_init, blk0):
    n_chunks, tkc = kk_ref.shape[1], kk_ref.shape[2]
    tq = qt_ref.shape[-1]
    hd = pl.program_id(1)
    slope, inv_slope = slope_ref[0, hd], slope_ref[1, hd]
    cq = pl.program_id(2)
    chunk0 = blk0 + pl.program_id(0) * n_chunks

    def scores(c, sign, mp):
        return _dot(kk_ref[mp, c], qt_ref[2 * mp + sign])

    def absorb(c, mp, s, mc):
        shift = -slope * (tkc * jnp.abs(cq - c)).astype(F32)
        m_old = m_ref[mp]
        m_new = jnp.maximum(m_old, mc + shift)
        p = jnp.exp(s - (m_new - shift)).astype(BF16)
        acc_ref[mp] = jnp.exp(m_old - m_new) * acc_ref[mp] + _dot(vt_ref[c], p)
        m_ref[mp] = m_new

    m_ref[...] = jnp.full(m_ref.shape, -jnp.inf, F32)
    acc_ref[...] = jnp.zeros(acc_ref.shape, F32)

    jj = lax.broadcasted_iota(jnp.int32, (tkc, tq), 0)
    ii = lax.broadcasted_iota(jnp.int32, (tkc, tq), 1)
    fixup = (-2.0 * slope) * jnp.maximum(ii - jj, 0).astype(F32)
    diag = [scores(cq, 0, mp) + fixup for mp in range(2)]
    for mp in range(2):
        absorb(cq, mp, diag[mp], jnp.max(diag[mp], axis=0, keepdims=True))

    k_max = lax.fori_loop(0, n_chunks, lambda c, a: jnp.maximum(a, kn_ref[hd, chunk0 + c]),
                          jnp.float32(0.0))
    gap = NORM_SLACK * qn_ref[hd, chunk0 + cq] * k_max + EXP_ZERO_BELOW - jnp.min(m_ref[...])
    reach = (gap * inv_slope - 1.0) * (1.0 / tkc)
    n_dist = jnp.where(reach >= 0.0,
                       jnp.minimum(reach, float(n_chunks)).astype(jnp.int32) + 1, 0)
    n_left = jnp.minimum(n_dist, cq)
    total = n_left + jnp.minimum(n_dist, n_chunks - 1 - cq)

    def item(t):
        left = t < n_left
        c = jnp.where(left, cq - 1 - t, cq + 1 + t - n_left)
        return jnp.clip(c, 0, n_chunks - 1), left.astype(jnp.int32)

    def issue(t, mp):
        c, sign = item(t)
        s = scores(c, sign, mp)
        s_ref[mp] = s
        mc_ref[mp] = jnp.max(s, axis=0, keepdims=True)

    def trip(t):
        c, _ = item(t)
        issue(t, 1)
        absorb(c, 0, s_ref[0], mc_ref[0])
        issue(t + 1, 0)
        absorb(c, 1, s_ref[1], mc_ref[1])

    issue(0, 0)
    rem = lax.rem(total, PIPE_UNROLL)
    for r in range(PIPE_UNROLL - 1):
        @pl.when(r < rem)
        def _():
            trip(r)

    def group(u, carry):
        for r in range(PIPE_UNROLL):
            trip(rem + PIPE_UNROLL * u + r)
        return carry

    lax.fori_loop(0, total // PIPE_UNROLL, group, 0)

    lam = (jnp.exp(jnp.sum(lq1_ref[...] * lk1_ref[...], keepdims=True))
           - jnp.exp(jnp.sum(lq2_ref[...] * lk2_ref[...], keepdims=True)) + lambda_init)
    o_t = (acc_ref[0, :V_DIM] / acc_ref[0, V_DIM:V_DIM + 1]
           - lam * (acc_ref[1, :V_DIM] / acc_ref[1, V_DIM:V_DIM + 1]))
    o = _rms(o_t.T, sg_ref[...], SUBLN_EPS) * (1.0 - lambda_init)
    o_ref[...] = o.astype(o_ref.dtype)


def _flash(slopes, qn, kn, lam_vecs, subln_g, qt, kk, vt, *, tok_start, n_seq, seq_len,
           lambda_init):
    tq = TOKEN_TILE
    nq = seq_len // tq
    blk0 = tok_start // tq
    seq0 = tok_start // seq_len
    smem = pl.BlockSpec(memory_space=pltpu.SMEM)
    vec = _const_spec((1, HEAD_DIM))
    return pl.pallas_call(
        functools.partial(_flash_kernel, lambda_init=lambda_init, blk0=blk0),
        grid=(n_seq, N_HEADS, nq),
        in_specs=[
            smem, smem, smem, vec, vec, vec, vec, _const_spec((1, V_DIM)),
            pl.BlockSpec((None, 4, V_DIM, tq), lambda b, h, i: (h, 0, 0, blk0 + b * nq + i)),
            pl.BlockSpec((None, 2, nq, tq, V_DIM), lambda b, h, i: (h, 0, seq0 + b, 0, 0)),
            pl.BlockSpec((None, nq, V_ROWS, tq), lambda b, h, i: (h, seq0 + b, 0, 0)),
        ],
        out_specs=pl.BlockSpec((tq, V_DIM), lambda b, h, i: (b * nq + i, h)),
        out_shape=jax.ShapeDtypeStruct((n_seq * seq_len, N_HEADS * V_DIM), BF16),
        scratch_shapes=[
            pltpu.VMEM((2, tq, tq), F32),
            pltpu.VMEM((2, 1, tq), F32),
            pltpu.VMEM((2, 1, tq), F32),
            pltpu.VMEM((2, V_ROWS, tq), F32),
        ],
        compiler_params=_params(3),
        name="diff_flash",
    )(slopes, qn, kn, *lam_vecs, subln_g.reshape(1, V_DIM), qt, kk, vt)


def kernel(x_prompt, x_sample, norm1_g, norm2_g, final_g, mix_in_w, pool_w, pool_scale, conv_w,
           mix_out_w, attn_qkv_w, attn_out_w, lambda_q1, lambda_k1, lambda_q2, lambda_k2, subln_g,
           mlp_w1, mlp_w2):
    pb, ps, d = x_prompt.shape
    sb, ss, _ = x_sample.shape
    n_prompt = pb * ps
    x = jnp.concatenate([x_prompt.reshape(n_prompt, d), x_sample.reshape(sb * ss, d)], axis=0)
    bounds = tuple(ps * b for b in range(pb)) + tuple(n_prompt + ss * b for b in range(sb + 1))
    slopes = jnp.exp2(-8.0 * (jnp.arange(N_HEADS, dtype=F32) + 1.0) / N_HEADS)
    slopes = jnp.stack([slopes, 1.0 / slopes])

    for i in range(DEPTH):
        j = i // 2
        w1 = mlp_w1[i].astype(BF16)
        w2 = mlp_w2[i].astype(BF16)
        fg = final_g if i == DEPTH - 1 else None
        if i % 2 == 0:
            x = _mixer(x, norm1_g[i], mix_in_w[j].astype(BF16), pool_w[j].astype(BF16),
                       pool_scale[j], conv_w[j], mix_out_w[j].astype(BF16), bounds)
            x = _mlp(x, norm2_g[i], w1, w2, final_g=fg)
        else:
            lambda_init = 0.8 - 0.6 * math.exp(-0.3 * i)
            qt, kk, vt, nrm = _qkv(x, norm1_g[i], attn_qkv_w[j].astype(BF16))
            qn, kn = (jnp.sqrt(nrm[:, r, :2 * N_HEADS].reshape(-1, N_HEADS, 2).max(axis=-1)).T
                      for r in range(2))
            lam_vecs = [v[j].reshape(1, HEAD_DIM) for v in (lambda_q1, lambda_k1, lambda_q2, lambda_k2)]
            flash = functools.partial(_flash, slopes, qn, kn, lam_vecs, subln_g[j], qt, kk, vt,
                                      lambda_init=lambda_init)
            attn = jnp.concatenate([
                flash(tok_start=0, n_seq=pb, seq_len=ps),
                flash(tok_start=n_prompt, n_seq=sb, seq_len=ss),
            ], axis=0)
            x = _mlp(x, norm2_g[i], w1, w2, attn=attn, attn_w=attn_out_w[j].astype(BF16), final_g=fg)

    return (x[:n_prompt].reshape(pb, ps, d), x[n_prompt:].reshape(sb, ss, d))
```

```python
import functools
import math

import jax
import jax.numpy as jnp
from jax import lax
from jax.experimental import pallas as pl
from jax.experimental.pallas import tpu as pltpu

D_MODEL = 1024
DEPTH = 4
POOL_WIDTH = D_MODEL // 2
N_POOL_GROUPS = 4
POOL_GROUP_DIM = POOL_WIDTH // N_POOL_GROUPS
POOL_WINDOWS = (2, 4, 8, 16)
CONV_WIDTH_CH = D_MODEL // 2
MIX_IN_COLS = POOL_WIDTH + 3 * CONV_WIDTH_CH
N_HEADS = 8
HEAD_DIM = D_MODEL // (2 * N_HEADS)
V_DIM = 2 * HEAD_DIM
ATTN_SCALE = HEAD_DIM ** -0.5
D_FF = 4 * D_MODEL
NORM_EPS = 1e-6
SUBLN_EPS = 1e-5

F32 = jnp.float32
BF16 = jnp.bfloat16

TOKEN_TILE = 512
HALO = 16
FF_CHUNK = 1024
VMEM_LIMIT_BYTES = 48 * 1024 * 1024
DENOM_ROWS = 16
V_ROWS = V_DIM + DENOM_ROWS
PIPE_UNROLL = 4
EXP_ZERO_BELOW = 106.0
NORM_SLACK = 1.01
POS_LO_MASK = 255
POS_HI_MASK = TOKEN_TILE - 1 - POS_LO_MASK


def _rms(x, g, eps):
    return x * lax.rsqrt(jnp.mean(x * x, axis=-1, keepdims=True) + eps) * g


def _dot(a, b):
    return jnp.dot(a, b, preferred_element_type=F32)


def _const_spec(shape):
    zeros = (0,) * len(shape)
    return pl.BlockSpec(shape, lambda *_: zeros)


def _part_starts(part_rows, rows):
    starts, s = [], 0
    for r in part_rows:
        starts.append(s // rows)
        s += r
    return starts


def _part_specs(parts, rows, block_of):
    specs = []
    for p, s0 in zip(parts, _part_starts([p.shape[0] for p in parts], rows)):
        nb = p.shape[0] // rows
        specs.append(pl.BlockSpec(
            (rows, p.shape[1]), lambda i, s0=s0, nb=nb: (jnp.clip(block_of(i) - s0, 0, nb - 1), 0)))
    return specs


def _pick_part(refs, block, part_rows, rows):
    value = refs[0][...]
    for ref, s0 in zip(refs[1:], _part_starts(part_rows, rows)[1:]):
        value = jnp.where(block >= s0, ref[...], value)
    return value


def _params(n_axes):
    return pltpu.CompilerParams(
        dimension_semantics=("arbitrary",) * n_axes,
        vmem_limit_bytes=VMEM_LIMIT_BYTES,
    )


def _mixer_kernel(*refs, bounds, part_rows):
    n = len(part_rows)
    xp_refs, x_refs, xn_refs = refs[:n], refs[n:2 * n], refs[2 * n:3 * n]
    g_ref, win_ref, pw_ref, ps_ref, cw_ref, wout_ref, o_ref, u_ref, z_ref = refs[3 * n:]
    tm = x_refs[0].shape[0]
    i = pl.program_id(0)
    per = tm // HALO
    last = sum(part_rows) // HALO - 1
    start = i * tm
    seq_start = jnp.int32(bounds[0])
    seq_end = jnp.int32(bounds[-1])
    for b in bounds[1:-1]:
        seq_start = jnp.where(start >= b, b, seq_start)
    for b in reversed(bounds[1:-1]):
        seq_end = jnp.where(start < b, b, seq_end)

    x = _pick_part(x_refs, i, part_rows, tm)
    xe = jnp.concatenate([
        _pick_part(xp_refs, jnp.maximum(i * per - 1, 0), part_rows, HALO),
        x,
        _pick_part(xn_refs, jnp.minimum((i + 1) * per, last), part_rows, HALO),
    ], axis=0)
    pos = start - HALO + lax.broadcasted_iota(jnp.int32, (tm + 2 * HALO, 1), 0)
    valid = (pos >= seq_start) & (pos < seq_end)
    he = _rms(xe, g_ref[...], NORM_EPS).astype(BF16)
    proj = _dot(he, win_ref[...])
    c0 = POOL_WIDTH
    u_ref[...] = jnp.where(valid, proj[:, :c0], 0.0)
    z_ref[...] = jnp.where(valid, proj[:, c0 + 2 * CONV_WIDTH_CH:] * proj[:, c0:c0 + CONV_WIDTH_CH], 0.0)

    rel = start - seq_start + lax.broadcasted_iota(jnp.int32, (tm, 1), 0)
    seq_len = seq_end - seq_start
    ys = []
    for g, w in enumerate(POOL_WINDOWS):
        cols = slice(g * POOL_GROUP_DIM, (g + 1) * POOL_GROUP_DIM)
        tot = u_ref[HALO - w // 2:HALO - w // 2 + tm, cols]
        for o in range(-w // 2 + 1, w // 2):
            tot = tot + u_ref[HALO + o:HALO + o + tm, cols]
        cnt = (jnp.minimum(rel + w // 2, seq_len) - jnp.maximum(rel - w // 2, 0)).astype(F32)
        d = tot / cnt - u_ref[HALO:HALO + tm, cols]
        ys.append(_dot(d.astype(BF16), pw_ref[g]))
    a_out = jnp.concatenate(ys, axis=-1) * ps_ref[...]

    conv = (cw_ref[0:1, :] * z_ref[HALO - 1:HALO - 1 + tm, :]
            + cw_ref[1:2, :] * z_ref[HALO:HALO + tm, :]
            + cw_ref[2:3, :] * z_ref[HALO + 1:HALO + 1 + tm, :])
    b_out = proj[HALO:HALO + tm, c0 + CONV_WIDTH_CH:c0 + 2 * CONV_WIDTH_CH] * conv
    mixed = jnp.concatenate([a_out, b_out], axis=-1).astype(BF16)
    o_ref[...] = x + _dot(mixed, wout_ref[...])


def _mixer(x_parts, g, win, pw, ps, cw, wout, bounds):
    d = x_parts[0].shape[1]
    part_rows = tuple(p.shape[0] for p in x_parts)
    t = sum(part_rows)
    tm = TOKEN_TILE
    per = tm // HALO
    last = t // HALO - 1
    return pl.pallas_call(
        functools.partial(_mixer_kernel, bounds=bounds, part_rows=part_rows),
        grid=(t // tm,),
        in_specs=[
            *_part_specs(x_parts, HALO, lambda i: jnp.maximum(i * per - 1, 0)),
            *_part_specs(x_parts, tm, lambda i: i),
            *_part_specs(x_parts, HALO, lambda i: jnp.minimum((i + 1) * per, last)),
            _const_spec((1, d)),
            _const_spec(win.shape),
            _const_spec(pw.shape),
            _const_spec((1, POOL_WIDTH)),
            _const_spec(cw.shape),
            _const_spec(wout.shape),
        ],
        out_specs=pl.BlockSpec((tm, d), lambda i: (i, 0)),
        out_shape=jax.ShapeDtypeStruct((t, d), F32),
        scratch_shapes=[
            pltpu.VMEM((tm + 2 * HALO, POOL_WIDTH), F32),
            pltpu.VMEM((tm + 2 * HALO, CONV_WIDTH_CH), F32),
        ],
        compiler_params=_params(1),
        name="even_mixer",
    )(*x_parts, *x_parts, *x_parts, g.reshape(1, d), win, pw, ps.reshape(1, POOL_WIDTH), cw, wout)


def _mlp_kernel(*refs, attn_rows, has_final):
    refs = list(refs)
    x_ref = refs.pop(0)
    if attn_rows:
        a_refs = [refs.pop(0) for _ in attn_rows]
        wo_ref = refs.pop(0)
    g_ref, w1_ref, w2_ref = refs[:3]
    refs = refs[3:]
    if has_final:
        fg_ref = refs.pop(0)
    o_ref = refs.pop(0)

    x = x_ref[...]
    if attn_rows:
        attn = _pick_part(a_refs, pl.program_id(0), attn_rows, x_ref.shape[0])
        x = x + _dot(attn, wo_ref[...])
    h = _rms(x, g_ref[...], NORM_EPS).astype(BF16)
    acc = x
    for c in range(D_FF // FF_CHUNK):
        cols = slice(c * FF_CHUNK, (c + 1) * FF_CHUNK)
        a = jnp.maximum(_dot(h, w1_ref[:, cols]), 0.0)
        acc = acc + _dot((a * a).astype(BF16), w2_ref[cols, :])
    if has_final:
        acc = _rms(acc, fg_ref[...], NORM_EPS)
    o_ref[...] = acc


def _mlp(x, g, w1, w2, attn_parts=(), attn_w=None, final_g=None, row_start=0, rows=None):
    d = x.shape[1]
    rows = x.shape[0] if rows is None else rows
    tm = TOKEN_TILE
    tile0 = row_start // tm
    row_spec = pl.BlockSpec((tm, d), lambda i: (i, 0))
    args = [x]
    in_specs = [pl.BlockSpec((tm, d), lambda i: (tile0 + i, 0))]
    if attn_parts:
        args += [*attn_parts, attn_w]
        in_specs += [*_part_specs(attn_parts, tm, lambda i: i), _const_spec(attn_w.shape)]
    args += [g.reshape(1, d), w1, w2]
    in_specs += [_const_spec((1, d)), _const_spec(w1.shape), _const_spec(w2.shape)]
    if final_g is not None:
        args.append(final_g.reshape(1, d))
        in_specs.append(_const_spec((1, d)))
    return pl.pallas_call(
        functools.partial(_mlp_kernel, attn_rows=tuple(p.shape[0] for p in attn_parts),
                          has_final=final_g is not None),
        grid=(rows // tm,),
        in_specs=in_specs,
        out_specs=row_spec,
        out_shape=jax.ShapeDtypeStruct((rows, d), F32),
        compiler_params=_params(1),
        name="sq_relu_mlp",
    )(*args)


def _qkv_kernel(x_ref, g_ref, w_ref, qt_ref, kk_ref, vt_ref, nrm_ref):
    tm = x_ref.shape[0]
    h = _rms(x_ref[...], g_ref[...], NORM_EPS).astype(BF16)
    qkv = _dot(h, w_ref[...])
    q_all = (qkv[:, :D_MODEL] * ATTN_SCALE).astype(BF16).astype(F32)
    k_all = qkv[:, D_MODEL:2 * D_MODEL].astype(BF16).astype(F32)

    sel = (lax.shift_right_logical(lax.broadcasted_iota(jnp.int32, (D_MODEL, V_DIM), 0),
                                   HEAD_DIM.bit_length() - 1)
           == lax.broadcasted_iota(jnp.int32, (D_MODEL, V_DIM), 1)).astype(BF16)

    def max_sq_norm(a):
        sq = a * a
        hi = sq.astype(BF16)
        lo = (sq - hi.astype(F32)).astype(BF16)
        return jnp.max(_dot(hi, sel) + _dot(lo, sel), axis=0, keepdims=True)

    nrm_ref[0] = jnp.concatenate(
        [max_sq_norm(q_all), max_sq_norm(k_all), jnp.zeros((6, V_DIM), F32)], axis=0)

    q_row = lax.broadcasted_iota(jnp.int32, (HEAD_DIM, tm), 0)
    q_tok = lax.broadcasted_iota(jnp.int32, (HEAD_DIM, tm), 1)
    q_lo = (q_tok & POS_LO_MASK).astype(F32)
    q_hi = (q_tok & POS_HI_MASK).astype(F32)
    k_lane = lax.broadcasted_iota(jnp.int32, (tm, V_DIM), 1)
    k_tok = lax.broadcasted_iota(jnp.int32, (tm, V_DIM), 0)
    k_lo = (k_tok & POS_LO_MASK).astype(F32)
    k_hi = (k_tok & POS_HI_MASK).astype(F32)

    denom_rows = (lax.broadcasted_iota(jnp.int32, (DENOM_ROWS, tm), 0) == 0).astype(F32)

    def k_extra(base, slope):
        return jnp.where(
            (k_lane == base) | (k_lane == base + 1), 1.0,
            jnp.where(k_lane == base + 2, -slope * k_lo,
                      jnp.where(k_lane == base + 3, -slope * k_hi, 0.0)))

    for hd in range(N_HEADS):
        slope = 2.0 ** (-8.0 * (hd + 1) / N_HEADS)
        cols = slice(hd * V_DIM, (hd + 1) * V_DIM)
        q_t = q_all[:, cols].T
        top, bot = q_t[:HEAD_DIM], q_t[HEAD_DIM:]
        q_extra = jnp.where(q_row == 0, slope * q_lo,
                            jnp.where(q_row == 1, slope * q_hi,
                                      jnp.where(q_row < 4, 1.0, 0.0)))
        qt_ref[hd, 0] = jnp.concatenate([top, q_extra], axis=0).astype(BF16)
        qt_ref[hd, 1] = jnp.concatenate([top, -q_extra], axis=0).astype(BF16)
        qt_ref[hd, 2] = jnp.concatenate([q_extra, bot], axis=0).astype(BF16)
        qt_ref[hd, 3] = jnp.concatenate([-q_extra, bot], axis=0).astype(BF16)

        kh = k_all[:, cols]
        kk_ref[hd, 0, 0] = jnp.where(k_lane < HEAD_DIM, kh, k_extra(HEAD_DIM, slope)).astype(BF16)
        kk_ref[hd, 1, 0] = jnp.where(k_lane >= HEAD_DIM, kh, k_extra(0, slope)).astype(BF16)

        vh = qkv[:, 2 * D_MODEL + hd * V_DIM:2 * D_MODEL + (hd + 1) * V_DIM]
        vt_ref[hd, 0] = jnp.concatenate([vh.T, denom_rows], axis=0).astype(BF16)


def _qkv(x, g, w):
    t, d = x.shape
    tm = TOKEN_TILE
    nc = t // tm
    return pl.pallas_call(
        _qkv_kernel,
        grid=(nc,),
        in_specs=[
            pl.BlockSpec((tm, d), lambda i: (i, 0)),
            _const_spec((1, d)),
            _const_spec(w.shape),
        ],
        out_specs=[
            pl.BlockSpec((N_HEADS, 4, V_DIM, tm), lambda i: (0, 0, 0, i)),
            pl.BlockSpec((N_HEADS, 2, 1, tm, V_DIM), lambda i: (0, 0, i, 0, 0)),
            pl.BlockSpec((N_HEADS, 1, V_ROWS, tm), lambda i: (0, i, 0, 0)),
            pl.BlockSpec((1, 8, V_DIM), lambda i: (i, 0, 0)),
        ],
        out_shape=[
            jax.ShapeDtypeStruct((N_HEADS, 4, V_DIM, t), BF16),
            jax.ShapeDtypeStruct((N_HEADS, 2, nc, tm, V_DIM), BF16),
            jax.ShapeDtypeStruct((N_HEADS, nc, V_ROWS, tm), BF16),
            jax.ShapeDtypeStruct((nc, 8, V_DIM), F32),
        ],
        compiler_params=_params(1),
        name="attn_qkv",
    )(x, g.reshape(1, d), w)


def _flash_kernel(slope_ref, qn_ref, kn_ref, lq1_ref, lk1_ref, lq2_ref, lk2_ref, sg_ref,
                  qt_ref, kk_ref, vt_ref, o_ref, s_ref, mc_ref, m_ref, acc_ref,
                  *, lambda_init, blk0):
    n_chunks, tkc = kk_ref.shape[1], kk_ref.shape[2]
    tq = qt_ref.shape[-1]
    hd = pl.program_id(1)
    slope, inv_slope = slope_ref[0, hd], slope_ref[1, hd]
    cq = pl.program_id(2)
    chunk0 = blk0 + pl.program_id(0) * n_chunks

    def scores(c, sign, mp):
        return _dot(kk_ref[mp, c], qt_ref[2 * mp + sign])

    def absorb(c, mp, s, mc):
        shift = -slope * (tkc * jnp.abs(cq - c)).astype(F32)
        m_old = m_ref[mp]
        m_new = jnp.maximum(m_old, mc + shift)
        p = jnp.exp(s - (m_new - shift)).astype(BF16)
        acc_ref[mp] = jnp.exp(m_old - m_new) * acc_ref[mp] + _dot(vt_ref[c], p)
        m_ref[mp] = m_new

    def issue_chunk(c, sign, mp):
        s = scores(c, sign, mp)
        s_ref[mp] = s
        mc_ref[mp] = jnp.max(s, axis=0, keepdims=True)

    jj = lax.broadcasted_iota(jnp.int32, (tkc, tq), 0)
    ii = lax.broadcasted_iota(jnp.int32, (tkc, tq), 1)
    fixup = (-2.0 * slope) * jnp.maximum(ii - jj, 0).astype(F32)
    diag = [scores(cq, 0, mp) + fixup for mp in range(2)]
    issue_chunk(jnp.where(cq > 0, cq - 1, jnp.minimum(cq + 1, n_chunks - 1)),
                (cq > 0).astype(jnp.int32), 0)
    for mp in range(2):
        m_first = jnp.max(diag[mp], axis=0, keepdims=True)
        m_ref[mp] = m_first
        acc_ref[mp] = _dot(vt_ref[cq], jnp.exp(diag[mp] - m_first).astype(BF16))

    gap = (NORM_SLACK * qn_ref[hd, chunk0 + cq] * kn_ref[hd, pl.program_id(0)]
           + EXP_ZERO_BELOW - jnp.min(m_ref[...]))
    reach = (gap * inv_slope - 1.0) * (1.0 / tkc)
    n_dist = jnp.where(reach >= 1.0,
                       jnp.minimum(reach, float(n_chunks)).astype(jnp.int32) + 1, 1)
    n_left = jnp.minimum(n_dist, cq)
    total = n_left + jnp.minimum(n_dist, n_chunks - 1 - cq)

    def item(t):
        left = t < n_left
        c = jnp.where(left, cq - 1 - t, cq + 1 + t - n_left)
        return jnp.clip(c, 0, n_chunks - 1), left.astype(jnp.int32)

    def trip(t):
        c, sign = item(t)
        issue_chunk(c, sign, 1)
        absorb(c, 0, s_ref[0], mc_ref[0])
        issue_chunk(*item(t + 1), 0)
        absorb(c, 1, s_ref[1], mc_ref[1])

    def trips(t0, count):
        for r in range(count):
            trip(t0 + r)

    assert PIPE_UNROLL == 4
    odd = total & 1
    pl.when(odd == 1)(functools.partial(trips, 0, 1))
    pl.when((total & 2) == 2)(functools.partial(trips, odd, 2))
    rem = total & (PIPE_UNROLL - 1)

    def group(u, carry):
        trips(rem + PIPE_UNROLL * u, PIPE_UNROLL)
        return carry

    lax.fori_loop(0, total // PIPE_UNROLL, group, 0)

    lam = (jnp.exp(jnp.sum(lq1_ref[...] * lk1_ref[...], keepdims=True))
           - jnp.exp(jnp.sum(lq2_ref[...] * lk2_ref[...], keepdims=True)) + lambda_init)
    o_t = (acc_ref[0, :V_DIM] / acc_ref[0, V_DIM:V_DIM + 1]
           - lam * (acc_ref[1, :V_DIM] / acc_ref[1, V_DIM:V_DIM + 1]))
    o = _rms(o_t.T, sg_ref[...], SUBLN_EPS) * (1.0 - lambda_init)
    o_ref[...] = o.astype(o_ref.dtype)


def _flash(slopes, qn, kn, lam_vecs, subln_g, qt, kk, vt, *, tok_start, n_seq, seq_len,
           lambda_init):
    tq = TOKEN_TILE
    nq = seq_len // tq
    blk0 = tok_start // tq
    seq0 = tok_start // seq_len
    smem = pl.BlockSpec(memory_space=pltpu.SMEM)
    vec = _const_spec((1, HEAD_DIM))
    return pl.pallas_call(
        functools.partial(_flash_kernel, lambda_init=lambda_init, blk0=blk0),
        grid=(n_seq, N_HEADS, nq),
        in_specs=[
            smem, smem, smem, vec, vec, vec, vec, _const_spec((1, V_DIM)),
            pl.BlockSpec((None, 4, V_DIM, tq), lambda b, h, i: (h, 0, 0, blk0 + b * nq + i)),
            pl.BlockSpec((None, 2, nq, tq, V_DIM), lambda b, h, i: (h, 0, seq0 + b, 0, 0)),
            pl.BlockSpec((None, nq, V_ROWS, tq), lambda b, h, i: (h, seq0 + b, 0, 0)),
        ],
        out_specs=pl.BlockSpec((tq, V_DIM), lambda b, h, i: (b * nq + i, h)),
        out_shape=jax.ShapeDtypeStruct((n_seq * seq_len, N_HEADS * V_DIM), BF16),
        scratch_shapes=[
            pltpu.VMEM((2, tq, tq), F32),
            pltpu.VMEM((2, 1, tq), F32),
            pltpu.VMEM((2, 1, tq), F32),
            pltpu.VMEM((2, V_ROWS, tq), F32),
        ],
        compiler_params=_params(3),
        name="diff_flash",
    )(slopes, qn, kn[:, blk0:blk0 + n_seq * nq].reshape(N_HEADS, n_seq, nq).max(axis=-1),
      *lam_vecs, subln_g.reshape(1, V_DIM), qt, kk, vt)


def kernel(x_prompt, x_sample, norm1_g, norm2_g, final_g, mix_in_w, pool_w, pool_scale, conv_w,
           mix_out_w, attn_qkv_w, attn_out_w, lambda_q1, lambda_k1, lambda_q2, lambda_k2, subln_g,
           mlp_w1, mlp_w2):
    pb, ps, d = x_prompt.shape
    sb, ss, _ = x_sample.shape
    n_prompt = pb * ps
    n_sample = sb * ss
    x_parts = [x_prompt.reshape(n_prompt, d), x_sample.reshape(n_sample, d)]
    bounds = tuple(ps * b for b in range(pb)) + tuple(n_prompt + ss * b for b in range(sb + 1))
    slopes = jnp.exp2(-8.0 * (jnp.arange(N_HEADS, dtype=F32) + 1.0) / N_HEADS)
    slopes = jnp.stack([slopes, 1.0 / slopes])

    for i in range(DEPTH):
        j = i // 2
        w1 = mlp_w1[i].astype(BF16)
        w2 = mlp_w2[i].astype(BF16)
        attn_parts, attn_w = (), None
        if i % 2 == 0:
            x = _mixer(x_parts, norm1_g[i], mix_in_w[j].astype(BF16), pool_w[j].astype(BF16),
                       pool_scale[j], conv_w[j], mix_out_w[j].astype(BF16), bounds)
        else:
            x = x_parts[0]
            lambda_init = 0.8 - 0.6 * math.exp(-0.3 * i)
            qt, kk, vt, nrm = _qkv(x, norm1_g[i], attn_qkv_w[j].astype(BF16))
            qn, kn = (jnp.sqrt(nrm[:, r, :2 * N_HEADS].reshape(-1, N_HEADS, 2).max(axis=-1)).T
                      for r in range(2))
            lam_vecs = [v[j].reshape(1, HEAD_DIM) for v in (lambda_q1, lambda_k1, lambda_q2, lambda_k2)]
            flash = functools.partial(_flash, slopes, qn, kn, lam_vecs, subln_g[j], qt, kk, vt,
                                      lambda_init=lambda_init)
            attn_parts = (flash(tok_start=0, n_seq=pb, seq_len=ps),
                          flash(tok_start=n_prompt, n_seq=sb, seq_len=ss))
            attn_w = attn_out_w[j].astype(BF16)
        if i < DEPTH - 1:
            x_parts = [_mlp(x, norm2_g[i], w1, w2, attn_parts, attn_w)]
        else:
            y_prompt, y_sample = (
                _mlp(x, norm2_g[i], w1, w2, attn_parts[k:k + 1], attn_w, final_g, r0, nr)
                for k, (r0, nr) in enumerate(((0, n_prompt), (n_prompt, n_sample))))

    return (y_prompt.reshape(pb, ps, d), y_sample.reshape(sb, ss, d))
```

```python
import functools
import math

import jax
import jax.numpy as jnp
from jax import lax
from jax.experimental import pallas as pl
from jax.experimental.pallas import tpu as pltpu

D_MODEL = 1024
DEPTH = 4
POOL_WIDTH = D_MODEL // 2
N_POOL_GROUPS = 4
POOL_GROUP_DIM = POOL_WIDTH // N_POOL_GROUPS
POOL_WINDOWS = (2, 4, 8, 16)
CONV_WIDTH_CH = D_MODEL // 2
MIX_IN_COLS = POOL_WIDTH + 3 * CONV_WIDTH_CH
N_HEADS = 8
HEAD_DIM = D_MODEL // (2 * N_HEADS)
V_DIM = 2 * HEAD_DIM
ATTN_SCALE = HEAD_DIM ** -0.5
D_FF = 4 * D_MODEL
NORM_EPS = 1e-6
SUBLN_EPS = 1e-5

F32 = jnp.float32
BF16 = jnp.bfloat16

TOKEN_TILE = 512
HALO = 16
FF_CHUNK = 1024
VMEM_LIMIT_BYTES = 48 * 1024 * 1024
DENOM_ROWS = 16
V_ROWS = V_DIM + DENOM_ROWS
PIPE_UNROLL = 4
EXP_ZERO_BELOW = 106.0
NORM_SLACK = 1.01
POS_LO_MASK = 255
POS_HI_MASK = TOKEN_TILE - 1 - POS_LO_MASK


def _rms(x, g, eps):
    return x * lax.rsqrt(jnp.mean(x * x, axis=-1, keepdims=True) + eps) * g


def _dot(a, b):
    return jnp.dot(a, b, preferred_element_type=F32)


def _const_spec(shape):
    zeros = (0,) * len(shape)
    return pl.BlockSpec(shape, lambda *_: zeros)


def _part_starts(part_rows, rows):
    starts, s = [], 0
    for r in part_rows:
        starts.append(s // rows)
        s += r
    return starts


def _part_specs(parts, rows, block_of):
    specs = []
    for p, s0 in zip(parts, _part_starts([p.shape[0] for p in parts], rows)):
        nb = p.shape[0] // rows
        specs.append(pl.BlockSpec(
            (rows, p.shape[1]), lambda i, s0=s0, nb=nb: (jnp.clip(block_of(i) - s0, 0, nb - 1), 0)))
    return specs


def _pick_part(refs, block, part_rows, rows):
    value = refs[0][...]
    for ref, s0 in zip(refs[1:], _part_starts(part_rows, rows)[1:]):
        value = jnp.where(block >= s0, ref[...], value)
    return value


def _params(n_axes):
    return pltpu.CompilerParams(
        dimension_semantics=("arbitrary",) * n_axes,
        vmem_limit_bytes=VMEM_LIMIT_BYTES,
    )


def _mixer_kernel(*refs, bounds, part_rows):
    n = len(part_rows)
    xp_refs, x_refs, xn_refs = refs[:n], refs[n:2 * n], refs[2 * n:3 * n]
    g_ref, win_ref, pw_ref, ps_ref, cw_ref, wout_ref, o_ref, u_ref, z_ref = refs[3 * n:]
    tm = x_refs[0].shape[0]
    i = pl.program_id(0)
    per = tm // HALO
    last = sum(part_rows) // HALO - 1
    start = i * tm
    seq_start = jnp.int32(bounds[0])
    seq_end = jnp.int32(bounds[-1])
    for b in bounds[1:-1]:
        seq_start = jnp.where(start >= b, b, seq_start)
    for b in reversed(bounds[1:-1]):
        seq_end = jnp.where(start < b, b, seq_end)

    x = _pick_part(x_refs, i, part_rows, tm)
    xe = jnp.concatenate([
        _pick_part(xp_refs, jnp.maximum(i * per - 1, 0), part_rows, HALO),
        x,
        _pick_part(xn_refs, jnp.minimum((i + 1) * per, last), part_rows, HALO),
    ], axis=0)
    pos = start - HALO + lax.broadcasted_iota(jnp.int32, (tm + 2 * HALO, 1), 0)
    valid = (pos >= seq_start) & (pos < seq_end)
    he = _rms(xe, g_ref[...], NORM_EPS).astype(BF16)
    proj = _dot(he, win_ref[...])
    c0 = POOL_WIDTH
    u_ref[...] = jnp.where(valid, proj[:, :c0], 0.0)
    z_ref[...] = jnp.where(valid, proj[:, c0 + 2 * CONV_WIDTH_CH:] * proj[:, c0:c0 + CONV_WIDTH_CH], 0.0)

    rel = start - seq_start + lax.broadcasted_iota(jnp.int32, (tm, 1), 0)
    seq_len = seq_end - seq_start
    ys = []
    for g, w in enumerate(POOL_WINDOWS):
        cols = slice(g * POOL_GROUP_DIM, (g + 1) * POOL_GROUP_DIM)
        tot = u_ref[HALO - w // 2:HALO - w // 2 + tm, cols]
        for o in range(-w // 2 + 1, w // 2):
            tot = tot + u_ref[HALO + o:HALO + o + tm, cols]
        cnt = (jnp.minimum(rel + w // 2, seq_len) - jnp.maximum(rel - w // 2, 0)).astype(F32)
        d = tot / cnt - u_ref[HALO:HALO + tm, cols]
        ys.append(_dot(d.astype(BF16), pw_ref[g]))
    a_out = jnp.concatenate(ys, axis=-1) * ps_ref[...]

    conv = (cw_ref[0:1, :] * z_ref[HALO - 1:HALO - 1 + tm, :]
            + cw_ref[1:2, :] * z_ref[HALO:HALO + tm, :]
            + cw_ref[2:3, :] * z_ref[HALO + 1:HALO + 1 + tm, :])
    b_out = proj[HALO:HALO + tm, c0 + CONV_WIDTH_CH:c0 + 2 * CONV_WIDTH_CH] * conv
    mixed = jnp.concatenate([a_out, b_out], axis=-1).astype(BF16)
    o_ref[...] = x + _dot(mixed, wout_ref[...])


def _mixer(x_parts, g, win, pw, ps, cw, wout, bounds):
    d = x_parts[0].shape[1]
    part_rows = tuple(p.shape[0] for p in x_parts)
    t = sum(part_rows)
    tm = TOKEN_TILE
    per = tm // HALO
    last = t // HALO - 1
    return pl.pallas_call(
        functools.partial(_mixer_kernel, bounds=bounds, part_rows=part_rows),
        grid=(t // tm,),
        in_specs=[
            *_part_specs(x_parts, HALO, lambda i: jnp.maximum(i * per - 1, 0)),
            *_part_specs(x_parts, tm, lambda i: i),
            *_part_specs(x_parts, HALO, lambda i: jnp.minimum((i + 1) * per, last)),
            _const_spec((1, d)),
            _const_spec(win.shape),
            _const_spec(pw.shape),
            _const_spec((1, POOL_WIDTH)),
            _const_spec(cw.shape),
            _const_spec(wout.shape),
        ],
        out_specs=pl.BlockSpec((tm, d), lambda i: (i, 0)),
        out_shape=jax.ShapeDtypeStruct((t, d), F32),
        scratch_shapes=[
            pltpu.VMEM((tm + 2 * HALO, POOL_WIDTH), F32),
            pltpu.VMEM((tm + 2 * HALO, CONV_WIDTH_CH), F32),
        ],
        compiler_params=_params(1),
        name="even_mixer",
    )(*x_parts, *x_parts, *x_parts, g.reshape(1, d), win, pw, ps.reshape(1, POOL_WIDTH), cw, wout)


def _mlp_kernel(*refs, attn_rows, has_final):
    refs = list(refs)
    x_ref = refs.pop(0)
    if attn_rows:
        a_refs = [refs.pop(0) for _ in attn_rows]
        wo_ref = refs.pop(0)
    g_ref, w1_ref, w2_ref = refs[:3]
    refs = refs[3:]
    if has_final:
        fg_ref = refs.pop(0)
    o_ref = refs.pop(0)

    x = x_ref[...]
    if attn_rows:
        attn = _pick_part(a_refs, pl.program_id(0), attn_rows, x_ref.shape[0])
        x = x + _dot(attn, wo_ref[...])
    h = _rms(x, g_ref[...], NORM_EPS).astype(BF16)
    acc = x
    for c in range(D_FF // FF_CHUNK):
        cols = slice(c * FF_CHUNK, (c + 1) * FF_CHUNK)
        a = jnp.maximum(_dot(h, w1_ref[:, cols]), 0.0)
        acc = acc + _dot((a * a).astype(BF16), w2_ref[cols, :])
    if has_final:
        acc = _rms(acc, fg_ref[...], NORM_EPS)
    o_ref[...] = acc


def _mlp(x, g, w1, w2, attn_parts=(), attn_w=None, final_g=None, row_start=0, rows=None):
    d = x.shape[1]
    rows = x.shape[0] if rows is None else rows
    tm = TOKEN_TILE
    tile0 = row_start // tm
    row_spec = pl.BlockSpec((tm, d), lambda i: (i, 0))
    args = [x]
    in_specs = [pl.BlockSpec((tm, d), lambda i: (tile0 + i, 0))]
    if attn_parts:
        args += [*attn_parts, attn_w]
        in_specs += [*_part_specs(attn_parts, tm, lambda i: i), _const_spec(attn_w.shape)]
    args += [g.reshape(1, d), w1, w2]
    in_specs += [_const_spec((1, d)), _const_spec(w1.shape), _const_spec(w2.shape)]
    if final_g is not None:
        args.append(final_g.reshape(1, d))
        in_specs.append(_const_spec((1, d)))
    return pl.pallas_call(
        functools.partial(_mlp_kernel, attn_rows=tuple(p.shape[0] for p in attn_parts),
                          has_final=final_g is not None),
        grid=(rows // tm,),
        in_specs=in_specs,
        out_specs=row_spec,
        out_shape=jax.ShapeDtypeStruct((rows, d), F32),
        compiler_params=_params(1),
        name="sq_relu_mlp",
    )(*args)


def _qkv_kernel(x_ref, g_ref, w_ref, qt_ref, kk_ref, vt_ref, nrm_ref):
    tm = x_ref.shape[0]
    h = _rms(x_ref[...], g_ref[...], NORM_EPS).astype(BF16)
    qkv = _dot(h, w_ref[...])
    q_all = (qkv[:, :D_MODEL] * ATTN_SCALE).astype(BF16).astype(F32)
    k_all = qkv[:, D_MODEL:2 * D_MODEL].astype(BF16).astype(F32)

    sel = (lax.shift_right_logical(lax.broadcasted_iota(jnp.int32, (D_MODEL, V_DIM), 0),
                                   HEAD_DIM.bit_length() - 1)
           == lax.broadcasted_iota(jnp.int32, (D_MODEL, V_DIM), 1)).astype(BF16)

    def max_sq_norm(a):
        sq = a * a
        hi = sq.astype(BF16)
        lo = (sq - hi.astype(F32)).astype(BF16)
        return jnp.max(_dot(hi, sel) + _dot(lo, sel), axis=0, keepdims=True)

    nrm_ref[0] = jnp.concatenate(
        [max_sq_norm(q_all), max_sq_norm(k_all), jnp.zeros((6, V_DIM), F32)], axis=0)

    q_row = lax.broadcasted_iota(jnp.int32, (HEAD_DIM, tm), 0)
    q_tok = lax.broadcasted_iota(jnp.int32, (HEAD_DIM, tm), 1)
    q_lo = (q_tok & POS_LO_MASK).astype(F32)
    q_hi = (q_tok & POS_HI_MASK).astype(F32)
    k_lane = lax.broadcasted_iota(jnp.int32, (tm, V_DIM), 1)
    k_tok = lax.broadcasted_iota(jnp.int32, (tm, V_DIM), 0)
    k_lo = (k_tok & POS_LO_MASK).astype(F32)
    k_hi = (k_tok & POS_HI_MASK).astype(F32)

    denom_rows = (lax.broadcasted_iota(jnp.int32, (DENOM_ROWS, tm), 0) == 0).astype(F32)

    def k_extra(base, slope):
        return jnp.where(
            (k_lane == base) | (k_lane == base + 1), 1.0,
            jnp.where(k_lane == base + 2, -slope * k_lo,
                      jnp.where(k_lane == base + 3, -slope * k_hi, 0.0)))

    for hd in range(N_HEADS):
        slope = 2.0 ** (-8.0 * (hd + 1) / N_HEADS)
        cols = slice(hd * V_DIM, (hd + 1) * V_DIM)
        q_t = q_all[:, cols].T
        top, bot = q_t[:HEAD_DIM], q_t[HEAD_DIM:]
        q_extra = jnp.where(q_row == 0, slope * q_lo,
                            jnp.where(q_row == 1, slope * q_hi,
                                      jnp.where(q_row < 4, 1.0, 0.0)))
        qt_ref[hd, 0] = jnp.concatenate([top, q_extra], axis=0).astype(BF16)
        qt_ref[hd, 1] = jnp.concatenate([top, -q_extra], axis=0).astype(BF16)
        qt_ref[hd, 2] = jnp.concatenate([q_extra, bot], axis=0).astype(BF16)
        qt_ref[hd, 3] = jnp.concatenate([-q_extra, bot], axis=0).astype(BF16)

        kh = k_all[:, cols]
        kk_ref[hd, 0, 0] = jnp.where(k_lane < HEAD_DIM, kh, k_extra(HEAD_DIM, slope)).astype(BF16)
        kk_ref[hd, 1, 0] = jnp.where(k_lane >= HEAD_DIM, kh, k_extra(0, slope)).astype(BF16)

        vh = qkv[:, 2 * D_MODEL + hd * V_DIM:2 * D_MODEL + (hd + 1) * V_DIM]
        vt_ref[hd, 0] = jnp.concatenate([vh.T, denom_rows], axis=0).astype(BF16)


def _qkv(x, g, w):
    t, d = x.shape
    tm = TOKEN_TILE
    nc = t // tm
    return pl.pallas_call(
        _qkv_kernel,
        grid=(nc,),
        in_specs=[
            pl.BlockSpec((tm, d), lambda i: (i, 0)),
            _const_spec((1, d)),
            _const_spec(w.shape),
        ],
        out_specs=[
            pl.BlockSpec((N_HEADS, 4, V_DIM, tm), lambda i: (0, 0, 0, i)),
            pl.BlockSpec((N_HEADS, 2, 1, tm, V_DIM), lambda i: (0, 0, i, 0, 0)),
            pl.BlockSpec((N_HEADS, 1, V_ROWS, tm), lambda i: (0, i, 0, 0)),
            pl.BlockSpec((1, 8, V_DIM), lambda i: (i, 0, 0)),
        ],
        out_shape=[
            jax.ShapeDtypeStruct((N_HEADS, 4, V_DIM, t), BF16),
            jax.ShapeDtypeStruct((N_HEADS, 2, nc, tm, V_DIM), BF16),
            jax.ShapeDtypeStruct((N_HEADS, nc, V_ROWS, tm), BF16),
            jax.ShapeDtypeStruct((nc, 8, V_DIM), F32),
        ],
        compiler_params=_params(1),
        name="attn_qkv",
    )(x, g.reshape(1, d), w)


def _flash_kernel(slope_ref, qn_ref, kn_ref, lq1_ref, lk1_ref, lq2_ref, lk2_ref, sg_ref, fix_ref,
                  qt_ref, qtn_ref, kk_ref, vt_ref, o_ref, d_ref, s_ref, mc_ref, m_ref, acc_ref,
                  *, lambda_init, blk0):
    n_chunks, tkc = kk_ref.shape[1], kk_ref.shape[2]
    hd = pl.program_id(1)
    slope, inv_slope = slope_ref[0, hd], slope_ref[1, hd]
    cq = pl.program_id(2)
    chunk0 = blk0 + pl.program_id(0) * n_chunks

    def scores(c, sign, mp):
        return _dot(kk_ref[mp, c], qt_ref[2 * mp + sign])

    def absorb(c, mp, s, mc):
        shift = -slope * (tkc * jnp.abs(cq - c)).astype(F32)
        m_old = m_ref[mp]
        m_new = jnp.maximum(m_old, mc + shift)
        p = jnp.exp(s - (m_new - shift)).astype(BF16)
        acc_ref[mp] = jnp.exp(m_old - m_new) * acc_ref[mp] + _dot(vt_ref[c], p)
        m_ref[mp] = m_new

    def issue_chunk(c, sign, mp):
        s = scores(c, sign, mp)
        s_ref[mp] = s
        mc_ref[mp] = jnp.max(s, axis=0, keepdims=True)

    @pl.when(cq == 0)
    def _():
        for mp in range(2):
            d_ref[mp] = scores(cq, 0, mp)

    assert n_chunks >= 2
    c_near = jnp.where(cq > 0, cq - 1, cq + 1)
    sign_near = (cq > 0).astype(jnp.int32)
    for mp in range(2):
        issue_chunk(c_near, sign_near, mp)
        diag = d_ref[mp] + slope * fix_ref[...]
        m_first = jnp.max(diag, axis=0, keepdims=True)
        m_ref[mp] = m_first
        acc_ref[mp] = _dot(vt_ref[cq], jnp.exp(diag - m_first).astype(BF16))

    gap = (NORM_SLACK * qn_ref[hd, chunk0 + cq] * kn_ref[hd, pl.program_id(0)]
           + EXP_ZERO_BELOW - jnp.min(m_ref[...]))
    reach = (gap * inv_slope - 1.0) * (1.0 / tkc)
    n_dist = jnp.where(reach >= 1.0,
                       jnp.minimum(reach, float(n_chunks)).astype(jnp.int32) + 1, 1)
    n_left = jnp.minimum(n_dist, cq)
    total = n_left + jnp.minimum(n_dist, n_chunks - 1 - cq)

    def item(t):
        left = t < n_left
        c = jnp.where(left, cq - 1 - t, cq + 1 + t - n_left)
        return jnp.clip(c, 0, n_chunks - 1), left.astype(jnp.int32)

    def trip(t):
        c, sign = item(t)
        issue_chunk(c, sign, 1)
        absorb(c, 0, s_ref[0], mc_ref[0])
        issue_chunk(*item(t + 1), 0)
        absorb(c, 1, s_ref[1], mc_ref[1])

    def trips(t0, count):
        for r in range(count):
            trip(t0 + r)

    absorb(c_near, 0, s_ref[0], mc_ref[0])
    issue_chunk(*item(1), 0)
    absorb(c_near, 1, s_ref[1], mc_ref[1])

    assert PIPE_UNROLL == 4
    rest = total - 1
    odd = rest & 1
    pl.when(odd == 1)(functools.partial(trips, 1, 1))
    pl.when((rest & 2) == 2)(functools.partial(trips, 1 + odd, 2))
    rem = rest & (PIPE_UNROLL - 1)

    def group(u, carry):
        trips(1 + rem + PIPE_UNROLL * u, PIPE_UNROLL)
        return carry

    lax.fori_loop(0, rest // PIPE_UNROLL, group, 0)

    c_next = jnp.minimum(cq + 1, n_chunks - 1)
    for mp in range(2):
        d_ref[mp] = _dot(kk_ref[mp, c_next], qtn_ref[2 * mp])

    lam = (jnp.exp(jnp.sum(lq1_ref[...] * lk1_ref[...], keepdims=True))
           - jnp.exp(jnp.sum(lq2_ref[...] * lk2_ref[...], keepdims=True)) + lambda_init)
    o_t = (acc_ref[0, :V_DIM] * (1.0 / acc_ref[0, V_DIM:V_DIM + 1])
           - (lam / acc_ref[1, V_DIM:V_DIM + 1]) * acc_ref[1, :V_DIM])
    scale = lax.rsqrt(jnp.mean(o_t * o_t, axis=0, keepdims=True) + SUBLN_EPS) * (1.0 - lambda_init)
    o_ref[...] = (o_t * scale * sg_ref[...]).T.astype(o_ref.dtype)


def _flash(slopes, qn, kn, lam_vecs, subln_g, qt, kk, vt, *, tok_start, n_seq, seq_len,
           lambda_init):
    tq = TOKEN_TILE
    nq = seq_len // tq
    blk0 = tok_start // tq
    seq0 = tok_start // seq_len
    smem = pl.BlockSpec(memory_space=pltpu.SMEM)
    vec = _const_spec((1, HEAD_DIM))
    offs = jnp.arange(tq, dtype=jnp.int32)
    fix = -2.0 * jnp.maximum(offs[None, :] - offs[:, None], 0).astype(F32)
    return pl.pallas_call(
        functools.partial(_flash_kernel, lambda_init=lambda_init, blk0=blk0),
        grid=(n_seq, N_HEADS, nq),
        in_specs=[
            smem, smem, smem, vec, vec, vec, vec, _const_spec((V_DIM, tq)), _const_spec((tq, tq)),
            pl.BlockSpec((None, 4, V_DIM, tq), lambda b, h, i: (h, 0, 0, blk0 + b * nq + i)),
            pl.BlockSpec((None, 4, V_DIM, tq),
                         lambda b, h, i: (h, 0, 0, blk0 + b * nq + jnp.minimum(i + 1, nq - 1))),
            pl.BlockSpec((None, 2, nq, tq, V_DIM), lambda b, h, i: (h, 0, seq0 + b, 0, 0)),
            pl.BlockSpec((None, nq, V_ROWS, tq), lambda b, h, i: (h, seq0 + b, 0, 0)),
        ],
        out_specs=pl.BlockSpec((tq, V_DIM), lambda b, h, i: (b * nq + i, h)),
        out_shape=jax.ShapeDtypeStruct((n_seq * seq_len, N_HEADS * V_DIM), BF16),
        scratch_shapes=[
            pltpu.VMEM((2, tq, tq), F32),
            pltpu.VMEM((2, tq, tq), F32),
            pltpu.VMEM((2, 1, tq), F32),
            pltpu.VMEM((2, 1, tq), F32),
            pltpu.VMEM((2, V_ROWS, tq), F32),
        ],
        compiler_params=_params(3),
        name="diff_flash",
    )(slopes, qn, kn[:, blk0:blk0 + n_seq * nq].reshape(N_HEADS, n_seq, nq).max(axis=-1),
      *lam_vecs, jnp.broadcast_to(subln_g.reshape(V_DIM, 1), (V_DIM, tq)), fix, qt, qt, kk, vt)


def kernel(x_prompt, x_sample, norm1_g, norm2_g, final_g, mix_in_w, pool_w, pool_scale, conv_w,
           mix_out_w, attn_qkv_w, attn_out_w, lambda_q1, lambda_k1, lambda_q2, lambda_k2, subln_g,
           mlp_w1, mlp_w2):
    pb, ps, d = x_prompt.shape
    sb, ss, _ = x_sample.shape
    n_prompt = pb * ps
    n_sample = sb * ss
    x_parts = [x_prompt.reshape(n_prompt, d), x_sample.reshape(n_sample, d)]
    bounds = tuple(ps * b for b in range(pb)) + tuple(n_prompt + ss * b for b in range(sb + 1))
    slopes = jnp.exp2(-8.0 * (jnp.arange(N_HEADS, dtype=F32) + 1.0) / N_HEADS)
    slopes = jnp.stack([slopes, 1.0 / slopes])

    for i in range(DEPTH):
        j = i // 2
        w1 = mlp_w1[i].astype(BF16)
        w2 = mlp_w2[i].astype(BF16)
        attn_parts, attn_w = (), None
        if i % 2 == 0:
            x = _mixer(x_parts, norm1_g[i], mix_in_w[j].astype(BF16), pool_w[j].astype(BF16),
                       pool_scale[j], conv_w[j], mix_out_w[j].astype(BF16), bounds)
        else:
            x = x_parts[0]
            lambda_init = 0.8 - 0.6 * math.exp(-0.3 * i)
            qt, kk, vt, nrm = _qkv(x, norm1_g[i], attn_qkv_w[j].astype(BF16))
            qn, kn = (jnp.sqrt(nrm[:, r, :2 * N_HEADS].reshape(-1, N_HEADS, 2).max(axis=-1)).T
                      for r in range(2))
            lam_vecs = [v[j].reshape(1, HEAD_DIM) for v in (lambda_q1, lambda_k1, lambda_q2, lambda_k2)]
            flash = functools.partial(_flash, slopes, qn, kn, lam_vecs, subln_g[j], qt, kk, vt,
                                      lambda_init=lambda_init)
            attn_parts = (flash(tok_start=0, n_seq=pb, seq_len=ps),
                          flash(tok_start=n_prompt, n_seq=sb, seq_len=ss))
            attn_w = attn_out_w[j].astype(BF16)
        if i < DEPTH - 1:
            x_parts = [_mlp(x, norm2_g[i], w1, w2, attn_parts, attn_w)]
        else:
            y_prompt, y_sample = (
                _mlp(x, norm2_g[i], w1, w2, attn_parts[k:k + 1], attn_w, final_g, r0, nr)
                for k, (r0, nr) in enumerate(((0, n_prompt), (n_prompt, n_sample))))

    return (y_prompt.reshape(pb, ps, d), y_sample.reshape(sb, ss, d))
```

```python
import functools
import math

import jax
import jax.numpy as jnp
from jax import lax
from jax.experimental import pallas as pl
from jax.experimental.pallas import tpu as pltpu

D_MODEL = 1024
DEPTH = 4
POOL_WIDTH = D_MODEL // 2
N_POOL_GROUPS = 4
POOL_GROUP_DIM = POOL_WIDTH // N_POOL_GROUPS
POOL_WINDOWS = (2, 4, 8, 16)
CONV_WIDTH_CH = D_MODEL // 2
MIX_IN_COLS = POOL_WIDTH + 3 * CONV_WIDTH_CH
N_HEADS = 8
HEAD_DIM = D_MODEL // (2 * N_HEADS)
V_DIM = 2 * HEAD_DIM
ATTN_SCALE = HEAD_DIM ** -0.5
D_FF = 4 * D_MODEL
NORM_EPS = 1e-6
SUBLN_EPS = 1e-5

F32 = jnp.float32
BF16 = jnp.bfloat16

TOKEN_TILE = 512
HALO = 16
FF_CHUNK = 1024
VMEM_LIMIT_BYTES = 48 * 1024 * 1024
DENOM_ROWS = 16
V_ROWS = V_DIM + DENOM_ROWS
PIPE_UNROLL = 8
EXP_ZERO_BELOW = 106.0
NORM_SLACK = 1.01
POS_LO_MASK = 255
POS_HI_MASK = TOKEN_TILE - 1 - POS_LO_MASK


def _rms(x, g, eps):
    return x * lax.rsqrt(jnp.mean(x * x, axis=-1, keepdims=True) + eps) * g


def _dot(a, b):
    return jnp.dot(a, b, preferred_element_type=F32)


def _const_spec(shape):
    zeros = (0,) * len(shape)
    return pl.BlockSpec(shape, lambda *_: zeros)


def _part_starts(part_rows, rows):
    starts, s = [], 0
    for r in part_rows:
        starts.append(s // rows)
        s += r
    return starts


def _part_specs(parts, rows, block_of):
    specs = []
    for p, s0 in zip(parts, _part_starts([p.shape[0] for p in parts], rows)):
        nb = p.shape[0] // rows
        specs.append(pl.BlockSpec(
            (rows, p.shape[1]), lambda i, s0=s0, nb=nb: (jnp.clip(block_of(i) - s0, 0, nb - 1), 0)))
    return specs


def _pick_part(refs, block, part_rows, rows):
    value = refs[0][...]
    for ref, s0 in zip(refs[1:], _part_starts(part_rows, rows)[1:]):
        value = jnp.where(block >= s0, ref[...], value)
    return value


def _params(n_axes):
    return pltpu.CompilerParams(
        dimension_semantics=("arbitrary",) * n_axes,
        vmem_limit_bytes=VMEM_LIMIT_BYTES,
    )


def _mixer_kernel(*refs, bounds, part_rows):
    n = len(part_rows)
    xp_refs, x_refs, xn_refs = refs[:n], refs[n:2 * n], refs[2 * n:3 * n]
    g_ref, win_ref, pw_ref, ps_ref, cw_ref, wout_ref, o_ref, u_ref, z_ref = refs[3 * n:]
    tm = x_refs[0].shape[0]
    i = pl.program_id(0)
    per = tm // HALO
    last = sum(part_rows) // HALO - 1
    start = i * tm
    seq_start = jnp.int32(bounds[0])
    seq_end = jnp.int32(bounds[-1])
    for b in bounds[1:-1]:
        seq_start = jnp.where(start >= b, b, seq_start)
    for b in reversed(bounds[1:-1]):
        seq_end = jnp.where(start < b, b, seq_end)

    x = _pick_part(x_refs, i, part_rows, tm)
    xe = jnp.concatenate([
        _pick_part(xp_refs, jnp.maximum(i * per - 1, 0), part_rows, HALO),
        x,
        _pick_part(xn_refs, jnp.minimum((i + 1) * per, last), part_rows, HALO),
    ], axis=0)
    pos = start - HALO + lax.broadcasted_iota(jnp.int32, (tm + 2 * HALO, 1), 0)
    valid = (pos >= seq_start) & (pos < seq_end)
    he = _rms(xe, g_ref[...], NORM_EPS).astype(BF16)
    proj = _dot(he, win_ref[...])
    c0 = POOL_WIDTH
    u_ref[...] = jnp.where(valid, proj[:, :c0], 0.0)
    z_ref[...] = jnp.where(valid, proj[:, c0 + 2 * CONV_WIDTH_CH:] * proj[:, c0:c0 + CONV_WIDTH_CH], 0.0)

    rel = start - seq_start + lax.broadcasted_iota(jnp.int32, (tm, 1), 0)
    seq_len = seq_end - seq_start
    ys = []
    for g, w in enumerate(POOL_WINDOWS):
        cols = slice(g * POOL_GROUP_DIM, (g + 1) * POOL_GROUP_DIM)
        tot = u_ref[HALO - w // 2:HALO - w // 2 + tm, cols]
        for o in range(-w // 2 + 1, w // 2):
            tot = tot + u_ref[HALO + o:HALO + o + tm, cols]
        cnt = (jnp.minimum(rel + w // 2, seq_len) - jnp.maximum(rel - w // 2, 0)).astype(F32)
        d = tot / cnt - u_ref[HALO:HALO + tm, cols]
        ys.append(_dot(d.astype(BF16), pw_ref[g]))
    a_out = jnp.concatenate(ys, axis=-1) * ps_ref[...]

    conv = (cw_ref[0:1, :] * z_ref[HALO - 1:HALO - 1 + tm, :]
            + cw_ref[1:2, :] * z_ref[HALO:HALO + tm, :]
            + cw_ref[2:3, :] * z_ref[HALO + 1:HALO + 1 + tm, :])
    b_out = proj[HALO:HALO + tm, c0 + CONV_WIDTH_CH:c0 + 2 * CONV_WIDTH_CH] * conv
    mixed = jnp.concatenate([a_out, b_out], axis=-1).astype(BF16)
    o_ref[...] = x + _dot(mixed, wout_ref[...])


def _mixer(x_parts, g, win, pw, ps, cw, wout, bounds):
    d = x_parts[0].shape[1]
    part_rows = tuple(p.shape[0] for p in x_parts)
    t = sum(part_rows)
    tm = TOKEN_TILE
    per = tm // HALO
    last = t // HALO - 1
    return pl.pallas_call(
        functools.partial(_mixer_kernel, bounds=bounds, part_rows=part_rows),
        grid=(t // tm,),
        in_specs=[
            *_part_specs(x_parts, HALO, lambda i: jnp.maximum(i * per - 1, 0)),
            *_part_specs(x_parts, tm, lambda i: i),
            *_part_specs(x_parts, HALO, lambda i: jnp.minimum((i + 1) * per, last)),
            _const_spec((1, d)),
            _const_spec(win.shape),
            _const_spec(pw.shape),
            _const_spec((1, POOL_WIDTH)),
            _const_spec(cw.shape),
            _const_spec(wout.shape),
        ],
        out_specs=pl.BlockSpec((tm, d), lambda i: (i, 0)),
        out_shape=jax.ShapeDtypeStruct((t, d), F32),
        scratch_shapes=[
            pltpu.VMEM((tm + 2 * HALO, POOL_WIDTH), F32),
            pltpu.VMEM((tm + 2 * HALO, CONV_WIDTH_CH), F32),
        ],
        compiler_params=_params(1),
        name="even_mixer",
    )(*x_parts, *x_parts, *x_parts, g.reshape(1, d), win, pw, ps.reshape(1, POOL_WIDTH), cw, wout)


def _mlp_kernel(*refs, attn_rows, has_final):
    refs = list(refs)
    x_ref = refs.pop(0)
    if attn_rows:
        a_refs = [refs.pop(0) for _ in attn_rows]
        wo_ref = refs.pop(0)
    g_ref, w1_ref, w2_ref = refs[:3]
    refs = refs[3:]
    if has_final:
        fg_ref = refs.pop(0)
    o_ref = refs.pop(0)

    x = x_ref[...]
    if attn_rows:
        attn = _pick_part(a_refs, pl.program_id(0), attn_rows, x_ref.shape[0])
        x = x + _dot(attn, wo_ref[...])
    h = _rms(x, g_ref[...], NORM_EPS).astype(BF16)
    acc = x
    for c in range(D_FF // FF_CHUNK):
        cols = slice(c * FF_CHUNK, (c + 1) * FF_CHUNK)
        a = jnp.maximum(_dot(h, w1_ref[:, cols]), 0.0)
        acc = acc + _dot((a * a).astype(BF16), w2_ref[cols, :])
    if has_final:
        acc = _rms(acc, fg_ref[...], NORM_EPS)
    o_ref[...] = acc


def _mlp(x, g, w1, w2, attn_parts=(), attn_w=None, final_g=None, row_start=0, rows=None):
    d = x.shape[1]
    rows = x.shape[0] if rows is None else rows
    tm = TOKEN_TILE
    tile0 = row_start // tm
    row_spec = pl.BlockSpec((tm, d), lambda i: (i, 0))
    args = [x]
    in_specs = [pl.BlockSpec((tm, d), lambda i: (tile0 + i, 0))]
    if attn_parts:
        args += [*attn_parts, attn_w]
        in_specs += [*_part_specs(attn_parts, tm, lambda i: i), _const_spec(attn_w.shape)]
    args += [g.reshape(1, d), w1, w2]
    in_specs += [_const_spec((1, d)), _const_spec(w1.shape), _const_spec(w2.shape)]
    if final_g is not None:
        args.append(final_g.reshape(1, d))
        in_specs.append(_const_spec((1, d)))
    return pl.pallas_call(
        functools.partial(_mlp_kernel, attn_rows=tuple(p.shape[0] for p in attn_parts),
                          has_final=final_g is not None),
        grid=(rows // tm,),
        in_specs=in_specs,
        out_specs=row_spec,
        out_shape=jax.ShapeDtypeStruct((rows, d), F32),
        compiler_params=_params(1),
        name="sq_relu_mlp",
    )(*args)


def _qkv_kernel(x_ref, g_ref, w_ref, qt_ref, kk_ref, vt_ref, nrm_ref):
    tm = x_ref.shape[0]
    h = _rms(x_ref[...], g_ref[...], NORM_EPS).astype(BF16)
    qkv = _dot(h, w_ref[...])
    q_all = (qkv[:, :D_MODEL] * ATTN_SCALE).astype(BF16).astype(F32)
    k_all = qkv[:, D_MODEL:2 * D_MODEL].astype(BF16).astype(F32)

    sel = (lax.shift_right_logical(lax.broadcasted_iota(jnp.int32, (D_MODEL, V_DIM), 0),
                                   HEAD_DIM.bit_length() - 1)
           == lax.broadcasted_iota(jnp.int32, (D_MODEL, V_DIM), 1)).astype(BF16)

    def max_sq_norm(a):
        sq = a * a
        hi = sq.astype(BF16)
        lo = (sq - hi.astype(F32)).astype(BF16)
        return jnp.max(_dot(hi, sel) + _dot(lo, sel), axis=0, keepdims=True)

    nrm_ref[0] = jnp.concatenate(
        [max_sq_norm(q_all), max_sq_norm(k_all), jnp.zeros((6, V_DIM), F32)], axis=0)

    q_row = lax.broadcasted_iota(jnp.int32, (HEAD_DIM, tm), 0)
    q_tok = lax.broadcasted_iota(jnp.int32, (HEAD_DIM, tm), 1)
    q_lo = (q_tok & POS_LO_MASK).astype(F32)
    q_hi = (q_tok & POS_HI_MASK).astype(F32)
    k_lane = lax.broadcasted_iota(jnp.int32, (tm, V_DIM), 1)
    k_tok = lax.broadcasted_iota(jnp.int32, (tm, V_DIM), 0)
    k_lo = (k_tok & POS_LO_MASK).astype(F32)
    k_hi = (k_tok & POS_HI_MASK).astype(F32)

    denom_rows = (lax.broadcasted_iota(jnp.int32, (DENOM_ROWS, tm), 0) == 0).astype(F32)

    def k_extra(base, slope):
        return jnp.where(
            (k_lane == base) | (k_lane == base + 1), 1.0,
            jnp.where(k_lane == base + 2, -slope * k_lo,
                      jnp.where(k_lane == base + 3, -slope * k_hi, 0.0)))

    for hd in range(N_HEADS):
        slope = 2.0 ** (-8.0 * (hd + 1) / N_HEADS)
        cols = slice(hd * V_DIM, (hd + 1) * V_DIM)
        q_t = q_all[:, cols].T
        top, bot = q_t[:HEAD_DIM], q_t[HEAD_DIM:]
        q_extra = jnp.where(q_row == 0, slope * q_lo,
                            jnp.where(q_row == 1, slope * q_hi,
                                      jnp.where(q_row < 4, 1.0, 0.0)))
        qt_ref[hd, 0] = jnp.concatenate([top, q_extra], axis=0).astype(BF16)
        qt_ref[hd, 1] = jnp.concatenate([top, -q_extra], axis=0).astype(BF16)
        qt_ref[hd, 2] = jnp.concatenate([q_extra, bot], axis=0).astype(BF16)
        qt_ref[hd, 3] = jnp.concatenate([-q_extra, bot], axis=0).astype(BF16)

        kh = k_all[:, cols]
        kk_ref[hd, 0, 0] = jnp.where(k_lane < HEAD_DIM, kh, k_extra(HEAD_DIM, slope)).astype(BF16)
        kk_ref[hd, 1, 0] = jnp.where(k_lane >= HEAD_DIM, kh, k_extra(0, slope)).astype(BF16)

        vh = qkv[:, 2 * D_MODEL + hd * V_DIM:2 * D_MODEL + (hd + 1) * V_DIM]
        vt_ref[hd, 0] = jnp.concatenate([vh.T, denom_rows], axis=0).astype(BF16)


def _qkv(x, g, w):
    t, d = x.shape
    tm = TOKEN_TILE
    nc = t // tm
    return pl.pallas_call(
        _qkv_kernel,
        grid=(nc,),
        in_specs=[
            pl.BlockSpec((tm, d), lambda i: (i, 0)),
            _const_spec((1, d)),
            _const_spec(w.shape),
        ],
        out_specs=[
            pl.BlockSpec((N_HEADS, 4, V_DIM, tm), lambda i: (0, 0, 0, i)),
            pl.BlockSpec((N_HEADS, 2, 1, tm, V_DIM), lambda i: (0, 0, i, 0, 0)),
            pl.BlockSpec((N_HEADS, 1, V_ROWS, tm), lambda i: (0, i, 0, 0)),
            pl.BlockSpec((1, 8, V_DIM), lambda i: (i, 0, 0)),
        ],
        out_shape=[
            jax.ShapeDtypeStruct((N_HEADS, 4, V_DIM, t), BF16),
            jax.ShapeDtypeStruct((N_HEADS, 2, nc, tm, V_DIM), BF16),
            jax.ShapeDtypeStruct((N_HEADS, nc, V_ROWS, tm), BF16),
            jax.ShapeDtypeStruct((nc, 8, V_DIM), F32),
        ],
        compiler_params=_params(1),
        name="attn_qkv",
    )(x, g.reshape(1, d), w)


def _flash_kernel(slope_ref, qn_ref, kn_ref, lq1_ref, lk1_ref, lq2_ref, lk2_ref, sg_ref,
                  qt_ref, kk_ref, vt_ref, o_ref, s_ref, mc_ref, m_ref, acc_ref,
                  *, lambda_init, blk0):
    n_chunks, tkc = kk_ref.shape[1], kk_ref.shape[2]
    tq = qt_ref.shape[-1]
    hd = pl.program_id(1)
    slope, inv_slope = slope_ref[0, hd], slope_ref[1, hd]
    cq = pl.program_id(2)
    chunk0 = blk0 + pl.program_id(0) * n_chunks

    def scores(c, sign, mp):
        return _dot(kk_ref[mp, c], qt_ref[2 * mp + sign])

    def absorb(c, mp, s, mc):
        shift = -slope * (tkc * jnp.abs(cq - c)).astype(F32)
        m_old = m_ref[mp]
        m_new = jnp.maximum(m_old, mc + shift)
        p = jnp.exp(s - (m_new - shift)).astype(BF16)
        acc_ref[mp] = jnp.exp(m_old - m_new) * acc_ref[mp] + _dot(vt_ref[c], p)
        m_ref[mp] = m_new

    def issue_chunk(c, sign, mp):
        s = scores(c, sign, mp)
        s_ref[mp] = s
        mc_ref[mp] = jnp.max(s, axis=0, keepdims=True)

    jj = lax.broadcasted_iota(jnp.int32, (tkc, tq), 0)
    ii = lax.broadcasted_iota(jnp.int32, (tkc, tq), 1)
    fixup = (-2.0 * slope) * jnp.maximum(ii - jj, 0).astype(F32)
    diag = [scores(cq, 0, mp) + fixup for mp in range(2)]
    issue_chunk(jnp.where(cq > 0, cq - 1, jnp.minimum(cq + 1, n_chunks - 1)),
                (cq > 0).astype(jnp.int32), 0)
    for mp in range(2):
        m_first = jnp.max(diag[mp], axis=0, keepdims=True)
        m_ref[mp] = m_first
        acc_ref[mp] = _dot(vt_ref[cq], jnp.exp(diag[mp] - m_first).astype(BF16))

    gap = (NORM_SLACK * qn_ref[hd, chunk0 + cq] * kn_ref[hd, pl.program_id(0)]
           + EXP_ZERO_BELOW - jnp.min(m_ref[...]))
    reach = (gap * inv_slope - 1.0) * (1.0 / tkc)
    n_dist = jnp.where(reach >= 1.0,
                       jnp.minimum(reach, float(n_chunks)).astype(jnp.int32) + 1, 1)
    n_left = jnp.minimum(n_dist, cq)
    total = n_left + jnp.minimum(n_dist, n_chunks - 1 - cq)

    def item(t):
        left = t < n_left
        c = jnp.where(left, cq - 1 - t, cq + 1 + t - n_left)
        return jnp.clip(c, 0, n_chunks - 1), left.astype(jnp.int32)

    def trip(t):
        c, sign = item(t)
        issue_chunk(c, sign, 1)
        absorb(c, 0, s_ref[0], mc_ref[0])
        issue_chunk(*item(t + 1), 0)
        absorb(c, 1, s_ref[1], mc_ref[1])

    def trips(t0, count):
        for r in range(count):
            trip(t0 + r)

    assert PIPE_UNROLL & (PIPE_UNROLL - 1) == 0
    rem = total & (PIPE_UNROLL - 1)
    done, size = 0, 1
    while size < PIPE_UNROLL:
        pl.when((total & size) == size)(functools.partial(trips, done, size))
        done = done + (total & size)
        size *= 2

    def group(u, carry):
        trips(rem + PIPE_UNROLL * u, PIPE_UNROLL)
        return carry

    lax.fori_loop(0, total // PIPE_UNROLL, group, 0)

    lam = (jnp.exp(jnp.sum(lq1_ref[...] * lk1_ref[...], keepdims=True))
           - jnp.exp(jnp.sum(lq2_ref[...] * lk2_ref[...], keepdims=True)) + lambda_init)
    o_t = (acc_ref[0, :V_DIM] / acc_ref[0, V_DIM:V_DIM + 1]
           - lam * (acc_ref[1, :V_DIM] / acc_ref[1, V_DIM:V_DIM + 1]))
    o = _rms(o_t.T, sg_ref[...], SUBLN_EPS) * (1.0 - lambda_init)
    o_ref[...] = o.astype(o_ref.dtype)


def _flash(slopes, qn, kn, lam_vecs, subln_g, qt, kk, vt, *, tok_start, n_seq, seq_len,
           lambda_init):
    tq = TOKEN_TILE
    nq = seq_len // tq
    blk0 = tok_start // tq
    seq0 = tok_start // seq_len
    smem = pl.BlockSpec(memory_space=pltpu.SMEM)
    vec = _const_spec((1, HEAD_DIM))
    return pl.pallas_call(
        functools.partial(_flash_kernel, lambda_init=lambda_init, blk0=blk0),
        grid=(n_seq, N_HEADS, nq),
        in_specs=[
            smem, smem, smem, vec, vec, vec, vec, _const_spec((1, V_DIM)),
            pl.BlockSpec((None, 4, V_DIM, tq), lambda b, h, i: (h, 0, 0, blk0 + b * nq + i)),
            pl.BlockSpec((None, 2, nq, tq, V_DIM), lambda b, h, i: (h, 0, seq0 + b, 0, 0)),
            pl.BlockSpec((None, nq, V_ROWS, tq), lambda b, h, i: (h, seq0 + b, 0, 0)),
        ],
        out_specs=pl.BlockSpec((tq, V_DIM), lambda b, h, i: (b * nq + i, h)),
        out_shape=jax.ShapeDtypeStruct((n_seq * seq_len, N_HEADS * V_DIM), BF16),
        scratch_shapes=[
            pltpu.VMEM((2, tq, tq), F32),
            pltpu.VMEM((2, 1, tq), F32),
            pltpu.VMEM((2, 1, tq), F32),
            pltpu.VMEM((2, V_ROWS, tq), F32),
        ],
        compiler_params=_params(3),
        name="diff_flash",
    )(slopes, qn, kn[:, blk0:blk0 + n_seq * nq].reshape(N_HEADS, n_seq, nq).max(axis=-1),
      *lam_vecs, subln_g.reshape(1, V_DIM), qt, kk, vt)


def kernel(x_prompt, x_sample, norm1_g, norm2_g, final_g, mix_in_w, pool_w, pool_scale, conv_w,
           mix_out_w, attn_qkv_w, attn_out_w, lambda_q1, lambda_k1, lambda_q2, lambda_k2, subln_g,
           mlp_w1, mlp_w2):
    pb, ps, d = x_prompt.shape
    sb, ss, _ = x_sample.shape
    n_prompt = pb * ps
    n_sample = sb * ss
    x_parts = [x_prompt.reshape(n_prompt, d), x_sample.reshape(n_sample, d)]
    bounds = tuple(ps * b for b in range(pb)) + tuple(n_prompt + ss * b for b in range(sb + 1))
    slopes = jnp.exp2(-8.0 * (jnp.arange(N_HEADS, dtype=F32) + 1.0) / N_HEADS)
    slopes = jnp.stack([slopes, 1.0 / slopes])

    for i in range(DEPTH):
        j = i // 2
        w1 = mlp_w1[i].astype(BF16)
        w2 = mlp_w2[i].astype(BF16)
        attn_parts, attn_w = (), None
        if i % 2 == 0:
            x = _mixer(x_parts, norm1_g[i], mix_in_w[j].astype(BF16), pool_w[j].astype(BF16),
                       pool_scale[j], conv_w[j], mix_out_w[j].astype(BF16), bounds)
        else:
            x = x_parts[0]
            lambda_init = 0.8 - 0.6 * math.exp(-0.3 * i)
            qt, kk, vt, nrm = _qkv(x, norm1_g[i], attn_qkv_w[j].astype(BF16))
            qn, kn = (jnp.sqrt(nrm[:, r, :2 * N_HEADS].reshape(-1, N_HEADS, 2).max(axis=-1)).T
                      for r in range(2))
            lam_vecs = [v[j].reshape(1, HEAD_DIM) for v in (lambda_q1, lambda_k1, lambda_q2, lambda_k2)]
            flash = functools.partial(_flash, slopes, qn, kn, lam_vecs, subln_g[j], qt, kk, vt,
                                      lambda_init=lambda_init)
            attn_parts = (flash(tok_start=0, n_seq=pb, seq_len=ps),
                          flash(tok_start=n_prompt, n_seq=sb, seq_len=ss))
            attn_w = attn_out_w[j].astype(BF16)
        if i < DEPTH - 1:
            x_parts = [_mlp(x, norm2_g[i], w1, w2, attn_parts, attn_w)]
        else:
            y_prompt, y_sample = (
                _mlp(x, norm2_g[i], w1, w2, attn_parts[k:k + 1], attn_w, final_g, r0, nr)
                for k, (r0, nr) in enumerate(((0, n_prompt), (n_prompt, n_sample))))

    return (y_prompt.reshape(pb, ps, d), y_sample.reshape(sb, ss, d))
```

```python
import functools
import math

import jax
import jax.numpy as jnp
from jax import lax
from jax.experimental import pallas as pl
from jax.experimental.pallas import tpu as pltpu

D_MODEL = 1024
DEPTH = 4
POOL_WIDTH = D_MODEL // 2
N_POOL_GROUPS = 4
POOL_GROUP_DIM = POOL_WIDTH // N_POOL_GROUPS
POOL_WINDOWS = (2, 4, 8, 16)
CONV_WIDTH_CH = D_MODEL // 2
MIX_IN_COLS = POOL_WIDTH + 3 * CONV_WIDTH_CH
N_HEADS = 8
HEAD_DIM = D_MODEL // (2 * N_HEADS)
V_DIM = 2 * HEAD_DIM
ATTN_SCALE = HEAD_DIM ** -0.5
D_FF = 4 * D_MODEL
NORM_EPS = 1e-6
SUBLN_EPS = 1e-5

F32 = jnp.float32
BF16 = jnp.bfloat16

TOKEN_TILE = 512
HALO = 16
FF_CHUNK = 1024
VMEM_LIMIT_BYTES = 48 * 1024 * 1024
DENOM_ROWS = 16
V_ROWS = V_DIM + DENOM_ROWS
PIPE_UNROLL = 8
EXP_ZERO_BELOW = 106.0
NORM_SLACK = 1.01
POS_LO_MASK = 255
POS_HI_MASK = TOKEN_TILE - 1 - POS_LO_MASK


def _rms(x, g, eps):
    return x * lax.rsqrt(jnp.mean(x * x, axis=-1, keepdims=True) + eps) * g


def _dot(a, b):
    return jnp.dot(a, b, preferred_element_type=F32)


def _const_spec(shape):
    zeros = (0,) * len(shape)
    return pl.BlockSpec(shape, lambda *_: zeros)


def _part_starts(part_rows, rows):
    starts, s = [], 0
    for r in part_rows:
        starts.append(s // rows)
        s += r
    return starts


def _part_specs(parts, rows, block_of):
    specs = []
    for p, s0 in zip(parts, _part_starts([p.shape[0] for p in parts], rows)):
        nb = p.shape[0] // rows
        specs.append(pl.BlockSpec(
            (rows, p.shape[1]), lambda i, s0=s0, nb=nb: (jnp.clip(block_of(i) - s0, 0, nb - 1), 0)))
    return specs


def _pick_part(refs, block, part_rows, rows):
    value = refs[0][...]
    for ref, s0 in zip(refs[1:], _part_starts(part_rows, rows)[1:]):
        value = jnp.where(block >= s0, ref[...], value)
    return value


def _params(n_axes):
    return pltpu.CompilerParams(
        dimension_semantics=("arbitrary",) * n_axes,
        vmem_limit_bytes=VMEM_LIMIT_BYTES,
    )


def _mixer_kernel(*refs, bounds, part_rows):
    n = len(part_rows)
    xp_refs, x_refs, xn_refs = refs[:n], refs[n:2 * n], refs[2 * n:3 * n]
    g_ref, win_ref, pw_ref, ps_ref, cw_ref, wout_ref, o_ref, u_ref, z_ref = refs[3 * n:]
    tm = x_refs[0].shape[0]
    i = pl.program_id(0)
    per = tm // HALO
    last = sum(part_rows) // HALO - 1
    start = i * tm
    seq_start = jnp.int32(bounds[0])
    seq_end = jnp.int32(bounds[-1])
    for b in bounds[1:-1]:
        seq_start = jnp.where(start >= b, b, seq_start)
    for b in reversed(bounds[1:-1]):
        seq_end = jnp.where(start < b, b, seq_end)

    x = _pick_part(x_refs, i, part_rows, tm)
    xe = jnp.concatenate([
        _pick_part(xp_refs, jnp.maximum(i * per - 1, 0), part_rows, HALO),
        x,
        _pick_part(xn_refs, jnp.minimum((i + 1) * per, last), part_rows, HALO),
    ], axis=0)
    pos = start - HALO + lax.broadcasted_iota(jnp.int32, (tm + 2 * HALO, 1), 0)
    valid = (pos >= seq_start) & (pos < seq_end)
    he = _rms(xe, g_ref[...], NORM_EPS).astype(BF16)
    proj = _dot(he, win_ref[...])
    c0 = POOL_WIDTH
    u_ref[...] = jnp.where(valid, proj[:, :c0], 0.0)
    z_ref[...] = jnp.where(valid, proj[:, c0 + 2 * CONV_WIDTH_CH:] * proj[:, c0:c0 + CONV_WIDTH_CH], 0.0)

    rel = start - seq_start + lax.broadcasted_iota(jnp.int32, (tm, 1), 0)
    seq_len = seq_end - seq_start
    ys = []
    for g, w in enumerate(POOL_WINDOWS):
        cols = slice(g * POOL_GROUP_DIM, (g + 1) * POOL_GROUP_DIM)
        tot = u_ref[HALO - w // 2:HALO - w // 2 + tm, cols]
        for o in range(-w // 2 + 1, w // 2):
            tot = tot + u_ref[HALO + o:HALO + o + tm, cols]
        cnt = (jnp.minimum(rel + w // 2, seq_len) - jnp.maximum(rel - w // 2, 0)).astype(F32)
        d = tot / cnt - u_ref[HALO:HALO + tm, cols]
        ys.append(_dot(d.astype(BF16), pw_ref[g]))
    a_out = jnp.concatenate(ys, axis=-1) * ps_ref[...]

    conv = (cw_ref[0:1, :] * z_ref[HALO - 1:HALO - 1 + tm, :]
            + cw_ref[1:2, :] * z_ref[HALO:HALO + tm, :]
            + cw_ref[2:3, :] * z_ref[HALO + 1:HALO + 1 + tm, :])
    b_out = proj[HALO:HALO + tm, c0 + CONV_WIDTH_CH:c0 + 2 * CONV_WIDTH_CH] * conv
    mixed = jnp.concatenate([a_out, b_out], axis=-1).astype(BF16)
    o_ref[...] = x + _dot(mixed, wout_ref[...])


def _mixer(x_parts, g, win, pw, ps, cw, wout, bounds):
    d = x_parts[0].shape[1]
    part_rows = tuple(p.shape[0] for p in x_parts)
    t = sum(part_rows)
    tm = TOKEN_TILE
    per = tm // HALO
    last = t // HALO - 1
    return pl.pallas_call(
        functools.partial(_mixer_kernel, bounds=bounds, part_rows=part_rows),
        grid=(t // tm,),
        in_specs=[
            *_part_specs(x_parts, HALO, lambda i: jnp.maximum(i * per - 1, 0)),
            *_part_specs(x_parts, tm, lambda i: i),
            *_part_specs(x_parts, HALO, lambda i: jnp.minimum((i + 1) * per, last)),
            _const_spec((1, d)),
            _const_spec(win.shape),
            _const_spec(pw.shape),
            _const_spec((1, POOL_WIDTH)),
            _const_spec(cw.shape),
            _const_spec(wout.shape),
        ],
        out_specs=pl.BlockSpec((tm, d), lambda i: (i, 0)),
        out_shape=jax.ShapeDtypeStruct((t, d), F32),
        scratch_shapes=[
            pltpu.VMEM((tm + 2 * HALO, POOL_WIDTH), F32),
            pltpu.VMEM((tm + 2 * HALO, CONV_WIDTH_CH), F32),
        ],
        compiler_params=_params(1),
        name="even_mixer",
    )(*x_parts, *x_parts, *x_parts, g.reshape(1, d), win, pw, ps.reshape(1, POOL_WIDTH), cw, wout)


def _mlp_kernel(*refs, attn_rows, has_final):
    refs = list(refs)
    x_ref = refs.pop(0)
    if attn_rows:
        a_refs = [refs.pop(0) for _ in attn_rows]
        wo_ref = refs.pop(0)
    g_ref, w1_ref, w2_ref = refs[:3]
    refs = refs[3:]
    if has_final:
        fg_ref = refs.pop(0)
    o_ref = refs.pop(0)

    x = x_ref[...]
    if attn_rows:
        attn = _pick_part(a_refs, pl.program_id(0), attn_rows, x_ref.shape[0])
        x = x + _dot(attn, wo_ref[...])
    h = _rms(x, g_ref[...], NORM_EPS).astype(BF16)
    acc = x
    for c in range(D_FF // FF_CHUNK):
        cols = slice(c * FF_CHUNK, (c + 1) * FF_CHUNK)
        a = jnp.maximum(_dot(h, w1_ref[:, cols]), 0.0)
        acc = acc + _dot((a * a).astype(BF16), w2_ref[cols, :])
    if has_final:
        acc = _rms(acc, fg_ref[...], NORM_EPS)
    o_ref[...] = acc


def _mlp(x, g, w1, w2, attn_parts=(), attn_w=None, final_g=None, row_start=0, rows=None):
    d = x.shape[1]
    rows = x.shape[0] if rows is None else rows
    tm = TOKEN_TILE
    tile0 = row_start // tm
    row_spec = pl.BlockSpec((tm, d), lambda i: (i, 0))
    args = [x]
    in_specs = [pl.BlockSpec((tm, d), lambda i: (tile0 + i, 0))]
    if attn_parts:
        args += [*attn_parts, attn_w]
        in_specs += [*_part_specs(attn_parts, tm, lambda i: i), _const_spec(attn_w.shape)]
    args += [g.reshape(1, d), w1, w2]
    in_specs += [_const_spec((1, d)), _const_spec(w1.shape), _const_spec(w2.shape)]
    if final_g is not None:
        args.append(final_g.reshape(1, d))
        in_specs.append(_const_spec((1, d)))
    return pl.pallas_call(
        functools.partial(_mlp_kernel, attn_rows=tuple(p.shape[0] for p in attn_parts),
                          has_final=final_g is not None),
        grid=(rows // tm,),
        in_specs=in_specs,
        out_specs=row_spec,
        out_shape=jax.ShapeDtypeStruct((rows, d), F32),
        compiler_params=_params(1),
        name="sq_relu_mlp",
    )(*args)


def _qkv_kernel(x_ref, g_ref, w_ref, qt_ref, kk_ref, vt_ref, nrm_ref):
    tm = x_ref.shape[0]
    h = _rms(x_ref[...], g_ref[...], NORM_EPS).astype(BF16)
    qkv = _dot(h, w_ref[...])
    q_all = (qkv[:, :D_MODEL] * ATTN_SCALE).astype(BF16).astype(F32)
    k_all = qkv[:, D_MODEL:2 * D_MODEL].astype(BF16).astype(F32)

    sel = (lax.shift_right_logical(lax.broadcasted_iota(jnp.int32, (D_MODEL, V_DIM), 0),
                                   HEAD_DIM.bit_length() - 1)
           == lax.broadcasted_iota(jnp.int32, (D_MODEL, V_DIM), 1)).astype(BF16)

    def max_sq_norm(a):
        return jnp.max(_dot((a * a).astype(BF16), sel), axis=0, keepdims=True)

    nrm_ref[0] = jnp.concatenate(
        [max_sq_norm(q_all), max_sq_norm(k_all), jnp.zeros((6, V_DIM), F32)], axis=0)

    q_row = lax.broadcasted_iota(jnp.int32, (HEAD_DIM, tm), 0)
    q_tok = lax.broadcasted_iota(jnp.int32, (HEAD_DIM, tm), 1)
    q_lo = (q_tok & POS_LO_MASK).astype(F32)
    q_hi = (q_tok & POS_HI_MASK).astype(F32)
    k_lane = lax.broadcasted_iota(jnp.int32, (tm, V_DIM), 1)
    k_tok = lax.broadcasted_iota(jnp.int32, (tm, V_DIM), 0)
    k_lo = (k_tok & POS_LO_MASK).astype(F32)
    k_hi = (k_tok & POS_HI_MASK).astype(F32)

    denom_rows = (lax.broadcasted_iota(jnp.int32, (DENOM_ROWS, tm), 0) == 0).astype(F32)

    def k_extra(base, slope):
        return jnp.where(
            (k_lane == base) | (k_lane == base + 1), 1.0,
            jnp.where(k_lane == base + 2, -slope * k_lo,
                      jnp.where(k_lane == base + 3, -slope * k_hi, 0.0)))

    for hd in range(N_HEADS):
        slope = 2.0 ** (-8.0 * (hd + 1) / N_HEADS)
        cols = slice(hd * V_DIM, (hd + 1) * V_DIM)
        q_t = q_all[:, cols].T
        top, bot = q_t[:HEAD_DIM], q_t[HEAD_DIM:]
        q_extra = jnp.where(q_row == 0, slope * q_lo,
                            jnp.where(q_row == 1, slope * q_hi,
                                      jnp.where(q_row < 4, 1.0, 0.0)))
        qt_ref[hd, 0] = jnp.concatenate([top, q_extra], axis=0).astype(BF16)
        qt_ref[hd, 1] = jnp.concatenate([top, -q_extra], axis=0).astype(BF16)
        qt_ref[hd, 2] = jnp.concatenate([q_extra, bot], axis=0).astype(BF16)
        qt_ref[hd, 3] = jnp.concatenate([-q_extra, bot], axis=0).astype(BF16)

        kh = k_all[:, cols]
        kk_ref[hd, 0, 0] = jnp.where(k_lane < HEAD_DIM, kh, k_extra(HEAD_DIM, slope)).astype(BF16)
        kk_ref[hd, 1, 0] = jnp.where(k_lane >= HEAD_DIM, kh, k_extra(0, slope)).astype(BF16)

        vh = qkv[:, 2 * D_MODEL + hd * V_DIM:2 * D_MODEL + (hd + 1) * V_DIM]
        vt_ref[hd, 0] = jnp.concatenate([vh.T, denom_rows], axis=0).astype(BF16)


def _qkv(x, g, w):
    t, d = x.shape
    tm = TOKEN_TILE
    nc = t // tm
    return pl.pallas_call(
        _qkv_kernel,
        grid=(nc,),
        in_specs=[
            pl.BlockSpec((tm, d), lambda i: (i, 0)),
            _const_spec((1, d)),
            _const_spec(w.shape),
        ],
        out_specs=[
            pl.BlockSpec((N_HEADS, 4, V_DIM, tm), lambda i: (0, 0, 0, i)),
            pl.BlockSpec((N_HEADS, 2, 1, tm, V_DIM), lambda i: (0, 0, i, 0, 0)),
            pl.BlockSpec((N_HEADS, 1, V_ROWS, tm), lambda i: (0, i, 0, 0)),
            pl.BlockSpec((1, 8, V_DIM), lambda i: (i, 0, 0)),
        ],
        out_shape=[
            jax.ShapeDtypeStruct((N_HEADS, 4, V_DIM, t), BF16),
            jax.ShapeDtypeStruct((N_HEADS, 2, nc, tm, V_DIM), BF16),
            jax.ShapeDtypeStruct((N_HEADS, nc, V_ROWS, tm), BF16),
            jax.ShapeDtypeStruct((nc, 8, V_DIM), F32),
        ],
        compiler_params=_params(1),
        name="attn_qkv",
    )(x, g.reshape(1, d), w)


def _flash_kernel(slope_ref, qn_ref, kn_ref, lq1_ref, lk1_ref, lq2_ref, lk2_ref, sg_ref,
                  qt_ref, kk_ref, vt_ref, o_ref, s_ref, mc_ref, m_ref, acc_ref,
                  *, lambda_init, blk0):
    n_chunks, tkc = kk_ref.shape[1], kk_ref.shape[2]
    tq = qt_ref.shape[-1]
    hd = pl.program_id(1)
    slope, inv_slope = slope_ref[0, hd], slope_ref[1, hd]
    cq = pl.program_id(2)
    chunk0 = blk0 + pl.program_id(0) * n_chunks

    def scores(c, sign, mp):
        return _dot(kk_ref[mp, c], qt_ref[2 * mp + sign])

    def absorb(c, mp, s, mc):
        shift = -slope * (tkc * jnp.abs(cq - c)).astype(F32)
        m_old = m_ref[mp]
        m_new = jnp.maximum(m_old, mc + shift)
        p = jnp.exp(s - (m_new - shift)).astype(BF16)
        acc_ref[mp] = jnp.exp(m_old - m_new) * acc_ref[mp] + _dot(vt_ref[c], p)
        m_ref[mp] = m_new

    def issue_chunk(c, sign, mp):
        s = scores(c, sign, mp)
        s_ref[mp] = s
        mc_ref[mp] = jnp.max(s, axis=0, keepdims=True)

    jj = lax.broadcasted_iota(jnp.int32, (tkc, tq), 0)
    ii = lax.broadcasted_iota(jnp.int32, (tkc, tq), 1)
    fixup = (-2.0 * slope) * jnp.maximum(ii - jj, 0).astype(F32)
    diag = [scores(cq, 0, mp) + fixup for mp in range(2)]
    issue_chunk(jnp.where(cq > 0, cq - 1, jnp.minimum(cq + 1, n_chunks - 1)),
                (cq > 0).astype(jnp.int32), 0)
    for mp in range(2):
        m_first = jnp.max(diag[mp], axis=0, keepdims=True)
        m_ref[mp] = m_first
        acc_ref[mp] = _dot(vt_ref[cq], jnp.exp(diag[mp] - m_first).astype(BF16))

    gap = (NORM_SLACK * qn_ref[hd, chunk0 + cq] * kn_ref[hd, pl.program_id(0)]
           + EXP_ZERO_BELOW - jnp.min(m_ref[...]))
    reach = (gap * inv_slope - 1.0) * (1.0 / tkc)
    n_dist = jnp.where(reach >= 1.0,
                       jnp.minimum(reach, float(n_chunks)).astype(jnp.int32) + 1, 1)
    n_left = jnp.minimum(n_dist, cq)
    total = n_left + jnp.minimum(n_dist, n_chunks - 1 - cq)

    def item(t):
        left = t < n_left
        c = jnp.where(left, cq - 1 - t, cq + 1 + t - n_left)
        return jnp.clip(c, 0, n_chunks - 1), left.astype(jnp.int32)

    def trip(t):
        c, sign = item(t)
        issue_chunk(c, sign, 1)
        absorb(c, 0, s_ref[0], mc_ref[0])
        issue_chunk(*item(t + 1), 0)
        absorb(c, 1, s_ref[1], mc_ref[1])

    def trips(t0, count):
        for r in range(count):
            trip(t0 + r)

    assert PIPE_UNROLL & (PIPE_UNROLL - 1) == 0
    rem = total & (PIPE_UNROLL - 1)
    done, size = 0, 1
    while size < PIPE_UNROLL:
        pl.when((total & size) == size)(functools.partial(trips, done, size))
        done = done + (total & size)
        size *= 2

    def group(u, carry):
        trips(rem + PIPE_UNROLL * u, PIPE_UNROLL)
        return carry

    lax.fori_loop(0, total // PIPE_UNROLL, group, 0)

    lam = (jnp.exp(jnp.sum(lq1_ref[...] * lk1_ref[...], keepdims=True))
           - jnp.exp(jnp.sum(lq2_ref[...] * lk2_ref[...], keepdims=True)) + lambda_init)
    o_t = (acc_ref[0, :V_DIM] * (1.0 / acc_ref[0, V_DIM:V_DIM + 1])
           - (lam / acc_ref[1, V_DIM:V_DIM + 1]) * acc_ref[1, :V_DIM])
    scale = lax.rsqrt(jnp.mean(o_t * o_t, axis=0, keepdims=True) + SUBLN_EPS) * (1.0 - lambda_init)
    o_ref[...] = (o_t * scale * sg_ref[...]).T.astype(o_ref.dtype)


def _flash(slopes, qn, kn, lam_vecs, subln_g, qt, kk, vt, *, tok_start, n_seq, seq_len,
           lambda_init):
    tq = TOKEN_TILE
    nq = seq_len // tq
    blk0 = tok_start // tq
    seq0 = tok_start // seq_len
    smem = pl.BlockSpec(memory_space=pltpu.SMEM)
    vec = _const_spec((1, HEAD_DIM))
    return pl.pallas_call(
        functools.partial(_flash_kernel, lambda_init=lambda_init, blk0=blk0),
        grid=(n_seq, N_HEADS, nq),
        in_specs=[
            smem, smem, smem, vec, vec, vec, vec, _const_spec((V_DIM, tq)),
            pl.BlockSpec((None, 4, V_DIM, tq), lambda b, h, i: (h, 0, 0, blk0 + b * nq + i)),
            pl.BlockSpec((None, 2, nq, tq, V_DIM), lambda b, h, i: (h, 0, seq0 + b, 0, 0)),
            pl.BlockSpec((None, nq, V_ROWS, tq), lambda b, h, i: (h, seq0 + b, 0, 0)),
        ],
        out_specs=pl.BlockSpec((tq, V_DIM), lambda b, h, i: (b * nq + i, h)),
        out_shape=jax.ShapeDtypeStruct((n_seq * seq_len, N_HEADS * V_DIM), BF16),
        scratch_shapes=[
            pltpu.VMEM((2, tq, tq), F32),
            pltpu.VMEM((2, 1, tq), F32),
            pltpu.VMEM((2, 1, tq), F32),
            pltpu.VMEM((2, V_ROWS, tq), F32),
        ],
        compiler_params=_params(3),
        name="diff_flash",
    )(slopes, qn, kn[:, blk0:blk0 + n_seq * nq].reshape(N_HEADS, n_seq, nq).max(axis=-1),
      *lam_vecs, jnp.broadcast_to(subln_g.reshape(V_DIM, 1), (V_DIM, tq)), qt, kk, vt)


def kernel(x_prompt, x_sample, norm1_g, norm2_g, final_g, mix_in_w, pool_w, pool_scale, conv_w,
           mix_out_w, attn_qkv_w, attn_out_w, lambda_q1, lambda_k1, lambda_q2, lambda_k2, subln_g,
           mlp_w1, mlp_w2):
    pb, ps, d = x_prompt.shape
    sb, ss, _ = x_sample.shape
    n_prompt = pb * ps
    n_sample = sb * ss
    x_parts = [x_prompt.reshape(n_prompt, d), x_sample.reshape(n_sample, d)]
    bounds = tuple(ps * b for b in range(pb)) + tuple(n_prompt + ss * b for b in range(sb + 1))
    slopes = jnp.exp2(-8.0 * (jnp.arange(N_HEADS, dtype=F32) + 1.0) / N_HEADS)
    slopes = jnp.stack([slopes, 1.0 / slopes])

    for i in range(DEPTH):
        j = i // 2
        w1 = mlp_w1[i].astype(BF16)
        w2 = mlp_w2[i].astype(BF16)
        attn_parts, attn_w = (), None
        if i % 2 == 0:
            x = _mixer(x_parts, norm1_g[i], mix_in_w[j].astype(BF16), pool_w[j].astype(BF16),
                       pool_scale[j], conv_w[j], mix_out_w[j].astype(BF16), bounds)
        else:
            x = x_parts[0]
            lambda_init = 0.8 - 0.6 * math.exp(-0.3 * i)
            qt, kk, vt, nrm = _qkv(x, norm1_g[i], attn_qkv_w[j].astype(BF16))
            qn, kn = (jnp.sqrt(nrm[:, r, :2 * N_HEADS].reshape(-1, N_HEADS, 2).max(axis=-1)).T
                      for r in range(2))
            lam_vecs = [v[j].reshape(1, HEAD_DIM) for v in (lambda_q1, lambda_k1, lambda_q2, lambda_k2)]
            flash = functools.partial(_flash, slopes, qn, kn, lam_vecs, subln_g[j], qt, kk, vt,
                                      lambda_init=lambda_init)
            attn_parts = (flash(tok_start=0, n_seq=pb, seq_len=ps),
                          flash(tok_start=n_prompt, n_seq=sb, seq_len=ss))
            attn_w = attn_out_w[j].astype(BF16)
        if i < DEPTH - 1:
            x_parts = [_mlp(x, norm2_g[i], w1, w2, attn_parts, attn_w)]
        else:
            y_prompt, y_sample = (
                _mlp(x, norm2_g[i], w1, w2, attn_parts[k:k + 1], attn_w, final_g, r0, nr)
                for k, (r0, nr) in enumerate(((0, n_prompt), (n_prompt, n_sample))))

    return (y_prompt.reshape(pb, ps, d), y_sample.reshape(sb, ss, d))
```

```python
import functools
import math

import jax
import jax.numpy as jnp
from jax import lax
from jax.experimental import pallas as pl
from jax.experimental.pallas import tpu as pltpu

D_MODEL = 1024
DEPTH = 4
POOL_WIDTH = D_MODEL // 2
N_POOL_GROUPS = 4
POOL_GROUP_DIM = POOL_WIDTH // N_POOL_GROUPS
POOL_WINDOWS = (2, 4, 8, 16)
CONV_WIDTH_CH = D_MODEL // 2
MIX_IN_COLS = POOL_WIDTH + 3 * CONV_WIDTH_CH
N_HEADS = 8
HEAD_DIM = D_MODEL // (2 * N_HEADS)
V_DIM = 2 * HEAD_DIM
ATTN_SCALE = HEAD_DIM ** -0.5
D_FF = 4 * D_MODEL
NORM_EPS = 1e-6
SUBLN_EPS = 1e-5

F32 = jnp.float32
BF16 = jnp.bfloat16

TOKEN_TILE = 512
HALO = 16
FF_CHUNK = 1024
VMEM_LIMIT_BYTES = 48 * 1024 * 1024
V_ROWS = V_DIM
PIPE_UNROLL = 8
EXP_ZERO_BELOW = 106.0
NORM_SLACK = 1.01
POS_LO_MASK = 255
POS_HI_MASK = TOKEN_TILE - 1 - POS_LO_MASK


def _rms(x, g, eps):
    return x * lax.rsqrt(jnp.mean(x * x, axis=-1, keepdims=True) + eps) * g


def _dot(a, b):
    return jnp.dot(a, b, preferred_element_type=F32)


def _const_spec(shape):
    zeros = (0,) * len(shape)
    return pl.BlockSpec(shape, lambda *_: zeros)


def _part_starts(part_rows, rows):
    starts, s = [], 0
    for r in part_rows:
        starts.append(s // rows)
        s += r
    return starts


def _part_specs(parts, rows, block_of):
    specs = []
    for p, s0 in zip(parts, _part_starts([p.shape[0] for p in parts], rows)):
        nb = p.shape[0] // rows
        specs.append(pl.BlockSpec(
            (rows, p.shape[1]), lambda i, s0=s0, nb=nb: (jnp.clip(block_of(i) - s0, 0, nb - 1), 0)))
    return specs


def _pick_part(refs, block, part_rows, rows):
    value = refs[0][...]
    for ref, s0 in zip(refs[1:], _part_starts(part_rows, rows)[1:]):
        value = jnp.where(block >= s0, ref[...], value)
    return value


def _params(n_axes):
    return pltpu.CompilerParams(
        dimension_semantics=("arbitrary",) * n_axes,
        vmem_limit_bytes=VMEM_LIMIT_BYTES,
    )


def _mixer_kernel(*refs, bounds, part_rows):
    n = len(part_rows)
    xp_refs, x_refs, xn_refs = refs[:n], refs[n:2 * n], refs[2 * n:3 * n]
    g_ref, win_ref, pw_ref, ps_ref, cw_ref, wout_ref, o_ref, u_ref, z_ref = refs[3 * n:]
    tm = x_refs[0].shape[0]
    i = pl.program_id(0)
    per = tm // HALO
    last = sum(part_rows) // HALO - 1
    start = i * tm
    seq_start = jnp.int32(bounds[0])
    seq_end = jnp.int32(bounds[-1])
    for b in bounds[1:-1]:
        seq_start = jnp.where(start >= b, b, seq_start)
    for b in reversed(bounds[1:-1]):
        seq_end = jnp.where(start < b, b, seq_end)

    x = _pick_part(x_refs, i, part_rows, tm)
    xe = jnp.concatenate([
        _pick_part(xp_refs, jnp.maximum(i * per - 1, 0), part_rows, HALO),
        x,
        _pick_part(xn_refs, jnp.minimum((i + 1) * per, last), part_rows, HALO),
    ], axis=0)
    pos = start - HALO + lax.broadcasted_iota(jnp.int32, (tm + 2 * HALO, 1), 0)
    valid = (pos >= seq_start) & (pos < seq_end)
    he = _rms(xe, g_ref[...], NORM_EPS).astype(BF16)
    proj = _dot(he, win_ref[...])
    c0 = POOL_WIDTH
    u_ref[...] = jnp.where(valid, proj[:, :c0], 0.0)
    z_ref[...] = jnp.where(valid, proj[:, c0 + 2 * CONV_WIDTH_CH:] * proj[:, c0:c0 + CONV_WIDTH_CH], 0.0)

    rel = start - seq_start + lax.broadcasted_iota(jnp.int32, (tm, 1), 0)
    seq_len = seq_end - seq_start
    ys = []
    for g, w in enumerate(POOL_WINDOWS):
        cols = slice(g * POOL_GROUP_DIM, (g + 1) * POOL_GROUP_DIM)
        tot = u_ref[HALO - w // 2:HALO - w // 2 + tm, cols]
        for o in range(-w // 2 + 1, w // 2):
            tot = tot + u_ref[HALO + o:HALO + o + tm, cols]
        cnt = (jnp.minimum(rel + w // 2, seq_len) - jnp.maximum(rel - w // 2, 0)).astype(F32)
        d = tot / cnt - u_ref[HALO:HALO + tm, cols]
        ys.append(_dot(d.astype(BF16), pw_ref[g]))
    a_out = jnp.concatenate(ys, axis=-1) * ps_ref[...]

    conv = (cw_ref[0:1, :] * z_ref[HALO - 1:HALO - 1 + tm, :]
            + cw_ref[1:2, :] * z_ref[HALO:HALO + tm, :]
            + cw_ref[2:3, :] * z_ref[HALO + 1:HALO + 1 + tm, :])
    b_out = proj[HALO:HALO + tm, c0 + CONV_WIDTH_CH:c0 + 2 * CONV_WIDTH_CH] * conv
    mixed = jnp.concatenate([a_out, b_out], axis=-1).astype(BF16)
    o_ref[...] = x + _dot(mixed, wout_ref[...])


def _mixer(x_parts, g, win, pw, ps, cw, wout, bounds):
    d = x_parts[0].shape[1]
    part_rows = tuple(p.shape[0] for p in x_parts)
    t = sum(part_rows)
    tm = TOKEN_TILE
    per = tm // HALO
    last = t // HALO - 1
    return pl.pallas_call(
        functools.partial(_mixer_kernel, bounds=bounds, part_rows=part_rows),
        grid=(t // tm,),
        in_specs=[
            *_part_specs(x_parts, HALO, lambda i: jnp.maximum(i * per - 1, 0)),
            *_part_specs(x_parts, tm, lambda i: i),
            *_part_specs(x_parts, HALO, lambda i: jnp.minimum((i + 1) * per, last)),
            _const_spec((1, d)),
            _const_spec(win.shape),
            _const_spec(pw.shape),
            _const_spec((1, POOL_WIDTH)),
            _const_spec(cw.shape),
            _const_spec(wout.shape),
        ],
        out_specs=pl.BlockSpec((tm, d), lambda i: (i, 0)),
        out_shape=jax.ShapeDtypeStruct((t, d), F32),
        scratch_shapes=[
            pltpu.VMEM((tm + 2 * HALO, POOL_WIDTH), F32),
            pltpu.VMEM((tm + 2 * HALO, CONV_WIDTH_CH), F32),
        ],
        compiler_params=_params(1),
        name="even_mixer",
    )(*x_parts, *x_parts, *x_parts, g.reshape(1, d), win, pw, ps.reshape(1, POOL_WIDTH), cw, wout)


def _mlp_kernel(*refs, attn_rows, has_final):
    refs = list(refs)
    x_ref = refs.pop(0)
    if attn_rows:
        a_refs = [refs.pop(0) for _ in attn_rows]
        wo_ref = refs.pop(0)
    g_ref, w1_ref, w2_ref = refs[:3]
    refs = refs[3:]
    if has_final:
        fg_ref = refs.pop(0)
    o_ref = refs.pop(0)

    x = x_ref[...]
    if attn_rows:
        attn = _pick_part(a_refs, pl.program_id(0), attn_rows, x_ref.shape[0])
        x = x + _dot(attn, wo_ref[...])
    h = _rms(x, g_ref[...], NORM_EPS).astype(BF16)
    acc = x
    for c in range(D_FF // FF_CHUNK):
        cols = slice(c * FF_CHUNK, (c + 1) * FF_CHUNK)
        a = jnp.maximum(_dot(h, w1_ref[:, cols]), 0.0)
        acc = acc + _dot((a * a).astype(BF16), w2_ref[cols, :])
    if has_final:
        acc = _rms(acc, fg_ref[...], NORM_EPS)
    o_ref[...] = acc


def _mlp(x, g, w1, w2, attn_parts=(), attn_w=None, final_g=None, row_start=0, rows=None):
    d = x.shape[1]
    rows = x.shape[0] if rows is None else rows
    tm = TOKEN_TILE
    tile0 = row_start // tm
    row_spec = pl.BlockSpec((tm, d), lambda i: (i, 0))
    args = [x]
    in_specs = [pl.BlockSpec((tm, d), lambda i: (tile0 + i, 0))]
    if attn_parts:
        args += [*attn_parts, attn_w]
        in_specs += [*_part_specs(attn_parts, tm, lambda i: i), _const_spec(attn_w.shape)]
    args += [g.reshape(1, d), w1, w2]
    in_specs += [_const_spec((1, d)), _const_spec(w1.shape), _const_spec(w2.shape)]
    if final_g is not None:
        args.append(final_g.reshape(1, d))
        in_specs.append(_const_spec((1, d)))
    return pl.pallas_call(
        functools.partial(_mlp_kernel, attn_rows=tuple(p.shape[0] for p in attn_parts),
                          has_final=final_g is not None),
        grid=(rows // tm,),
        in_specs=in_specs,
        out_specs=row_spec,
        out_shape=jax.ShapeDtypeStruct((rows, d), F32),
        compiler_params=_params(1),
        name="sq_relu_mlp",
    )(*args)


def _qkv_kernel(x_ref, g_ref, w_ref, qt_ref, kk_ref, vt_ref, nrm_ref):
    tm = x_ref.shape[0]
    h = _rms(x_ref[...], g_ref[...], NORM_EPS).astype(BF16)
    qkv = _dot(h, w_ref[...])
    q_all = (qkv[:, :D_MODEL] * ATTN_SCALE).astype(BF16).astype(F32)
    k_all = qkv[:, D_MODEL:2 * D_MODEL].astype(BF16).astype(F32)

    sel = (lax.shift_right_logical(lax.broadcasted_iota(jnp.int32, (D_MODEL, V_DIM), 0),
                                   HEAD_DIM.bit_length() - 1)
           == lax.broadcasted_iota(jnp.int32, (D_MODEL, V_DIM), 1)).astype(BF16)

    def max_sq_norm(a):
        return jnp.max(_dot((a * a).astype(BF16), sel), axis=0, keepdims=True)

    nrm_ref[0] = jnp.concatenate(
        [max_sq_norm(q_all), max_sq_norm(k_all), jnp.zeros((6, V_DIM), F32)], axis=0)

    q_row = lax.broadcasted_iota(jnp.int32, (HEAD_DIM, tm), 0)
    q_tok = lax.broadcasted_iota(jnp.int32, (HEAD_DIM, tm), 1)
    q_lo = (q_tok & POS_LO_MASK).astype(F32)
    q_hi = (q_tok & POS_HI_MASK).astype(F32)
    k_lane = lax.broadcasted_iota(jnp.int32, (tm, V_DIM), 1)
    k_tok = lax.broadcasted_iota(jnp.int32, (tm, V_DIM), 0)
    k_lo = (k_tok & POS_LO_MASK).astype(F32)
    k_hi = (k_tok & POS_HI_MASK).astype(F32)

    def k_extra(base, slope):
        return jnp.where(
            (k_lane == base) | (k_lane == base + 1), 1.0,
            jnp.where(k_lane == base + 2, -slope * k_lo,
                      jnp.where(k_lane == base + 3, -slope * k_hi, 0.0)))

    for hd in range(N_HEADS):
        slope = 2.0 ** (-8.0 * (hd + 1) / N_HEADS)
        cols = slice(hd * V_DIM, (hd + 1) * V_DIM)
        q_t = q_all[:, cols].T
        top, bot = q_t[:HEAD_DIM], q_t[HEAD_DIM:]
        q_extra = jnp.where(q_row == 0, slope * q_lo,
                            jnp.where(q_row == 1, slope * q_hi,
                                      jnp.where(q_row < 4, 1.0, 0.0)))
        qt_ref[hd, 0] = jnp.concatenate([top, q_extra], axis=0).astype(BF16)
        qt_ref[hd, 1] = jnp.concatenate([top, -q_extra], axis=0).astype(BF16)
        qt_ref[hd, 2] = jnp.concatenate([q_extra, bot], axis=0).astype(BF16)
        qt_ref[hd, 3] = jnp.concatenate([-q_extra, bot], axis=0).astype(BF16)

        kh = k_all[:, cols]
        kk_ref[hd, 0, 0] = jnp.where(k_lane < HEAD_DIM, kh, k_extra(HEAD_DIM, slope)).astype(BF16)
        kk_ref[hd, 1, 0] = jnp.where(k_lane >= HEAD_DIM, kh, k_extra(0, slope)).astype(BF16)

        vh = qkv[:, 2 * D_MODEL + hd * V_DIM:2 * D_MODEL + (hd + 1) * V_DIM]
        vt_ref[hd, 0] = vh.T.astype(BF16)


def _qkv(x, g, w):
    t, d = x.shape
    tm = TOKEN_TILE
    nc = t // tm
    return pl.pallas_call(
        _qkv_kernel,
        grid=(nc,),
        in_specs=[
            pl.BlockSpec((tm, d), lambda i: (i, 0)),
            _const_spec((1, d)),
            _const_spec(w.shape),
        ],
        out_specs=[
            pl.BlockSpec((N_HEADS, 4, V_DIM, tm), lambda i: (0, 0, 0, i)),
            pl.BlockSpec((N_HEADS, 2, 1, tm, V_DIM), lambda i: (0, 0, i, 0, 0)),
            pl.BlockSpec((N_HEADS, 1, V_ROWS, tm), lambda i: (0, i, 0, 0)),
            pl.BlockSpec((1, 8, V_DIM), lambda i: (i, 0, 0)),
        ],
        out_shape=[
            jax.ShapeDtypeStruct((N_HEADS, 4, V_DIM, t), BF16),
            jax.ShapeDtypeStruct((N_HEADS, 2, nc, tm, V_DIM), BF16),
            jax.ShapeDtypeStruct((N_HEADS, nc, V_ROWS, tm), BF16),
            jax.ShapeDtypeStruct((nc, 8, V_DIM), F32),
        ],
        compiler_params=_params(1),
        name="attn_qkv",
    )(x, g.reshape(1, d), w)


def _flash_kernel(slope_ref, qn_ref, kn_ref, lq1_ref, lk1_ref, lq2_ref, lk2_ref, sg_ref,
                  qt_ref, kk_ref, vt_ref, o_ref, s_ref, mc_ref, m_ref, l_ref, acc_ref,
                  *, lambda_init, blk0):
    n_chunks, tkc = kk_ref.shape[1], kk_ref.shape[2]
    tq = qt_ref.shape[-1]
    hd = pl.program_id(1)
    slope, inv_slope = slope_ref[0, hd], slope_ref[1, hd]
    cq = pl.program_id(2)
    chunk0 = blk0 + pl.program_id(0) * n_chunks

    def scores(c, sign, mp):
        return _dot(kk_ref[mp, c], qt_ref[2 * mp + sign])

    def absorb(c, mp, s, mc):
        shift = -slope * (tkc * jnp.abs(cq - c)).astype(F32)
        m_old = m_ref[mp]
        m_new = jnp.maximum(m_old, mc + shift)
        p = jnp.exp(s - (m_new - shift))
        alpha = jnp.exp(m_old - m_new)
        l_ref[mp] = alpha * l_ref[mp] + jnp.sum(p, axis=0, keepdims=True)
        acc_ref[mp] = alpha * acc_ref[mp] + _dot(vt_ref[c], p.astype(BF16))
        m_ref[mp] = m_new

    def issue_chunk(c, sign, mp):
        s = scores(c, sign, mp)
        s_ref[mp] = s
        mc_ref[mp] = jnp.max(s, axis=0, keepdims=True)

    jj = lax.broadcasted_iota(jnp.int32, (tkc, tq), 0)
    ii = lax.broadcasted_iota(jnp.int32, (tkc, tq), 1)
    fixup = (-2.0 * slope) * jnp.maximum(ii - jj, 0).astype(F32)
    diag = [scores(cq, 0, mp) + fixup for mp in range(2)]
    issue_chunk(jnp.where(cq > 0, cq - 1, jnp.minimum(cq + 1, n_chunks - 1)),
                (cq > 0).astype(jnp.int32), 0)
    for mp in range(2):
        m_first = jnp.max(diag[mp], axis=0, keepdims=True)
        m_ref[mp] = m_first
        p_first = jnp.exp(diag[mp] - m_first)
        l_ref[mp] = jnp.sum(p_first, axis=0, keepdims=True)
        acc_ref[mp] = _dot(vt_ref[cq], p_first.astype(BF16))

    gap = (NORM_SLACK * qn_ref[hd, chunk0 + cq] * kn_ref[hd, pl.program_id(0)]
           + EXP_ZERO_BELOW - jnp.min(m_ref[...]))
    reach = (gap * inv_slope - 1.0) * (1.0 / tkc)
    n_dist = jnp.where(reach >= 1.0,
                       jnp.minimum(reach, float(n_chunks)).astype(jnp.int32) + 1, 1)
    n_left = jnp.minimum(n_dist, cq)
    total = n_left + jnp.minimum(n_dist, n_chunks - 1 - cq)

    def item(t):
        left = t < n_left
        c = jnp.where(left, cq - 1 - t, cq + 1 + t - n_left)
        return jnp.clip(c, 0, n_chunks - 1), left.astype(jnp.int32)

    def trip(t):
        c, sign = item(t)
        issue_chunk(c, sign, 1)
        absorb(c, 0, s_ref[0], mc_ref[0])
        issue_chunk(*item(t + 1), 0)
        absorb(c, 1, s_ref[1], mc_ref[1])

    def trips(t0, count):
        for r in range(count):
            trip(t0 + r)

    assert PIPE_UNROLL & (PIPE_UNROLL - 1) == 0
    rem = total & (PIPE_UNROLL - 1)
    done, size = 0, 1
    while size < PIPE_UNROLL:
        pl.when((total & size) == size)(functools.partial(trips, done, size))
        done = done + (total & size)
        size *= 2

    def group(u, carry):
        trips(rem + PIPE_UNROLL * u, PIPE_UNROLL)
        return carry

    lax.fori_loop(0, total // PIPE_UNROLL, group, 0)

    lam = (jnp.exp(jnp.sum(lq1_ref[...] * lk1_ref[...], keepdims=True))
           - jnp.exp(jnp.sum(lq2_ref[...] * lk2_ref[...], keepdims=True)) + lambda_init)
    o_t = acc_ref[0] * (1.0 / l_ref[0]) - (lam / l_ref[1]) * acc_ref[1]
    scale = lax.rsqrt(jnp.mean(o_t * o_t, axis=0, keepdims=True) + SUBLN_EPS) * (1.0 - lambda_init)
    o_ref[...] = (o_t * scale * sg_ref[...]).T.astype(o_ref.dtype)


def _flash(slopes, qn, kn, lam_vecs, subln_g, qt, kk, vt, *, tok_start, n_seq, seq_len,
           lambda_init):
    tq = TOKEN_TILE
    nq = seq_len // tq
    blk0 = tok_start // tq
    seq0 = tok_start // seq_len
    smem = pl.BlockSpec(memory_space=pltpu.SMEM)
    vec = _const_spec((1, HEAD_DIM))
    return pl.pallas_call(
        functools.partial(_flash_kernel, lambda_init=lambda_init, blk0=blk0),
        grid=(n_seq, N_HEADS, nq),
        in_specs=[
            smem, smem, smem, vec, vec, vec, vec, _const_spec((V_DIM, tq)),
            pl.BlockSpec((None, 4, V_DIM, tq), lambda b, h, i: (h, 0, 0, blk0 + b * nq + i)),
            pl.BlockSpec((None, 2, nq, tq, V_DIM), lambda b, h, i: (h, 0, seq0 + b, 0, 0)),
            pl.BlockSpec((None, nq, V_ROWS, tq), lambda b, h, i: (h, seq0 + b, 0, 0)),
        ],
        out_specs=pl.BlockSpec((tq, V_DIM), lambda b, h, i: (b * nq + i, h)),
        out_shape=jax.ShapeDtypeStruct((n_seq * seq_len, N_HEADS * V_DIM), BF16),
        scratch_shapes=[
            pltpu.VMEM((2, tq, tq), F32),
            pltpu.VMEM((2, 1, tq), F32),
            pltpu.VMEM((2, 1, tq), F32),
            pltpu.VMEM((2, 1, tq), F32),
            pltpu.VMEM((2, V_DIM, tq), F32),
        ],
        compiler_params=_params(3),
        name="diff_flash",
    )(slopes, qn, kn[:, blk0:blk0 + n_seq * nq].reshape(N_HEADS, n_seq, nq).max(axis=-1),
      *lam_vecs, jnp.broadcast_to(subln_g.reshape(V_DIM, 1), (V_DIM, tq)), qt, kk, vt)


def kernel(x_prompt, x_sample, norm1_g, norm2_g, final_g, mix_in_w, pool_w, pool_scale, conv_w,
           mix_out_w, attn_qkv_w, attn_out_w, lambda_q1, lambda_k1, lambda_q2, lambda_k2, subln_g,
           mlp_w1, mlp_w2):
    pb, ps, d = x_prompt.shape
    sb, ss, _ = x_sample.shape
    n_prompt = pb * ps
    n_sample = sb * ss
    x_parts = [x_prompt.reshape(n_prompt, d), x_sample.reshape(n_sample, d)]
    bounds = tuple(ps * b for b in range(pb)) + tuple(n_prompt + ss * b for b in range(sb + 1))
    slopes = jnp.exp2(-8.0 * (jnp.arange(N_HEADS, dtype=F32) + 1.0) / N_HEADS)
    slopes = jnp.stack([slopes, 1.0 / slopes])

    for i in range(DEPTH):
        j = i // 2
        w1 = mlp_w1[i].astype(BF16)
        w2 = mlp_w2[i].astype(BF16)
        attn_parts, attn_w = (), None
        if i % 2 == 0:
            x = _mixer(x_parts, norm1_g[i], mix_in_w[j].astype(BF16), pool_w[j].astype(BF16),
                       pool_scale[j], conv_w[j], mix_out_w[j].astype(BF16), bounds)
        else:
            x = x_parts[0]
            lambda_init = 0.8 - 0.6 * math.exp(-0.3 * i)
            qt, kk, vt, nrm = _qkv(x, norm1_g[i], attn_qkv_w[j].astype(BF16))
            qn, kn = (jnp.sqrt(nrm[:, r, :2 * N_HEADS].reshape(-1, N_HEADS, 2).max(axis=-1)).T
                      for r in range(2))
            lam_vecs = [v[j].reshape(1, HEAD_DIM) for v in (lambda_q1, lambda_k1, lambda_q2, lambda_k2)]
            flash = functools.partial(_flash, slopes, qn, kn, lam_vecs, subln_g[j], qt, kk, vt,
                                      lambda_init=lambda_init)
            attn_parts = (flash(tok_start=0, n_seq=pb, seq_len=ps),
                          flash(tok_start=n_prompt, n_seq=sb, seq_len=ss))
            attn_w = attn_out_w[j].astype(BF16)
        if i < DEPTH - 1:
            x_parts = [_mlp(x, norm2_g[i], w1, w2, attn_parts, attn_w)]
        else:
            y_prompt, y_sample = (
                _mlp(x, norm2_g[i], w1, w2, attn_parts[k:k + 1], attn_w, final_g, r0, nr)
                for k, (r0, nr) in enumerate(((0, n_prompt), (n_prompt, n_sample))))

    return (y_prompt.reshape(pb, ps, d), y_sample.reshape(sb, ss, d))
```

```python
import functools
import math

import jax
import jax.numpy as jnp
import numpy as np
from jax import lax
from jax.experimental import pallas as pl
from jax.experimental.pallas import tpu as pltpu

D_MODEL = 1024
DEPTH = 4
POOL_WIDTH = D_MODEL // 2
N_POOL_GROUPS = 4
POOL_GROUP_DIM = POOL_WIDTH // N_POOL_GROUPS
POOL_WINDOWS = (2, 4, 8, 16)
CONV_WIDTH_CH = D_MODEL // 2
MIX_IN_COLS = POOL_WIDTH + 3 * CONV_WIDTH_CH
N_HEADS = 8
HEAD_DIM = D_MODEL // (2 * N_HEADS)
V_DIM = 2 * HEAD_DIM
ATTN_SCALE = HEAD_DIM ** -0.5
D_FF = 4 * D_MODEL
NORM_EPS = 1e-6
SUBLN_EPS = 1e-5

F32 = jnp.float32
BF16 = jnp.bfloat16

TOKEN_TILE = 512
HALO = 16
FF_CHUNK = 1024
VMEM_LIMIT_BYTES = 48 * 1024 * 1024
DENOM_ROWS = 16
V_ROWS = V_DIM + DENOM_ROWS
PIPE_UNROLL = 8
EXP_ZERO_BELOW = 153.0


def _bf16_pieces(value, n):
    pieces, rest = [], np.float32(value)
    for _ in range(n):
        piece = np.float32(np.asarray(rest, dtype=BF16))
        pieces.append(float(piece))
        rest = np.float32(rest - piece)
    return tuple(pieces)


LOG2E = float(np.float32(math.log2(math.e)))
LOG2E_PIECES = _bf16_pieces(LOG2E, 3)
NORM_SLACK = 1.01
POS_LO_MASK = 255
POS_HI_MASK = TOKEN_TILE - 1 - POS_LO_MASK


def _rms(x, g, eps):
    return x * lax.rsqrt(jnp.mean(x * x, axis=-1, keepdims=True) + eps) * g


def _dot(a, b):
    return jnp.dot(a, b, preferred_element_type=F32)


def _const_spec(shape):
    zeros = (0,) * len(shape)
    return pl.BlockSpec(shape, lambda *_: zeros)


def _part_starts(part_rows, rows):
    starts, s = [], 0
    for r in part_rows:
        starts.append(s // rows)
        s += r
    return starts


def _part_specs(parts, rows, block_of):
    specs = []
    for p, s0 in zip(parts, _part_starts([p.shape[0] for p in parts], rows)):
        nb = p.shape[0] // rows
        specs.append(pl.BlockSpec(
            (rows, p.shape[1]), lambda i, s0=s0, nb=nb: (jnp.clip(block_of(i) - s0, 0, nb - 1), 0)))
    return specs


def _pick_part(refs, block, part_rows, rows):
    value = refs[0][...]
    for ref, s0 in zip(refs[1:], _part_starts(part_rows, rows)[1:]):
        value = jnp.where(block >= s0, ref[...], value)
    return value


def _params(n_axes):
    return pltpu.CompilerParams(
        dimension_semantics=("arbitrary",) * n_axes,
        vmem_limit_bytes=VMEM_LIMIT_BYTES,
    )


def _mixer_kernel(*refs, bounds, part_rows):
    n = len(part_rows)
    xp_refs, x_refs, xn_refs = refs[:n], refs[n:2 * n], refs[2 * n:3 * n]
    g_ref, win_ref, pw_ref, ps_ref, cw_ref, wout_ref, o_ref, u_ref, z_ref = refs[3 * n:]
    tm = x_refs[0].shape[0]
    i = pl.program_id(0)
    per = tm // HALO
    last = sum(part_rows) // HALO - 1
    start = i * tm
    seq_start = jnp.int32(bounds[0])
    seq_end = jnp.int32(bounds[-1])
    for b in bounds[1:-1]:
        seq_start = jnp.where(start >= b, b, seq_start)
    for b in reversed(bounds[1:-1]):
        seq_end = jnp.where(start < b, b, seq_end)

    x = _pick_part(x_refs, i, part_rows, tm)
    xe = jnp.concatenate([
        _pick_part(xp_refs, jnp.maximum(i * per - 1, 0), part_rows, HALO),
        x,
        _pick_part(xn_refs, jnp.minimum((i + 1) * per, last), part_rows, HALO),
    ], axis=0)
    pos = start - HALO + lax.broadcasted_iota(jnp.int32, (tm + 2 * HALO, 1), 0)
    valid = (pos >= seq_start) & (pos < seq_end)
    he = _rms(xe, g_ref[...], NORM_EPS).astype(BF16)
    proj = _dot(he, win_ref[...])
    c0 = POOL_WIDTH
    u_ref[...] = jnp.where(valid, proj[:, :c0], 0.0)
    z_ref[...] = jnp.where(valid, proj[:, c0 + 2 * CONV_WIDTH_CH:] * proj[:, c0:c0 + CONV_WIDTH_CH], 0.0)

    rel = start - seq_start + lax.broadcasted_iota(jnp.int32, (tm, 1), 0)
    seq_len = seq_end - seq_start
    ys = []
    for g, w in enumerate(POOL_WINDOWS):
        cols = slice(g * POOL_GROUP_DIM, (g + 1) * POOL_GROUP_DIM)
        tot = u_ref[HALO - w // 2:HALO - w // 2 + tm, cols]
        for o in range(-w // 2 + 1, w // 2):
            tot = tot + u_ref[HALO + o:HALO + o + tm, cols]
        cnt = (jnp.minimum(rel + w // 2, seq_len) - jnp.maximum(rel - w // 2, 0)).astype(F32)
        d = tot / cnt - u_ref[HALO:HALO + tm, cols]
        ys.append(_dot(d.astype(BF16), pw_ref[g]))
    a_out = jnp.concatenate(ys, axis=-1) * ps_ref[...]

    conv = (cw_ref[0:1, :] * z_ref[HALO - 1:HALO - 1 + tm, :]
            + cw_ref[1:2, :] * z_ref[HALO:HALO + tm, :]
            + cw_ref[2:3, :] * z_ref[HALO + 1:HALO + 1 + tm, :])
    b_out = proj[HALO:HALO + tm, c0 + CONV_WIDTH_CH:c0 + 2 * CONV_WIDTH_CH] * conv
    mixed = jnp.concatenate([a_out, b_out], axis=-1).astype(BF16)
    o_ref[...] = x + _dot(mixed, wout_ref[...])


def _mixer(x_parts, g, win, pw, ps, cw, wout, bounds):
    d = x_parts[0].shape[1]
    part_rows = tuple(p.shape[0] for p in x_parts)
    t = sum(part_rows)
    tm = TOKEN_TILE
    per = tm // HALO
    last = t // HALO - 1
    return pl.pallas_call(
        functools.partial(_mixer_kernel, bounds=bounds, part_rows=part_rows),
        grid=(t // tm,),
        in_specs=[
            *_part_specs(x_parts, HALO, lambda i: jnp.maximum(i * per - 1, 0)),
            *_part_specs(x_parts, tm, lambda i: i),
            *_part_specs(x_parts, HALO, lambda i: jnp.minimum((i + 1) * per, last)),
            _const_spec((1, d)),
            _const_spec(win.shape),
            _const_spec(pw.shape),
            _const_spec((1, POOL_WIDTH)),
            _const_spec(cw.shape),
            _const_spec(wout.shape),
        ],
        out_specs=pl.BlockSpec((tm, d), lambda i: (i, 0)),
        out_shape=jax.ShapeDtypeStruct((t, d), F32),
        scratch_shapes=[
            pltpu.VMEM((tm + 2 * HALO, POOL_WIDTH), F32),
            pltpu.VMEM((tm + 2 * HALO, CONV_WIDTH_CH), F32),
        ],
        compiler_params=_params(1),
        name="even_mixer",
    )(*x_parts, *x_parts, *x_parts, g.reshape(1, d), win, pw, ps.reshape(1, POOL_WIDTH), cw, wout)


def _mlp_kernel(*refs, attn_rows, has_final):
    refs = list(refs)
    x_ref = refs.pop(0)
    if attn_rows:
        a_refs = [refs.pop(0) for _ in attn_rows]
        wo_ref = refs.pop(0)
    g_ref, w1_ref, w2_ref = refs[:3]
    refs = refs[3:]
    if has_final:
        fg_ref = refs.pop(0)
    o_ref = refs.pop(0)

    x = x_ref[...]
    if attn_rows:
        attn = _pick_part(a_refs, pl.program_id(0), attn_rows, x_ref.shape[0])
        x = x + _dot(attn, wo_ref[...])
    h = _rms(x, g_ref[...], NORM_EPS).astype(BF16)
    acc = x
    for c in range(D_FF // FF_CHUNK):
        cols = slice(c * FF_CHUNK, (c + 1) * FF_CHUNK)
        a = jnp.maximum(_dot(h, w1_ref[:, cols]), 0.0)
        acc = acc + _dot((a * a).astype(BF16), w2_ref[cols, :])
    if has_final:
        acc = _rms(acc, fg_ref[...], NORM_EPS)
    o_ref[...] = acc


def _mlp(x, g, w1, w2, attn_parts=(), attn_w=None, final_g=None, row_start=0, rows=None):
    d = x.shape[1]
    rows = x.shape[0] if rows is None else rows
    tm = TOKEN_TILE
    tile0 = row_start // tm
    row_spec = pl.BlockSpec((tm, d), lambda i: (i, 0))
    args = [x]
    in_specs = [pl.BlockSpec((tm, d), lambda i: (tile0 + i, 0))]
    if attn_parts:
        args += [*attn_parts, attn_w]
        in_specs += [*_part_specs(attn_parts, tm, lambda i: i), _const_spec(attn_w.shape)]
    args += [g.reshape(1, d), w1, w2]
    in_specs += [_const_spec((1, d)), _const_spec(w1.shape), _const_spec(w2.shape)]
    if final_g is not None:
        args.append(final_g.reshape(1, d))
        in_specs.append(_const_spec((1, d)))
    return pl.pallas_call(
        functools.partial(_mlp_kernel, attn_rows=tuple(p.shape[0] for p in attn_parts),
                          has_final=final_g is not None),
        grid=(rows // tm,),
        in_specs=in_specs,
        out_specs=row_spec,
        out_shape=jax.ShapeDtypeStruct((rows, d), F32),
        compiler_params=_params(1),
        name="sq_relu_mlp",
    )(*args)


def _qkv_kernel(x_ref, g_ref, w_ref, qt_ref, kk_ref, vt_ref, nrm_ref):
    tm = x_ref.shape[0]
    h = _rms(x_ref[...], g_ref[...], NORM_EPS).astype(BF16)
    qkv = _dot(h, w_ref[...])
    q_all = (qkv[:, :D_MODEL] * (ATTN_SCALE * LOG2E)).astype(BF16).astype(F32)
    k_all = qkv[:, D_MODEL:2 * D_MODEL].astype(BF16).astype(F32)

    sel = (lax.shift_right_logical(lax.broadcasted_iota(jnp.int32, (D_MODEL, V_DIM), 0),
                                   HEAD_DIM.bit_length() - 1)
           == lax.broadcasted_iota(jnp.int32, (D_MODEL, V_DIM), 1)).astype(BF16)

    def max_sq_norm(a):
        return jnp.max(_dot((a * a).astype(BF16), sel), axis=0, keepdims=True)

    nrm_ref[0] = jnp.concatenate(
        [max_sq_norm(q_all), max_sq_norm(k_all), jnp.zeros((6, V_DIM), F32)], axis=0)

    n_p = len(LOG2E_PIECES)

    def extras(slot, lo, hi, pos_first):
        pos_slot = slot if pos_first else slot - 2 * n_p
        const_slot = slot - 2 * n_p if pos_first else slot
        out = jnp.where((pos_slot >= 0) & (pos_slot < n_p), lo,
                        jnp.where((pos_slot >= n_p) & (pos_slot < 2 * n_p), hi, 0.0))
        for p, piece in enumerate(LOG2E_PIECES):
            out = jnp.where((const_slot == p) | (const_slot == n_p + p), piece, out)
        return out

    q_row = lax.broadcasted_iota(jnp.int32, (HEAD_DIM, tm), 0)
    q_tok = lax.broadcasted_iota(jnp.int32, (HEAD_DIM, tm), 1)
    q_base = extras(q_row, (q_tok & POS_LO_MASK).astype(F32), (q_tok & POS_HI_MASK).astype(F32), True)
    q_has_slope = q_row >= 2 * n_p
    k_lane = lax.broadcasted_iota(jnp.int32, (tm, V_DIM), 1)
    k_tok = lax.broadcasted_iota(jnp.int32, (tm, V_DIM), 0)
    k_lo = -(k_tok & POS_LO_MASK).astype(F32)
    k_hi = -(k_tok & POS_HI_MASK).astype(F32)
    k_base = [extras(k_lane - off, k_lo, k_hi, False) for off in (HEAD_DIM, 0)]
    k_has_slope = [(k_lane >= off) & (k_lane < off + 2 * n_p) for off in (HEAD_DIM, 0)]
    denom_rows = (lax.broadcasted_iota(jnp.int32, (DENOM_ROWS, tm), 0) == 0).astype(F32)

    for hd in range(N_HEADS):
        slope = 2.0 ** (-8.0 * (hd + 1) / N_HEADS)
        cols = slice(hd * V_DIM, (hd + 1) * V_DIM)
        q_t = q_all[:, cols].T
        top, bot = q_t[:HEAD_DIM], q_t[HEAD_DIM:]
        q_extra = jnp.where(q_has_slope, slope * q_base, q_base)
        qt_ref[hd, 0] = jnp.concatenate([top, q_extra], axis=0).astype(BF16)
        qt_ref[hd, 1] = jnp.concatenate([top, -q_extra], axis=0).astype(BF16)
        qt_ref[hd, 2] = jnp.concatenate([q_extra, bot], axis=0).astype(BF16)
        qt_ref[hd, 3] = jnp.concatenate([-q_extra, bot], axis=0).astype(BF16)

        kh = k_all[:, cols]
        k_extra = [jnp.where(k_has_slope[m], slope * k_base[m], k_base[m]) for m in range(2)]
        kk_ref[hd, 0, 0] = jnp.where(k_lane < HEAD_DIM, kh, k_extra[0]).astype(BF16)
        kk_ref[hd, 1, 0] = jnp.where(k_lane >= HEAD_DIM, kh, k_extra[1]).astype(BF16)

        vh = qkv[:, 2 * D_MODEL + hd * V_DIM:2 * D_MODEL + (hd + 1) * V_DIM]
        vt_ref[hd, 0] = jnp.concatenate([vh.T, denom_rows], axis=0).astype(BF16)


def _qkv(x, g, w):
    t, d = x.shape
    tm = TOKEN_TILE
    nc = t // tm
    return pl.pallas_call(
        _qkv_kernel,
        grid=(nc,),
        in_specs=[
            pl.BlockSpec((tm, d), lambda i: (i, 0)),
            _const_spec((1, d)),
            _const_spec(w.shape),
        ],
        out_specs=[
            pl.BlockSpec((N_HEADS, 4, V_DIM, tm), lambda i: (0, 0, 0, i)),
            pl.BlockSpec((N_HEADS, 2, 1, tm, V_DIM), lambda i: (0, 0, i, 0, 0)),
            pl.BlockSpec((N_HEADS, 1, V_ROWS, tm), lambda i: (0, i, 0, 0)),
            pl.BlockSpec((1, 8, V_DIM), lambda i: (i, 0, 0)),
        ],
        out_shape=[
            jax.ShapeDtypeStruct((N_HEADS, 4, V_DIM, t), BF16),
            jax.ShapeDtypeStruct((N_HEADS, 2, nc, tm, V_DIM), BF16),
            jax.ShapeDtypeStruct((N_HEADS, nc, V_ROWS, tm), BF16),
            jax.ShapeDtypeStruct((nc, 8, V_DIM), F32),
        ],
        compiler_params=_params(1),
        name="attn_qkv",
    )(x, g.reshape(1, d), w)


def _flash_kernel(slope_ref, qn_ref, kn_ref, lq1_ref, lk1_ref, lq2_ref, lk2_ref, sg_ref,
                  qt_ref, kk_ref, vt_ref, o_ref, s_ref, mc_ref, m_ref, acc_ref,
                  *, lambda_init, blk0):
    n_chunks, tkc = kk_ref.shape[1], kk_ref.shape[2]
    tq = qt_ref.shape[-1]
    hd = pl.program_id(1)
    slope, inv_slope = slope_ref[0, hd], slope_ref[1, hd]
    cq = pl.program_id(2)
    chunk0 = blk0 + pl.program_id(0) * n_chunks

    def scores(c, sign, mp):
        return _dot(kk_ref[mp, c], qt_ref[2 * mp + sign])

    def absorb(c, mp, s, mc):
        shift = -slope * (tkc * jnp.abs(cq - c)).astype(F32)
        m_old = m_ref[mp]
        m_new = jnp.maximum(m_old, mc + shift)
        p = jnp.exp2(s - (m_new - shift)).astype(BF16)
        acc_ref[mp] = jnp.exp2(m_old - m_new) * acc_ref[mp] + _dot(vt_ref[c], p)
        m_ref[mp] = m_new

    def issue_chunk(c, sign, mp):
        s = scores(c, sign, mp)
        s_ref[mp] = s
        mc_ref[mp] = jnp.max(s, axis=0, keepdims=True)

    jj = lax.broadcasted_iota(jnp.int32, (tkc, tq), 0)
    ii = lax.broadcasted_iota(jnp.int32, (tkc, tq), 1)
    fixup = (-2.0 * slope) * jnp.maximum(ii - jj, 0).astype(F32)
    diag = [scores(cq, 0, mp) + fixup for mp in range(2)]
    issue_chunk(jnp.where(cq > 0, cq - 1, jnp.minimum(cq + 1, n_chunks - 1)),
                (cq > 0).astype(jnp.int32), 0)
    for mp in range(2):
        m_first = jnp.max(diag[mp], axis=0, keepdims=True)
        m_ref[mp] = m_first
        acc_ref[mp] = _dot(vt_ref[cq], jnp.exp2(diag[mp] - m_first).astype(BF16))

    gap = (NORM_SLACK * qn_ref[hd, chunk0 + cq] * kn_ref[hd, pl.program_id(0)]
           + EXP_ZERO_BELOW - jnp.min(m_ref[...]))
    reach = (gap * inv_slope - 1.0) * (1.0 / tkc)
    n_dist = jnp.where(reach >= 1.0,
                       jnp.minimum(reach, float(n_chunks)).astype(jnp.int32) + 1, 1)
    n_left = jnp.minimum(n_dist, cq)
    total = n_left + jnp.minimum(n_dist, n_chunks - 1 - cq)

    def item(t):
        left = t < n_left
        c = jnp.where(left, cq - 1 - t, cq + 1 + t - n_left)
        return jnp.clip(c, 0, n_chunks - 1), left.astype(jnp.int32)

    def trip(t):
        c, sign = item(t)
        issue_chunk(c, sign, 1)
        absorb(c, 0, s_ref[0], mc_ref[0])
        issue_chunk(*item(t + 1), 0)
        absorb(c, 1, s_ref[1], mc_ref[1])

    def trips(t0, count):
        for r in range(count):
            trip(t0 + r)

    assert PIPE_UNROLL & (PIPE_UNROLL - 1) == 0
    rem = total & (PIPE_UNROLL - 1)
    done, size = 0, 1
    while size < PIPE_UNROLL:
        pl.when((total & size) == size)(functools.partial(trips, done, size))
        done = done + (total & size)
        size *= 2

    def group(u, carry):
        trips(rem + PIPE_UNROLL * u, PIPE_UNROLL)
        return carry

    lax.fori_loop(0, total // PIPE_UNROLL, group, 0)

    lam = (jnp.exp(jnp.sum(lq1_ref[...] * lk1_ref[...], keepdims=True))
           - jnp.exp(jnp.sum(lq2_ref[...] * lk2_ref[...], keepdims=True)) + lambda_init)
    o_t = (acc_ref[0, :V_DIM] * (1.0 / acc_ref[0, V_DIM:V_DIM + 1])
           - (lam / acc_ref[1, V_DIM:V_DIM + 1]) * acc_ref[1, :V_DIM])
    scale = lax.rsqrt(jnp.mean(o_t * o_t, axis=0, keepdims=True) + SUBLN_EPS) * (1.0 - lambda_init)
    o_ref[...] = (o_t * scale * sg_ref[...]).T.astype(o_ref.dtype)


def _flash(slopes, qn, kn, lam_vecs, subln_g, qt, kk, vt, *, tok_start, n_seq, seq_len,
           lambda_init):
    tq = TOKEN_TILE
    nq = seq_len // tq
    blk0 = tok_start // tq
    seq0 = tok_start // seq_len
    smem = pl.BlockSpec(memory_space=pltpu.SMEM)
    vec = _const_spec((1, HEAD_DIM))
    return pl.pallas_call(
        functools.partial(_flash_kernel, lambda_init=lambda_init, blk0=blk0),
        grid=(n_seq, N_HEADS, nq),
        in_specs=[
            smem, smem, smem, vec, vec, vec, vec, _const_spec((V_DIM, tq)),
            pl.BlockSpec((None, 4, V_DIM, tq), lambda b, h, i: (h, 0, 0, blk0 + b * nq + i)),
            pl.BlockSpec((None, 2, nq, tq, V_DIM), lambda b, h, i: (h, 0, seq0 + b, 0, 0)),
            pl.BlockSpec((None, nq, V_ROWS, tq), lambda b, h, i: (h, seq0 + b, 0, 0)),
        ],
        out_specs=pl.BlockSpec((tq, V_DIM), lambda b, h, i: (b * nq + i, h)),
        out_shape=jax.ShapeDtypeStruct((n_seq * seq_len, N_HEADS * V_DIM), BF16),
        scratch_shapes=[
            pltpu.VMEM((2, tq, tq), F32),
            pltpu.VMEM((2, 1, tq), F32),
            pltpu.VMEM((2, 1, tq), F32),
            pltpu.VMEM((2, V_ROWS, tq), F32),
        ],
        compiler_params=_params(3),
        name="diff_flash",
    )(slopes, qn, kn[:, blk0:blk0 + n_seq * nq].reshape(N_HEADS, n_seq, nq).max(axis=-1),
      *lam_vecs, jnp.broadcast_to(subln_g.reshape(V_DIM, 1), (V_DIM, tq)), qt, kk, vt)


def kernel(x_prompt, x_sample, norm1_g, norm2_g, final_g, mix_in_w, pool_w, pool_scale, conv_w,
           mix_out_w, attn_qkv_w, attn_out_w, lambda_q1, lambda_k1, lambda_q2, lambda_k2, subln_g,
           mlp_w1, mlp_w2):
    pb, ps, d = x_prompt.shape
    sb, ss, _ = x_sample.shape
    n_prompt = pb * ps
    n_sample = sb * ss
    x_parts = [x_prompt.reshape(n_prompt, d), x_sample.reshape(n_sample, d)]
    bounds = tuple(ps * b for b in range(pb)) + tuple(n_prompt + ss * b for b in range(sb + 1))
    slopes = jnp.exp2(-8.0 * (jnp.arange(N_HEADS, dtype=F32) + 1.0) / N_HEADS)
    slopes = jnp.stack([slopes * LOG2E, 1.0 / (slopes * LOG2E)])

    for i in range(DEPTH):
        j = i // 2
        w1 = mlp_w1[i].astype(BF16)
        w2 = mlp_w2[i].astype(BF16)
        attn_parts, attn_w = (), None
        if i % 2 == 0:
            x = _mixer(x_parts, norm1_g[i], mix_in_w[j].astype(BF16), pool_w[j].astype(BF16),
                       pool_scale[j], conv_w[j], mix_out_w[j].astype(BF16), bounds)
        else:
            x = x_parts[0]
            lambda_init = 0.8 - 0.6 * math.exp(-0.3 * i)
            qt, kk, vt, nrm = _qkv(x, norm1_g[i], attn_qkv_w[j].astype(BF16))
            qn, kn = (jnp.sqrt(nrm[:, r, :2 * N_HEADS].reshape(-1, N_HEADS, 2).max(axis=-1)).T
                      for r in range(2))
            lam_vecs = [v[j].reshape(1, HEAD_DIM) for v in (lambda_q1, lambda_k1, lambda_q2, lambda_k2)]
            flash = functools.partial(_flash, slopes, qn, kn, lam_vecs, subln_g[j], qt, kk, vt,
                                      lambda_init=lambda_init)
            attn_parts = (flash(tok_start=0, n_seq=pb, seq_len=ps),
                          flash(tok_start=n_prompt, n_seq=sb, seq_len=ss))
            attn_w = attn_out_w[j].astype(BF16)
        if i < DEPTH - 1:
            x_parts = [_mlp(x, norm2_g[i], w1, w2, attn_parts, attn_w)]
        else:
            y_prompt, y_sample = (
                _mlp(x, norm2_g[i], w1, w2, attn_parts[k:k + 1], attn_w, final_g, r0, nr)
                for k, (r0, nr) in enumerate(((0, n_prompt), (n_prompt, n_sample))))

    return (y_prompt.reshape(pb, ps, d), y_sample.reshape(sb, ss, d))
```

```python
import functools
import math

import jax
import jax.numpy as jnp
import numpy as np
from jax import lax
from jax.experimental import pallas as pl
from jax.experimental.pallas import tpu as pltpu

D_MODEL = 1024
DEPTH = 4
POOL_WIDTH = D_MODEL // 2
N_POOL_GROUPS = 4
POOL_GROUP_DIM = POOL_WIDTH // N_POOL_GROUPS
POOL_WINDOWS = (2, 4, 8, 16)
CONV_WIDTH_CH = D_MODEL // 2
MIX_IN_COLS = POOL_WIDTH + 3 * CONV_WIDTH_CH
N_HEADS = 8
HEAD_DIM = D_MODEL // (2 * N_HEADS)
V_DIM = 2 * HEAD_DIM
ATTN_SCALE = HEAD_DIM ** -0.5
D_FF = 4 * D_MODEL
NORM_EPS = 1e-6
SUBLN_EPS = 1e-5

F32 = jnp.float32
BF16 = jnp.bfloat16

TOKEN_TILE = 512
HALO = 16
FF_CHUNK = 1024
VMEM_LIMIT_BYTES = 48 * 1024 * 1024
DENOM_ROWS = 16
V_ROWS = V_DIM + DENOM_ROWS
PIPE_UNROLL = 8
Q_BLOCKS_PER_STEP = 2
EXP_ZERO_BELOW = 153.0


def _bf16_pieces(value, n):
    pieces, rest = [], np.float32(value)
    for _ in range(n):
        piece = np.float32(np.asarray(rest, dtype=BF16))
        pieces.append(float(piece))
        rest = np.float32(rest - piece)
    return tuple(pieces)


LOG2E = float(np.float32(math.log2(math.e)))
LOG2E_PIECES = _bf16_pieces(LOG2E, 3)
NORM_SLACK = 1.01
POS_LO_MASK = 255
POS_HI_MASK = TOKEN_TILE - 1 - POS_LO_MASK


def _rms(x, g, eps):
    return x * lax.rsqrt(jnp.mean(x * x, axis=-1, keepdims=True) + eps) * g


def _dot(a, b):
    return jnp.dot(a, b, preferred_element_type=F32)


def _const_spec(shape):
    zeros = (0,) * len(shape)
    return pl.BlockSpec(shape, lambda *_: zeros)


def _part_starts(part_rows, rows):
    starts, s = [], 0
    for r in part_rows:
        starts.append(s // rows)
        s += r
    return starts


def _part_specs(parts, rows, block_of):
    specs = []
    for p, s0 in zip(parts, _part_starts([p.shape[0] for p in parts], rows)):
        nb = p.shape[0] // rows
        specs.append(pl.BlockSpec(
            (rows, p.shape[1]), lambda i, s0=s0, nb=nb: (jnp.clip(block_of(i) - s0, 0, nb - 1), 0)))
    return specs


def _pick_part(refs, block, part_rows, rows):
    value = refs[0][...]
    for ref, s0 in zip(refs[1:], _part_starts(part_rows, rows)[1:]):
        value = jnp.where(block >= s0, ref[...], value)
    return value


def _params(n_axes):
    return pltpu.CompilerParams(
        dimension_semantics=("arbitrary",) * n_axes,
        vmem_limit_bytes=VMEM_LIMIT_BYTES,
    )


def _mixer_kernel(*refs, bounds, part_rows):
    n = len(part_rows)
    xp_refs, x_refs, xn_refs = refs[:n], refs[n:2 * n], refs[2 * n:3 * n]
    g_ref, win_ref, pw_ref, ps_ref, cw_ref, wout_ref, o_ref, u_ref, z_ref = refs[3 * n:]
    tm = x_refs[0].shape[0]
    i = pl.program_id(0)
    per = tm // HALO
    last = sum(part_rows) // HALO - 1
    start = i * tm
    seq_start = jnp.int32(bounds[0])
    seq_end = jnp.int32(bounds[-1])
    for b in bounds[1:-1]:
        seq_start = jnp.where(start >= b, b, seq_start)
    for b in reversed(bounds[1:-1]):
        seq_end = jnp.where(start < b, b, seq_end)

    x = _pick_part(x_refs, i, part_rows, tm)
    xe = jnp.concatenate([
        _pick_part(xp_refs, jnp.maximum(i * per - 1, 0), part_rows, HALO),
        x,
        _pick_part(xn_refs, jnp.minimum((i + 1) * per, last), part_rows, HALO),
    ], axis=0)
    pos = start - HALO + lax.broadcasted_iota(jnp.int32, (tm + 2 * HALO, 1), 0)
    valid = (pos >= seq_start) & (pos < seq_end)
    he = _rms(xe, g_ref[...], NORM_EPS).astype(BF16)
    proj = _dot(he, win_ref[...])
    c0 = POOL_WIDTH
    u_ref[...] = jnp.where(valid, proj[:, :c0], 0.0)
    z_ref[...] = jnp.where(valid, proj[:, c0 + 2 * CONV_WIDTH_CH:] * proj[:, c0:c0 + CONV_WIDTH_CH], 0.0)

    rel = start - seq_start + lax.broadcasted_iota(jnp.int32, (tm, 1), 0)
    seq_len = seq_end - seq_start
    ys = []
    for g, w in enumerate(POOL_WINDOWS):
        cols = slice(g * POOL_GROUP_DIM, (g + 1) * POOL_GROUP_DIM)
        tot = u_ref[HALO - w // 2:HALO - w // 2 + tm, cols]
        for o in range(-w // 2 + 1, w // 2):
            tot = tot + u_ref[HALO + o:HALO + o + tm, cols]
        cnt = (jnp.minimum(rel + w // 2, seq_len) - jnp.maximum(rel - w // 2, 0)).astype(F32)
        d = tot / cnt - u_ref[HALO:HALO + tm, cols]
        ys.append(_dot(d.astype(BF16), pw_ref[g]))
    a_out = jnp.concatenate(ys, axis=-1) * ps_ref[...]

    conv = (cw_ref[0:1, :] * z_ref[HALO - 1:HALO - 1 + tm, :]
            + cw_ref[1:2, :] * z_ref[HALO:HALO + tm, :]
            + cw_ref[2:3, :] * z_ref[HALO + 1:HALO + 1 + tm, :])
    b_out = proj[HALO:HALO + tm, c0 + CONV_WIDTH_CH:c0 + 2 * CONV_WIDTH_CH] * conv
    mixed = jnp.concatenate([a_out, b_out], axis=-1).astype(BF16)
    o_ref[...] = x + _dot(mixed, wout_ref[...])


def _mixer(x_parts, g, win, pw, ps, cw, wout, bounds):
    d = x_parts[0].shape[1]
    part_rows = tuple(p.shape[0] for p in x_parts)
    t = sum(part_rows)
    tm = TOKEN_TILE
    per = tm // HALO
    last = t // HALO - 1
    return pl.pallas_call(
        functools.partial(_mixer_kernel, bounds=bounds, part_rows=part_rows),
        grid=(t // tm,),
        in_specs=[
            *_part_specs(x_parts, HALO, lambda i: jnp.maximum(i * per - 1, 0)),
            *_part_specs(x_parts, tm, lambda i: i),
            *_part_specs(x_parts, HALO, lambda i: jnp.minimum((i + 1) * per, last)),
            _const_spec((1, d)),
            _const_spec(win.shape),
            _const_spec(pw.shape),
            _const_spec((1, POOL_WIDTH)),
            _const_spec(cw.shape),
            _const_spec(wout.shape),
        ],
        out_specs=pl.BlockSpec((tm, d), lambda i: (i, 0)),
        out_shape=jax.ShapeDtypeStruct((t, d), F32),
        scratch_shapes=[
            pltpu.VMEM((tm + 2 * HALO, POOL_WIDTH), F32),
            pltpu.VMEM((tm + 2 * HALO, CONV_WIDTH_CH), F32),
        ],
        compiler_params=_params(1),
        name="even_mixer",
    )(*x_parts, *x_parts, *x_parts, g.reshape(1, d), win, pw, ps.reshape(1, POOL_WIDTH), cw, wout)


def _mlp_kernel(*refs, attn_rows, has_final):
    refs = list(refs)
    x_ref = refs.pop(0)
    if attn_rows:
        a_refs = [refs.pop(0) for _ in attn_rows]
        wo_ref = refs.pop(0)
    g_ref, w1_ref, w2_ref = refs[:3]
    refs = refs[3:]
    if has_final:
        fg_ref = refs.pop(0)
    o_ref = refs.pop(0)

    x = x_ref[...]
    if attn_rows:
        attn = _pick_part(a_refs, pl.program_id(0), attn_rows, x_ref.shape[0])
        x = x + _dot(attn, wo_ref[...])
    h = _rms(x, g_ref[...], NORM_EPS).astype(BF16)
    acc = x
    for c in range(D_FF // FF_CHUNK):
        cols = slice(c * FF_CHUNK, (c + 1) * FF_CHUNK)
        a = jnp.maximum(_dot(h, w1_ref[:, cols]), 0.0)
        acc = acc + _dot((a * a).astype(BF16), w2_ref[cols, :])
    if has_final:
        acc = _rms(acc, fg_ref[...], NORM_EPS)
    o_ref[...] = acc


def _mlp(x, g, w1, w2, attn_parts=(), attn_w=None, final_g=None, row_start=0, rows=None):
    d = x.shape[1]
    rows = x.shape[0] if rows is None else rows
    tm = TOKEN_TILE
    tile0 = row_start // tm
    row_spec = pl.BlockSpec((tm, d), lambda i: (i, 0))
    args = [x]
    in_specs = [pl.BlockSpec((tm, d), lambda i: (tile0 + i, 0))]
    if attn_parts:
        args += [*attn_parts, attn_w]
        in_specs += [*_part_specs(attn_parts, tm, lambda i: i), _const_spec(attn_w.shape)]
    args += [g.reshape(1, d), w1, w2]
    in_specs += [_const_spec((1, d)), _const_spec(w1.shape), _const_spec(w2.shape)]
    if final_g is not None:
        args.append(final_g.reshape(1, d))
        in_specs.append(_const_spec((1, d)))
    return pl.pallas_call(
        functools.partial(_mlp_kernel, attn_rows=tuple(p.shape[0] for p in attn_parts),
                          has_final=final_g is not None),
        grid=(rows // tm,),
        in_specs=in_specs,
        out_specs=row_spec,
        out_shape=jax.ShapeDtypeStruct((rows, d), F32),
        compiler_params=_params(1),
        name="sq_relu_mlp",
    )(*args)


def _qkv_kernel(x_ref, g_ref, w_ref, qt_ref, kk_ref, vt_ref, nrm_ref):
    tm = x_ref.shape[0]
    h = _rms(x_ref[...], g_ref[...], NORM_EPS).astype(BF16)
    qkv = _dot(h, w_ref[...])
    q_all = (qkv[:, :D_MODEL] * (ATTN_SCALE * LOG2E)).astype(BF16).astype(F32)
    k_all = qkv[:, D_MODEL:2 * D_MODEL].astype(BF16).astype(F32)

    sel = (lax.shift_right_logical(lax.broadcasted_iota(jnp.int32, (D_MODEL, V_DIM), 0),
                                   HEAD_DIM.bit_length() - 1)
           == lax.broadcasted_iota(jnp.int32, (D_MODEL, V_DIM), 1)).astype(BF16)

    def max_sq_norm(a):
        return jnp.max(_dot((a * a).astype(BF16), sel), axis=0, keepdims=True)

    nrm_ref[0] = jnp.concatenate(
        [max_sq_norm(q_all), max_sq_norm(k_all), jnp.zeros((6, V_DIM), F32)], axis=0)

    n_p = len(LOG2E_PIECES)

    def extras(slot, lo, hi, pos_first):
        pos_slot = slot if pos_first else slot - 2 * n_p
        const_slot = slot - 2 * n_p if pos_first else slot
        out = jnp.where((pos_slot >= 0) & (pos_slot < n_p), lo,
                        jnp.where((pos_slot >= n_p) & (pos_slot < 2 * n_p), hi, 0.0))
        for p, piece in enumerate(LOG2E_PIECES):
            out = jnp.where((const_slot == p) | (const_slot == n_p + p), piece, out)
        return out

    q_row = lax.broadcasted_iota(jnp.int32, (HEAD_DIM, tm), 0)
    q_tok = lax.broadcasted_iota(jnp.int32, (HEAD_DIM, tm), 1)
    q_base = extras(q_row, (q_tok & POS_LO_MASK).astype(F32), (q_tok & POS_HI_MASK).astype(F32), True)
    q_has_slope = q_row >= 2 * n_p
    k_lane = lax.broadcasted_iota(jnp.int32, (tm, V_DIM), 1)
    k_tok = lax.broadcasted_iota(jnp.int32, (tm, V_DIM), 0)
    k_lo = -(k_tok & POS_LO_MASK).astype(F32)
    k_hi = -(k_tok & POS_HI_MASK).astype(F32)
    k_base = [extras(k_lane - off, k_lo, k_hi, False) for off in (HEAD_DIM, 0)]
    k_has_slope = [(k_lane >= off) & (k_lane < off + 2 * n_p) for off in (HEAD_DIM, 0)]
    denom_rows = (lax.broadcasted_iota(jnp.int32, (DENOM_ROWS, tm), 0) == 0).astype(F32)

    for hd in range(N_HEADS):
        slope = 2.0 ** (-8.0 * (hd + 1) / N_HEADS)
        cols = slice(hd * V_DIM, (hd + 1) * V_DIM)
        q_t = q_all[:, cols].T
        top, bot = q_t[:HEAD_DIM], q_t[HEAD_DIM:]
        q_extra = jnp.where(q_has_slope, slope * q_base, q_base)
        qt_ref[hd, 0, 0] = jnp.concatenate([top, q_extra], axis=0).astype(BF16)
        qt_ref[hd, 0, 1] = jnp.concatenate([top, -q_extra], axis=0).astype(BF16)
        qt_ref[hd, 0, 2] = jnp.concatenate([q_extra, bot], axis=0).astype(BF16)
        qt_ref[hd, 0, 3] = jnp.concatenate([-q_extra, bot], axis=0).astype(BF16)

        kh = k_all[:, cols]
        k_extra = [jnp.where(k_has_slope[m], slope * k_base[m], k_base[m]) for m in range(2)]
        kk_ref[hd, 0, 0] = jnp.where(k_lane < HEAD_DIM, kh, k_extra[0]).astype(BF16)
        kk_ref[hd, 1, 0] = jnp.where(k_lane >= HEAD_DIM, kh, k_extra[1]).astype(BF16)

        vh = qkv[:, 2 * D_MODEL + hd * V_DIM:2 * D_MODEL + (hd + 1) * V_DIM]
        vt_ref[hd, 0] = jnp.concatenate([vh.T, denom_rows], axis=0).astype(BF16)


def _qkv(x, g, w):
    t, d = x.shape
    tm = TOKEN_TILE
    nc = t // tm
    return pl.pallas_call(
        _qkv_kernel,
        grid=(nc,),
        in_specs=[
            pl.BlockSpec((tm, d), lambda i: (i, 0)),
            _const_spec((1, d)),
            _const_spec(w.shape),
        ],
        out_specs=[
            pl.BlockSpec((N_HEADS, 1, 4, V_DIM, tm), lambda i: (0, i, 0, 0, 0)),
            pl.BlockSpec((N_HEADS, 2, 1, tm, V_DIM), lambda i: (0, 0, i, 0, 0)),
            pl.BlockSpec((N_HEADS, 1, V_ROWS, tm), lambda i: (0, i, 0, 0)),
            pl.BlockSpec((1, 8, V_DIM), lambda i: (i, 0, 0)),
        ],
        out_shape=[
            jax.ShapeDtypeStruct((N_HEADS, nc, 4, V_DIM, tm), BF16),
            jax.ShapeDtypeStruct((N_HEADS, 2, nc, tm, V_DIM), BF16),
            jax.ShapeDtypeStruct((N_HEADS, nc, V_ROWS, tm), BF16),
            jax.ShapeDtypeStruct((nc, 8, V_DIM), F32),
        ],
        compiler_params=_params(1),
        name="attn_qkv",
    )(x, g.reshape(1, d), w)


def _flash_kernel(slope_ref, qn_ref, kn_ref, lq1_ref, lk1_ref, lq2_ref, lk2_ref, sg_ref,
                  qt_ref, kk_ref, vt_ref, o_ref, s_ref, mc_ref, m_ref, acc_ref,
                  *, lambda_init, blk0):
    n_chunks, tkc = kk_ref.shape[1], kk_ref.shape[2]
    n_sub, tq = qt_ref.shape[0], qt_ref.shape[-1]
    hd = pl.program_id(1)
    slope, inv_slope = slope_ref[0, hd], slope_ref[1, hd]
    cq0 = pl.program_id(2) * n_sub
    chunk0 = blk0 + pl.program_id(0) * n_chunks

    def scores(sub, c, sign, mp):
        return _dot(kk_ref[mp, c], qt_ref[sub, 2 * mp + sign])

    def issue_chunk(sub, c, sign, mp):
        s = scores(sub, c, sign, mp)
        s_ref[sub, mp] = s
        mc_ref[sub, mp] = jnp.max(s, axis=0, keepdims=True)

    def absorb(sub, cq, c, mp):
        shift = -slope * (tkc * jnp.abs(cq - c)).astype(F32)
        m_old = m_ref[sub, mp]
        m_new = jnp.maximum(m_old, mc_ref[sub, mp] + shift)
        p = jnp.exp2(s_ref[sub, mp] - (m_new - shift)).astype(BF16)
        acc_ref[sub, mp] = jnp.exp2(m_old - m_new) * acc_ref[sub, mp] + _dot(vt_ref[c], p)
        m_ref[sub, mp] = m_new

    jj = lax.broadcasted_iota(jnp.int32, (tkc, tq), 0)
    ii = lax.broadcasted_iota(jnp.int32, (tkc, tq), 1)
    fixup = (-2.0 * slope) * jnp.maximum(ii - jj, 0).astype(F32)
    diag = [[scores(sub, cq0 + sub, 0, mp) + fixup for mp in range(2)] for sub in range(n_sub)]
    for sub in range(n_sub):
        cq = cq0 + sub
        issue_chunk(sub, jnp.where(cq > 0, cq - 1, jnp.minimum(cq + 1, n_chunks - 1)),
                    (cq > 0).astype(jnp.int32), 0)
    for sub in range(n_sub):
        for mp in range(2):
            m_first = jnp.max(diag[sub][mp], axis=0, keepdims=True)
            m_ref[sub, mp] = m_first
            acc_ref[sub, mp] = _dot(vt_ref[cq0 + sub],
                                    jnp.exp2(diag[sub][mp] - m_first).astype(BF16))

    plans = []
    for sub in range(n_sub):
        cq = cq0 + sub
        gap = (NORM_SLACK * qn_ref[hd, chunk0 + cq] * kn_ref[hd, pl.program_id(0)]
               + EXP_ZERO_BELOW - jnp.min(m_ref[sub]))
        reach = (gap * inv_slope - 1.0) * (1.0 / tkc)
        n_dist = jnp.where(reach >= 1.0,
                           jnp.minimum(reach, float(n_chunks)).astype(jnp.int32) + 1, 1)
        n_left = jnp.minimum(n_dist, cq)
        plans.append((n_left, n_left + jnp.minimum(n_dist, n_chunks - 1 - cq)))

    def visit(sub, carry):
        cq = cq0 + sub
        n_left, total = plans[0]
        for other, plan in enumerate(plans[1:], 1):
            n_left, total = (jnp.where(sub == other, new, old) for new, old in zip(plan, (n_left, total)))

        def item(t):
            left = t < n_left
            c = jnp.where(left, cq - 1 - t, cq + 1 + t - n_left)
            return jnp.clip(c, 0, n_chunks - 1), left.astype(jnp.int32)

        def trip(t):
            c, sign = item(t)
            issue_chunk(sub, c, sign, 1)
            absorb(sub, cq, c, 0)
            issue_chunk(sub, *item(t + 1), 0)
            absorb(sub, cq, c, 1)

        def trips(t0, count):
            for r in range(count):
                trip(t0 + r)

        assert PIPE_UNROLL & (PIPE_UNROLL - 1) == 0
        rem = total & (PIPE_UNROLL - 1)
        done, size = 0, 1
        while size < PIPE_UNROLL:
            pl.when((total & size) == size)(functools.partial(trips, done, size))
            done = done + (total & size)
            size *= 2

        def group(u, inner):
            trips(rem + PIPE_UNROLL * u, PIPE_UNROLL)
            return inner

        lax.fori_loop(0, total // PIPE_UNROLL, group, 0)
        return carry

    lax.fori_loop(0, n_sub, visit, 0)

    lam = (jnp.exp(jnp.sum(lq1_ref[...] * lk1_ref[...], keepdims=True))
           - jnp.exp(jnp.sum(lq2_ref[...] * lk2_ref[...], keepdims=True)) + lambda_init)
    for sub in range(n_sub):
        o_t = (acc_ref[sub, 0, :V_DIM] * (1.0 / acc_ref[sub, 0, V_DIM:V_DIM + 1])
               - (lam / acc_ref[sub, 1, V_DIM:V_DIM + 1]) * acc_ref[sub, 1, :V_DIM])
        scale = (lax.rsqrt(jnp.mean(o_t * o_t, axis=0, keepdims=True) + SUBLN_EPS)
                 * (1.0 - lambda_init))
        o_ref[sub * tq:(sub + 1) * tq, :] = (o_t * scale * sg_ref[...]).T.astype(o_ref.dtype)


def _flash(slopes, qn, kn, lam_vecs, subln_g, qt, kk, vt, *, tok_start, n_seq, seq_len,
           lambda_init):
    tq = TOKEN_TILE
    ns = Q_BLOCKS_PER_STEP
    nq = seq_len // tq
    blk0 = tok_start // tq
    seq0 = tok_start // seq_len
    steps = nq // ns
    step0 = blk0 // ns
    assert nq % ns == 0 and blk0 % ns == 0
    smem = pl.BlockSpec(memory_space=pltpu.SMEM)
    vec = _const_spec((1, HEAD_DIM))
    return pl.pallas_call(
        functools.partial(_flash_kernel, lambda_init=lambda_init, blk0=blk0),
        grid=(n_seq, N_HEADS, steps),
        in_specs=[
            smem, smem, smem, vec, vec, vec, vec, _const_spec((V_DIM, tq)),
            pl.BlockSpec((None, ns, 4, V_DIM, tq),
                         lambda b, h, i: (h, step0 + b * steps + i, 0, 0, 0)),
            pl.BlockSpec((None, 2, nq, tq, V_DIM), lambda b, h, i: (h, 0, seq0 + b, 0, 0)),
            pl.BlockSpec((None, nq, V_ROWS, tq), lambda b, h, i: (h, seq0 + b, 0, 0)),
        ],
        out_specs=pl.BlockSpec((ns * tq, V_DIM), lambda b, h, i: (b * steps + i, h)),
        out_shape=jax.ShapeDtypeStruct((n_seq * seq_len, N_HEADS * V_DIM), BF16),
        scratch_shapes=[
            pltpu.VMEM((ns, 2, tq, tq), F32),
            pltpu.VMEM((ns, 2, 1, tq), F32),
            pltpu.VMEM((ns, 2, 1, tq), F32),
            pltpu.VMEM((ns, 2, V_ROWS, tq), F32),
        ],
        compiler_params=_params(3),
        name="diff_flash",
    )(slopes, qn, kn[:, blk0:blk0 + n_seq * nq].reshape(N_HEADS, n_seq, nq).max(axis=-1),
      *lam_vecs, jnp.broadcast_to(subln_g.reshape(V_DIM, 1), (V_DIM, tq)), qt, kk, vt)


def kernel(x_prompt, x_sample, norm1_g, norm2_g, final_g, mix_in_w, pool_w, pool_scale, conv_w,
           mix_out_w, attn_qkv_w, attn_out_w, lambda_q1, lambda_k1, lambda_q2, lambda_k2, subln_g,
           mlp_w1, mlp_w2):
    pb, ps, d = x_prompt.shape
    sb, ss, _ = x_sample.shape
    n_prompt = pb * ps
    n_sample = sb * ss
    x_parts = [x_prompt.reshape(n_prompt, d), x_sample.reshape(n_sample, d)]
    bounds = tuple(ps * b for b in range(pb)) + tuple(n_prompt + ss * b for b in range(sb + 1))
    slopes = jnp.exp2(-8.0 * (jnp.arange(N_HEADS, dtype=F32) + 1.0) / N_HEADS)
    slopes = jnp.stack([slopes * LOG2E, 1.0 / (slopes * LOG2E)])

    for i in range(DEPTH):
        j = i // 2
        w1 = mlp_w1[i].astype(BF16)
        w2 = mlp_w2[i].astype(BF16)
        attn_parts, attn_w = (), None
        if i % 2 == 0:
            x = _mixer(x_parts, norm1_g[i], mix_in_w[j].astype(BF16), pool_w[j].astype(BF16),
                       pool_scale[j], conv_w[j], mix_out_w[j].astype(BF16), bounds)
        else:
            x = x_parts[0]
            lambda_init = 0.8 - 0.6 * math.exp(-0.3 * i)
            qt, kk, vt, nrm = _qkv(x, norm1_g[i], attn_qkv_w[j].astype(BF16))
            qn, kn = (jnp.sqrt(nrm[:, r, :2 * N_HEADS].reshape(-1, N_HEADS, 2).max(axis=-1)).T
                      for r in range(2))
            lam_vecs = [v[j].reshape(1, HEAD_DIM) for v in (lambda_q1, lambda_k1, lambda_q2, lambda_k2)]
            flash = functools.partial(_flash, slopes, qn, kn, lam_vecs, subln_g[j], qt, kk, vt,
                                      lambda_init=lambda_init)
            attn_parts = (flash(tok_start=0, n_seq=pb, seq_len=ps),
                          flash(tok_start=n_prompt, n_seq=sb, seq_len=ss))
            attn_w = attn_out_w[j].astype(BF16)
        if i < DEPTH - 1:
            x_parts = [_mlp(x, norm2_g[i], w1, w2, attn_parts, attn_w)]
        else:
            y_prompt, y_sample = (
                _mlp(x, norm2_g[i], w1, w2, attn_parts[k:k + 1], attn_w, final_g, r0, nr)
                for k, (r0, nr) in enumerate(((0, n_prompt), (n_prompt, n_sample))))

    return (y_prompt.reshape(pb, ps, d), y_sample.reshape(sb, ss, d))
```

```python
import functools
import math

import jax
import jax.numpy as jnp
import numpy as np
from jax import lax
from jax.experimental import pallas as pl
from jax.experimental.pallas import tpu as pltpu

D_MODEL = 1024
DEPTH = 4
POOL_WIDTH = D_MODEL // 2
N_POOL_GROUPS = 4
POOL_GROUP_DIM = POOL_WIDTH // N_POOL_GROUPS
POOL_WINDOWS = (2, 4, 8, 16)
CONV_WIDTH_CH = D_MODEL // 2
MIX_IN_COLS = POOL_WIDTH + 3 * CONV_WIDTH_CH
N_HEADS = 8
HEAD_DIM = D_MODEL // (2 * N_HEADS)
V_DIM = 2 * HEAD_DIM
ATTN_SCALE = HEAD_DIM ** -0.5
D_FF = 4 * D_MODEL
NORM_EPS = 1e-6
SUBLN_EPS = 1e-5

F32 = jnp.float32
BF16 = jnp.bfloat16

TOKEN_TILE = 512
HALO = 16
FF_CHUNK = 1024
VMEM_LIMIT_BYTES = 48 * 1024 * 1024
ATTN_VMEM_LIMIT_BYTES = 58 * 1024 * 1024
DENOM_ROWS = 16
V_ROWS = V_DIM + DENOM_ROWS
PIPE_UNROLL = 8
Q_BLOCKS_PER_STEP = 4
EXP_ZERO_BELOW = 153.0


def _bf16_pieces(value, n):
    pieces, rest = [], np.float32(value)
    for _ in range(n):
        piece = np.float32(np.asarray(rest, dtype=BF16))
        pieces.append(float(piece))
        rest = np.float32(rest - piece)
    return tuple(pieces)


LOG2E = float(np.float32(math.log2(math.e)))
LOG2E_PIECES = _bf16_pieces(LOG2E, 3)
NORM_SLACK = 1.01
POS_LO_MASK = 255
POS_HI_MASK = TOKEN_TILE - 1 - POS_LO_MASK


def _rms(x, g, eps):
    return x * lax.rsqrt(jnp.mean(x * x, axis=-1, keepdims=True) + eps) * g


def _dot(a, b):
    return jnp.dot(a, b, preferred_element_type=F32)


def _const_spec(shape):
    zeros = (0,) * len(shape)
    return pl.BlockSpec(shape, lambda *_: zeros)


def _part_starts(part_rows, rows):
    starts, s = [], 0
    for r in part_rows:
        starts.append(s // rows)
        s += r
    return starts


def _part_specs(parts, rows, block_of):
    specs = []
    for p, s0 in zip(parts, _part_starts([p.shape[0] for p in parts], rows)):
        nb = p.shape[0] // rows
        specs.append(pl.BlockSpec(
            (rows, p.shape[1]), lambda i, s0=s0, nb=nb: (jnp.clip(block_of(i) - s0, 0, nb - 1), 0)))
    return specs


def _pick_part(refs, block, part_rows, rows):
    value = refs[0][...]
    for ref, s0 in zip(refs[1:], _part_starts(part_rows, rows)[1:]):
        value = jnp.where(block >= s0, ref[...], value)
    return value


def _params(n_axes, vmem_limit_bytes=VMEM_LIMIT_BYTES):
    return pltpu.CompilerParams(
        dimension_semantics=("arbitrary",) * n_axes,
        vmem_limit_bytes=vmem_limit_bytes,
    )


def _mixer_kernel(*refs, bounds, part_rows):
    n = len(part_rows)
    xp_refs, x_refs, xn_refs = refs[:n], refs[n:2 * n], refs[2 * n:3 * n]
    g_ref, win_ref, pw_ref, ps_ref, cw_ref, wout_ref, o_ref, u_ref, z_ref = refs[3 * n:]
    tm = x_refs[0].shape[0]
    i = pl.program_id(0)
    per = tm // HALO
    last = sum(part_rows) // HALO - 1
    start = i * tm
    seq_start = jnp.int32(bounds[0])
    seq_end = jnp.int32(bounds[-1])
    for b in bounds[1:-1]:
        seq_start = jnp.where(start >= b, b, seq_start)
    for b in reversed(bounds[1:-1]):
        seq_end = jnp.where(start < b, b, seq_end)

    x = _pick_part(x_refs, i, part_rows, tm)
    xe = jnp.concatenate([
        _pick_part(xp_refs, jnp.maximum(i * per - 1, 0), part_rows, HALO),
        x,
        _pick_part(xn_refs, jnp.minimum((i + 1) * per, last), part_rows, HALO),
    ], axis=0)
    pos = start - HALO + lax.broadcasted_iota(jnp.int32, (tm + 2 * HALO, 1), 0)
    valid = (pos >= seq_start) & (pos < seq_end)
    he = _rms(xe, g_ref[...], NORM_EPS).astype(BF16)
    proj = _dot(he, win_ref[...])
    c0 = POOL_WIDTH
    u_ref[...] = jnp.where(valid, proj[:, :c0], 0.0)
    z_ref[...] = jnp.where(valid, proj[:, c0 + 2 * CONV_WIDTH_CH:] * proj[:, c0:c0 + CONV_WIDTH_CH], 0.0)

    rel = start - seq_start + lax.broadcasted_iota(jnp.int32, (tm, 1), 0)
    seq_len = seq_end - seq_start
    ys = []
    for g, w in enumerate(POOL_WINDOWS):
        cols = slice(g * POOL_GROUP_DIM, (g + 1) * POOL_GROUP_DIM)
        tot = u_ref[HALO - w // 2:HALO - w // 2 + tm, cols]
        for o in range(-w // 2 + 1, w // 2):
            tot = tot + u_ref[HALO + o:HALO + o + tm, cols]
        cnt = (jnp.minimum(rel + w // 2, seq_len) - jnp.maximum(rel - w // 2, 0)).astype(F32)
        d = tot / cnt - u_ref[HALO:HALO + tm, cols]
        ys.append(_dot(d.astype(BF16), pw_ref[g]))
    a_out = jnp.concatenate(ys, axis=-1) * ps_ref[...]

    conv = (cw_ref[0:1, :] * z_ref[HALO - 1:HALO - 1 + tm, :]
            + cw_ref[1:2, :] * z_ref[HALO:HALO + tm, :]
            + cw_ref[2:3, :] * z_ref[HALO + 1:HALO + 1 + tm, :])
    b_out = proj[HALO:HALO + tm, c0 + CONV_WIDTH_CH:c0 + 2 * CONV_WIDTH_CH] * conv
    mixed = jnp.concatenate([a_out, b_out], axis=-1).astype(BF16)
    o_ref[...] = x + _dot(mixed, wout_ref[...])


def _mixer(x_parts, g, win, pw, ps, cw, wout, bounds):
    d = x_parts[0].shape[1]
    part_rows = tuple(p.shape[0] for p in x_parts)
    t = sum(part_rows)
    tm = TOKEN_TILE
    per = tm // HALO
    last = t // HALO - 1
    return pl.pallas_call(
        functools.partial(_mixer_kernel, bounds=bounds, part_rows=part_rows),
        grid=(t // tm,),
        in_specs=[
            *_part_specs(x_parts, HALO, lambda i: jnp.maximum(i * per - 1, 0)),
            *_part_specs(x_parts, tm, lambda i: i),
            *_part_specs(x_parts, HALO, lambda i: jnp.minimum((i + 1) * per, last)),
            _const_spec((1, d)),
            _const_spec(win.shape),
            _const_spec(pw.shape),
            _const_spec((1, POOL_WIDTH)),
            _const_spec(cw.shape),
            _const_spec(wout.shape),
        ],
        out_specs=pl.BlockSpec((tm, d), lambda i: (i, 0)),
        out_shape=jax.ShapeDtypeStruct((t, d), F32),
        scratch_shapes=[
            pltpu.VMEM((tm + 2 * HALO, POOL_WIDTH), F32),
            pltpu.VMEM((tm + 2 * HALO, CONV_WIDTH_CH), F32),
        ],
        compiler_params=_params(1),
        name="even_mixer",
    )(*x_parts, *x_parts, *x_parts, g.reshape(1, d), win, pw, ps.reshape(1, POOL_WIDTH), cw, wout)


def _mlp_kernel(*refs, attn_rows, has_final):
    refs = list(refs)
    x_ref = refs.pop(0)
    if attn_rows:
        a_refs = [refs.pop(0) for _ in attn_rows]
        wo_ref = refs.pop(0)
    g_ref, w1_ref, w2_ref = refs[:3]
    refs = refs[3:]
    if has_final:
        fg_ref = refs.pop(0)
    o_ref = refs.pop(0)

    x = x_ref[...]
    if attn_rows:
        attn = _pick_part(a_refs, pl.program_id(0), attn_rows, x_ref.shape[0])
        x = x + _dot(attn, wo_ref[...])
    h = _rms(x, g_ref[...], NORM_EPS).astype(BF16)
    acc = x
    for c in range(D_FF // FF_CHUNK):
        cols = slice(c * FF_CHUNK, (c + 1) * FF_CHUNK)
        a = jnp.maximum(_dot(h, w1_ref[:, cols]), 0.0)
        acc = acc + _dot((a * a).astype(BF16), w2_ref[cols, :])
    if has_final:
        acc = _rms(acc, fg_ref[...], NORM_EPS)
    o_ref[...] = acc


def _mlp(x, g, w1, w2, attn_parts=(), attn_w=None, final_g=None, row_start=0, rows=None):
    d = x.shape[1]
    rows = x.shape[0] if rows is None else rows
    tm = TOKEN_TILE
    tile0 = row_start // tm
    row_spec = pl.BlockSpec((tm, d), lambda i: (i, 0))
    args = [x]
    in_specs = [pl.BlockSpec((tm, d), lambda i: (tile0 + i, 0))]
    if attn_parts:
        args += [*attn_parts, attn_w]
        in_specs += [*_part_specs(attn_parts, tm, lambda i: i), _const_spec(attn_w.shape)]
    args += [g.reshape(1, d), w1, w2]
    in_specs += [_const_spec((1, d)), _const_spec(w1.shape), _const_spec(w2.shape)]
    if final_g is not None:
        args.append(final_g.reshape(1, d))
        in_specs.append(_const_spec((1, d)))
    return pl.pallas_call(
        functools.partial(_mlp_kernel, attn_rows=tuple(p.shape[0] for p in attn_parts),
                          has_final=final_g is not None),
        grid=(rows // tm,),
        in_specs=in_specs,
        out_specs=row_spec,
        out_shape=jax.ShapeDtypeStruct((rows, d), F32),
        compiler_params=_params(1),
        name="sq_relu_mlp",
    )(*args)


def _qkv_kernel(x_ref, g_ref, w_ref, qt_ref, kk_ref, vt_ref, nrm_ref):
    tm = x_ref.shape[0]
    h = _rms(x_ref[...], g_ref[...], NORM_EPS).astype(BF16)
    qkv = _dot(h, w_ref[...])
    q_all = (qkv[:, :D_MODEL] * (ATTN_SCALE * LOG2E)).astype(BF16).astype(F32)
    k_all = qkv[:, D_MODEL:2 * D_MODEL].astype(BF16).astype(F32)

    sel = (lax.shift_right_logical(lax.broadcasted_iota(jnp.int32, (D_MODEL, V_DIM), 0),
                                   HEAD_DIM.bit_length() - 1)
           == lax.broadcasted_iota(jnp.int32, (D_MODEL, V_DIM), 1)).astype(BF16)

    def max_sq_norm(a):
        return jnp.max(_dot((a * a).astype(BF16), sel), axis=0, keepdims=True)

    nrm_ref[0] = jnp.concatenate(
        [max_sq_norm(q_all), max_sq_norm(k_all), jnp.zeros((6, V_DIM), F32)], axis=0)

    n_p = len(LOG2E_PIECES)

    def extras(slot, lo, hi, pos_first):
        pos_slot = slot if pos_first else slot - 2 * n_p
        const_slot = slot - 2 * n_p if pos_first else slot
        out = jnp.where((pos_slot >= 0) & (pos_slot < n_p), lo,
                        jnp.where((pos_slot >= n_p) & (pos_slot < 2 * n_p), hi, 0.0))
        for p, piece in enumerate(LOG2E_PIECES):
            out = jnp.where((const_slot == p) | (const_slot == n_p + p), piece, out)
        return out

    q_row = lax.broadcasted_iota(jnp.int32, (HEAD_DIM, tm), 0)
    q_tok = lax.broadcasted_iota(jnp.int32, (HEAD_DIM, tm), 1)
    q_base = extras(q_row, (q_tok & POS_LO_MASK).astype(F32), (q_tok & POS_HI_MASK).astype(F32), True)
    q_has_slope = q_row >= 2 * n_p
    k_lane = lax.broadcasted_iota(jnp.int32, (tm, V_DIM), 1)
    k_tok = lax.broadcasted_iota(jnp.int32, (tm, V_DIM), 0)
    k_lo = -(k_tok & POS_LO_MASK).astype(F32)
    k_hi = -(k_tok & POS_HI_MASK).astype(F32)
    k_base = [extras(k_lane - off, k_lo, k_hi, False) for off in (HEAD_DIM, 0)]
    k_has_slope = [(k_lane >= off) & (k_lane < off + 2 * n_p) for off in (HEAD_DIM, 0)]
    denom_rows = (lax.broadcasted_iota(jnp.int32, (DENOM_ROWS, tm), 0) == 0).astype(F32)

    for hd in range(N_HEADS):
        slope = 2.0 ** (-8.0 * (hd + 1) / N_HEADS)
        cols = slice(hd * V_DIM, (hd + 1) * V_DIM)
        q_t = q_all[:, cols].T
        top, bot = q_t[:HEAD_DIM], q_t[HEAD_DIM:]
        q_extra = jnp.where(q_has_slope, slope * q_base, q_base)
        qt_ref[hd, 0, 0] = jnp.concatenate([top, q_extra], axis=0).astype(BF16)
        qt_ref[hd, 0, 1] = jnp.concatenate([top, -q_extra], axis=0).astype(BF16)
        qt_ref[hd, 0, 2] = jnp.concatenate([q_extra, bot], axis=0).astype(BF16)
        qt_ref[hd, 0, 3] = jnp.concatenate([-q_extra, bot], axis=0).astype(BF16)

        kh = k_all[:, cols]
        k_extra = [jnp.where(k_has_slope[m], slope * k_base[m], k_base[m]) for m in range(2)]
        kk_ref[hd, 0, 0] = jnp.where(k_lane < HEAD_DIM, kh, k_extra[0]).astype(BF16)
        kk_ref[hd, 1, 0] = jnp.where(k_lane >= HEAD_DIM, kh, k_extra[1]).astype(BF16)

        vh = qkv[:, 2 * D_MODEL + hd * V_DIM:2 * D_MODEL + (hd + 1) * V_DIM]
        vt_ref[hd, 0] = jnp.concatenate([vh.T, denom_rows], axis=0).astype(BF16)


def _qkv(x, g, w):
    t, d = x.shape
    tm = TOKEN_TILE
    nc = t // tm
    return pl.pallas_call(
        _qkv_kernel,
        grid=(nc,),
        in_specs=[
            pl.BlockSpec((tm, d), lambda i: (i, 0)),
            _const_spec((1, d)),
            _const_spec(w.shape),
        ],
        out_specs=[
            pl.BlockSpec((N_HEADS, 1, 4, V_DIM, tm), lambda i: (0, i, 0, 0, 0)),
            pl.BlockSpec((N_HEADS, 2, 1, tm, V_DIM), lambda i: (0, 0, i, 0, 0)),
            pl.BlockSpec((N_HEADS, 1, V_ROWS, tm), lambda i: (0, i, 0, 0)),
            pl.BlockSpec((1, 8, V_DIM), lambda i: (i, 0, 0)),
        ],
        out_shape=[
            jax.ShapeDtypeStruct((N_HEADS, nc, 4, V_DIM, tm), BF16),
            jax.ShapeDtypeStruct((N_HEADS, 2, nc, tm, V_DIM), BF16),
            jax.ShapeDtypeStruct((N_HEADS, nc, V_ROWS, tm), BF16),
            jax.ShapeDtypeStruct((nc, 8, V_DIM), F32),
        ],
        compiler_params=_params(1),
        name="attn_qkv",
    )(x, g.reshape(1, d), w)


def _flash_kernel(slope_ref, qn_ref, kn_ref, lq1_ref, lk1_ref, lq2_ref, lk2_ref, sg_ref,
                  qt_ref, kk_ref, vt_ref, o_ref, s_ref, mc_ref, m_ref, acc_ref,
                  *, lambda_init, blk0):
    n_chunks, tkc = kk_ref.shape[1], kk_ref.shape[2]
    n_sub, tq = qt_ref.shape[0], qt_ref.shape[-1]
    hd = pl.program_id(1)
    slope, inv_slope = slope_ref[0, hd], slope_ref[1, hd]
    cq0 = pl.program_id(2) * n_sub
    chunk0 = blk0 + pl.program_id(0) * n_chunks

    def scores(sub, c, sign, mp):
        return _dot(kk_ref[mp, c], qt_ref[sub, 2 * mp + sign])

    def issue_chunk(sub, c, sign, mp):
        s = scores(sub, c, sign, mp)
        s_ref[sub, mp] = s
        mc_ref[sub, mp] = jnp.max(s, axis=0, keepdims=True)

    def absorb(sub, cq, c, mp):
        shift = -slope * (tkc * jnp.abs(cq - c)).astype(F32)
        m_old = m_ref[sub, mp]
        m_new = jnp.maximum(m_old, mc_ref[sub, mp] + shift)
        p = jnp.exp2(s_ref[sub, mp] - (m_new - shift)).astype(BF16)
        acc_ref[sub, mp] = jnp.exp2(m_old - m_new) * acc_ref[sub, mp] + _dot(vt_ref[c], p)
        m_ref[sub, mp] = m_new

    jj = lax.broadcasted_iota(jnp.int32, (tkc, tq), 0)
    ii = lax.broadcasted_iota(jnp.int32, (tkc, tq), 1)
    fixup = (-2.0 * slope) * jnp.maximum(ii - jj, 0).astype(F32)
    diag = [[scores(sub, cq0 + sub, 0, mp) + fixup for mp in range(2)] for sub in range(n_sub)]
    for sub in range(n_sub):
        cq = cq0 + sub
        issue_chunk(sub, jnp.where(cq > 0, cq - 1, jnp.minimum(cq + 1, n_chunks - 1)),
                    (cq > 0).astype(jnp.int32), 0)
    for sub in range(n_sub):
        for mp in range(2):
            m_first = jnp.max(diag[sub][mp], axis=0, keepdims=True)
            m_ref[sub, mp] = m_first
            acc_ref[sub, mp] = _dot(vt_ref[cq0 + sub],
                                    jnp.exp2(diag[sub][mp] - m_first).astype(BF16))

    plans = []
    for sub in range(n_sub):
        cq = cq0 + sub
        gap = (NORM_SLACK * qn_ref[hd, chunk0 + cq] * kn_ref[hd, pl.program_id(0)]
               + EXP_ZERO_BELOW - jnp.min(m_ref[sub]))
        reach = (gap * inv_slope - 1.0) * (1.0 / tkc)
        n_dist = jnp.where(reach >= 1.0,
                           jnp.minimum(reach, float(n_chunks)).astype(jnp.int32) + 1, 1)
        n_left = jnp.minimum(n_dist, cq)
        plans.append((n_left, n_left + jnp.minimum(n_dist, n_chunks - 1 - cq)))

    def visit(sub, carry):
        cq = cq0 + sub
        n_left, total = plans[0]
        for other, plan in enumerate(plans[1:], 1):
            n_left, total = (jnp.where(sub == other, new, old) for new, old in zip(plan, (n_left, total)))

        def item(t):
            left = t < n_left
            c = jnp.where(left, cq - 1 - t, cq + 1 + t - n_left)
            return jnp.clip(c, 0, n_chunks - 1), left.astype(jnp.int32)

        def trip(t):
            c, sign = item(t)
            issue_chunk(sub, c, sign, 1)
            absorb(sub, cq, c, 0)
            issue_chunk(sub, *item(t + 1), 0)
            absorb(sub, cq, c, 1)

        def trips(t0, count):
            for r in range(count):
                trip(t0 + r)

        assert PIPE_UNROLL & (PIPE_UNROLL - 1) == 0
        rem = total & (PIPE_UNROLL - 1)
        done, size = 0, 1
        while size < PIPE_UNROLL:
            pl.when((total & size) == size)(functools.partial(trips, done, size))
            done = done + (total & size)
            size *= 2

        def group(u, inner):
            trips(rem + PIPE_UNROLL * u, PIPE_UNROLL)
            return inner

        lax.fori_loop(0, total // PIPE_UNROLL, group, 0)
        return carry

    lax.fori_loop(0, n_sub, visit, 0)

    lam = (jnp.exp(jnp.sum(lq1_ref[...] * lk1_ref[...], keepdims=True))
           - jnp.exp(jnp.sum(lq2_ref[...] * lk2_ref[...], keepdims=True)) + lambda_init)
    for sub in range(n_sub):
        o_t = (acc_ref[sub, 0, :V_DIM] * (1.0 / acc_ref[sub, 0, V_DIM:V_DIM + 1])
               - (lam / acc_ref[sub, 1, V_DIM:V_DIM + 1]) * acc_ref[sub, 1, :V_DIM])
        scale = (lax.rsqrt(jnp.mean(o_t * o_t, axis=0, keepdims=True) + SUBLN_EPS)
                 * (1.0 - lambda_init))
        o_ref[sub * tq:(sub + 1) * tq, :] = (o_t * scale * sg_ref[...]).T.astype(o_ref.dtype)


def _flash(slopes, qn, kn, lam_vecs, subln_g, qt, kk, vt, *, tok_start, n_seq, seq_len,
           lambda_init):
    tq = TOKEN_TILE
    ns = Q_BLOCKS_PER_STEP
    nq = seq_len // tq
    blk0 = tok_start // tq
    seq0 = tok_start // seq_len
    steps = nq // ns
    step0 = blk0 // ns
    assert nq % ns == 0 and blk0 % ns == 0
    smem = pl.BlockSpec(memory_space=pltpu.SMEM)
    vec = _const_spec((1, HEAD_DIM))
    return pl.pallas_call(
        functools.partial(_flash_kernel, lambda_init=lambda_init, blk0=blk0),
        grid=(n_seq, N_HEADS, steps),
        in_specs=[
            smem, smem, smem, vec, vec, vec, vec, _const_spec((V_DIM, tq)),
            pl.BlockSpec((None, ns, 4, V_DIM, tq),
                         lambda b, h, i: (h, step0 + b * steps + i, 0, 0, 0)),
            pl.BlockSpec((None, 2, nq, tq, V_DIM), lambda b, h, i: (h, 0, seq0 + b, 0, 0)),
            pl.BlockSpec((None, nq, V_ROWS, tq), lambda b, h, i: (h, seq0 + b, 0, 0)),
        ],
        out_specs=pl.BlockSpec((ns * tq, V_DIM), lambda b, h, i: (b * steps + i, h)),
        out_shape=jax.ShapeDtypeStruct((n_seq * seq_len, N_HEADS * V_DIM), BF16),
        scratch_shapes=[
            pltpu.VMEM((ns, 2, tq, tq), F32),
            pltpu.VMEM((ns, 2, 1, tq), F32),
            pltpu.VMEM((ns, 2, 1, tq), F32),
            pltpu.VMEM((ns, 2, V_ROWS, tq), F32),
        ],
        compiler_params=_params(3, ATTN_VMEM_LIMIT_BYTES),
        name="diff_flash",
    )(slopes, qn, kn[:, blk0:blk0 + n_seq * nq].reshape(N_HEADS, n_seq, nq).max(axis=-1),
      *lam_vecs, jnp.broadcast_to(subln_g.reshape(V_DIM, 1), (V_DIM, tq)), qt, kk, vt)


def kernel(x_prompt, x_sample, norm1_g, norm2_g, final_g, mix_in_w, pool_w, pool_scale, conv_w,
           mix_out_w, attn_qkv_w, attn_out_w, lambda_q1, lambda_k1, lambda_q2, lambda_k2, subln_g,
           mlp_w1, mlp_w2):
    pb, ps, d = x_prompt.shape
    sb, ss, _ = x_sample.shape
    n_prompt = pb * ps
    n_sample = sb * ss
    x_parts = [x_prompt.reshape(n_prompt, d), x_sample.reshape(n_sample, d)]
    bounds = tuple(ps * b for b in range(pb)) + tuple(n_prompt + ss * b for b in range(sb + 1))
    slopes = jnp.exp2(-8.0 * (jnp.arange(N_HEADS, dtype=F32) + 1.0) / N_HEADS)
    slopes = jnp.stack([slopes * LOG2E, 1.0 / (slopes * LOG2E)])

    for i in range(DEPTH):
        j = i // 2
        w1 = mlp_w1[i].astype(BF16)
        w2 = mlp_w2[i].astype(BF16)
        attn_parts, attn_w = (), None
        if i % 2 == 0:
            x = _mixer(x_parts, norm1_g[i], mix_in_w[j].astype(BF16), pool_w[j].astype(BF16),
                       pool_scale[j], conv_w[j], mix_out_w[j].astype(BF16), bounds)
        else:
            x = x_parts[0]
            lambda_init = 0.8 - 0.6 * math.exp(-0.3 * i)
            qt, kk, vt, nrm = _qkv(x, norm1_g[i], attn_qkv_w[j].astype(BF16))
            qn, kn = (jnp.sqrt(nrm[:, r, :2 * N_HEADS].reshape(-1, N_HEADS, 2).max(axis=-1)).T
                      for r in range(2))
            lam_vecs = [v[j].reshape(1, HEAD_DIM) for v in (lambda_q1, lambda_k1, lambda_q2, lambda_k2)]
            flash = functools.partial(_flash, slopes, qn, kn, lam_vecs, subln_g[j], qt, kk, vt,
                                      lambda_init=lambda_init)
            attn_parts = (flash(tok_start=0, n_seq=pb, seq_len=ps),
                          flash(tok_start=n_prompt, n_seq=sb, seq_len=ss))
            attn_w = attn_out_w[j].astype(BF16)
        if i < DEPTH - 1:
            x_parts = [_mlp(x, norm2_g[i], w1, w2, attn_parts, attn_w)]
        else:
            y_prompt, y_sample = (
                _mlp(x, norm2_g[i], w1, w2, attn_parts[k:k + 1], attn_w, final_g, r0, nr)
                for k, (r0, nr) in enumerate(((0, n_prompt), (n_prompt, n_sample))))

    return (y_prompt.reshape(pb, ps, d), y_sample.reshape(sb, ss, d))
```

```python
import functools
import math

import jax
import jax.numpy as jnp
import numpy as np
from jax import lax
from jax.experimental import pallas as pl
from jax.experimental.pallas import tpu as pltpu

D_MODEL = 1024
DEPTH = 4
POOL_WIDTH = D_MODEL // 2
N_POOL_GROUPS = 4
POOL_GROUP_DIM = POOL_WIDTH // N_POOL_GROUPS
POOL_WINDOWS = (2, 4, 8, 16)
CONV_WIDTH_CH = D_MODEL // 2
MIX_IN_COLS = POOL_WIDTH + 3 * CONV_WIDTH_CH
N_HEADS = 8
HEAD_DIM = D_MODEL // (2 * N_HEADS)
V_DIM = 2 * HEAD_DIM
ATTN_SCALE = HEAD_DIM ** -0.5
D_FF = 4 * D_MODEL
NORM_EPS = 1e-6
SUBLN_EPS = 1e-5

F32 = jnp.float32
BF16 = jnp.bfloat16

TOKEN_TILE = 512
HALO = 16
FF_CHUNK = 1024
VMEM_LIMIT_BYTES = 48 * 1024 * 1024
DENOM_ROWS = 16
V_ROWS = V_DIM + DENOM_ROWS
PIPE_UNROLL = 8
Q_BLOCKS_PER_STEP = 2
EXP_ZERO_BELOW = 153.0
EXP_SAFE_BELOW = 40.0


def _bf16_pieces(value, n):
    pieces, rest = [], np.float32(value)
    for _ in range(n):
        piece = np.float32(np.asarray(rest, dtype=BF16))
        pieces.append(float(piece))
        rest = np.float32(rest - piece)
    return tuple(pieces)


LOG2E = float(np.float32(math.log2(math.e)))
LOG2E_PIECES = _bf16_pieces(LOG2E, 3)
NORM_SLACK = 1.01
POS_LO_MASK = 255
POS_HI_MASK = TOKEN_TILE - 1 - POS_LO_MASK


def _rms(x, g, eps):
    return x * lax.rsqrt(jnp.mean(x * x, axis=-1, keepdims=True) + eps) * g


def _dot(a, b):
    return jnp.dot(a, b, preferred_element_type=F32)


def _const_spec(shape):
    zeros = (0,) * len(shape)
    return pl.BlockSpec(shape, lambda *_: zeros)


def _part_starts(part_rows, rows):
    starts, s = [], 0
    for r in part_rows:
        starts.append(s // rows)
        s += r
    return starts


def _part_specs(parts, rows, block_of):
    specs = []
    for p, s0 in zip(parts, _part_starts([p.shape[0] for p in parts], rows)):
        nb = p.shape[0] // rows
        specs.append(pl.BlockSpec(
            (rows, p.shape[1]), lambda i, s0=s0, nb=nb: (jnp.clip(block_of(i) - s0, 0, nb - 1), 0)))
    return specs


def _pick_part(refs, block, part_rows, rows):
    value = refs[0][...]
    for ref, s0 in zip(refs[1:], _part_starts(part_rows, rows)[1:]):
        value = jnp.where(block >= s0, ref[...], value)
    return value


def _params(n_axes):
    return pltpu.CompilerParams(
        dimension_semantics=("arbitrary",) * n_axes,
        vmem_limit_bytes=VMEM_LIMIT_BYTES,
    )


def _mixer_kernel(*refs, bounds, part_rows):
    n = len(part_rows)
    xp_refs, x_refs, xn_refs = refs[:n], refs[n:2 * n], refs[2 * n:3 * n]
    g_ref, win_ref, pw_ref, ps_ref, cw_ref, wout_ref, o_ref, u_ref, z_ref = refs[3 * n:]
    tm = x_refs[0].shape[0]
    i = pl.program_id(0)
    per = tm // HALO
    last = sum(part_rows) // HALO - 1
    start = i * tm
    seq_start = jnp.int32(bounds[0])
    seq_end = jnp.int32(bounds[-1])
    for b in bounds[1:-1]:
        seq_start = jnp.where(start >= b, b, seq_start)
    for b in reversed(bounds[1:-1]):
        seq_end = jnp.where(start < b, b, seq_end)

    x = _pick_part(x_refs, i, part_rows, tm)
    xe = jnp.concatenate([
        _pick_part(xp_refs, jnp.maximum(i * per - 1, 0), part_rows, HALO),
        x,
        _pick_part(xn_refs, jnp.minimum((i + 1) * per, last), part_rows, HALO),
    ], axis=0)
    pos = start - HALO + lax.broadcasted_iota(jnp.int32, (tm + 2 * HALO, 1), 0)
    valid = (pos >= seq_start) & (pos < seq_end)
    he = _rms(xe, g_ref[...], NORM_EPS).astype(BF16)
    proj = _dot(he, win_ref[...])
    c0 = POOL_WIDTH
    u_ref[...] = jnp.where(valid, proj[:, :c0], 0.0)
    z_ref[...] = jnp.where(valid, proj[:, c0 + 2 * CONV_WIDTH_CH:] * proj[:, c0:c0 + CONV_WIDTH_CH], 0.0)

    rel = start - seq_start + lax.broadcasted_iota(jnp.int32, (tm, 1), 0)
    seq_len = seq_end - seq_start
    ys = []
    for g, w in enumerate(POOL_WINDOWS):
        cols = slice(g * POOL_GROUP_DIM, (g + 1) * POOL_GROUP_DIM)
        tot = u_ref[HALO - w // 2:HALO - w // 2 + tm, cols]
        for o in range(-w // 2 + 1, w // 2):
            tot = tot + u_ref[HALO + o:HALO + o + tm, cols]
        cnt = (jnp.minimum(rel + w // 2, seq_len) - jnp.maximum(rel - w // 2, 0)).astype(F32)
        d = tot / cnt - u_ref[HALO:HALO + tm, cols]
        ys.append(_dot(d.astype(BF16), pw_ref[g]))
    a_out = jnp.concatenate(ys, axis=-1) * ps_ref[...]

    conv = (cw_ref[0:1, :] * z_ref[HALO - 1:HALO - 1 + tm, :]
            + cw_ref[1:2, :] * z_ref[HALO:HALO + tm, :]
            + cw_ref[2:3, :] * z_ref[HALO + 1:HALO + 1 + tm, :])
    b_out = proj[HALO:HALO + tm, c0 + CONV_WIDTH_CH:c0 + 2 * CONV_WIDTH_CH] * conv
    mixed = jnp.concatenate([a_out, b_out], axis=-1).astype(BF16)
    o_ref[...] = x + _dot(mixed, wout_ref[...])


def _mixer(x_parts, g, win, pw, ps, cw, wout, bounds):
    d = x_parts[0].shape[1]
    part_rows = tuple(p.shape[0] for p in x_parts)
    t = sum(part_rows)
    tm = TOKEN_TILE
    per = tm // HALO
    last = t // HALO - 1
    return pl.pallas_call(
        functools.partial(_mixer_kernel, bounds=bounds, part_rows=part_rows),
        grid=(t // tm,),
        in_specs=[
            *_part_specs(x_parts, HALO, lambda i: jnp.maximum(i * per - 1, 0)),
            *_part_specs(x_parts, tm, lambda i: i),
            *_part_specs(x_parts, HALO, lambda i: jnp.minimum((i + 1) * per, last)),
            _const_spec((1, d)),
            _const_spec(win.shape),
            _const_spec(pw.shape),
            _const_spec((1, POOL_WIDTH)),
            _const_spec(cw.shape),
            _const_spec(wout.shape),
        ],
        out_specs=pl.BlockSpec((tm, d), lambda i: (i, 0)),
        out_shape=jax.ShapeDtypeStruct((t, d), F32),
        scratch_shapes=[
            pltpu.VMEM((tm + 2 * HALO, POOL_WIDTH), F32),
            pltpu.VMEM((tm + 2 * HALO, CONV_WIDTH_CH), F32),
        ],
        compiler_params=_params(1),
        name="even_mixer",
    )(*x_parts, *x_parts, *x_parts, g.reshape(1, d), win, pw, ps.reshape(1, POOL_WIDTH), cw, wout)


def _mlp_kernel(*refs, attn_rows, has_final):
    refs = list(refs)
    x_ref = refs.pop(0)
    if attn_rows:
        a_refs = [refs.pop(0) for _ in attn_rows]
        wo_ref = refs.pop(0)
    g_ref, w1_ref, w2_ref = refs[:3]
    refs = refs[3:]
    if has_final:
        fg_ref = refs.pop(0)
    o_ref = refs.pop(0)

    x = x_ref[...]
    if attn_rows:
        attn = _pick_part(a_refs, pl.program_id(0), attn_rows, x_ref.shape[0])
        x = x + _dot(attn, wo_ref[...])
    h = _rms(x, g_ref[...], NORM_EPS).astype(BF16)
    acc = x
    for c in range(D_FF // FF_CHUNK):
        cols = slice(c * FF_CHUNK, (c + 1) * FF_CHUNK)
        a = jnp.maximum(_dot(h, w1_ref[:, cols]), 0.0)
        acc = acc + _dot((a * a).astype(BF16), w2_ref[cols, :])
    if has_final:
        acc = _rms(acc, fg_ref[...], NORM_EPS)
    o_ref[...] = acc


def _mlp(x, g, w1, w2, attn_parts=(), attn_w=None, final_g=None, row_start=0, rows=None):
    d = x.shape[1]
    rows = x.shape[0] if rows is None else rows
    tm = TOKEN_TILE
    tile0 = row_start // tm
    row_spec = pl.BlockSpec((tm, d), lambda i: (i, 0))
    args = [x]
    in_specs = [pl.BlockSpec((tm, d), lambda i: (tile0 + i, 0))]
    if attn_parts:
        args += [*attn_parts, attn_w]
        in_specs += [*_part_specs(attn_parts, tm, lambda i: i), _const_spec(attn_w.shape)]
    args += [g.reshape(1, d), w1, w2]
    in_specs += [_const_spec((1, d)), _const_spec(w1.shape), _const_spec(w2.shape)]
    if final_g is not None:
        args.append(final_g.reshape(1, d))
        in_specs.append(_const_spec((1, d)))
    return pl.pallas_call(
        functools.partial(_mlp_kernel, attn_rows=tuple(p.shape[0] for p in attn_parts),
                          has_final=final_g is not None),
        grid=(rows // tm,),
        in_specs=in_specs,
        out_specs=row_spec,
        out_shape=jax.ShapeDtypeStruct((rows, d), F32),
        compiler_params=_params(1),
        name="sq_relu_mlp",
    )(*args)


def _qkv_kernel(x_ref, g_ref, w_ref, qt_ref, kk_ref, vt_ref, nrm_ref):
    tm = x_ref.shape[0]
    h = _rms(x_ref[...], g_ref[...], NORM_EPS).astype(BF16)
    qkv = _dot(h, w_ref[...])
    q_all = (qkv[:, :D_MODEL] * (ATTN_SCALE * LOG2E)).astype(BF16).astype(F32)
    k_all = qkv[:, D_MODEL:2 * D_MODEL].astype(BF16).astype(F32)

    sel = (lax.shift_right_logical(lax.broadcasted_iota(jnp.int32, (D_MODEL, V_DIM), 0),
                                   HEAD_DIM.bit_length() - 1)
           == lax.broadcasted_iota(jnp.int32, (D_MODEL, V_DIM), 1)).astype(BF16)

    def max_sq_norm(a):
        return jnp.max(_dot((a * a).astype(BF16), sel), axis=0, keepdims=True)

    nrm_ref[0] = jnp.concatenate(
        [max_sq_norm(q_all), max_sq_norm(k_all), jnp.zeros((6, V_DIM), F32)], axis=0)

    n_p = len(LOG2E_PIECES)

    def extras(slot, lo, hi, pos_first):
        pos_slot = slot if pos_first else slot - 2 * n_p
        const_slot = slot - 2 * n_p if pos_first else slot
        out = jnp.where((pos_slot >= 0) & (pos_slot < n_p), lo,
                        jnp.where((pos_slot >= n_p) & (pos_slot < 2 * n_p), hi, 0.0))
        for p, piece in enumerate(LOG2E_PIECES):
            out = jnp.where((const_slot == p) | (const_slot == n_p + p), piece, out)
        return out

    q_row = lax.broadcasted_iota(jnp.int32, (HEAD_DIM, tm), 0)
    q_tok = lax.broadcasted_iota(jnp.int32, (HEAD_DIM, tm), 1)
    q_base = extras(q_row, (q_tok & POS_LO_MASK).astype(F32), (q_tok & POS_HI_MASK).astype(F32), True)
    q_has_slope = q_row >= 2 * n_p
    k_lane = lax.broadcasted_iota(jnp.int32, (tm, V_DIM), 1)
    k_tok = lax.broadcasted_iota(jnp.int32, (tm, V_DIM), 0)
    k_lo = -(k_tok & POS_LO_MASK).astype(F32)
    k_hi = -(k_tok & POS_HI_MASK).astype(F32)
    k_base = [extras(k_lane - off, k_lo, k_hi, False) for off in (HEAD_DIM, 0)]
    k_has_slope = [(k_lane >= off) & (k_lane < off + 2 * n_p) for off in (HEAD_DIM, 0)]
    denom_rows = (lax.broadcasted_iota(jnp.int32, (DENOM_ROWS, tm), 0) == 0).astype(F32)

    for hd in range(N_HEADS):
        slope = 2.0 ** (-8.0 * (hd + 1) / N_HEADS)
        cols = slice(hd * V_DIM, (hd + 1) * V_DIM)
        q_t = q_all[:, cols].T
        top, bot = q_t[:HEAD_DIM], q_t[HEAD_DIM:]
        q_extra = jnp.where(q_has_slope, slope * q_base, q_base)
        qt_ref[hd, 0, 0] = jnp.concatenate([top, q_extra], axis=0).astype(BF16)
        qt_ref[hd, 0, 1] = jnp.concatenate([top, -q_extra], axis=0).astype(BF16)
        qt_ref[hd, 0, 2] = jnp.concatenate([q_extra, bot], axis=0).astype(BF16)
        qt_ref[hd, 0, 3] = jnp.concatenate([-q_extra, bot], axis=0).astype(BF16)

        kh = k_all[:, cols]
        k_extra = [jnp.where(k_has_slope[m], slope * k_base[m], k_base[m]) for m in range(2)]
        kk_ref[hd, 0, 0] = jnp.where(k_lane < HEAD_DIM, kh, k_extra[0]).astype(BF16)
        kk_ref[hd, 1, 0] = jnp.where(k_lane >= HEAD_DIM, kh, k_extra[1]).astype(BF16)

        vh = qkv[:, 2 * D_MODEL + hd * V_DIM:2 * D_MODEL + (hd + 1) * V_DIM]
        vt_ref[hd, 0] = jnp.concatenate([vh.T, denom_rows], axis=0).astype(BF16)


def _qkv(x, g, w):
    t, d = x.shape
    tm = TOKEN_TILE
    nc = t // tm
    return pl.pallas_call(
        _qkv_kernel,
        grid=(nc,),
        in_specs=[
            pl.BlockSpec((tm, d), lambda i: (i, 0)),
            _const_spec((1, d)),
            _const_spec(w.shape),
        ],
        out_specs=[
            pl.BlockSpec((N_HEADS, 1, 4, V_DIM, tm), lambda i: (0, i, 0, 0, 0)),
            pl.BlockSpec((N_HEADS, 2, 1, tm, V_DIM), lambda i: (0, 0, i, 0, 0)),
            pl.BlockSpec((N_HEADS, 1, V_ROWS, tm), lambda i: (0, i, 0, 0)),
            pl.BlockSpec((1, 8, V_DIM), lambda i: (i, 0, 0)),
        ],
        out_shape=[
            jax.ShapeDtypeStruct((N_HEADS, nc, 4, V_DIM, tm), BF16),
            jax.ShapeDtypeStruct((N_HEADS, 2, nc, tm, V_DIM), BF16),
            jax.ShapeDtypeStruct((N_HEADS, nc, V_ROWS, tm), BF16),
            jax.ShapeDtypeStruct((nc, 8, V_DIM), F32),
        ],
        compiler_params=_params(1),
        name="attn_qkv",
    )(x, g.reshape(1, d), w)


def _flash_kernel(slope_ref, qn_ref, kn_ref, lq1_ref, lk1_ref, lq2_ref, lk2_ref, sg_ref,
                  qt_ref, kk_ref, vt_ref, o_ref, s_ref, mc_ref, m_ref, acc_ref,
                  *, lambda_init, blk0):
    n_chunks, tkc = kk_ref.shape[1], kk_ref.shape[2]
    n_sub, tq = qt_ref.shape[0], qt_ref.shape[-1]
    hd = pl.program_id(1)
    slope, inv_slope = slope_ref[0, hd], slope_ref[1, hd]
    cq0 = pl.program_id(2) * n_sub
    chunk0 = blk0 + pl.program_id(0) * n_chunks

    def scores(sub, c, sign, mp):
        return _dot(kk_ref[mp, c], qt_ref[sub, 2 * mp + sign])

    def issue_chunk(sub, c, sign, mp):
        s = scores(sub, c, sign, mp)
        s_ref[sub, mp] = s
        mc_ref[sub, mp] = jnp.max(s, axis=0, keepdims=True)

    def absorb(sub, cq, c, mp):
        shift = -slope * (tkc * jnp.abs(cq - c)).astype(F32)
        m_old = m_ref[sub, mp]
        m_new = jnp.maximum(m_old, mc_ref[sub, mp] + shift)
        p = jnp.exp2(s_ref[sub, mp] - (m_new - shift)).astype(BF16)
        acc_ref[sub, mp] = jnp.exp2(m_old - m_new) * acc_ref[sub, mp] + _dot(vt_ref[c], p)
        m_ref[sub, mp] = m_new

    jj = lax.broadcasted_iota(jnp.int32, (tkc, tq), 0)
    ii = lax.broadcasted_iota(jnp.int32, (tkc, tq), 1)
    fixup = (-2.0 * slope) * jnp.maximum(ii - jj, 0).astype(F32)
    diag = [[scores(sub, cq0 + sub, 0, mp) + fixup for mp in range(2)] for sub in range(n_sub)]
    for sub in range(n_sub):
        for mp in range(2):
            m_first = jnp.max(diag[sub][mp], axis=0, keepdims=True)
            m_ref[sub, mp] = m_first
            acc_ref[sub, mp] = _dot(vt_ref[cq0 + sub],
                                    jnp.exp2(diag[sub][mp] - m_first).astype(BF16))

    plans = []
    for sub in range(n_sub):
        cq = cq0 + sub
        over = (NORM_SLACK * qn_ref[hd, chunk0 + cq] * kn_ref[hd, pl.program_id(0)]
                - jnp.min(m_ref[sub]))
        reach = ((over + EXP_ZERO_BELOW) * inv_slope - 1.0) * (1.0 / tkc)
        n_dist = jnp.where(reach >= 1.0,
                           jnp.minimum(reach, float(n_chunks)).astype(jnp.int32) + 1, 1)
        n_left = jnp.minimum(n_dist, cq)
        plans.append((n_left, n_left + jnp.minimum(n_dist, n_chunks - 1 - cq),
                      (over < EXP_SAFE_BELOW).astype(jnp.int32)))

    def visit(sub, carry):
        cq = cq0 + sub
        plan = plans[0]
        for other, alt in enumerate(plans[1:], 1):
            plan = tuple(jnp.where(sub == other, new, old) for new, old in zip(alt, plan))
        n_left, total, fixed_ok = plan

        def item(t):
            left = t < n_left
            c = jnp.where(left, cq - 1 - t, cq + 1 + t - n_left)
            return jnp.clip(c, 0, n_chunks - 1), left.astype(jnp.int32)

        def fixed_trip(t):
            c, sign = item(t)
            shift = -slope * (tkc * jnp.abs(cq - c)).astype(F32)
            for mp in range(2):
                p = jnp.exp2(scores(sub, c, sign, mp) + (shift - m_ref[sub, mp])).astype(BF16)
                acc_ref[sub, mp] += _dot(vt_ref[c], p)

        def running_trip(t):
            c, sign = item(t)
            issue_chunk(sub, c, sign, 1)
            absorb(sub, cq, c, 0)
            issue_chunk(sub, *item(t + 1), 0)
            absorb(sub, cq, c, 1)

        def run(trip, unroll):
            assert unroll & (unroll - 1) == 0

            def trips(t0, count):
                for r in range(count):
                    trip(t0 + r)

            rem = total & (unroll - 1)
            done, size = 0, 1
            while size < unroll:
                pl.when((total & size) == size)(functools.partial(trips, done, size))
                done = done + (total & size)
                size *= 2

            def group(u, inner):
                trips(rem + unroll * u, unroll)
                return inner

            lax.fori_loop(0, total // unroll, group, 0)

        @pl.when(fixed_ok == 1)
        def _():
            run(fixed_trip, PIPE_UNROLL)

        @pl.when(fixed_ok == 0)
        def _():
            issue_chunk(sub, *item(0), 0)
            run(running_trip, 1)

        return carry

    lax.fori_loop(0, n_sub, visit, 0)

    lam = (jnp.exp(jnp.sum(lq1_ref[...] * lk1_ref[...], keepdims=True))
           - jnp.exp(jnp.sum(lq2_ref[...] * lk2_ref[...], keepdims=True)) + lambda_init)
    for sub in range(n_sub):
        o_t = (acc_ref[sub, 0, :V_DIM] * (1.0 / acc_ref[sub, 0, V_DIM:V_DIM + 1])
               - (lam / acc_ref[sub, 1, V_DIM:V_DIM + 1]) * acc_ref[sub, 1, :V_DIM])
        scale = (lax.rsqrt(jnp.mean(o_t * o_t, axis=0, keepdims=True) + SUBLN_EPS)
                 * (1.0 - lambda_init))
        o_ref[sub * tq:(sub + 1) * tq, :] = (o_t * scale * sg_ref[...]).T.astype(o_ref.dtype)


def _flash(slopes, qn, kn, lam_vecs, subln_g, qt, kk, vt, *, tok_start, n_seq, seq_len,
           lambda_init):
    tq = TOKEN_TILE
    ns = Q_BLOCKS_PER_STEP
    nq = seq_len // tq
    blk0 = tok_start // tq
    seq0 = tok_start // seq_len
    steps = nq // ns
    step0 = blk0 // ns
    assert nq % ns == 0 and blk0 % ns == 0
    smem = pl.BlockSpec(memory_space=pltpu.SMEM)
    vec = _const_spec((1, HEAD_DIM))
    return pl.pallas_call(
        functools.partial(_flash_kernel, lambda_init=lambda_init, blk0=blk0),
        grid=(n_seq, N_HEADS, steps),
        in_specs=[
            smem, smem, smem, vec, vec, vec, vec, _const_spec((V_DIM, tq)),
            pl.BlockSpec((None, ns, 4, V_DIM, tq),
                         lambda b, h, i: (h, step0 + b * steps + i, 0, 0, 0)),
            pl.BlockSpec((None, 2, nq, tq, V_DIM), lambda b, h, i: (h, 0, seq0 + b, 0, 0)),
            pl.BlockSpec((None, nq, V_ROWS, tq), lambda b, h, i: (h, seq0 + b, 0, 0)),
        ],
        out_specs=pl.BlockSpec((ns * tq, V_DIM), lambda b, h, i: (b * steps + i, h)),
        out_shape=jax.ShapeDtypeStruct((n_seq * seq_len, N_HEADS * V_DIM), BF16),
        scratch_shapes=[
            pltpu.VMEM((ns, 2, tq, tq), F32),
            pltpu.VMEM((ns, 2, 1, tq), F32),
            pltpu.VMEM((ns, 2, 1, tq), F32),
            pltpu.VMEM((ns, 2, V_ROWS, tq), F32),
        ],
        compiler_params=_params(3),
        name="diff_flash",
    )(slopes, qn, kn[:, blk0:blk0 + n_seq * nq].reshape(N_HEADS, n_seq, nq).max(axis=-1),
      *lam_vecs, jnp.broadcast_to(subln_g.reshape(V_DIM, 1), (V_DIM, tq)), qt, kk, vt)


def kernel(x_prompt, x_sample, norm1_g, norm2_g, final_g, mix_in_w, pool_w, pool_scale, conv_w,
           mix_out_w, attn_qkv_w, attn_out_w, lambda_q1, lambda_k1, lambda_q2, lambda_k2, subln_g,
           mlp_w1, mlp_w2):
    pb, ps, d = x_prompt.shape
    sb, ss, _ = x_sample.shape
    n_prompt = pb * ps
    n_sample = sb * ss
    x_parts = [x_prompt.reshape(n_prompt, d), x_sample.reshape(n_sample, d)]
    bounds = tuple(ps * b for b in range(pb)) + tuple(n_prompt + ss * b for b in range(sb + 1))
    slopes = jnp.exp2(-8.0 * (jnp.arange(N_HEADS, dtype=F32) + 1.0) / N_HEADS)
    slopes = jnp.stack([slopes * LOG2E, 1.0 / (slopes * LOG2E)])

    for i in range(DEPTH):
        j = i // 2
        w1 = mlp_w1[i].astype(BF16)
        w2 = mlp_w2[i].astype(BF16)
        attn_parts, attn_w = (), None
        if i % 2 == 0:
            x = _mixer(x_parts, norm1_g[i], mix_in_w[j].astype(BF16), pool_w[j].astype(BF16),
                       pool_scale[j], conv_w[j], mix_out_w[j].astype(BF16), bounds)
        else:
            x = x_parts[0]
            lambda_init = 0.8 - 0.6 * math.exp(-0.3 * i)
            qt, kk, vt, nrm = _qkv(x, norm1_g[i], attn_qkv_w[j].astype(BF16))
            qn, kn = (jnp.sqrt(nrm[:, r, :2 * N_HEADS].reshape(-1, N_HEADS, 2).max(axis=-1)).T
                      for r in range(2))
            lam_vecs = [v[j].reshape(1, HEAD_DIM) for v in (lambda_q1, lambda_k1, lambda_q2, lambda_k2)]
            flash = functools.partial(_flash, slopes, qn, kn, lam_vecs, subln_g[j], qt, kk, vt,
                                      lambda_init=lambda_init)
            attn_parts = (flash(tok_start=0, n_seq=pb, seq_len=ps),
                          flash(tok_start=n_prompt, n_seq=sb, seq_len=ss))
            attn_w = attn_out_w[j].astype(BF16)
        if i < DEPTH - 1:
            x_parts = [_mlp(x, norm2_g[i], w1, w2, attn_parts, attn_w)]
        else:
            y_prompt, y_sample = (
                _mlp(x, norm2_g[i], w1, w2, attn_parts[k:k + 1], attn_w, final_g, r0, nr)
                for k, (r0, nr) in enumerate(((0, n_prompt), (n_prompt, n_sample))))

    return (y_prompt.reshape(pb, ps, d), y_sample.reshape(sb, ss, d))
```

```python
import functools
import math

import jax
import jax.numpy as jnp
import numpy as np
from jax import lax
from jax.experimental import pallas as pl
from jax.experimental.pallas import tpu as pltpu

D_MODEL = 1024
DEPTH = 4
POOL_WIDTH = D_MODEL // 2
N_POOL_GROUPS = 4
POOL_GROUP_DIM = POOL_WIDTH // N_POOL_GROUPS
POOL_WINDOWS = (2, 4, 8, 16)
CONV_WIDTH_CH = D_MODEL // 2
MIX_IN_COLS = POOL_WIDTH + 3 * CONV_WIDTH_CH
N_HEADS = 8
HEAD_DIM = D_MODEL // (2 * N_HEADS)
V_DIM = 2 * HEAD_DIM
ATTN_SCALE = HEAD_DIM ** -0.5
D_FF = 4 * D_MODEL
NORM_EPS = 1e-6
SUBLN_EPS = 1e-5

F32 = jnp.float32
BF16 = jnp.bfloat16

TOKEN_TILE = 512
HALO = 16
FF_CHUNK = 1024
VMEM_LIMIT_BYTES = 48 * 1024 * 1024
DENOM_ROWS = 16
V_ROWS = V_DIM + DENOM_ROWS
PIPE_UNROLL = 8
Q_BLOCKS_PER_STEP = 2
EXP_ZERO_BELOW = 153.0
EXP_SAFE_BELOW = 40.0


def _bf16_pieces(value, n):
    pieces, rest = [], np.float32(value)
    for _ in range(n):
        piece = np.float32(np.asarray(rest, dtype=BF16))
        pieces.append(float(piece))
        rest = np.float32(rest - piece)
    return tuple(pieces)


LOG2E = float(np.float32(math.log2(math.e)))
LOG2E_PIECES = _bf16_pieces(LOG2E, 3)
NORM_SLACK = 1.01
POS_LO_MASK = 255
POS_HI_MASK = TOKEN_TILE - 1 - POS_LO_MASK


def _rms(x, g, eps):
    return x * lax.rsqrt(jnp.mean(x * x, axis=-1, keepdims=True) + eps) * g


def _dot(a, b):
    return jnp.dot(a, b, preferred_element_type=F32)


def _const_spec(shape):
    zeros = (0,) * len(shape)
    return pl.BlockSpec(shape, lambda *_: zeros)


def _part_starts(part_rows, rows):
    starts, s = [], 0
    for r in part_rows:
        starts.append(s // rows)
        s += r
    return starts


def _part_specs(parts, rows, block_of):
    specs = []
    for p, s0 in zip(parts, _part_starts([p.shape[0] for p in parts], rows)):
        nb = p.shape[0] // rows
        specs.append(pl.BlockSpec(
            (rows, p.shape[1]), lambda i, s0=s0, nb=nb: (jnp.clip(block_of(i) - s0, 0, nb - 1), 0)))
    return specs


def _pick_part(refs, block, part_rows, rows):
    value = refs[0][...]
    for ref, s0 in zip(refs[1:], _part_starts(part_rows, rows)[1:]):
        value = jnp.where(block >= s0, ref[...], value)
    return value


def _params(n_axes):
    return pltpu.CompilerParams(
        dimension_semantics=("arbitrary",) * n_axes,
        vmem_limit_bytes=VMEM_LIMIT_BYTES,
    )


def _mixer_kernel(*refs, bounds, part_rows):
    n = len(part_rows)
    xp_refs, x_refs, xn_refs = refs[:n], refs[n:2 * n], refs[2 * n:3 * n]
    g_ref, win_ref, pw_ref, ps_ref, cw_ref, wout_ref, o_ref, u_ref, z_ref = refs[3 * n:]
    tm = x_refs[0].shape[0]
    i = pl.program_id(0)
    per = tm // HALO
    last = sum(part_rows) // HALO - 1
    start = i * tm
    seq_start = jnp.int32(bounds[0])
    seq_end = jnp.int32(bounds[-1])
    for b in bounds[1:-1]:
        seq_start = jnp.where(start >= b, b, seq_start)
    for b in reversed(bounds[1:-1]):
        seq_end = jnp.where(start < b, b, seq_end)

    x = _pick_part(x_refs, i, part_rows, tm)
    xe = jnp.concatenate([
        _pick_part(xp_refs, jnp.maximum(i * per - 1, 0), part_rows, HALO),
        x,
        _pick_part(xn_refs, jnp.minimum((i + 1) * per, last), part_rows, HALO),
    ], axis=0)
    pos = start - HALO + lax.broadcasted_iota(jnp.int32, (tm + 2 * HALO, 1), 0)
    valid = (pos >= seq_start) & (pos < seq_end)
    he = _rms(xe, g_ref[...], NORM_EPS).astype(BF16)
    proj = _dot(he, win_ref[...])
    c0 = POOL_WIDTH
    u_ref[...] = jnp.where(valid, proj[:, :c0], 0.0)
    z_ref[...] = jnp.where(valid, proj[:, c0 + 2 * CONV_WIDTH_CH:] * proj[:, c0:c0 + CONV_WIDTH_CH], 0.0)

    rel = start - seq_start + lax.broadcasted_iota(jnp.int32, (tm, 1), 0)
    seq_len = seq_end - seq_start
    ys = []
    for g, w in enumerate(POOL_WINDOWS):
        cols = slice(g * POOL_GROUP_DIM, (g + 1) * POOL_GROUP_DIM)
        tot = u_ref[HALO - w // 2:HALO - w // 2 + tm, cols]
        for o in range(-w // 2 + 1, w // 2):
            tot = tot + u_ref[HALO + o:HALO + o + tm, cols]
        cnt = (jnp.minimum(rel + w // 2, seq_len) - jnp.maximum(rel - w // 2, 0)).astype(F32)
        d = tot / cnt - u_ref[HALO:HALO + tm, cols]
        ys.append(_dot(d.astype(BF16), pw_ref[g]))
    a_out = jnp.concatenate(ys, axis=-1) * ps_ref[...]

    conv = (cw_ref[0:1, :] * z_ref[HALO - 1:HALO - 1 + tm, :]
            + cw_ref[1:2, :] * z_ref[HALO:HALO + tm, :]
            + cw_ref[2:3, :] * z_ref[HALO + 1:HALO + 1 + tm, :])
    b_out = proj[HALO:HALO + tm, c0 + CONV_WIDTH_CH:c0 + 2 * CONV_WIDTH_CH] * conv
    mixed = jnp.concatenate([a_out, b_out], axis=-1).astype(BF16)
    o_ref[...] = x + _dot(mixed, wout_ref[...])


def _mixer(x_parts, g, win, pw, ps, cw, wout, bounds):
    d = x_parts[0].shape[1]
    part_rows = tuple(p.shape[0] for p in x_parts)
    t = sum(part_rows)
    tm = TOKEN_TILE
    per = tm // HALO
    last = t // HALO - 1
    return pl.pallas_call(
        functools.partial(_mixer_kernel, bounds=bounds, part_rows=part_rows),
        grid=(t // tm,),
        in_specs=[
            *_part_specs(x_parts, HALO, lambda i: jnp.maximum(i * per - 1, 0)),
            *_part_specs(x_parts, tm, lambda i: i),
            *_part_specs(x_parts, HALO, lambda i: jnp.minimum((i + 1) * per, last)),
            _const_spec((1, d)),
            _const_spec(win.shape),
            _const_spec(pw.shape),
            _const_spec((1, POOL_WIDTH)),
            _const_spec(cw.shape),
            _const_spec(wout.shape),
        ],
        out_specs=pl.BlockSpec((tm, d), lambda i: (i, 0)),
        out_shape=jax.ShapeDtypeStruct((t, d), F32),
        scratch_shapes=[
            pltpu.VMEM((tm + 2 * HALO, POOL_WIDTH), F32),
            pltpu.VMEM((tm + 2 * HALO, CONV_WIDTH_CH), F32),
        ],
        compiler_params=_params(1),
        name="even_mixer",
    )(*x_parts, *x_parts, *x_parts, g.reshape(1, d), win, pw, ps.reshape(1, POOL_WIDTH), cw, wout)


def _mlp_kernel(*refs, attn_rows, has_final):
    refs = list(refs)
    x_ref = refs.pop(0)
    if attn_rows:
        a_refs = [refs.pop(0) for _ in attn_rows]
        wo_ref = refs.pop(0)
    g_ref, w1_ref, w2_ref = refs[:3]
    refs = refs[3:]
    if has_final:
        fg_ref = refs.pop(0)
    o_ref = refs.pop(0)

    x = x_ref[...]
    if attn_rows:
        attn = _pick_part(a_refs, pl.program_id(0), attn_rows, x_ref.shape[0])
        x = x + _dot(attn, wo_ref[...])
    h = _rms(x, g_ref[...], NORM_EPS).astype(BF16)
    acc = x
    for c in range(D_FF // FF_CHUNK):
        cols = slice(c * FF_CHUNK, (c + 1) * FF_CHUNK)
        a = jnp.maximum(_dot(h, w1_ref[:, cols]), 0.0)
        acc = acc + _dot((a * a).astype(BF16), w2_ref[cols, :])
    if has_final:
        acc = _rms(acc, fg_ref[...], NORM_EPS)
    o_ref[...] = acc


def _mlp(x, g, w1, w2, attn_parts=(), attn_w=None, final_g=None, row_start=0, rows=None):
    d = x.shape[1]
    rows = x.shape[0] if rows is None else rows
    tm = TOKEN_TILE
    tile0 = row_start // tm
    row_spec = pl.BlockSpec((tm, d), lambda i: (i, 0))
    args = [x]
    in_specs = [pl.BlockSpec((tm, d), lambda i: (tile0 + i, 0))]
    if attn_parts:
        args += [*attn_parts, attn_w]
        in_specs += [*_part_specs(attn_parts, tm, lambda i: i), _const_spec(attn_w.shape)]
    args += [g.reshape(1, d), w1, w2]
    in_specs += [_const_spec((1, d)), _const_spec(w1.shape), _const_spec(w2.shape)]
    if final_g is not None:
        args.append(final_g.reshape(1, d))
        in_specs.append(_const_spec((1, d)))
    return pl.pallas_call(
        functools.partial(_mlp_kernel, attn_rows=tuple(p.shape[0] for p in attn_parts),
                          has_final=final_g is not None),
        grid=(rows // tm,),
        in_specs=in_specs,
        out_specs=row_spec,
        out_shape=jax.ShapeDtypeStruct((rows, d), F32),
        compiler_params=_params(1),
        name="sq_relu_mlp",
    )(*args)


def _qkv_kernel(x_ref, g_ref, w_ref, qt_ref, kk_ref, vt_ref, nrm_ref):
    tm = x_ref.shape[0]
    h = _rms(x_ref[...], g_ref[...], NORM_EPS).astype(BF16)
    qkv = _dot(h, w_ref[...])
    q_all = (qkv[:, :D_MODEL] * (ATTN_SCALE * LOG2E)).astype(BF16).astype(F32)
    k_all = qkv[:, D_MODEL:2 * D_MODEL].astype(BF16).astype(F32)

    sel = (lax.shift_right_logical(lax.broadcasted_iota(jnp.int32, (D_MODEL, V_DIM), 0),
                                   HEAD_DIM.bit_length() - 1)
           == lax.broadcasted_iota(jnp.int32, (D_MODEL, V_DIM), 1)).astype(BF16)

    def max_sq_norm(a):
        return jnp.max(_dot((a * a).astype(BF16), sel), axis=0, keepdims=True)

    nrm_ref[0] = jnp.concatenate(
        [max_sq_norm(q_all), max_sq_norm(k_all), jnp.zeros((6, V_DIM), F32)], axis=0)

    n_p = len(LOG2E_PIECES)

    def extras(slot, lo, hi, pos_first):
        pos_slot = slot if pos_first else slot - 2 * n_p
        const_slot = slot - 2 * n_p if pos_first else slot
        out = jnp.where((pos_slot >= 0) & (pos_slot < n_p), lo,
                        jnp.where((pos_slot >= n_p) & (pos_slot < 2 * n_p), hi, 0.0))
        for p, piece in enumerate(LOG2E_PIECES):
            out = jnp.where((const_slot == p) | (const_slot == n_p + p), piece, out)
        return out

    q_row = lax.broadcasted_iota(jnp.int32, (HEAD_DIM, tm), 0)
    q_tok = lax.broadcasted_iota(jnp.int32, (HEAD_DIM, tm), 1)
    q_base = extras(q_row, (q_tok & POS_LO_MASK).astype(F32), (q_tok & POS_HI_MASK).astype(F32), True)
    q_has_slope = q_row >= 2 * n_p
    k_lane = lax.broadcasted_iota(jnp.int32, (tm, V_DIM), 1)
    k_tok = lax.broadcasted_iota(jnp.int32, (tm, V_DIM), 0)
    k_lo = -(k_tok & POS_LO_MASK).astype(F32)
    k_hi = -(k_tok & POS_HI_MASK).astype(F32)
    k_base = [extras(k_lane - off, k_lo, k_hi, False) for off in (HEAD_DIM, 0)]
    k_has_slope = [(k_lane >= off) & (k_lane < off + 2 * n_p) for off in (HEAD_DIM, 0)]
    denom_rows = (lax.broadcasted_iota(jnp.int32, (DENOM_ROWS, tm), 0) == 0).astype(F32)

    for hd in range(N_HEADS):
        slope = 2.0 ** (-8.0 * (hd + 1) / N_HEADS)
        cols = slice(hd * V_DIM, (hd + 1) * V_DIM)
        q_t = q_all[:, cols].T
        top, bot = q_t[:HEAD_DIM], q_t[HEAD_DIM:]
        q_extra = jnp.where(q_has_slope, slope * q_base, q_base)
        qt_ref[hd, 0, 0] = jnp.concatenate([top, q_extra], axis=0).astype(BF16)
        qt_ref[hd, 0, 1] = jnp.concatenate([top, -q_extra], axis=0).astype(BF16)
        qt_ref[hd, 0, 2] = jnp.concatenate([q_extra, bot], axis=0).astype(BF16)
        qt_ref[hd, 0, 3] = jnp.concatenate([-q_extra, bot], axis=0).astype(BF16)

        kh = k_all[:, cols]
        k_extra = [jnp.where(k_has_slope[m], slope * k_base[m], k_base[m]) for m in range(2)]
        kk_ref[hd, 0, 0] = jnp.where(k_lane < HEAD_DIM, kh, k_extra[0]).astype(BF16)
        kk_ref[hd, 1, 0] = jnp.where(k_lane >= HEAD_DIM, kh, k_extra[1]).astype(BF16)

        vh = qkv[:, 2 * D_MODEL + hd * V_DIM:2 * D_MODEL + (hd + 1) * V_DIM]
        vt_ref[hd, 0] = jnp.concatenate([vh.T, denom_rows], axis=0).astype(BF16)


def _qkv(x, g, w):
    t, d = x.shape
    tm = TOKEN_TILE
    nc = t // tm
    return pl.pallas_call(
        _qkv_kernel,
        grid=(nc,),
        in_specs=[
            pl.BlockSpec((tm, d), lambda i: (i, 0)),
            _const_spec((1, d)),
            _const_spec(w.shape),
        ],
        out_specs=[
            pl.BlockSpec((N_HEADS, 1, 4, V_DIM, tm), lambda i: (0, i, 0, 0, 0)),
            pl.BlockSpec((N_HEADS, 2, 1, tm, V_DIM), lambda i: (0, 0, i, 0, 0)),
            pl.BlockSpec((N_HEADS, 1, V_ROWS, tm), lambda i: (0, i, 0, 0)),
            pl.BlockSpec((1, 8, V_DIM), lambda i: (i, 0, 0)),
        ],
        out_shape=[
            jax.ShapeDtypeStruct((N_HEADS, nc, 4, V_DIM, tm), BF16),
            jax.ShapeDtypeStruct((N_HEADS, 2, nc, tm, V_DIM), BF16),
            jax.ShapeDtypeStruct((N_HEADS, nc, V_ROWS, tm), BF16),
            jax.ShapeDtypeStruct((nc, 8, V_DIM), F32),
        ],
        compiler_params=_params(1),
        name="attn_qkv",
    )(x, g.reshape(1, d), w)


def _flash_kernel(slope_ref, qn_ref, kn_ref, lq1_ref, lk1_ref, lq2_ref, lk2_ref, sg_ref,
                  qt_ref, kk_ref, vt_ref, o_ref, s_ref, mc_ref, m_ref, acc_ref,
                  *, lambda_init, blk0):
    n_chunks, tkc = kk_ref.shape[1], kk_ref.shape[2]
    n_sub, tq = qt_ref.shape[0], qt_ref.shape[-1]
    hd = pl.program_id(1)
    slope, inv_slope = slope_ref[0, hd], slope_ref[1, hd]
    cq0 = pl.program_id(2) * n_sub
    chunk0 = blk0 + pl.program_id(0) * n_chunks

    def scores(sub, c, sign, mp):
        return _dot(kk_ref[mp, c], qt_ref[sub, 2 * mp + sign])

    def issue_chunk(sub, c, sign, mp):
        s = scores(sub, c, sign, mp)
        s_ref[sub, mp] = s
        mc_ref[sub, mp] = jnp.max(s, axis=0, keepdims=True)

    def absorb(sub, cq, c, mp):
        shift = -slope * (tkc * jnp.abs(cq - c)).astype(F32)
        m_old = m_ref[sub, mp]
        m_new = jnp.maximum(m_old, mc_ref[sub, mp] + shift)
        p = jnp.exp2(s_ref[sub, mp] - (m_new - shift)).astype(BF16)
        acc_ref[sub, mp] = jnp.exp2(m_old - m_new) * acc_ref[sub, mp] + _dot(vt_ref[c], p)
        m_ref[sub, mp] = m_new

    jj = lax.broadcasted_iota(jnp.int32, (tkc, tq), 0)
    ii = lax.broadcasted_iota(jnp.int32, (tkc, tq), 1)
    fixup = (-2.0 * slope) * jnp.maximum(ii - jj, 0).astype(F32)
    diag = [[scores(sub, cq0 + sub, 0, mp) + fixup for mp in range(2)] for sub in range(n_sub)]
    for sub in range(n_sub):
        for mp in range(2):
            m_first = jnp.max(diag[sub][mp], axis=0, keepdims=True)
            m_ref[sub, mp] = m_first
            acc_ref[sub, mp] = _dot(vt_ref[cq0 + sub],
                                    jnp.exp2(diag[sub][mp] - m_first).astype(BF16))

    plans = []
    for sub in range(n_sub):
        cq = cq0 + sub
        over = (NORM_SLACK * qn_ref[hd, chunk0 + cq] * kn_ref[hd, pl.program_id(0)]
                - jnp.min(m_ref[sub]))
        reach = ((over + EXP_ZERO_BELOW) * inv_slope - 1.0) * (1.0 / tkc)
        n_dist = jnp.where(reach >= 1.0,
                           jnp.minimum(reach, float(n_chunks)).astype(jnp.int32) + 1, 1)
        n_left = jnp.minimum(n_dist, cq)
        plans.append((n_left, n_left + jnp.minimum(n_dist, n_chunks - 1 - cq),
                      (over < EXP_SAFE_BELOW).astype(jnp.int32)))

    def visit(sub, carry):
        cq = cq0 + sub
        plan = plans[0]
        for other, alt in enumerate(plans[1:], 1):
            plan = tuple(jnp.where(sub == other, new, old) for new, old in zip(alt, plan))
        n_left, total, fixed_ok = plan

        def item(t):
            left = t < n_left
            c = jnp.where(left, cq - 1 - t, cq + 1 + t - n_left)
            return jnp.clip(c, 0, n_chunks - 1), left.astype(jnp.int32)

        def absorb_fixed(c, mp):
            shift = -slope * (tkc * jnp.abs(cq - c)).astype(F32)
            p = jnp.exp2(s_ref[sub, mp] + (shift - m_ref[sub, mp])).astype(BF16)
            acc_ref[sub, mp] += _dot(vt_ref[c], p)

        def fixed_trip(t):
            c, sign = item(t)
            s_ref[sub, 1] = scores(sub, c, sign, 1)
            absorb_fixed(c, 0)
            s_ref[sub, 0] = scores(sub, *item(t + 1), 0)
            absorb_fixed(c, 1)

        def running_trip(t):
            c, sign = item(t)
            issue_chunk(sub, c, sign, 1)
            absorb(sub, cq, c, 0)
            issue_chunk(sub, *item(t + 1), 0)
            absorb(sub, cq, c, 1)

        def run(trip, unroll):
            assert unroll & (unroll - 1) == 0

            def trips(t0, count):
                for r in range(count):
                    trip(t0 + r)

            rem = total & (unroll - 1)
            done, size = 0, 1
            while size < unroll:
                pl.when((total & size) == size)(functools.partial(trips, done, size))
                done = done + (total & size)
                size *= 2

            def group(u, inner):
                trips(rem + unroll * u, unroll)
                return inner

            lax.fori_loop(0, total // unroll, group, 0)

        issue_chunk(sub, *item(0), 0)

        @pl.when(fixed_ok == 1)
        def _():
            run(fixed_trip, PIPE_UNROLL)

        @pl.when(fixed_ok == 0)
        def _():
            run(running_trip, 1)

        return carry

    lax.fori_loop(0, n_sub, visit, 0)

    lam = (jnp.exp(jnp.sum(lq1_ref[...] * lk1_ref[...], keepdims=True))
           - jnp.exp(jnp.sum(lq2_ref[...] * lk2_ref[...], keepdims=True)) + lambda_init)
    for sub in range(n_sub):
        o_t = (acc_ref[sub, 0, :V_DIM] * (1.0 / acc_ref[sub, 0, V_DIM:V_DIM + 1])
               - (lam / acc_ref[sub, 1, V_DIM:V_DIM + 1]) * acc_ref[sub, 1, :V_DIM])
        scale = (lax.rsqrt(jnp.mean(o_t * o_t, axis=0, keepdims=True) + SUBLN_EPS)
                 * (1.0 - lambda_init))
        o_ref[sub * tq:(sub + 1) * tq, :] = (o_t * scale * sg_ref[...]).T.astype(o_ref.dtype)


def _flash(slopes, qn, kn, lam_vecs, subln_g, qt, kk, vt, *, tok_start, n_seq, seq_len,
           lambda_init):
    tq = TOKEN_TILE
    ns = Q_BLOCKS_PER_STEP
    nq = seq_len // tq
    blk0 = tok_start // tq
    seq0 = tok_start // seq_len
    steps = nq // ns
    step0 = blk0 // ns
    assert nq % ns == 0 and blk0 % ns == 0
    smem = pl.BlockSpec(memory_space=pltpu.SMEM)
    vec = _const_spec((1, HEAD_DIM))
    return pl.pallas_call(
        functools.partial(_flash_kernel, lambda_init=lambda_init, blk0=blk0),
        grid=(n_seq, N_HEADS, steps),
        in_specs=[
            smem, smem, smem, vec, vec, vec, vec, _const_spec((V_DIM, tq)),
            pl.BlockSpec((None, ns, 4, V_DIM, tq),
                         lambda b, h, i: (h, step0 + b * steps + i, 0, 0, 0)),
            pl.BlockSpec((None, 2, nq, tq, V_DIM), lambda b, h, i: (h, 0, seq0 + b, 0, 0)),
            pl.BlockSpec((None, nq, V_ROWS, tq), lambda b, h, i: (h, seq0 + b, 0, 0)),
        ],
        out_specs=pl.BlockSpec((ns * tq, V_DIM), lambda b, h, i: (b * steps + i, h)),
        out_shape=jax.ShapeDtypeStruct((n_seq * seq_len, N_HEADS * V_DIM), BF16),
        scratch_shapes=[
            pltpu.VMEM((ns, 2, tq, tq), F32),
            pltpu.VMEM((ns, 2, 1, tq), F32),
            pltpu.VMEM((ns, 2, 1, tq), F32),
            pltpu.VMEM((ns, 2, V_ROWS, tq), F32),
        ],
        compiler_params=_params(3),
        name="diff_flash",
    )(slopes, qn, kn[:, blk0:blk0 + n_seq * nq].reshape(N_HEADS, n_seq, nq).max(axis=-1),
      *lam_vecs, jnp.broadcast_to(subln_g.reshape(V_DIM, 1), (V_DIM, tq)), qt, kk, vt)


def kernel(x_prompt, x_sample, norm1_g, norm2_g, final_g, mix_in_w, pool_w, pool_scale, conv_w,
           mix_out_w, attn_qkv_w, attn_out_w, lambda_q1, lambda_k1, lambda_q2, lambda_k2, subln_g,
           mlp_w1, mlp_w2):
    pb, ps, d = x_prompt.shape
    sb, ss, _ = x_sample.shape
    n_prompt = pb * ps
    n_sample = sb * ss
    x_parts = [x_prompt.reshape(n_prompt, d), x_sample.reshape(n_sample, d)]
    bounds = tuple(ps * b for b in range(pb)) + tuple(n_prompt + ss * b for b in range(sb + 1))
    slopes = jnp.exp2(-8.0 * (jnp.arange(N_HEADS, dtype=F32) + 1.0) / N_HEADS)
    slopes = jnp.stack([slopes * LOG2E, 1.0 / (slopes * LOG2E)])

    for i in range(DEPTH):
        j = i // 2
        w1 = mlp_w1[i].astype(BF16)
        w2 = mlp_w2[i].astype(BF16)
        attn_parts, attn_w = (), None
        if i % 2 == 0:
            x = _mixer(x_parts, norm1_g[i], mix_in_w[j].astype(BF16), pool_w[j].astype(BF16),
                       pool_scale[j], conv_w[j], mix_out_w[j].astype(BF16), bounds)
        else:
            x = x_parts[0]
            lambda_init = 0.8 - 0.6 * math.exp(-0.3 * i)
            qt, kk, vt, nrm = _qkv(x, norm1_g[i], attn_qkv_w[j].astype(BF16))
            qn, kn = (jnp.sqrt(nrm[:, r, :2 * N_HEADS].reshape(-1, N_HEADS, 2).max(axis=-1)).T
                      for r in range(2))
            lam_vecs = [v[j].reshape(1, HEAD_DIM) for v in (lambda_q1, lambda_k1, lambda_q2, lambda_k2)]
            flash = functools.partial(_flash, slopes, qn, kn, lam_vecs, subln_g[j], qt, kk, vt,
                                      lambda_init=lambda_init)
            attn_parts = (flash(tok_start=0, n_seq=pb, seq_len=ps),
                          flash(tok_start=n_prompt, n_seq=sb, seq_len=ss))
            attn_w = attn_out_w[j].astype(BF16)
        if i < DEPTH - 1:
            x_parts = [_mlp(x, norm2_g[i], w1, w2, attn_parts, attn_w)]
        else:
            y_prompt, y_sample = (
                _mlp(x, norm2_g[i], w1, w2, attn_parts[k:k + 1], attn_w, final_g, r0, nr)
                for k, (r0, nr) in enumerate(((0, n_prompt), (n_prompt, n_sample))))

    return (y_prompt.reshape(pb, ps, d), y_sample.reshape(sb, ss, d))
```

```python
import functools
import math

import jax
import jax.numpy as jnp
import numpy as np
from jax import lax
from jax.experimental import pallas as pl
from jax.experimental.pallas import tpu as pltpu

D_MODEL = 1024
DEPTH = 4
POOL_WIDTH = D_MODEL // 2
N_POOL_GROUPS = 4
POOL_GROUP_DIM = POOL_WIDTH // N_POOL_GROUPS
POOL_WINDOWS = (2, 4, 8, 16)
CONV_WIDTH_CH = D_MODEL // 2
MIX_IN_COLS = POOL_WIDTH + 3 * CONV_WIDTH_CH
N_HEADS = 8
HEAD_DIM = D_MODEL // (2 * N_HEADS)
V_DIM = 2 * HEAD_DIM
ATTN_SCALE = HEAD_DIM ** -0.5
D_FF = 4 * D_MODEL
NORM_EPS = 1e-6
SUBLN_EPS = 1e-5

F32 = jnp.float32
BF16 = jnp.bfloat16

TOKEN_TILE = 512
HALO = 16
FF_CHUNK = 1024
VMEM_LIMIT_BYTES = 48 * 1024 * 1024
DENOM_ROWS = 16
V_ROWS = V_DIM + DENOM_ROWS
PIPE_UNROLL = 8
Q_BLOCKS_PER_STEP = 2
EXP_ZERO_BELOW = 153.0
EXP_SAFE_BELOW = 40.0


def _bf16_pieces(value, n):
    pieces, rest = [], np.float32(value)
    for _ in range(n):
        piece = np.float32(np.asarray(rest, dtype=BF16))
        pieces.append(float(piece))
        rest = np.float32(rest - piece)
    return tuple(pieces)


LOG2E = float(np.float32(math.log2(math.e)))
LOG2E_PIECES = _bf16_pieces(LOG2E, 3)
NORM_SLACK = 1.01
POS_LO_MASK = 255
POS_HI_MASK = TOKEN_TILE - 1 - POS_LO_MASK


def _rms(x, g, eps):
    return x * lax.rsqrt(jnp.mean(x * x, axis=-1, keepdims=True) + eps) * g


def _dot(a, b):
    return jnp.dot(a, b, preferred_element_type=F32)


def _const_spec(shape):
    zeros = (0,) * len(shape)
    return pl.BlockSpec(shape, lambda *_: zeros)


def _part_starts(part_rows, rows):
    starts, s = [], 0
    for r in part_rows:
        starts.append(s // rows)
        s += r
    return starts


def _part_specs(parts, rows, block_of):
    specs = []
    for p, s0 in zip(parts, _part_starts([p.shape[0] for p in parts], rows)):
        nb = p.shape[0] // rows
        specs.append(pl.BlockSpec(
            (rows, p.shape[1]), lambda i, s0=s0, nb=nb: (jnp.clip(block_of(i) - s0, 0, nb - 1), 0)))
    return specs


def _pick_part(refs, block, part_rows, rows):
    value = refs[0][...]
    for ref, s0 in zip(refs[1:], _part_starts(part_rows, rows)[1:]):
        value = jnp.where(block >= s0, ref[...], value)
    return value


def _params(n_axes):
    return pltpu.CompilerParams(
        dimension_semantics=("arbitrary",) * n_axes,
        vmem_limit_bytes=VMEM_LIMIT_BYTES,
    )


def _mixer_kernel(*refs, bounds, part_rows):
    n = len(part_rows)
    xp_refs, x_refs, xn_refs = refs[:n], refs[n:2 * n], refs[2 * n:3 * n]
    g_ref, win_ref, pw_ref, ps_ref, cw_ref, wout_ref, o_ref, u_ref, z_ref = refs[3 * n:]
    tm = x_refs[0].shape[0]
    i = pl.program_id(0)
    per = tm // HALO
    last = sum(part_rows) // HALO - 1
    start = i * tm
    seq_start = jnp.int32(bounds[0])
    seq_end = jnp.int32(bounds[-1])
    for b in bounds[1:-1]:
        seq_start = jnp.where(start >= b, b, seq_start)
    for b in reversed(bounds[1:-1]):
        seq_end = jnp.where(start < b, b, seq_end)

    x = _pick_part(x_refs, i, part_rows, tm)
    xe = jnp.concatenate([
        _pick_part(xp_refs, jnp.maximum(i * per - 1, 0), part_rows, HALO),
        x,
        _pick_part(xn_refs, jnp.minimum((i + 1) * per, last), part_rows, HALO),
    ], axis=0)
    pos = start - HALO + lax.broadcasted_iota(jnp.int32, (tm + 2 * HALO, 1), 0)
    valid = (pos >= seq_start) & (pos < seq_end)
    he = _rms(xe, g_ref[...], NORM_EPS).astype(BF16)
    proj = _dot(he, win_ref[...])
    c0 = POOL_WIDTH
    u_ref[...] = jnp.where(valid, proj[:, :c0], 0.0)
    z_ref[...] = jnp.where(valid, proj[:, c0 + 2 * CONV_WIDTH_CH:] * proj[:, c0:c0 + CONV_WIDTH_CH], 0.0)

    rel = start - seq_start + lax.broadcasted_iota(jnp.int32, (tm, 1), 0)
    seq_len = seq_end - seq_start
    ys = []
    for g, w in enumerate(POOL_WINDOWS):
        cols = slice(g * POOL_GROUP_DIM, (g + 1) * POOL_GROUP_DIM)
        tot = u_ref[HALO - w // 2:HALO - w // 2 + tm, cols]
        for o in range(-w // 2 + 1, w // 2):
            tot = tot + u_ref[HALO + o:HALO + o + tm, cols]
        cnt = (jnp.minimum(rel + w // 2, seq_len) - jnp.maximum(rel - w // 2, 0)).astype(F32)
        d = tot / cnt - u_ref[HALO:HALO + tm, cols]
        ys.append(_dot(d.astype(BF16), pw_ref[g]))
    a_out = jnp.concatenate(ys, axis=-1) * ps_ref[...]

    conv = (cw_ref[0:1, :] * z_ref[HALO - 1:HALO - 1 + tm, :]
            + cw_ref[1:2, :] * z_ref[HALO:HALO + tm, :]
            + cw_ref[2:3, :] * z_ref[HALO + 1:HALO + 1 + tm, :])
    b_out = proj[HALO:HALO + tm, c0 + CONV_WIDTH_CH:c0 + 2 * CONV_WIDTH_CH] * conv
    mixed = jnp.concatenate([a_out, b_out], axis=-1).astype(BF16)
    o_ref[...] = x + _dot(mixed, wout_ref[...])


def _mixer(x_parts, g, win, pw, ps, cw, wout, bounds):
    d = x_parts[0].shape[1]
    part_rows = tuple(p.shape[0] for p in x_parts)
    t = sum(part_rows)
    tm = TOKEN_TILE
    per = tm // HALO
    last = t // HALO - 1
    return pl.pallas_call(
        functools.partial(_mixer_kernel, bounds=bounds, part_rows=part_rows),
        grid=(t // tm,),
        in_specs=[
            *_part_specs(x_parts, HALO, lambda i: jnp.maximum(i * per - 1, 0)),
            *_part_specs(x_parts, tm, lambda i: i),
            *_part_specs(x_parts, HALO, lambda i: jnp.minimum((i + 1) * per, last)),
            _const_spec((1, d)),
            _const_spec(win.shape),
            _const_spec(pw.shape),
            _const_spec((1, POOL_WIDTH)),
            _const_spec(cw.shape),
            _const_spec(wout.shape),
        ],
        out_specs=pl.BlockSpec((tm, d), lambda i: (i, 0)),
        out_shape=jax.ShapeDtypeStruct((t, d), F32),
        scratch_shapes=[
            pltpu.VMEM((tm + 2 * HALO, POOL_WIDTH), F32),
            pltpu.VMEM((tm + 2 * HALO, CONV_WIDTH_CH), F32),
        ],
        compiler_params=_params(1),
        name="even_mixer",
    )(*x_parts, *x_parts, *x_parts, g.reshape(1, d), win, pw, ps.reshape(1, POOL_WIDTH), cw, wout)


def _mlp_kernel(*refs, attn_rows, has_final):
    refs = list(refs)
    x_ref = refs.pop(0)
    if attn_rows:
        a_refs = [refs.pop(0) for _ in attn_rows]
        wo_ref = refs.pop(0)
    g_ref, w1_ref, w2_ref = refs[:3]
    refs = refs[3:]
    if has_final:
        fg_ref = refs.pop(0)
    o_ref = refs.pop(0)

    x = x_ref[...]
    if attn_rows:
        attn = _pick_part(a_refs, pl.program_id(0), attn_rows, x_ref.shape[0])
        x = x + _dot(attn, wo_ref[...])
    h = _rms(x, g_ref[...], NORM_EPS).astype(BF16)
    acc = x
    for c in range(D_FF // FF_CHUNK):
        cols = slice(c * FF_CHUNK, (c + 1) * FF_CHUNK)
        a = jnp.maximum(_dot(h, w1_ref[:, cols]), 0.0)
        acc = acc + _dot((a * a).astype(BF16), w2_ref[cols, :])
    if has_final:
        acc = _rms(acc, fg_ref[...], NORM_EPS)
    o_ref[...] = acc


def _mlp(x, g, w1, w2, attn_parts=(), attn_w=None, final_g=None, row_start=0, rows=None):
    d = x.shape[1]
    rows = x.shape[0] if rows is None else rows
    tm = TOKEN_TILE
    tile0 = row_start // tm
    row_spec = pl.BlockSpec((tm, d), lambda i: (i, 0))
    args = [x]
    in_specs = [pl.BlockSpec((tm, d), lambda i: (tile0 + i, 0))]
    if attn_parts:
        args += [*attn_parts, attn_w]
        in_specs += [*_part_specs(attn_parts, tm, lambda i: i), _const_spec(attn_w.shape)]
    args += [g.reshape(1, d), w1, w2]
    in_specs += [_const_spec((1, d)), _const_spec(w1.shape), _const_spec(w2.shape)]
    if final_g is not None:
        args.append(final_g.reshape(1, d))
        in_specs.append(_const_spec((1, d)))
    return pl.pallas_call(
        functools.partial(_mlp_kernel, attn_rows=tuple(p.shape[0] for p in attn_parts),
                          has_final=final_g is not None),
        grid=(rows // tm,),
        in_specs=in_specs,
        out_specs=row_spec,
        out_shape=jax.ShapeDtypeStruct((rows, d), F32),
        compiler_params=_params(1),
        name="sq_relu_mlp",
    )(*args)


def _qkv_kernel(x_ref, g_ref, w_ref, qt_ref, kk_ref, vt_ref, nrm_ref):
    tm = x_ref.shape[0]
    h = _rms(x_ref[...], g_ref[...], NORM_EPS).astype(BF16)
    qkv = _dot(h, w_ref[...])
    q_all = (qkv[:, :D_MODEL] * (ATTN_SCALE * LOG2E)).astype(BF16).astype(F32)
    k_all = qkv[:, D_MODEL:2 * D_MODEL].astype(BF16).astype(F32)

    sel = (lax.shift_right_logical(lax.broadcasted_iota(jnp.int32, (D_MODEL, V_DIM), 0),
                                   HEAD_DIM.bit_length() - 1)
           == lax.broadcasted_iota(jnp.int32, (D_MODEL, V_DIM), 1)).astype(BF16)

    def max_sq_norm(a):
        return jnp.max(_dot((a * a).astype(BF16), sel), axis=0, keepdims=True)

    nrm_ref[0] = jnp.concatenate(
        [max_sq_norm(q_all), max_sq_norm(k_all), jnp.zeros((6, V_DIM), F32)], axis=0)

    n_p = len(LOG2E_PIECES)

    def extras(slot, lo, hi, pos_first):
        pos_slot = slot if pos_first else slot - 2 * n_p
        const_slot = slot - 2 * n_p if pos_first else slot
        out = jnp.where((pos_slot >= 0) & (pos_slot < n_p), lo,
                        jnp.where((pos_slot >= n_p) & (pos_slot < 2 * n_p), hi, 0.0))
        for p, piece in enumerate(LOG2E_PIECES):
            out = jnp.where((const_slot == p) | (const_slot == n_p + p), piece, out)
        return out

    q_row = lax.broadcasted_iota(jnp.int32, (HEAD_DIM, tm), 0)
    q_tok = lax.broadcasted_iota(jnp.int32, (HEAD_DIM, tm), 1)
    q_base = extras(q_row, (q_tok & POS_LO_MASK).astype(F32), (q_tok & POS_HI_MASK).astype(F32), True)
    q_has_slope = q_row >= 2 * n_p
    k_lane = lax.broadcasted_iota(jnp.int32, (tm, V_DIM), 1)
    k_tok = lax.broadcasted_iota(jnp.int32, (tm, V_DIM), 0)
    k_lo = -(k_tok & POS_LO_MASK).astype(F32)
    k_hi = -(k_tok & POS_HI_MASK).astype(F32)
    k_base = [extras(k_lane - off, k_lo, k_hi, False) for off in (HEAD_DIM, 0)]
    k_has_slope = [(k_lane >= off) & (k_lane < off + 2 * n_p) for off in (HEAD_DIM, 0)]
    denom_rows = (lax.broadcasted_iota(jnp.int32, (DENOM_ROWS, tm), 0) == 0).astype(F32)

    for hd in range(N_HEADS):
        slope = 2.0 ** (-8.0 * (hd + 1) / N_HEADS)
        cols = slice(hd * V_DIM, (hd + 1) * V_DIM)
        q_t = q_all[:, cols].T
        top, bot = q_t[:HEAD_DIM], q_t[HEAD_DIM:]
        q_extra = jnp.where(q_has_slope, slope * q_base, q_base)
        qt_ref[hd, 0, 0] = jnp.concatenate([top, q_extra], axis=0).astype(BF16)
        qt_ref[hd, 0, 1] = jnp.concatenate([top, -q_extra], axis=0).astype(BF16)
        qt_ref[hd, 0, 2] = jnp.concatenate([q_extra, bot], axis=0).astype(BF16)
        qt_ref[hd, 0, 3] = jnp.concatenate([-q_extra, bot], axis=0).astype(BF16)

        kh = k_all[:, cols]
        k_extra = [jnp.where(k_has_slope[m], slope * k_base[m], k_base[m]) for m in range(2)]
        kk_ref[hd, 0, 0] = jnp.where(k_lane < HEAD_DIM, kh, k_extra[0]).astype(BF16)
        kk_ref[hd, 1, 0] = jnp.where(k_lane >= HEAD_DIM, kh, k_extra[1]).astype(BF16)

        vh = qkv[:, 2 * D_MODEL + hd * V_DIM:2 * D_MODEL + (hd + 1) * V_DIM]
        vt_ref[hd, 0] = jnp.concatenate([vh.T, denom_rows], axis=0).astype(BF16)


def _qkv(x, g, w):
    t, d = x.shape
    tm = TOKEN_TILE
    nc = t // tm
    return pl.pallas_call(
        _qkv_kernel,
        grid=(nc,),
        in_specs=[
            pl.BlockSpec((tm, d), lambda i: (i, 0)),
            _const_spec((1, d)),
            _const_spec(w.shape),
        ],
        out_specs=[
            pl.BlockSpec((N_HEADS, 1, 4, V_DIM, tm), lambda i: (0, i, 0, 0, 0)),
            pl.BlockSpec((N_HEADS, 2, 1, tm, V_DIM), lambda i: (0, 0, i, 0, 0)),
            pl.BlockSpec((N_HEADS, 1, V_ROWS, tm), lambda i: (0, i, 0, 0)),
            pl.BlockSpec((1, 8, V_DIM), lambda i: (i, 0, 0)),
        ],
        out_shape=[
            jax.ShapeDtypeStruct((N_HEADS, nc, 4, V_DIM, tm), BF16),
            jax.ShapeDtypeStruct((N_HEADS, 2, nc, tm, V_DIM), BF16),
            jax.ShapeDtypeStruct((N_HEADS, nc, V_ROWS, tm), BF16),
            jax.ShapeDtypeStruct((nc, 8, V_DIM), F32),
        ],
        compiler_params=_params(1),
        name="attn_qkv",
    )(x, g.reshape(1, d), w)


def _flash_kernel(slope_ref, qn_ref, kn_ref, lq1_ref, lk1_ref, lq2_ref, lk2_ref, sg_ref,
                  qt_ref, kk_ref, vt_ref, o_ref, s_ref, mc_ref, m_ref, acc_ref,
                  *, lambda_init, blk0):
    n_chunks, tkc = kk_ref.shape[1], kk_ref.shape[2]
    n_sub, tq = qt_ref.shape[0], qt_ref.shape[-1]
    hd = pl.program_id(1)
    slope, inv_slope = slope_ref[0, hd], slope_ref[1, hd]
    cq0 = pl.program_id(2) * n_sub
    chunk0 = blk0 + pl.program_id(0) * n_chunks

    def scores(sub, c, sign, mp):
        return _dot(kk_ref[mp, c], qt_ref[sub, 2 * mp + sign])

    def issue_chunk(sub, c, sign, mp):
        s = scores(sub, c, sign, mp)
        s_ref[sub, mp] = s
        mc_ref[sub, mp] = jnp.max(s, axis=0, keepdims=True)

    def absorb(sub, cq, c, mp):
        shift = -slope * (tkc * jnp.abs(cq - c)).astype(F32)
        m_old = m_ref[sub, mp]
        m_new = jnp.maximum(m_old, mc_ref[sub, mp] + shift)
        p = jnp.exp2(s_ref[sub, mp] - (m_new - shift)).astype(BF16)
        acc_ref[sub, mp] = jnp.exp2(m_old - m_new) * acc_ref[sub, mp] + _dot(vt_ref[c], p)
        m_ref[sub, mp] = m_new

    jj = lax.broadcasted_iota(jnp.int32, (tkc, tq), 0)
    ii = lax.broadcasted_iota(jnp.int32, (tkc, tq), 1)
    fixup = (-2.0 * slope) * jnp.maximum(ii - jj, 0).astype(F32)
    diag = [[scores(sub, cq0 + sub, 0, mp) + fixup for mp in range(2)] for sub in range(n_sub)]
    for sub in range(n_sub):
        cq = cq0 + sub
        issue_chunk(sub, jnp.where(cq > 0, cq - 1, jnp.minimum(cq + 1, n_chunks - 1)),
                    (cq > 0).astype(jnp.int32), 0)
    for sub in range(n_sub):
        for mp in range(2):
            m_first = jnp.max(diag[sub][mp], axis=0, keepdims=True)
            m_ref[sub, mp] = m_first
            acc_ref[sub, mp] = _dot(vt_ref[cq0 + sub],
                                    jnp.exp2(diag[sub][mp] - m_first).astype(BF16))

    plans = []
    for sub in range(n_sub):
        cq = cq0 + sub
        over = (NORM_SLACK * qn_ref[hd, chunk0 + cq] * kn_ref[hd, pl.program_id(0)]
                - jnp.min(m_ref[sub]))
        reach = ((over + EXP_ZERO_BELOW) * inv_slope - 1.0) * (1.0 / tkc)
        n_dist = jnp.where(reach >= 1.0,
                           jnp.minimum(reach, float(n_chunks)).astype(jnp.int32) + 1, 1)
        n_left = jnp.minimum(n_dist, cq)
        plans.append((n_left, n_left + jnp.minimum(n_dist, n_chunks - 1 - cq),
                      (over < EXP_SAFE_BELOW).astype(jnp.int32)))

    def visit(sub, carry):
        cq = cq0 + sub
        plan = plans[0]
        for other, alt in enumerate(plans[1:], 1):
            plan = tuple(jnp.where(sub == other, new, old) for new, old in zip(alt, plan))
        n_left, total, fixed_ok = plan

        def item(t):
            left = t < n_left
            c = jnp.where(left, cq - 1 - t, cq + 1 + t - n_left)
            return jnp.clip(c, 0, n_chunks - 1), left.astype(jnp.int32)

        def absorb_fixed(c, mp):
            shift = -slope * (tkc * jnp.abs(cq - c)).astype(F32)
            p = jnp.exp2(s_ref[sub, mp] + (shift - m_ref[sub, mp])).astype(BF16)
            acc_ref[sub, mp] += _dot(vt_ref[c], p)

        def fixed_trip(t):
            c, sign = item(t)
            s_ref[sub, 1] = scores(sub, c, sign, 1)
            absorb_fixed(c, 0)
            s_ref[sub, 0] = scores(sub, *item(t + 1), 0)
            absorb_fixed(c, 1)

        def running_trip(t):
            c, sign = item(t)
            issue_chunk(sub, c, sign, 1)
            absorb(sub, cq, c, 0)
            issue_chunk(sub, *item(t + 1), 0)
            absorb(sub, cq, c, 1)

        def run(trip, unroll):
            assert unroll & (unroll - 1) == 0

            def trips(t0, count):
                for r in range(count):
                    trip(t0 + r)

            rem = total & (unroll - 1)
            done, size = 0, 1
            while size < unroll:
                pl.when((total & size) == size)(functools.partial(trips, done, size))
                done = done + (total & size)
                size *= 2

            def group(u, inner):
                trips(rem + unroll * u, unroll)
                return inner

            lax.fori_loop(0, total // unroll, group, 0)

        @pl.when(fixed_ok == 1)
        def _():
            run(fixed_trip, PIPE_UNROLL)

        @pl.when(fixed_ok == 0)
        def _():
            run(running_trip, 1)

        return carry

    lax.fori_loop(0, n_sub, visit, 0)

    lam = (jnp.exp(jnp.sum(lq1_ref[...] * lk1_ref[...], keepdims=True))
           - jnp.exp(jnp.sum(lq2_ref[...] * lk2_ref[...], keepdims=True)) + lambda_init)
    for sub in range(n_sub):
        o_t = (acc_ref[sub, 0, :V_DIM] * (1.0 / acc_ref[sub, 0, V_DIM:V_DIM + 1])
               - (lam / acc_ref[sub, 1, V_DIM:V_DIM + 1]) * acc_ref[sub, 1, :V_DIM])
        scale = (lax.rsqrt(jnp.mean(o_t * o_t, axis=0, keepdims=True) + SUBLN_EPS)
                 * (1.0 - lambda_init))
        o_ref[sub * tq:(sub + 1) * tq, :] = (o_t * scale * sg_ref[...]).T.astype(o_ref.dtype)


def _flash(slopes, qn, kn, lam_vecs, subln_g, qt, kk, vt, *, tok_start, n_seq, seq_len,
           lambda_init):
    tq = TOKEN_TILE
    ns = Q_BLOCKS_PER_STEP
    nq = seq_len // tq
    blk0 = tok_start // tq
    seq0 = tok_start // seq_len
    steps = nq // ns
    step0 = blk0 // ns
    assert nq % ns == 0 and blk0 % ns == 0
    smem = pl.BlockSpec(memory_space=pltpu.SMEM)
    vec = _const_spec((1, HEAD_DIM))
    return pl.pallas_call(
        functools.partial(_flash_kernel, lambda_init=lambda_init, blk0=blk0),
        grid=(n_seq, N_HEADS, steps),
        in_specs=[
            smem, smem, smem, vec, vec, vec, vec, _const_spec((V_DIM, tq)),
            pl.BlockSpec((None, ns, 4, V_DIM, tq),
                         lambda b, h, i: (h, step0 + b * steps + i, 0, 0, 0)),
            pl.BlockSpec((None, 2, nq, tq, V_DIM), lambda b, h, i: (h, 0, seq0 + b, 0, 0)),
            pl.BlockSpec((None, nq, V_ROWS, tq), lambda b, h, i: (h, seq0 + b, 0, 0)),
        ],
        out_specs=pl.BlockSpec((ns * tq, V_DIM), lambda b, h, i: (b * steps + i, h)),
        out_shape=jax.ShapeDtypeStruct((n_seq * seq_len, N_HEADS * V_DIM), BF16),
        scratch_shapes=[
            pltpu.VMEM((ns, 2, tq, tq), F32),
            pltpu.VMEM((ns, 2, 1, tq), F32),
            pltpu.VMEM((ns, 2, 1, tq), F32),
            pltpu.VMEM((ns, 2, V_ROWS, tq), F32),
        ],
        compiler_params=_params(3),
        name="diff_flash",
    )(slopes, qn, kn[:, blk0:blk0 + n_seq * nq].reshape(N_HEADS, n_seq, nq).max(axis=-1),
      *lam_vecs, jnp.broadcast_to(subln_g.reshape(V_DIM, 1), (V_DIM, tq)), qt, kk, vt)


def kernel(x_prompt, x_sample, norm1_g, norm2_g, final_g, mix_in_w, pool_w, pool_scale, conv_w,
           mix_out_w, attn_qkv_w, attn_out_w, lambda_q1, lambda_k1, lambda_q2, lambda_k2, subln_g,
           mlp_w1, mlp_w2):
    pb, ps, d = x_prompt.shape
    sb, ss, _ = x_sample.shape
    n_prompt = pb * ps
    n_sample = sb * ss
    x_parts = [x_prompt.reshape(n_prompt, d), x_sample.reshape(n_sample, d)]
    bounds = tuple(ps * b for b in range(pb)) + tuple(n_prompt + ss * b for b in range(sb + 1))
    slopes = jnp.exp2(-8.0 * (jnp.arange(N_HEADS, dtype=F32) + 1.0) / N_HEADS)
    slopes = jnp.stack([slopes * LOG2E, 1.0 / (slopes * LOG2E)])

    for i in range(DEPTH):
        j = i // 2
        w1 = mlp_w1[i].astype(BF16)
        w2 = mlp_w2[i].astype(BF16)
        attn_parts, attn_w = (), None
        if i % 2 == 0:
            x = _mixer(x_parts, norm1_g[i], mix_in_w[j].astype(BF16), pool_w[j].astype(BF16),
                       pool_scale[j], conv_w[j], mix_out_w[j].astype(BF16), bounds)
        else:
            x = x_parts[0]
            lambda_init = 0.8 - 0.6 * math.exp(-0.3 * i)
            qt, kk, vt, nrm = _qkv(x, norm1_g[i], attn_qkv_w[j].astype(BF16))
            qn, kn = (jnp.sqrt(nrm[:, r, :2 * N_HEADS].reshape(-1, N_HEADS, 2).max(axis=-1)).T
                      for r in range(2))
            lam_vecs = [v[j].reshape(1, HEAD_DIM) for v in (lambda_q1, lambda_k1, lambda_q2, lambda_k2)]
            flash = functools.partial(_flash, slopes, qn, kn, lam_vecs, subln_g[j], qt, kk, vt,
                                      lambda_init=lambda_init)
            attn_parts = (flash(tok_start=0, n_seq=pb, seq_len=ps),
                          flash(tok_start=n_prompt, n_seq=sb, seq_len=ss))
            attn_w = attn_out_w[j].astype(BF16)
        if i < DEPTH - 1:
            x_parts = [_mlp(x, norm2_g[i], w1, w2, attn_parts, attn_w)]
        else:
            y_prompt, y_sample = (
                _mlp(x, norm2_g[i], w1, w2, attn_parts[k:k + 1], attn_w, final_g, r0, nr)
                for k, (r0, nr) in enumerate(((0, n_prompt), (n_prompt, n_sample))))

    return (y_prompt.reshape(pb, ps, d), y_sample.reshape(sb, ss, d))
```

```python
import functools
import math

import jax
import jax.numpy as jnp
import numpy as np
from jax import lax
from jax.experimental import pallas as pl
from jax.experimental.pallas import tpu as pltpu

D_MODEL = 1024
DEPTH = 4
POOL_WIDTH = D_MODEL // 2
N_POOL_GROUPS = 4
POOL_GROUP_DIM = POOL_WIDTH // N_POOL_GROUPS
POOL_WINDOWS = (2, 4, 8, 16)
CONV_WIDTH_CH = D_MODEL // 2
MIX_IN_COLS = POOL_WIDTH + 3 * CONV_WIDTH_CH
N_HEADS = 8
HEAD_DIM = D_MODEL // (2 * N_HEADS)
V_DIM = 2 * HEAD_DIM
ATTN_SCALE = HEAD_DIM ** -0.5
D_FF = 4 * D_MODEL
NORM_EPS = 1e-6
SUBLN_EPS = 1e-5

F32 = jnp.float32
BF16 = jnp.bfloat16

TOKEN_TILE = 512
HALO = 16
FF_CHUNK = 1024
VMEM_LIMIT_BYTES = 48 * 1024 * 1024
DENOM_ROWS = 16
V_ROWS = V_DIM + DENOM_ROWS
PIPE_UNROLL = 8
Q_BLOCKS_PER_STEP = 2
EXP_ZERO_BELOW = 153.0
STAB_SLACK_BELOW = 80.0


def _bf16_pieces(value, n):
    pieces, rest = [], np.float32(value)
    for _ in range(n):
        piece = np.float32(np.asarray(rest, dtype=BF16))
        pieces.append(float(piece))
        rest = np.float32(rest - piece)
    return tuple(pieces)


LOG2E = float(np.float32(math.log2(math.e)))
LOG2E_PIECES = _bf16_pieces(LOG2E, 3)
NORM_SLACK = 1.01
POS_LO_MASK = 255
POS_HI_MASK = TOKEN_TILE - 1 - POS_LO_MASK


def _rms(x, g, eps):
    return x * lax.rsqrt(jnp.mean(x * x, axis=-1, keepdims=True) + eps) * g


def _dot(a, b):
    return jnp.dot(a, b, preferred_element_type=F32)


def _const_spec(shape):
    zeros = (0,) * len(shape)
    return pl.BlockSpec(shape, lambda *_: zeros)


def _part_starts(part_rows, rows):
    starts, s = [], 0
    for r in part_rows:
        starts.append(s // rows)
        s += r
    return starts


def _part_specs(parts, rows, block_of):
    specs = []
    for p, s0 in zip(parts, _part_starts([p.shape[0] for p in parts], rows)):
        nb = p.shape[0] // rows
        specs.append(pl.BlockSpec(
            (rows, p.shape[1]), lambda i, s0=s0, nb=nb: (jnp.clip(block_of(i) - s0, 0, nb - 1), 0)))
    return specs


def _pick_part(refs, block, part_rows, rows):
    value = refs[0][...]
    for ref, s0 in zip(refs[1:], _part_starts(part_rows, rows)[1:]):
        value = jnp.where(block >= s0, ref[...], value)
    return value


def _params(n_axes):
    return pltpu.CompilerParams(
        dimension_semantics=("arbitrary",) * n_axes,
        vmem_limit_bytes=VMEM_LIMIT_BYTES,
    )


def _mixer_kernel(*refs, bounds, part_rows):
    n = len(part_rows)
    xp_refs, x_refs, xn_refs = refs[:n], refs[n:2 * n], refs[2 * n:3 * n]
    g_ref, win_ref, pw_ref, ps_ref, cw_ref, wout_ref, o_ref, u_ref, z_ref = refs[3 * n:]
    tm = x_refs[0].shape[0]
    i = pl.program_id(0)
    per = tm // HALO
    last = sum(part_rows) // HALO - 1
    start = i * tm
    seq_start = jnp.int32(bounds[0])
    seq_end = jnp.int32(bounds[-1])
    for b in bounds[1:-1]:
        seq_start = jnp.where(start >= b, b, seq_start)
    for b in reversed(bounds[1:-1]):
        seq_end = jnp.where(start < b, b, seq_end)

    x = _pick_part(x_refs, i, part_rows, tm)
    xe = jnp.concatenate([
        _pick_part(xp_refs, jnp.maximum(i * per - 1, 0), part_rows, HALO),
        x,
        _pick_part(xn_refs, jnp.minimum((i + 1) * per, last), part_rows, HALO),
    ], axis=0)
    pos = start - HALO + lax.broadcasted_iota(jnp.int32, (tm + 2 * HALO, 1), 0)
    valid = (pos >= seq_start) & (pos < seq_end)
    he = _rms(xe, g_ref[...], NORM_EPS).astype(BF16)
    proj = _dot(he, win_ref[...])
    c0 = POOL_WIDTH
    u_ref[...] = jnp.where(valid, proj[:, :c0], 0.0)
    z_ref[...] = jnp.where(valid, proj[:, c0 + 2 * CONV_WIDTH_CH:] * proj[:, c0:c0 + CONV_WIDTH_CH], 0.0)

    rel = start - seq_start + lax.broadcasted_iota(jnp.int32, (tm, 1), 0)
    seq_len = seq_end - seq_start
    ys = []
    for g, w in enumerate(POOL_WINDOWS):
        cols = slice(g * POOL_GROUP_DIM, (g + 1) * POOL_GROUP_DIM)
        tot = u_ref[HALO - w // 2:HALO - w // 2 + tm, cols]
        for o in range(-w // 2 + 1, w // 2):
            tot = tot + u_ref[HALO + o:HALO + o + tm, cols]
        cnt = (jnp.minimum(rel + w // 2, seq_len) - jnp.maximum(rel - w // 2, 0)).astype(F32)
        d = tot / cnt - u_ref[HALO:HALO + tm, cols]
        ys.append(_dot(d.astype(BF16), pw_ref[g]))
    a_out = jnp.concatenate(ys, axis=-1) * ps_ref[...]

    conv = (cw_ref[0:1, :] * z_ref[HALO - 1:HALO - 1 + tm, :]
            + cw_ref[1:2, :] * z_ref[HALO:HALO + tm, :]
            + cw_ref[2:3, :] * z_ref[HALO + 1:HALO + 1 + tm, :])
    b_out = proj[HALO:HALO + tm, c0 + CONV_WIDTH_CH:c0 + 2 * CONV_WIDTH_CH] * conv
    mixed = jnp.concatenate([a_out, b_out], axis=-1).astype(BF16)
    o_ref[...] = x + _dot(mixed, wout_ref[...])


def _mixer(x_parts, g, win, pw, ps, cw, wout, bounds):
    d = x_parts[0].shape[1]
    part_rows = tuple(p.shape[0] for p in x_parts)
    t = sum(part_rows)
    tm = TOKEN_TILE
    per = tm // HALO
    last = t // HALO - 1
    return pl.pallas_call(
        functools.partial(_mixer_kernel, bounds=bounds, part_rows=part_rows),
        grid=(t // tm,),
        in_specs=[
            *_part_specs(x_parts, HALO, lambda i: jnp.maximum(i * per - 1, 0)),
            *_part_specs(x_parts, tm, lambda i: i),
            *_part_specs(x_parts, HALO, lambda i: jnp.minimum((i + 1) * per, last)),
            _const_spec((1, d)),
            _const_spec(win.shape),
            _const_spec(pw.shape),
            _const_spec((1, POOL_WIDTH)),
            _const_spec(cw.shape),
            _const_spec(wout.shape),
        ],
        out_specs=pl.BlockSpec((tm, d), lambda i: (i, 0)),
        out_shape=jax.ShapeDtypeStruct((t, d), F32),
        scratch_shapes=[
            pltpu.VMEM((tm + 2 * HALO, POOL_WIDTH), F32),
            pltpu.VMEM((tm + 2 * HALO, CONV_WIDTH_CH), F32),
        ],
        compiler_params=_params(1),
        name="even_mixer",
    )(*x_parts, *x_parts, *x_parts, g.reshape(1, d), win, pw, ps.reshape(1, POOL_WIDTH), cw, wout)


def _mlp_kernel(*refs, attn_rows, has_final):
    refs = list(refs)
    x_ref = refs.pop(0)
    if attn_rows:
        a_refs = [refs.pop(0) for _ in attn_rows]
        wo_ref = refs.pop(0)
    g_ref, w1_ref, w2_ref = refs[:3]
    refs = refs[3:]
    if has_final:
        fg_ref = refs.pop(0)
    o_ref = refs.pop(0)

    x = x_ref[...]
    if attn_rows:
        attn = _pick_part(a_refs, pl.program_id(0), attn_rows, x_ref.shape[0])
        x = x + _dot(attn, wo_ref[...])
    h = _rms(x, g_ref[...], NORM_EPS).astype(BF16)
    acc = x
    for c in range(D_FF // FF_CHUNK):
        cols = slice(c * FF_CHUNK, (c + 1) * FF_CHUNK)
        a = jnp.maximum(_dot(h, w1_ref[:, cols]), 0.0)
        acc = acc + _dot((a * a).astype(BF16), w2_ref[cols, :])
    if has_final:
        acc = _rms(acc, fg_ref[...], NORM_EPS)
    o_ref[...] = acc


def _mlp(x, g, w1, w2, attn_parts=(), attn_w=None, final_g=None, row_start=0, rows=None):
    d = x.shape[1]
    rows = x.shape[0] if rows is None else rows
    tm = TOKEN_TILE
    tile0 = row_start // tm
    row_spec = pl.BlockSpec((tm, d), lambda i: (i, 0))
    args = [x]
    in_specs = [pl.BlockSpec((tm, d), lambda i: (tile0 + i, 0))]
    if attn_parts:
        args += [*attn_parts, attn_w]
        in_specs += [*_part_specs(attn_parts, tm, lambda i: i), _const_spec(attn_w.shape)]
    args += [g.reshape(1, d), w1, w2]
    in_specs += [_const_spec((1, d)), _const_spec(w1.shape), _const_spec(w2.shape)]
    if final_g is not None:
        args.append(final_g.reshape(1, d))
        in_specs.append(_const_spec((1, d)))
    return pl.pallas_call(
        functools.partial(_mlp_kernel, attn_rows=tuple(p.shape[0] for p in attn_parts),
                          has_final=final_g is not None),
        grid=(rows // tm,),
        in_specs=in_specs,
        out_specs=row_spec,
        out_shape=jax.ShapeDtypeStruct((rows, d), F32),
        compiler_params=_params(1),
        name="sq_relu_mlp",
    )(*args)


def _qkv_kernel(x_ref, g_ref, w_ref, qt_ref, kk_ref, vt_ref, nrm_ref, qtok_ref):
    tm = x_ref.shape[0]
    h = _rms(x_ref[...], g_ref[...], NORM_EPS).astype(BF16)
    qkv = _dot(h, w_ref[...])
    q_all = (qkv[:, :D_MODEL] * (ATTN_SCALE * LOG2E)).astype(BF16).astype(F32)
    k_all = qkv[:, D_MODEL:2 * D_MODEL].astype(BF16).astype(F32)

    sel = (lax.shift_right_logical(lax.broadcasted_iota(jnp.int32, (D_MODEL, V_DIM), 0),
                                   HEAD_DIM.bit_length() - 1)
           == lax.broadcasted_iota(jnp.int32, (D_MODEL, V_DIM), 1)).astype(BF16)

    def per_map(a):
        return _dot(a.astype(BF16), sel)

    q_sq, k_sq, self_score = per_map(q_all * q_all), per_map(k_all * k_all), per_map(q_all * k_all)
    nrm_ref[0] = jnp.concatenate([
        jnp.max(q_sq, axis=0, keepdims=True), jnp.max(k_sq, axis=0, keepdims=True),
        jnp.min(q_sq, axis=0, keepdims=True), jnp.min(self_score, axis=0, keepdims=True),
        jnp.zeros((4, V_DIM), F32)], axis=0)
    qtok_ref[...] = jnp.sqrt(q_sq).T[:2 * N_HEADS]

    n_p = len(LOG2E_PIECES)

    def extras(slot, lo, hi, pos_first):
        pos_slot = slot if pos_first else slot - 2 * n_p
        const_slot = slot - 2 * n_p if pos_first else slot
        out = jnp.where((pos_slot >= 0) & (pos_slot < n_p), lo,
                        jnp.where((pos_slot >= n_p) & (pos_slot < 2 * n_p), hi, 0.0))
        for p, piece in enumerate(LOG2E_PIECES):
            out = jnp.where((const_slot == p) | (const_slot == n_p + p), piece, out)
        return out

    q_row = lax.broadcasted_iota(jnp.int32, (HEAD_DIM, tm), 0)
    q_tok = lax.broadcasted_iota(jnp.int32, (HEAD_DIM, tm), 1)
    q_base = extras(q_row, (q_tok & POS_LO_MASK).astype(F32), (q_tok & POS_HI_MASK).astype(F32), True)
    q_has_slope = q_row >= 2 * n_p
    k_lane = lax.broadcasted_iota(jnp.int32, (tm, V_DIM), 1)
    k_tok = lax.broadcasted_iota(jnp.int32, (tm, V_DIM), 0)
    k_lo = -(k_tok & POS_LO_MASK).astype(F32)
    k_hi = -(k_tok & POS_HI_MASK).astype(F32)
    k_base = [extras(k_lane - off, k_lo, k_hi, False) for off in (HEAD_DIM, 0)]
    k_has_slope = [(k_lane >= off) & (k_lane < off + 2 * n_p) for off in (HEAD_DIM, 0)]
    denom_rows = (lax.broadcasted_iota(jnp.int32, (DENOM_ROWS, tm), 0) == 0).astype(F32)

    for hd in range(N_HEADS):
        slope = 2.0 ** (-8.0 * (hd + 1) / N_HEADS)
        cols = slice(hd * V_DIM, (hd + 1) * V_DIM)
        q_t = q_all[:, cols].T
        top, bot = q_t[:HEAD_DIM], q_t[HEAD_DIM:]
        q_extra = jnp.where(q_has_slope, slope * q_base, q_base)
        qt_ref[hd, 0, 0] = jnp.concatenate([top, q_extra], axis=0).astype(BF16)
        qt_ref[hd, 0, 1] = jnp.concatenate([top, -q_extra], axis=0).astype(BF16)
        qt_ref[hd, 0, 2] = jnp.concatenate([q_extra, bot], axis=0).astype(BF16)
        qt_ref[hd, 0, 3] = jnp.concatenate([-q_extra, bot], axis=0).astype(BF16)

        kh = k_all[:, cols]
        k_extra = [jnp.where(k_has_slope[m], slope * k_base[m], k_base[m]) for m in range(2)]
        kk_ref[hd, 0, 0] = jnp.where(k_lane < HEAD_DIM, kh, k_extra[0]).astype(BF16)
        kk_ref[hd, 1, 0] = jnp.where(k_lane >= HEAD_DIM, kh, k_extra[1]).astype(BF16)

        vh = qkv[:, 2 * D_MODEL + hd * V_DIM:2 * D_MODEL + (hd + 1) * V_DIM]
        vt_ref[hd, 0] = jnp.concatenate([vh.T, denom_rows], axis=0).astype(BF16)


def _qkv(x, g, w):
    t, d = x.shape
    tm = TOKEN_TILE
    nc = t // tm
    return pl.pallas_call(
        _qkv_kernel,
        grid=(nc,),
        in_specs=[
            pl.BlockSpec((tm, d), lambda i: (i, 0)),
            _const_spec((1, d)),
            _const_spec(w.shape),
        ],
        out_specs=[
            pl.BlockSpec((N_HEADS, 1, 4, V_DIM, tm), lambda i: (0, i, 0, 0, 0)),
            pl.BlockSpec((N_HEADS, 2, 1, tm, V_DIM), lambda i: (0, 0, i, 0, 0)),
            pl.BlockSpec((N_HEADS, 1, V_ROWS, tm), lambda i: (0, i, 0, 0)),
            pl.BlockSpec((1, 8, V_DIM), lambda i: (i, 0, 0)),
            pl.BlockSpec((2 * N_HEADS, tm), lambda i: (0, i)),
        ],
        out_shape=[
            jax.ShapeDtypeStruct((N_HEADS, nc, 4, V_DIM, tm), BF16),
            jax.ShapeDtypeStruct((N_HEADS, 2, nc, tm, V_DIM), BF16),
            jax.ShapeDtypeStruct((N_HEADS, nc, V_ROWS, tm), BF16),
            jax.ShapeDtypeStruct((nc, 8, V_DIM), F32),
            jax.ShapeDtypeStruct((2 * N_HEADS, t), F32),
        ],
        compiler_params=_params(1),
        name="attn_qkv",
    )(x, g.reshape(1, d), w)


def _flash_kernel(slope_ref, qmax_ref, qmin_ref, smin_ref, kseq_ref, lq1_ref, lk1_ref, lq2_ref,
                  lk2_ref, sg_ref, qtok_ref, qt_ref, kk_ref, vt_ref, o_ref,
                  s_ref, mc_ref, m_ref, acc_ref, *, lambda_init, blk0):
    n_chunks, tkc = kk_ref.shape[1], kk_ref.shape[2]
    n_sub, tq = qt_ref.shape[0], qt_ref.shape[-1]
    hd = pl.program_id(1)
    slope, inv_slope = slope_ref[0, hd], slope_ref[1, hd]
    cq0 = pl.program_id(2) * n_sub
    col0 = blk0 + pl.program_id(0) * n_chunks + cq0
    k_norm = NORM_SLACK * kseq_ref[hd, pl.program_id(0)]
    subs = range(n_sub)

    def select(sub, values):
        out = values[0]
        for other, value in enumerate(values[1:], 1):
            out = jnp.where(sub == other, value, out)
        return out

    def plan(cq, gap):
        reach = (gap * inv_slope - 1.0) * (1.0 / tkc)
        n_dist = jnp.where(reach >= 1.0,
                           jnp.minimum(reach, float(n_chunks)).astype(jnp.int32) + 1, 1)
        n_left = jnp.minimum(n_dist, cq)
        return n_left, n_left + jnp.minimum(n_dist, n_chunks - 1 - cq)

    def item(cq, n_left, t):
        left = t < n_left
        c = jnp.where(left, cq - 1 - t, cq + 1 + t - n_left)
        return jnp.clip(c, 0, n_chunks - 1), left.astype(jnp.int32)

    def nearest(cq):
        return (jnp.where(cq > 0, cq - 1, jnp.minimum(cq + 1, n_chunks - 1)),
                (cq > 0).astype(jnp.int32))

    def run(total, trip, unroll):
        assert unroll & (unroll - 1) == 0

        def trips(t0, count):
            for r in range(count):
                trip(t0 + r)

        rem = total & (unroll - 1)
        done, size = 0, 1
        while size < unroll:
            pl.when((total & size) == size)(functools.partial(trips, done, size))
            done = done + (total & size)
            size *= 2

        def group(u, carry):
            trips(rem + unroll * u, unroll)
            return carry

        lax.fori_loop(0, total // unroll, group, 0)

    def scores(sub, c, sign, mp):
        return _dot(kk_ref[mp, c], qt_ref[sub, 2 * mp + sign])

    def issue_chunk(sub, c, sign, mp):
        s = scores(sub, c, sign, mp)
        s_ref[sub, mp] = s
        mc_ref[sub, mp] = jnp.max(s, axis=0, keepdims=True)

    def absorb(sub, cq, c, mp):
        shift = -slope * (tkc * jnp.abs(cq - c)).astype(F32)
        m_old = m_ref[sub, mp]
        m_new = jnp.maximum(m_old, mc_ref[sub, mp] + shift)
        p = jnp.exp2(s_ref[sub, mp] - (m_new - shift)).astype(BF16)
        acc_ref[sub, mp] = jnp.exp2(m_old - m_new) * acc_ref[sub, mp] + _dot(vt_ref[c], p)
        m_ref[sub, mp] = m_new

    jj = lax.broadcasted_iota(jnp.int32, (tkc, tq), 0)
    ii = lax.broadcasted_iota(jnp.int32, (tkc, tq), 1)
    fixup = (-2.0 * slope) * jnp.maximum(ii - jj, 0).astype(F32)

    def fixed_path():
        plans = [plan(cq0 + sub, k_norm * (qmax_ref[hd, col0 + sub] - qmin_ref[hd, col0 + sub])
                      + EXP_ZERO_BELOW) for sub in subs]
        for sub in subs:
            for mp in range(2):
                m_ref[sub, mp] = k_norm * qtok_ref[mp:mp + 1, sub * tq:(sub + 1) * tq]
        for sub in subs:
            s_ref[sub, 0] = scores(sub, cq0 + sub, 0, 0)
        for sub in subs:
            cq = cq0 + sub
            s_ref[sub, 1] = scores(sub, cq, 0, 1)
            acc_ref[sub, 0] = _dot(
                vt_ref[cq], jnp.exp2(s_ref[sub, 0] + (fixup - m_ref[sub, 0])).astype(BF16))
            s_ref[sub, 0] = scores(sub, *nearest(cq), 0)
            acc_ref[sub, 1] = _dot(
                vt_ref[cq], jnp.exp2(s_ref[sub, 1] + (fixup - m_ref[sub, 1])).astype(BF16))

        def visit(sub, carry):
            cq = cq0 + sub
            n_left = select(sub, [p[0] for p in plans])
            total = select(sub, [p[1] for p in plans])

            def absorb(c, mp):
                shift = -slope * (tkc * jnp.abs(cq - c)).astype(F32)
                p = jnp.exp2(s_ref[sub, mp] + (shift - m_ref[sub, mp])).astype(BF16)
                acc_ref[sub, mp] += _dot(vt_ref[c], p)

            def trip(t):
                c, sign = item(cq, n_left, t)
                s_ref[sub, 1] = scores(sub, c, sign, 1)
                absorb(c, 0)
                s_ref[sub, 0] = scores(sub, *item(cq, n_left, t + 1), 0)
                absorb(c, 1)

            run(total, trip, PIPE_UNROLL)
            return carry

        lax.fori_loop(0, n_sub, visit, 0)

    def running_path():
        diag = [[scores(sub, cq0 + sub, 0, mp) + fixup for mp in range(2)] for sub in subs]
        for sub in subs:
            issue_chunk(sub, *nearest(cq0 + sub), 0)
        for sub in subs:
            for mp in range(2):
                m_first = jnp.max(diag[sub][mp], axis=0, keepdims=True)
                m_ref[sub, mp] = m_first
                acc_ref[sub, mp] = _dot(vt_ref[cq0 + sub],
                                        jnp.exp2(diag[sub][mp] - m_first).astype(BF16))
        plans = [plan(cq0 + sub, k_norm * qmax_ref[hd, col0 + sub] + EXP_ZERO_BELOW
                      - jnp.min(m_ref[sub])) for sub in subs]

        def visit(sub, carry):
            cq = cq0 + sub
            n_left = select(sub, [p[0] for p in plans])
            total = select(sub, [p[1] for p in plans])

            def trip(t):
                c, sign = item(cq, n_left, t)
                issue_chunk(sub, c, sign, 1)
                absorb(sub, cq, c, 0)
                issue_chunk(sub, *item(cq, n_left, t + 1), 0)
                absorb(sub, cq, c, 1)

            run(total, trip, 1)
            return carry

        lax.fori_loop(0, n_sub, visit, 0)

    slack = [k_norm * qmax_ref[hd, col0 + sub] - smin_ref[hd, col0 + sub] for sub in subs]
    fixed_ok = functools.reduce(jnp.logical_and, [s < STAB_SLACK_BELOW for s in slack])
    pl.when(fixed_ok)(fixed_path)
    pl.when(jnp.logical_not(fixed_ok))(running_path)

    lam = (jnp.exp(jnp.sum(lq1_ref[...] * lk1_ref[...], keepdims=True))
           - jnp.exp(jnp.sum(lq2_ref[...] * lk2_ref[...], keepdims=True)) + lambda_init)
    for sub in range(n_sub):
        o_t = (acc_ref[sub, 0, :V_DIM] * (1.0 / acc_ref[sub, 0, V_DIM:V_DIM + 1])
               - (lam / acc_ref[sub, 1, V_DIM:V_DIM + 1]) * acc_ref[sub, 1, :V_DIM])
        scale = (lax.rsqrt(jnp.mean(o_t * o_t, axis=0, keepdims=True) + SUBLN_EPS)
                 * (1.0 - lambda_init))
        o_ref[sub * tq:(sub + 1) * tq, :] = (o_t * scale * sg_ref[...]).T.astype(o_ref.dtype)


def _flash(slopes, tables, lam_vecs, subln_g, qtok, qt, kk, vt, *, tok_start, n_seq, seq_len,
           lambda_init):
    q_max, q_min, s_min, k_max = tables
    tq = TOKEN_TILE
    ns = Q_BLOCKS_PER_STEP
    nq = seq_len // tq
    blk0 = tok_start // tq
    seq0 = tok_start // seq_len
    steps = nq // ns
    step0 = blk0 // ns
    assert nq % ns == 0 and blk0 % ns == 0
    smem = pl.BlockSpec(memory_space=pltpu.SMEM)
    vec = _const_spec((1, HEAD_DIM))
    return pl.pallas_call(
        functools.partial(_flash_kernel, lambda_init=lambda_init, blk0=blk0),
        grid=(n_seq, N_HEADS, steps),
        in_specs=[
            smem, smem, smem, smem, smem, vec, vec, vec, vec, _const_spec((V_DIM, tq)),
            pl.BlockSpec((None, 2, ns * tq), lambda b, h, i: (h, 0, step0 + b * steps + i)),
            pl.BlockSpec((None, ns, 4, V_DIM, tq),
                         lambda b, h, i: (h, step0 + b * steps + i, 0, 0, 0)),
            pl.BlockSpec((None, 2, nq, tq, V_DIM), lambda b, h, i: (h, 0, seq0 + b, 0, 0)),
            pl.BlockSpec((None, nq, V_ROWS, tq), lambda b, h, i: (h, seq0 + b, 0, 0)),
        ],
        out_specs=pl.BlockSpec((ns * tq, V_DIM), lambda b, h, i: (b * steps + i, h)),
        out_shape=jax.ShapeDtypeStruct((n_seq * seq_len, N_HEADS * V_DIM), BF16),
        scratch_shapes=[
            pltpu.VMEM((ns, 2, tq, tq), F32),
            pltpu.VMEM((ns, 2, 1, tq), F32),
            pltpu.VMEM((ns, 2, 1, tq), F32),
            pltpu.VMEM((ns, 2, V_ROWS, tq), F32),
        ],
        compiler_params=_params(3),
        name="diff_flash",
    )(slopes, q_max, q_min, s_min,
      k_max[:, blk0:blk0 + n_seq * nq].reshape(N_HEADS, n_seq, nq).max(axis=-1),
      *lam_vecs, jnp.broadcast_to(subln_g.reshape(V_DIM, 1), (V_DIM, tq)), qtok, qt, kk, vt)


def kernel(x_prompt, x_sample, norm1_g, norm2_g, final_g, mix_in_w, pool_w, pool_scale, conv_w,
           mix_out_w, attn_qkv_w, attn_out_w, lambda_q1, lambda_k1, lambda_q2, lambda_k2, subln_g,
           mlp_w1, mlp_w2):
    pb, ps, d = x_prompt.shape
    sb, ss, _ = x_sample.shape
    n_prompt = pb * ps
    n_sample = sb * ss
    x_parts = [x_prompt.reshape(n_prompt, d), x_sample.reshape(n_sample, d)]
    bounds = tuple(ps * b for b in range(pb)) + tuple(n_prompt + ss * b for b in range(sb + 1))
    slopes = jnp.exp2(-8.0 * (jnp.arange(N_HEADS, dtype=F32) + 1.0) / N_HEADS)
    slopes = jnp.stack([slopes * LOG2E, 1.0 / (slopes * LOG2E)])

    for i in range(DEPTH):
        j = i // 2
        w1 = mlp_w1[i].astype(BF16)
        w2 = mlp_w2[i].astype(BF16)
        attn_parts, attn_w = (), None
        if i % 2 == 0:
            x = _mixer(x_parts, norm1_g[i], mix_in_w[j].astype(BF16), pool_w[j].astype(BF16),
                       pool_scale[j], conv_w[j], mix_out_w[j].astype(BF16), bounds)
        else:
            x = x_parts[0]
            lambda_init = 0.8 - 0.6 * math.exp(-0.3 * i)
            qt, kk, vt, nrm, qtok = _qkv(x, norm1_g[i], attn_qkv_w[j].astype(BF16))
            per_head = nrm[:, :4, :2 * N_HEADS].reshape(-1, 4, N_HEADS, 2)
            tables = (jnp.sqrt(per_head[:, 0].max(axis=-1)).T, jnp.sqrt(per_head[:, 2].min(axis=-1)).T,
                      per_head[:, 3].min(axis=-1).T, jnp.sqrt(per_head[:, 1].max(axis=-1)).T)
            lam_vecs = [v[j].reshape(1, HEAD_DIM) for v in (lambda_q1, lambda_k1, lambda_q2, lambda_k2)]
            flash = functools.partial(_flash, slopes, tables, lam_vecs, subln_g[j],
                                      qtok.reshape(N_HEADS, 2, -1), qt, kk, vt,
                                      lambda_init=lambda_init)
            attn_parts = (flash(tok_start=0, n_seq=pb, seq_len=ps),
                          flash(tok_start=n_prompt, n_seq=sb, seq_len=ss))
            attn_w = attn_out_w[j].astype(BF16)
        if i < DEPTH - 1:
            x_parts = [_mlp(x, norm2_g[i], w1, w2, attn_parts, attn_w)]
        else:
            y_prompt, y_sample = (
                _mlp(x, norm2_g[i], w1, w2, attn_parts[k:k + 1], attn_w, final_g, r0, nr)
                for k, (r0, nr) in enumerate(((0, n_prompt), (n_prompt, n_sample))))

    return (y_prompt.reshape(pb, ps, d), y_sample.reshape(sb, ss, d))
```

```python
import functools
import math

import jax
import jax.numpy as jnp
import numpy as np
from jax import lax
from jax.experimental import pallas as pl
from jax.experimental.pallas import tpu as pltpu

D_MODEL = 1024
DEPTH = 4
POOL_WIDTH = D_MODEL // 2
N_POOL_GROUPS = 4
POOL_GROUP_DIM = POOL_WIDTH // N_POOL_GROUPS
POOL_WINDOWS = (2, 4, 8, 16)
CONV_WIDTH_CH = D_MODEL // 2
MIX_IN_COLS = POOL_WIDTH + 3 * CONV_WIDTH_CH
N_HEADS = 8
HEAD_DIM = D_MODEL // (2 * N_HEADS)
V_DIM = 2 * HEAD_DIM
ATTN_SCALE = HEAD_DIM ** -0.5
D_FF = 4 * D_MODEL
NORM_EPS = 1e-6
SUBLN_EPS = 1e-5

F32 = jnp.float32
BF16 = jnp.bfloat16

TOKEN_TILE = 512
HALO = 16
FF_CHUNK = 1024
VMEM_LIMIT_BYTES = 48 * 1024 * 1024
DENOM_ROWS = 16
V_ROWS = V_DIM + DENOM_ROWS
NORM_TABLE_ROWS = 8
PIPE_UNROLL = 8
Q_BLOCKS_PER_STEP = 2
EXP_ZERO_BELOW = 153.0
STAB_SLACK_BELOW = 80.0


def _bf16_pieces(value, n):
    pieces, rest = [], np.float32(value)
    for _ in range(n):
        piece = np.float32(np.asarray(rest, dtype=BF16))
        pieces.append(float(piece))
        rest = np.float32(rest - piece)
    return tuple(pieces)


LOG2E = float(np.float32(math.log2(math.e)))
LOG2E_PIECES = _bf16_pieces(LOG2E, 3)
NORM_SLACK = 1.01
POS_LO_MASK = 255
POS_HI_MASK = TOKEN_TILE - 1 - POS_LO_MASK


def _rms(x, g, eps):
    return x * lax.rsqrt(jnp.mean(x * x, axis=-1, keepdims=True) + eps) * g


def _dot(a, b):
    return jnp.dot(a, b, preferred_element_type=F32)


def _const_spec(shape):
    zeros = (0,) * len(shape)
    return pl.BlockSpec(shape, lambda *_: zeros)


def _part_starts(part_rows, rows):
    starts, s = [], 0
    for r in part_rows:
        starts.append(s // rows)
        s += r
    return starts


def _part_specs(parts, rows, block_of):
    specs = []
    for p, s0 in zip(parts, _part_starts([p.shape[0] for p in parts], rows)):
        nb = p.shape[0] // rows
        specs.append(pl.BlockSpec(
            (rows, p.shape[1]), lambda i, s0=s0, nb=nb: (jnp.clip(block_of(i) - s0, 0, nb - 1), 0)))
    return specs


def _pick_part(refs, block, part_rows, rows):
    value = refs[0][...]
    for ref, s0 in zip(refs[1:], _part_starts(part_rows, rows)[1:]):
        value = jnp.where(block >= s0, ref[...], value)
    return value


def _params(n_axes):
    return pltpu.CompilerParams(
        dimension_semantics=("arbitrary",) * n_axes,
        vmem_limit_bytes=VMEM_LIMIT_BYTES,
    )


def _mixer_kernel(*refs, bounds, part_rows):
    n = len(part_rows)
    xp_refs, x_refs, xn_refs = refs[:n], refs[n:2 * n], refs[2 * n:3 * n]
    g_ref, win_ref, pw_ref, ps_ref, cw_ref, wout_ref, o_ref, u_ref, z_ref = refs[3 * n:]
    tm = x_refs[0].shape[0]
    i = pl.program_id(0)
    per = tm // HALO
    last = sum(part_rows) // HALO - 1
    start = i * tm
    seq_start = jnp.int32(bounds[0])
    seq_end = jnp.int32(bounds[-1])
    for b in bounds[1:-1]:
        seq_start = jnp.where(start >= b, b, seq_start)
    for b in reversed(bounds[1:-1]):
        seq_end = jnp.where(start < b, b, seq_end)

    x = _pick_part(x_refs, i, part_rows, tm)
    xe = jnp.concatenate([
        _pick_part(xp_refs, jnp.maximum(i * per - 1, 0), part_rows, HALO),
        x,
        _pick_part(xn_refs, jnp.minimum((i + 1) * per, last), part_rows, HALO),
    ], axis=0)
    pos = start - HALO + lax.broadcasted_iota(jnp.int32, (tm + 2 * HALO, 1), 0)
    valid = (pos >= seq_start) & (pos < seq_end)
    he = _rms(xe, g_ref[...], NORM_EPS).astype(BF16)
    proj = _dot(he, win_ref[...])
    c0 = POOL_WIDTH
    u_ref[...] = jnp.where(valid, proj[:, :c0], 0.0)
    z_ref[...] = jnp.where(valid, proj[:, c0 + 2 * CONV_WIDTH_CH:] * proj[:, c0:c0 + CONV_WIDTH_CH], 0.0)

    rel = start - seq_start + lax.broadcasted_iota(jnp.int32, (tm, 1), 0)
    seq_len = seq_end - seq_start
    ys = []
    for g, w in enumerate(POOL_WINDOWS):
        cols = slice(g * POOL_GROUP_DIM, (g + 1) * POOL_GROUP_DIM)
        tot = u_ref[HALO - w // 2:HALO - w // 2 + tm, cols]
        for o in range(-w // 2 + 1, w // 2):
            tot = tot + u_ref[HALO + o:HALO + o + tm, cols]
        cnt = (jnp.minimum(rel + w // 2, seq_len) - jnp.maximum(rel - w // 2, 0)).astype(F32)
        d = tot / cnt - u_ref[HALO:HALO + tm, cols]
        ys.append(_dot(d.astype(BF16), pw_ref[g]))
    a_out = jnp.concatenate(ys, axis=-1) * ps_ref[...]

    conv = (cw_ref[0:1, :] * z_ref[HALO - 1:HALO - 1 + tm, :]
            + cw_ref[1:2, :] * z_ref[HALO:HALO + tm, :]
            + cw_ref[2:3, :] * z_ref[HALO + 1:HALO + 1 + tm, :])
    b_out = proj[HALO:HALO + tm, c0 + CONV_WIDTH_CH:c0 + 2 * CONV_WIDTH_CH] * conv
    mixed = jnp.concatenate([a_out, b_out], axis=-1).astype(BF16)
    o_ref[...] = x + _dot(mixed, wout_ref[...])


def _mixer(x_parts, g, win, pw, ps, cw, wout, bounds):
    d = x_parts[0].shape[1]
    part_rows = tuple(p.shape[0] for p in x_parts)
    t = sum(part_rows)
    tm = TOKEN_TILE
    per = tm // HALO
    last = t // HALO - 1
    return pl.pallas_call(
        functools.partial(_mixer_kernel, bounds=bounds, part_rows=part_rows),
        grid=(t // tm,),
        in_specs=[
            *_part_specs(x_parts, HALO, lambda i: jnp.maximum(i * per - 1, 0)),
            *_part_specs(x_parts, tm, lambda i: i),
            *_part_specs(x_parts, HALO, lambda i: jnp.minimum((i + 1) * per, last)),
            _const_spec((1, d)),
            _const_spec(win.shape),
            _const_spec(pw.shape),
            _const_spec((1, POOL_WIDTH)),
            _const_spec(cw.shape),
            _const_spec(wout.shape),
        ],
        out_specs=pl.BlockSpec((tm, d), lambda i: (i, 0)),
        out_shape=jax.ShapeDtypeStruct((t, d), F32),
        scratch_shapes=[
            pltpu.VMEM((tm + 2 * HALO, POOL_WIDTH), F32),
            pltpu.VMEM((tm + 2 * HALO, CONV_WIDTH_CH), F32),
        ],
        compiler_params=_params(1),
        name="even_mixer",
    )(*x_parts, *x_parts, *x_parts, g.reshape(1, d), win, pw, ps.reshape(1, POOL_WIDTH), cw, wout)


def _mlp_kernel(*refs, attn_rows, has_final):
    refs = list(refs)
    x_ref = refs.pop(0)
    if attn_rows:
        a_refs = [refs.pop(0) for _ in attn_rows]
        wo_ref = refs.pop(0)
    g_ref, w1_ref, w2_ref = refs[:3]
    refs = refs[3:]
    if has_final:
        fg_ref = refs.pop(0)
    o_ref = refs.pop(0)

    x = x_ref[...]
    if attn_rows:
        attn = _pick_part(a_refs, pl.program_id(0), attn_rows, x_ref.shape[0])
        x = x + _dot(attn, wo_ref[...])
    h = _rms(x, g_ref[...], NORM_EPS).astype(BF16)
    acc = x
    for c in range(D_FF // FF_CHUNK):
        cols = slice(c * FF_CHUNK, (c + 1) * FF_CHUNK)
        a = jnp.maximum(_dot(h, w1_ref[:, cols]), 0.0)
        acc = acc + _dot((a * a).astype(BF16), w2_ref[cols, :])
    if has_final:
        acc = _rms(acc, fg_ref[...], NORM_EPS)
    o_ref[...] = acc


def _mlp(x, g, w1, w2, attn_parts=(), attn_w=None, final_g=None, row_start=0, rows=None):
    d = x.shape[1]
    rows = x.shape[0] if rows is None else rows
    tm = TOKEN_TILE
    tile0 = row_start // tm
    row_spec = pl.BlockSpec((tm, d), lambda i: (i, 0))
    args = [x]
    in_specs = [pl.BlockSpec((tm, d), lambda i: (tile0 + i, 0))]
    if attn_parts:
        args += [*attn_parts, attn_w]
        in_specs += [*_part_specs(attn_parts, tm, lambda i: i), _const_spec(attn_w.shape)]
    args += [g.reshape(1, d), w1, w2]
    in_specs += [_const_spec((1, d)), _const_spec(w1.shape), _const_spec(w2.shape)]
    if final_g is not None:
        args.append(final_g.reshape(1, d))
        in_specs.append(_const_spec((1, d)))
    return pl.pallas_call(
        functools.partial(_mlp_kernel, attn_rows=tuple(p.shape[0] for p in attn_parts),
                          has_final=final_g is not None),
        grid=(rows // tm,),
        in_specs=in_specs,
        out_specs=row_spec,
        out_shape=jax.ShapeDtypeStruct((rows, d), F32),
        compiler_params=_params(1),
        name="sq_relu_mlp",
    )(*args)


def _qkv_kernel(x_ref, g_ref, w_ref, qt_ref, kk_ref, vt_ref, nrm_ref, qtok_ref):
    tm = x_ref.shape[0]
    h = _rms(x_ref[...], g_ref[...], NORM_EPS).astype(BF16)
    qkv = _dot(h, w_ref[...])
    q_all = (qkv[:, :D_MODEL] * (ATTN_SCALE * LOG2E)).astype(BF16).astype(F32)
    k_all = qkv[:, D_MODEL:2 * D_MODEL].astype(BF16).astype(F32)

    sel = (lax.shift_right_logical(lax.broadcasted_iota(jnp.int32, (D_MODEL, V_DIM), 0),
                                   HEAD_DIM.bit_length() - 1)
           == lax.broadcasted_iota(jnp.int32, (D_MODEL, V_DIM), 1)).astype(BF16)

    def per_map(a):
        return _dot(a.astype(BF16), sel)

    q_sq, k_sq, self_score = per_map(q_all * q_all), per_map(k_all * k_all), per_map(q_all * k_all)
    nrm_ref[0] = jnp.concatenate([
        jnp.max(q_sq, axis=0, keepdims=True), jnp.max(k_sq, axis=0, keepdims=True),
        jnp.min(q_sq, axis=0, keepdims=True), jnp.min(self_score, axis=0, keepdims=True),
        jnp.zeros((NORM_TABLE_ROWS - 4, V_DIM), F32)], axis=0)
    qtok_ref[...] = jnp.sqrt(q_sq).T[:2 * N_HEADS]

    n_p = len(LOG2E_PIECES)

    def extras(slot, lo, hi, pos_first):
        pos_slot = slot if pos_first else slot - 2 * n_p
        const_slot = slot - 2 * n_p if pos_first else slot
        out = jnp.where((pos_slot >= 0) & (pos_slot < n_p), lo,
                        jnp.where((pos_slot >= n_p) & (pos_slot < 2 * n_p), hi, 0.0))
        for p, piece in enumerate(LOG2E_PIECES):
            out = jnp.where((const_slot == p) | (const_slot == n_p + p), piece, out)
        return out

    q_row = lax.broadcasted_iota(jnp.int32, (HEAD_DIM, tm), 0)
    q_tok = lax.broadcasted_iota(jnp.int32, (HEAD_DIM, tm), 1)
    q_base = extras(q_row, (q_tok & POS_LO_MASK).astype(F32), (q_tok & POS_HI_MASK).astype(F32), True)
    q_has_slope = q_row >= 2 * n_p
    k_lane = lax.broadcasted_iota(jnp.int32, (tm, V_DIM), 1)
    k_tok = lax.broadcasted_iota(jnp.int32, (tm, V_DIM), 0)
    k_lo = -(k_tok & POS_LO_MASK).astype(F32)
    k_hi = -(k_tok & POS_HI_MASK).astype(F32)
    k_base = [extras(k_lane - off, k_lo, k_hi, False) for off in (HEAD_DIM, 0)]
    k_has_slope = [(k_lane >= off) & (k_lane < off + 2 * n_p) for off in (HEAD_DIM, 0)]
    denom_rows = (lax.broadcasted_iota(jnp.int32, (DENOM_ROWS, tm), 0) == 0).astype(F32)

    for hd in range(N_HEADS):
        slope = 2.0 ** (-8.0 * (hd + 1) / N_HEADS)
        cols = slice(hd * V_DIM, (hd + 1) * V_DIM)
        q_t = q_all[:, cols].T
        top, bot = q_t[:HEAD_DIM], q_t[HEAD_DIM:]
        q_extra = jnp.where(q_has_slope, slope * q_base, q_base)
        qt_ref[hd, 0, 0] = jnp.concatenate([top, q_extra], axis=0).astype(BF16)
        qt_ref[hd, 0, 1] = jnp.concatenate([top, -q_extra], axis=0).astype(BF16)
        qt_ref[hd, 0, 2] = jnp.concatenate([q_extra, bot], axis=0).astype(BF16)
        qt_ref[hd, 0, 3] = jnp.concatenate([-q_extra, bot], axis=0).astype(BF16)

        kh = k_all[:, cols]
        k_extra = [jnp.where(k_has_slope[m], slope * k_base[m], k_base[m]) for m in range(2)]
        kk_ref[hd, 0, 0] = jnp.where(k_lane < HEAD_DIM, kh, k_extra[0]).astype(BF16)
        kk_ref[hd, 1, 0] = jnp.where(k_lane >= HEAD_DIM, kh, k_extra[1]).astype(BF16)

        vh = qkv[:, 2 * D_MODEL + hd * V_DIM:2 * D_MODEL + (hd + 1) * V_DIM]
        vt_ref[hd, 0] = jnp.concatenate([vh.T, denom_rows], axis=0).astype(BF16)


def _qkv(x, g, w):
    t, d = x.shape
    tm = TOKEN_TILE
    nc = t // tm
    return pl.pallas_call(
        _qkv_kernel,
        grid=(nc,),
        in_specs=[
            pl.BlockSpec((tm, d), lambda i: (i, 0)),
            _const_spec((1, d)),
            _const_spec(w.shape),
        ],
        out_specs=[
            pl.BlockSpec((N_HEADS, 1, 4, V_DIM, tm), lambda i: (0, i, 0, 0, 0)),
            pl.BlockSpec((N_HEADS, 2, 1, tm, V_DIM), lambda i: (0, 0, i, 0, 0)),
            pl.BlockSpec((N_HEADS, 1, V_ROWS, tm), lambda i: (0, i, 0, 0)),
            pl.BlockSpec((1, NORM_TABLE_ROWS, V_DIM), lambda i: (i, 0, 0)),
            pl.BlockSpec((2 * N_HEADS, tm), lambda i: (0, i)),
        ],
        out_shape=[
            jax.ShapeDtypeStruct((N_HEADS, nc, 4, V_DIM, tm), BF16),
            jax.ShapeDtypeStruct((N_HEADS, 2, nc, tm, V_DIM), BF16),
            jax.ShapeDtypeStruct((N_HEADS, nc, V_ROWS, tm), BF16),
            jax.ShapeDtypeStruct((nc, NORM_TABLE_ROWS, V_DIM), F32),
            jax.ShapeDtypeStruct((2 * N_HEADS, t), F32),
        ],
        compiler_params=_params(1),
        name="attn_qkv",
    )(x, g.reshape(1, d), w)


def _flash_kernel(slope_ref, qmax_ref, qmin_ref, smin_ref, kseq_ref, lq1_ref, lk1_ref, lq2_ref,
                  lk2_ref, sg_ref, qtok_ref, qt_ref, kk_ref, vt_ref, o_ref,
                  s_ref, mc_ref, m_ref, acc_ref, *, lambda_init, blk0):
    n_chunks, tkc = kk_ref.shape[1], kk_ref.shape[2]
    n_sub, tq = qt_ref.shape[0], qt_ref.shape[-1]
    hd = pl.program_id(1)
    slope, inv_slope = slope_ref[0, hd], slope_ref[1, hd]
    cq0 = pl.program_id(2) * n_sub
    col0 = blk0 + pl.program_id(0) * n_chunks + cq0
    k_norm = NORM_SLACK * kseq_ref[hd, pl.program_id(0)]
    subs = range(n_sub)

    def select(sub, values):
        out = values[0]
        for other, value in enumerate(values[1:], 1):
            out = jnp.where(sub == other, value, out)
        return out

    def plan(cq, gap):
        reach = (gap * inv_slope - 1.0) * (1.0 / tkc)
        n_dist = jnp.where(reach >= 1.0,
                           jnp.minimum(reach, float(n_chunks)).astype(jnp.int32) + 1, 1)
        n_left = jnp.minimum(n_dist, cq)
        return n_left, n_left + jnp.minimum(n_dist, n_chunks - 1 - cq)

    def item(cq, n_left, t):
        left = t < n_left
        c = jnp.where(left, cq - 1 - t, cq + 1 + t - n_left)
        return jnp.clip(c, 0, n_chunks - 1), left.astype(jnp.int32)

    def nearest(cq):
        return (jnp.where(cq > 0, cq - 1, jnp.minimum(cq + 1, n_chunks - 1)),
                (cq > 0).astype(jnp.int32))

    def run(total, trip, unroll):
        assert unroll & (unroll - 1) == 0

        def trips(t0, count):
            for r in range(count):
                trip(t0 + r)

        rem = total & (unroll - 1)
        done, size = 0, 1
        while size < unroll:
            pl.when((total & size) == size)(functools.partial(trips, done, size))
            done = done + (total & size)
            size *= 2

        def group(u, carry):
            trips(rem + unroll * u, unroll)
            return carry

        lax.fori_loop(0, total // unroll, group, 0)

    def scores(sub, c, sign, mp):
        return _dot(kk_ref[mp, c], qt_ref[sub, 2 * mp + sign])

    def issue_chunk(sub, c, sign, mp):
        s = scores(sub, c, sign, mp)
        s_ref[sub, mp] = s
        mc_ref[sub, mp] = jnp.max(s, axis=0, keepdims=True)

    def absorb(sub, cq, c, mp):
        shift = -slope * (tkc * jnp.abs(cq - c)).astype(F32)
        m_old = m_ref[sub, mp]
        m_new = jnp.maximum(m_old, mc_ref[sub, mp] + shift)
        p = jnp.exp2(s_ref[sub, mp] - (m_new - shift)).astype(BF16)
        acc_ref[sub, mp] = jnp.exp2(m_old - m_new) * acc_ref[sub, mp] + _dot(vt_ref[c], p)
        m_ref[sub, mp] = m_new

    jj = lax.broadcasted_iota(jnp.int32, (tkc, tq), 0)
    ii = lax.broadcasted_iota(jnp.int32, (tkc, tq), 1)
    fixup = (-2.0 * slope) * jnp.maximum(ii - jj, 0).astype(F32)

    def fixed_path():
        plans = [plan(cq0 + sub, k_norm * (qmax_ref[hd, col0 + sub] - qmin_ref[hd, col0 + sub])
                      + EXP_ZERO_BELOW) for sub in subs]
        for sub in subs:
            for mp in range(2):
                m_ref[sub, mp] = k_norm * qtok_ref[mp:mp + 1, sub * tq:(sub + 1) * tq]
        for sub in subs:
            s_ref[sub, 0] = scores(sub, cq0 + sub, 0, 0)
        for sub in subs:
            cq = cq0 + sub
            s_ref[sub, 1] = scores(sub, cq, 0, 1)
            acc_ref[sub, 0] = _dot(
                vt_ref[cq], jnp.exp2(s_ref[sub, 0] + (fixup - m_ref[sub, 0])).astype(BF16))
            s_ref[sub, 0] = scores(sub, *nearest(cq), 0)
            acc_ref[sub, 1] = _dot(
                vt_ref[cq], jnp.exp2(s_ref[sub, 1] + (fixup - m_ref[sub, 1])).astype(BF16))

        def visit(sub, carry):
            cq = cq0 + sub
            n_left = select(sub, [p[0] for p in plans])
            total = select(sub, [p[1] for p in plans])

            def absorb_fixed(c, mp):
                shift = -slope * (tkc * jnp.abs(cq - c)).astype(F32)
                p = jnp.exp2(s_ref[sub, mp] + (shift - m_ref[sub, mp])).astype(BF16)
                acc_ref[sub, mp] += _dot(vt_ref[c], p)

            def trip(t):
                c, sign = item(cq, n_left, t)
                s_ref[sub, 1] = scores(sub, c, sign, 1)
                absorb_fixed(c, 0)
                s_ref[sub, 0] = scores(sub, *item(cq, n_left, t + 1), 0)
                absorb_fixed(c, 1)

            run(total, trip, PIPE_UNROLL)
            return carry

        lax.fori_loop(0, n_sub, visit, 0)

    def running_path():
        diag = [[scores(sub, cq0 + sub, 0, mp) + fixup for mp in range(2)] for sub in subs]
        for sub in subs:
            issue_chunk(sub, *nearest(cq0 + sub), 0)
        for sub in subs:
            for mp in range(2):
                m_first = jnp.max(diag[sub][mp], axis=0, keepdims=True)
                m_ref[sub, mp] = m_first
                acc_ref[sub, mp] = _dot(vt_ref[cq0 + sub],
                                        jnp.exp2(diag[sub][mp] - m_first).astype(BF16))
        plans = [plan(cq0 + sub, k_norm * qmax_ref[hd, col0 + sub] + EXP_ZERO_BELOW
                      - jnp.min(m_ref[sub])) for sub in subs]

        def visit(sub, carry):
            cq = cq0 + sub
            n_left = select(sub, [p[0] for p in plans])
            total = select(sub, [p[1] for p in plans])

            def trip(t):
                c, sign = item(cq, n_left, t)
                issue_chunk(sub, c, sign, 1)
                absorb(sub, cq, c, 0)
                issue_chunk(sub, *item(cq, n_left, t + 1), 0)
                absorb(sub, cq, c, 1)

            run(total, trip, 1)
            return carry

        lax.fori_loop(0, n_sub, visit, 0)

    slack = [k_norm * qmax_ref[hd, col0 + sub] - smin_ref[hd, col0 + sub] for sub in subs]
    fixed_ok = functools.reduce(jnp.logical_and, [s < STAB_SLACK_BELOW for s in slack])
    pl.when(fixed_ok)(fixed_path)
    pl.when(jnp.logical_not(fixed_ok))(running_path)

    lam = (jnp.exp(jnp.sum(lq1_ref[...] * lk1_ref[...], keepdims=True))
           - jnp.exp(jnp.sum(lq2_ref[...] * lk2_ref[...], keepdims=True)) + lambda_init)
    for sub in range(n_sub):
        o_t = (acc_ref[sub, 0, :V_DIM] * (1.0 / acc_ref[sub, 0, V_DIM:V_DIM + 1])
               - (lam / acc_ref[sub, 1, V_DIM:V_DIM + 1]) * acc_ref[sub, 1, :V_DIM])
        scale = (lax.rsqrt(jnp.mean(o_t * o_t, axis=0, keepdims=True) + SUBLN_EPS)
                 * (1.0 - lambda_init))
        o_ref[sub * tq:(sub + 1) * tq, :] = (o_t * scale * sg_ref[...]).T.astype(o_ref.dtype)


def _flash(slopes, tables, lam_vecs, subln_g, qtok, qt, kk, vt, *, tok_start, n_seq, seq_len,
           lambda_init):
    q_max, q_min, s_min, k_max = tables
    tq = TOKEN_TILE
    ns = Q_BLOCKS_PER_STEP
    nq = seq_len // tq
    blk0 = tok_start // tq
    seq0 = tok_start // seq_len
    steps = nq // ns
    step0 = blk0 // ns
    assert nq % ns == 0 and blk0 % ns == 0
    smem = pl.BlockSpec(memory_space=pltpu.SMEM)
    vec = _const_spec((1, HEAD_DIM))
    return pl.pallas_call(
        functools.partial(_flash_kernel, lambda_init=lambda_init, blk0=blk0),
        grid=(n_seq, N_HEADS, steps),
        in_specs=[
            smem, smem, smem, smem, smem, vec, vec, vec, vec, _const_spec((V_DIM, tq)),
            pl.BlockSpec((None, 2, ns * tq), lambda b, h, i: (h, 0, step0 + b * steps + i)),
            pl.BlockSpec((None, ns, 4, V_DIM, tq),
                         lambda b, h, i: (h, step0 + b * steps + i, 0, 0, 0)),
            pl.BlockSpec((None, 2, nq, tq, V_DIM), lambda b, h, i: (h, 0, seq0 + b, 0, 0)),
            pl.BlockSpec((None, nq, V_ROWS, tq), lambda b, h, i: (h, seq0 + b, 0, 0)),
        ],
        out_specs=pl.BlockSpec((ns * tq, V_DIM), lambda b, h, i: (b * steps + i, h)),
        out_shape=jax.ShapeDtypeStruct((n_seq * seq_len, N_HEADS * V_DIM), BF16),
        scratch_shapes=[
            pltpu.VMEM((ns, 2, tq, tq), F32),
            pltpu.VMEM((ns, 2, 1, tq), F32),
            pltpu.VMEM((ns, 2, 1, tq), F32),
            pltpu.VMEM((ns, 2, V_ROWS, tq), F32),
        ],
        compiler_params=_params(3),
        name="diff_flash",
    )(slopes, q_max, q_min, s_min,
      k_max[:, blk0:blk0 + n_seq * nq].reshape(N_HEADS, n_seq, nq).max(axis=-1),
      *lam_vecs, jnp.broadcast_to(subln_g.reshape(V_DIM, 1), (V_DIM, tq)), qtok, qt, kk, vt)


def kernel(x_prompt, x_sample, norm1_g, norm2_g, final_g, mix_in_w, pool_w, pool_scale, conv_w,
           mix_out_w, attn_qkv_w, attn_out_w, lambda_q1, lambda_k1, lambda_q2, lambda_k2, subln_g,
           mlp_w1, mlp_w2):
    pb, ps, d = x_prompt.shape
    sb, ss, _ = x_sample.shape
    n_prompt = pb * ps
    n_sample = sb * ss
    x_parts = [x_prompt.reshape(n_prompt, d), x_sample.reshape(n_sample, d)]
    bounds = tuple(ps * b for b in range(pb)) + tuple(n_prompt + ss * b for b in range(sb + 1))
    slopes = jnp.exp2(-8.0 * (jnp.arange(N_HEADS, dtype=F32) + 1.0) / N_HEADS)
    slopes = jnp.stack([slopes * LOG2E, 1.0 / (slopes * LOG2E)])

    for i in range(DEPTH):
        j = i // 2
        w1 = mlp_w1[i].astype(BF16)
        w2 = mlp_w2[i].astype(BF16)
        attn_parts, attn_w = (), None
        if i % 2 == 0:
            x = _mixer(x_parts, norm1_g[i], mix_in_w[j].astype(BF16), pool_w[j].astype(BF16),
                       pool_scale[j], conv_w[j], mix_out_w[j].astype(BF16), bounds)
        else:
            x = x_parts[0]
            lambda_init = 0.8 - 0.6 * math.exp(-0.3 * i)
            qt, kk, vt, nrm, qtok = _qkv(x, norm1_g[i], attn_qkv_w[j].astype(BF16))
            per_head = nrm[:, :4, :2 * N_HEADS].reshape(-1, 4, N_HEADS, 2)
            tables = (jnp.sqrt(per_head[:, 0].max(axis=-1)).T, jnp.sqrt(per_head[:, 2].min(axis=-1)).T,
                      per_head[:, 3].min(axis=-1).T, jnp.sqrt(per_head[:, 1].max(axis=-1)).T)
            lam_vecs = [v[j].reshape(1, HEAD_DIM) for v in (lambda_q1, lambda_k1, lambda_q2, lambda_k2)]
            flash = functools.partial(_flash, slopes, tables, lam_vecs, subln_g[j],
                                      qtok.reshape(N_HEADS, 2, -1), qt, kk, vt,
                                      lambda_init=lambda_init)
            attn_parts = (flash(tok_start=0, n_seq=pb, seq_len=ps),
                          flash(tok_start=n_prompt, n_seq=sb, seq_len=ss))
            attn_w = attn_out_w[j].astype(BF16)
        if i < DEPTH - 1:
            x_parts = [_mlp(x, norm2_g[i], w1, w2, attn_parts, attn_w)]
        else:
            y_prompt, y_sample = (
                _mlp(x, norm2_g[i], w1, w2, attn_parts[k:k + 1], attn_w, final_g, r0, nr)
                for k, (r0, nr) in enumerate(((0, n_prompt), (n_prompt, n_sample))))

    return (y_prompt.reshape(pb, ps, d), y_sample.reshape(sb, ss, d))
```

```python
import functools
import math

import jax
import jax.numpy as jnp
import numpy as np
from jax import lax
from jax.experimental import pallas as pl
from jax.experimental.pallas import tpu as pltpu

D_MODEL = 1024
DEPTH = 4
POOL_WIDTH = D_MODEL // 2
N_POOL_GROUPS = 4
POOL_GROUP_DIM = POOL_WIDTH // N_POOL_GROUPS
POOL_WINDOWS = (2, 4, 8, 16)
CONV_WIDTH_CH = D_MODEL // 2
MIX_IN_COLS = POOL_WIDTH + 3 * CONV_WIDTH_CH
N_HEADS = 8
HEAD_DIM = D_MODEL // (2 * N_HEADS)
V_DIM = 2 * HEAD_DIM
ATTN_SCALE = HEAD_DIM ** -0.5
D_FF = 4 * D_MODEL
NORM_EPS = 1e-6
SUBLN_EPS = 1e-5

F32 = jnp.float32
BF16 = jnp.bfloat16

TOKEN_TILE = 512
HALO = 16
FF_CHUNK = 1024
VMEM_LIMIT_BYTES = 48 * 1024 * 1024
DENOM_ROWS = 16
V_ROWS = V_DIM + DENOM_ROWS
NORM_TABLE_ROWS = 8
PIPE_UNROLL = 8
Q_BLOCKS_PER_STEP = 2
EXP_ZERO_BELOW = 153.0
STAB_SLACK_BELOW = 80.0


def _bf16_pieces(value, n):
    pieces, rest = [], np.float32(value)
    for _ in range(n):
        piece = np.float32(np.asarray(rest, dtype=BF16))
        pieces.append(float(piece))
        rest = np.float32(rest - piece)
    return tuple(pieces)


LOG2E = float(np.float32(math.log2(math.e)))
LOG2E_PIECES = _bf16_pieces(LOG2E, 3)
NORM_SLACK = 1.01
POS_LO_MASK = 255
POS_HI_MASK = TOKEN_TILE - 1 - POS_LO_MASK


def _rms(x, g, eps):
    return x * lax.rsqrt(jnp.mean(x * x, axis=-1, keepdims=True) + eps) * g


def _dot(a, b):
    return jnp.dot(a, b, preferred_element_type=F32)


def _const_spec(shape):
    zeros = (0,) * len(shape)
    return pl.BlockSpec(shape, lambda *_: zeros)


def _part_starts(part_rows, rows):
    starts, s = [], 0
    for r in part_rows:
        starts.append(s // rows)
        s += r
    return starts


def _part_specs(parts, rows, block_of):
    specs = []
    for p, s0 in zip(parts, _part_starts([p.shape[0] for p in parts], rows)):
        nb = p.shape[0] // rows
        specs.append(pl.BlockSpec(
            (rows, p.shape[1]), lambda i, s0=s0, nb=nb: (jnp.clip(block_of(i) - s0, 0, nb - 1), 0)))
    return specs


def _pick_part(refs, block, part_rows, rows):
    value = refs[0][...]
    for ref, s0 in zip(refs[1:], _part_starts(part_rows, rows)[1:]):
        value = jnp.where(block >= s0, ref[...], value)
    return value


def _params(n_axes):
    return pltpu.CompilerParams(
        dimension_semantics=("arbitrary",) * n_axes,
        vmem_limit_bytes=VMEM_LIMIT_BYTES,
    )


def _mixer_kernel(*refs, bounds, part_rows):
    n = len(part_rows)
    xp_refs, x_refs, xn_refs = refs[:n], refs[n:2 * n], refs[2 * n:3 * n]
    g_ref, win_ref, pw_ref, ps_ref, cw_ref, wout_ref, o_ref, u_ref, z_ref = refs[3 * n:]
    tm = x_refs[0].shape[0]
    i = pl.program_id(0)
    per = tm // HALO
    last = sum(part_rows) // HALO - 1
    start = i * tm
    seq_start = jnp.int32(bounds[0])
    seq_end = jnp.int32(bounds[-1])
    for b in bounds[1:-1]:
        seq_start = jnp.where(start >= b, b, seq_start)
    for b in reversed(bounds[1:-1]):
        seq_end = jnp.where(start < b, b, seq_end)

    x = _pick_part(x_refs, i, part_rows, tm)
    xe = jnp.concatenate([
        _pick_part(xp_refs, jnp.maximum(i * per - 1, 0), part_rows, HALO),
        x,
        _pick_part(xn_refs, jnp.minimum((i + 1) * per, last), part_rows, HALO),
    ], axis=0)
    pos = start - HALO + lax.broadcasted_iota(jnp.int32, (tm + 2 * HALO, 1), 0)
    valid = (pos >= seq_start) & (pos < seq_end)
    he = _rms(xe, g_ref[...], NORM_EPS).astype(BF16)
    proj = _dot(he, win_ref[...])
    c0 = POOL_WIDTH
    u_ref[...] = jnp.where(valid, proj[:, :c0], 0.0)
    z_ref[...] = jnp.where(valid, proj[:, c0 + 2 * CONV_WIDTH_CH:] * proj[:, c0:c0 + CONV_WIDTH_CH], 0.0)

    rel = start - seq_start + lax.broadcasted_iota(jnp.int32, (tm, 1), 0)
    seq_len = seq_end - seq_start
    ys = []
    for g, w in enumerate(POOL_WINDOWS):
        cols = slice(g * POOL_GROUP_DIM, (g + 1) * POOL_GROUP_DIM)
        tot = u_ref[HALO - w // 2:HALO - w // 2 + tm, cols]
        for o in range(-w // 2 + 1, w // 2):
            tot = tot + u_ref[HALO + o:HALO + o + tm, cols]
        cnt = (jnp.minimum(rel + w // 2, seq_len) - jnp.maximum(rel - w // 2, 0)).astype(F32)
        d = tot / cnt - u_ref[HALO:HALO + tm, cols]
        ys.append(_dot(d.astype(BF16), pw_ref[g]))
    a_out = jnp.concatenate(ys, axis=-1) * ps_ref[...]

    conv = (cw_ref[0:1, :] * z_ref[HALO - 1:HALO - 1 + tm, :]
            + cw_ref[1:2, :] * z_ref[HALO:HALO + tm, :]
            + cw_ref[2:3, :] * z_ref[HALO + 1:HALO + 1 + tm, :])
    b_out = proj[HALO:HALO + tm, c0 + CONV_WIDTH_CH:c0 + 2 * CONV_WIDTH_CH] * conv
    mixed = jnp.concatenate([a_out, b_out], axis=-1).astype(BF16)
    o_ref[...] = x + _dot(mixed, wout_ref[...])


def _mixer(x_parts, g, win, pw, ps, cw, wout, bounds):
    d = x_parts[0].shape[1]
    part_rows = tuple(p.shape[0] for p in x_parts)
    t = sum(part_rows)
    tm = TOKEN_TILE
    per = tm // HALO
    last = t // HALO - 1
    return pl.pallas_call(
        functools.partial(_mixer_kernel, bounds=bounds, part_rows=part_rows),
        grid=(t // tm,),
        in_specs=[
            *_part_specs(x_parts, HALO, lambda i: jnp.maximum(i * per - 1, 0)),
            *_part_specs(x_parts, tm, lambda i: i),
            *_part_specs(x_parts, HALO, lambda i: jnp.minimum((i + 1) * per, last)),
            _const_spec((1, d)),
            _const_spec(win.shape),
            _const_spec(pw.shape),
            _const_spec((1, POOL_WIDTH)),
            _const_spec(cw.shape),
            _const_spec(wout.shape),
        ],
        out_specs=pl.BlockSpec((tm, d), lambda i: (i, 0)),
        out_shape=jax.ShapeDtypeStruct((t, d), F32),
        scratch_shapes=[
            pltpu.VMEM((tm + 2 * HALO, POOL_WIDTH), F32),
            pltpu.VMEM((tm + 2 * HALO, CONV_WIDTH_CH), F32),
        ],
        compiler_params=_params(1),
        name="even_mixer",
    )(*x_parts, *x_parts, *x_parts, g.reshape(1, d), win, pw, ps.reshape(1, POOL_WIDTH), cw, wout)


def _mlp_kernel(*refs, attn_rows, has_final):
    refs = list(refs)
    x_ref = refs.pop(0)
    if attn_rows:
        a_refs = [refs.pop(0) for _ in attn_rows]
        wo_ref = refs.pop(0)
    g_ref, w1_ref, w2_ref = refs[:3]
    refs = refs[3:]
    if has_final:
        fg_ref = refs.pop(0)
    o_ref = refs.pop(0)

    x = x_ref[...]
    if attn_rows:
        attn = _pick_part(a_refs, pl.program_id(0), attn_rows, x_ref.shape[0])
        x = x + _dot(attn, wo_ref[...])
    h = _rms(x, g_ref[...], NORM_EPS).astype(BF16)
    acc = x
    for c in range(D_FF // FF_CHUNK):
        cols = slice(c * FF_CHUNK, (c + 1) * FF_CHUNK)
        a = jnp.maximum(_dot(h, w1_ref[:, cols]), 0.0)
        acc = acc + _dot((a * a).astype(BF16), w2_ref[cols, :])
    if has_final:
        acc = _rms(acc, fg_ref[...], NORM_EPS)
    o_ref[...] = acc


def _mlp(x, g, w1, w2, attn_parts=(), attn_w=None, final_g=None, row_start=0, rows=None):
    d = x.shape[1]
    rows = x.shape[0] if rows is None else rows
    tm = TOKEN_TILE
    tile0 = row_start // tm
    row_spec = pl.BlockSpec((tm, d), lambda i: (i, 0))
    args = [x]
    in_specs = [pl.BlockSpec((tm, d), lambda i: (tile0 + i, 0))]
    if attn_parts:
        args += [*attn_parts, attn_w]
        in_specs += [*_part_specs(attn_parts, tm, lambda i: i), _const_spec(attn_w.shape)]
    args += [g.reshape(1, d), w1, w2]
    in_specs += [_const_spec((1, d)), _const_spec(w1.shape), _const_spec(w2.shape)]
    if final_g is not None:
        args.append(final_g.reshape(1, d))
        in_specs.append(_const_spec((1, d)))
    return pl.pallas_call(
        functools.partial(_mlp_kernel, attn_rows=tuple(p.shape[0] for p in attn_parts),
                          has_final=final_g is not None),
        grid=(rows // tm,),
        in_specs=in_specs,
        out_specs=row_spec,
        out_shape=jax.ShapeDtypeStruct((rows, d), F32),
        compiler_params=_params(1),
        name="sq_relu_mlp",
    )(*args)


def _qkv_kernel(x_ref, g_ref, w_ref, qt_ref, kk_ref, vt_ref, nrm_ref, qtok_ref):
    tm = x_ref.shape[0]
    h = _rms(x_ref[...], g_ref[...], NORM_EPS).astype(BF16)
    qkv = _dot(h, w_ref[...])
    q_all = (qkv[:, :D_MODEL] * (ATTN_SCALE * LOG2E)).astype(BF16).astype(F32)
    k_all = qkv[:, D_MODEL:2 * D_MODEL].astype(BF16).astype(F32)

    sel = (lax.shift_right_logical(lax.broadcasted_iota(jnp.int32, (D_MODEL, V_DIM), 0),
                                   HEAD_DIM.bit_length() - 1)
           == lax.broadcasted_iota(jnp.int32, (D_MODEL, V_DIM), 1)).astype(BF16)

    def per_map(a):
        return _dot(a.astype(BF16), sel)

    q_sq, k_sq, self_score = per_map(q_all * q_all), per_map(k_all * k_all), per_map(q_all * k_all)
    nrm_ref[0] = jnp.concatenate([
        jnp.max(q_sq, axis=0, keepdims=True), jnp.max(k_sq, axis=0, keepdims=True),
        jnp.min(self_score, axis=0, keepdims=True),
        jnp.zeros((NORM_TABLE_ROWS - 3, V_DIM), F32)], axis=0)
    qtok_ref[...] = jnp.sqrt(q_sq).T[:2 * N_HEADS]

    n_p = len(LOG2E_PIECES)

    def extras(slot, lo, hi, pos_first):
        pos_slot = slot if pos_first else slot - 2 * n_p
        const_slot = slot - 2 * n_p if pos_first else slot
        out = jnp.where((pos_slot >= 0) & (pos_slot < n_p), lo,
                        jnp.where((pos_slot >= n_p) & (pos_slot < 2 * n_p), hi, 0.0))
        for p, piece in enumerate(LOG2E_PIECES):
            out = jnp.where((const_slot == p) | (const_slot == n_p + p), piece, out)
        return out

    q_row = lax.broadcasted_iota(jnp.int32, (HEAD_DIM, tm), 0)
    q_tok = lax.broadcasted_iota(jnp.int32, (HEAD_DIM, tm), 1)
    q_base = extras(q_row, (q_tok & POS_LO_MASK).astype(F32), (q_tok & POS_HI_MASK).astype(F32), True)
    q_has_slope = q_row >= 2 * n_p
    k_lane = lax.broadcasted_iota(jnp.int32, (tm, V_DIM), 1)
    k_tok = lax.broadcasted_iota(jnp.int32, (tm, V_DIM), 0)
    k_lo = -(k_tok & POS_LO_MASK).astype(F32)
    k_hi = -(k_tok & POS_HI_MASK).astype(F32)
    k_base = [extras(k_lane - off, k_lo, k_hi, False) for off in (HEAD_DIM, 0)]
    k_has_slope = [(k_lane >= off) & (k_lane < off + 2 * n_p) for off in (HEAD_DIM, 0)]
    denom_rows = (lax.broadcasted_iota(jnp.int32, (DENOM_ROWS, tm), 0) == 0).astype(F32)

    for hd in range(N_HEADS):
        slope = 2.0 ** (-8.0 * (hd + 1) / N_HEADS)
        cols = slice(hd * V_DIM, (hd + 1) * V_DIM)
        q_t = q_all[:, cols].T
        top, bot = q_t[:HEAD_DIM], q_t[HEAD_DIM:]
        q_extra = jnp.where(q_has_slope, slope * q_base, q_base)
        qt_ref[hd, 0, 0] = jnp.concatenate([top, q_extra], axis=0).astype(BF16)
        qt_ref[hd, 0, 1] = jnp.concatenate([top, -q_extra], axis=0).astype(BF16)
        qt_ref[hd, 0, 2] = jnp.concatenate([q_extra, bot], axis=0).astype(BF16)
        qt_ref[hd, 0, 3] = jnp.concatenate([-q_extra, bot], axis=0).astype(BF16)

        kh = k_all[:, cols]
        k_extra = [jnp.where(k_has_slope[m], slope * k_base[m], k_base[m]) for m in range(2)]
        kk_ref[hd, 0, 0] = jnp.where(k_lane < HEAD_DIM, kh, k_extra[0]).astype(BF16)
        kk_ref[hd, 1, 0] = jnp.where(k_lane >= HEAD_DIM, kh, k_extra[1]).astype(BF16)

        vh = qkv[:, 2 * D_MODEL + hd * V_DIM:2 * D_MODEL + (hd + 1) * V_DIM]
        vt_ref[hd, 0] = jnp.concatenate([vh.T, denom_rows], axis=0).astype(BF16)


def _qkv(x, g, w):
    t, d = x.shape
    tm = TOKEN_TILE
    nc = t // tm
    return pl.pallas_call(
        _qkv_kernel,
        grid=(nc,),
        in_specs=[
            pl.BlockSpec((tm, d), lambda i: (i, 0)),
            _const_spec((1, d)),
            _const_spec(w.shape),
        ],
        out_specs=[
            pl.BlockSpec((N_HEADS, 1, 4, V_DIM, tm), lambda i: (0, i, 0, 0, 0)),
            pl.BlockSpec((N_HEADS, 2, 1, tm, V_DIM), lambda i: (0, 0, i, 0, 0)),
            pl.BlockSpec((N_HEADS, 1, V_ROWS, tm), lambda i: (0, i, 0, 0)),
            pl.BlockSpec((1, NORM_TABLE_ROWS, V_DIM), lambda i: (i, 0, 0)),
            pl.BlockSpec((2 * N_HEADS, tm), lambda i: (0, i)),
        ],
        out_shape=[
            jax.ShapeDtypeStruct((N_HEADS, nc, 4, V_DIM, tm), BF16),
            jax.ShapeDtypeStruct((N_HEADS, 2, nc, tm, V_DIM), BF16),
            jax.ShapeDtypeStruct((N_HEADS, nc, V_ROWS, tm), BF16),
            jax.ShapeDtypeStruct((nc, NORM_TABLE_ROWS, V_DIM), F32),
            jax.ShapeDtypeStruct((2 * N_HEADS, t), F32),
        ],
        compiler_params=_params(1),
        name="attn_qkv",
    )(x, g.reshape(1, d), w)


def _flash_kernel(slope_ref, qmax_ref, smin_ref, kseq_ref, lq1_ref, lk1_ref, lq2_ref, lk2_ref,
                  sg_ref, qtok_ref, qt_ref, kk_ref, vt_ref, o_ref,
                  s_ref, mc_ref, m_ref, acc_ref, *, lambda_init, blk0):
    n_chunks, tkc = kk_ref.shape[1], kk_ref.shape[2]
    n_sub, tq = qt_ref.shape[0], qt_ref.shape[-1]
    hd = pl.program_id(1)
    slope, inv_slope = slope_ref[0, hd], slope_ref[1, hd]
    cq0 = pl.program_id(2) * n_sub
    col0 = blk0 + pl.program_id(0) * n_chunks + cq0
    k_norm = NORM_SLACK * kseq_ref[hd, pl.program_id(0)]
    subs = range(n_sub)

    def select(sub, values):
        out = values[0]
        for other, value in enumerate(values[1:], 1):
            out = jnp.where(sub == other, value, out)
        return out

    def plan(cq, gap):
        reach = (gap * inv_slope - 1.0) * (1.0 / tkc)
        n_dist = jnp.where(reach >= 1.0,
                           jnp.minimum(reach, float(n_chunks)).astype(jnp.int32) + 1, 1)
        n_left = jnp.minimum(n_dist, cq)
        return n_left, n_left + jnp.minimum(n_dist, n_chunks - 1 - cq)

    def item(cq, n_left, t):
        left = t < n_left
        c = jnp.where(left, cq - 1 - t, cq + 1 + t - n_left)
        return jnp.clip(c, 0, n_chunks - 1), left.astype(jnp.int32)

    def nearest(cq):
        return (jnp.where(cq > 0, cq - 1, jnp.minimum(cq + 1, n_chunks - 1)),
                (cq > 0).astype(jnp.int32))

    def run(total, trip, unroll):
        assert unroll & (unroll - 1) == 0

        def trips(t0, count):
            for r in range(count):
                trip(t0 + r)

        rem = total & (unroll - 1)
        done, size = 0, 1
        while size < unroll:
            pl.when((total & size) == size)(functools.partial(trips, done, size))
            done = done + (total & size)
            size *= 2

        def group(u, carry):
            trips(rem + unroll * u, unroll)
            return carry

        lax.fori_loop(0, total // unroll, group, 0)

    def scores(sub, c, sign, mp):
        return _dot(kk_ref[mp, c], qt_ref[sub, 2 * mp + sign])

    def issue_chunk(sub, c, sign, mp):
        s = scores(sub, c, sign, mp)
        s_ref[sub, mp] = s
        mc_ref[sub, mp] = jnp.max(s, axis=0, keepdims=True)

    def absorb(sub, cq, c, mp):
        shift = -slope * (tkc * jnp.abs(cq - c)).astype(F32)
        m_old = m_ref[sub, mp]
        m_new = jnp.maximum(m_old, mc_ref[sub, mp] + shift)
        p = jnp.exp2(s_ref[sub, mp] - (m_new - shift)).astype(BF16)
        acc_ref[sub, mp] = jnp.exp2(m_old - m_new) * acc_ref[sub, mp] + _dot(vt_ref[c], p)
        m_ref[sub, mp] = m_new

    jj = lax.broadcasted_iota(jnp.int32, (tkc, tq), 0)
    ii = lax.broadcasted_iota(jnp.int32, (tkc, tq), 1)
    fixup = (-2.0 * slope) * jnp.maximum(ii - jj, 0).astype(F32)

    def fixed_path():
        plans = [plan(cq0 + sub, EXP_ZERO_BELOW) for sub in subs]
        for sub in subs:
            for mp in range(2):
                m_ref[sub, mp] = k_norm * qtok_ref[mp:mp + 1, sub * tq:(sub + 1) * tq]
        for sub in subs:
            s_ref[sub, 0] = scores(sub, cq0 + sub, 0, 0)
        for sub in subs:
            cq = cq0 + sub
            s_ref[sub, 1] = scores(sub, cq, 0, 1)
            acc_ref[sub, 0] = _dot(
                vt_ref[cq], jnp.exp2(s_ref[sub, 0] + (fixup - m_ref[sub, 0])).astype(BF16))
            s_ref[sub, 0] = scores(sub, *nearest(cq), 0)
            acc_ref[sub, 1] = _dot(
                vt_ref[cq], jnp.exp2(s_ref[sub, 1] + (fixup - m_ref[sub, 1])).astype(BF16))

        def visit(sub, carry):
            cq = cq0 + sub
            n_left = select(sub, [p[0] for p in plans])
            total = select(sub, [p[1] for p in plans])

            def absorb_fixed(c, mp):
                shift = -slope * (tkc * jnp.abs(cq - c)).astype(F32)
                p = jnp.exp2(s_ref[sub, mp] + (shift - m_ref[sub, mp])).astype(BF16)
                acc_ref[sub, mp] += _dot(vt_ref[c], p)

            def trip(t):
                c, sign = item(cq, n_left, t)
                s_ref[sub, 1] = scores(sub, c, sign, 1)
                absorb_fixed(c, 0)
                s_ref[sub, 0] = scores(sub, *item(cq, n_left, t + 1), 0)
                absorb_fixed(c, 1)

            run(total, trip, PIPE_UNROLL)
            return carry

        lax.fori_loop(0, n_sub, visit, 0)

    def running_path():
        diag = [[scores(sub, cq0 + sub, 0, mp) + fixup for mp in range(2)] for sub in subs]
        for sub in subs:
            issue_chunk(sub, *nearest(cq0 + sub), 0)
        for sub in subs:
            for mp in range(2):
                m_first = jnp.max(diag[sub][mp], axis=0, keepdims=True)
                m_ref[sub, mp] = m_first
                acc_ref[sub, mp] = _dot(vt_ref[cq0 + sub],
                                        jnp.exp2(diag[sub][mp] - m_first).astype(BF16))
        plans = [plan(cq0 + sub, k_norm * qmax_ref[hd, col0 + sub] + EXP_ZERO_BELOW
                      - jnp.min(m_ref[sub])) for sub in subs]

        def visit(sub, carry):
            cq = cq0 + sub
            n_left = select(sub, [p[0] for p in plans])
            total = select(sub, [p[1] for p in plans])

            def trip(t):
                c, sign = item(cq, n_left, t)
                issue_chunk(sub, c, sign, 1)
                absorb(sub, cq, c, 0)
                issue_chunk(sub, *item(cq, n_left, t + 1), 0)
                absorb(sub, cq, c, 1)

            run(total, trip, 1)
            return carry

        lax.fori_loop(0, n_sub, visit, 0)

    slack = [k_norm * qmax_ref[hd, col0 + sub] - smin_ref[hd, col0 + sub] for sub in subs]
    fixed_ok = functools.reduce(jnp.logical_and, [s < STAB_SLACK_BELOW for s in slack])
    pl.when(fixed_ok)(fixed_path)
    pl.when(jnp.logical_not(fixed_ok))(running_path)

    lam = (jnp.exp(jnp.sum(lq1_ref[...] * lk1_ref[...], keepdims=True))
           - jnp.exp(jnp.sum(lq2_ref[...] * lk2_ref[...], keepdims=True)) + lambda_init)
    for sub in range(n_sub):
        o_t = (acc_ref[sub, 0, :V_DIM] * (1.0 / acc_ref[sub, 0, V_DIM:V_DIM + 1])
               - (lam / acc_ref[sub, 1, V_DIM:V_DIM + 1]) * acc_ref[sub, 1, :V_DIM])
        scale = (lax.rsqrt(jnp.mean(o_t * o_t, axis=0, keepdims=True) + SUBLN_EPS)
                 * (1.0 - lambda_init))
        o_ref[sub * tq:(sub + 1) * tq, :] = (o_t * scale * sg_ref[...]).T.astype(o_ref.dtype)


def _flash(slopes, tables, lam_vecs, subln_g, qtok, qt, kk, vt, *, tok_start, n_seq, seq_len,
           lambda_init):
    q_max, s_min, k_max = tables
    tq = TOKEN_TILE
    ns = Q_BLOCKS_PER_STEP
    nq = seq_len // tq
    blk0 = tok_start // tq
    seq0 = tok_start // seq_len
    steps = nq // ns
    step0 = blk0 // ns
    assert nq % ns == 0 and blk0 % ns == 0
    smem = pl.BlockSpec(memory_space=pltpu.SMEM)
    vec = _const_spec((1, HEAD_DIM))
    return pl.pallas_call(
        functools.partial(_flash_kernel, lambda_init=lambda_init, blk0=blk0),
        grid=(n_seq, N_HEADS, steps),
        in_specs=[
            smem, smem, smem, smem, vec, vec, vec, vec, _const_spec((V_DIM, tq)),
            pl.BlockSpec((None, 2, ns * tq), lambda b, h, i: (h, 0, step0 + b * steps + i)),
            pl.BlockSpec((None, ns, 4, V_DIM, tq),
                         lambda b, h, i: (h, step0 + b * steps + i, 0, 0, 0)),
            pl.BlockSpec((None, 2, nq, tq, V_DIM), lambda b, h, i: (h, 0, seq0 + b, 0, 0)),
            pl.BlockSpec((None, nq, V_ROWS, tq), lambda b, h, i: (h, seq0 + b, 0, 0)),
        ],
        out_specs=pl.BlockSpec((ns * tq, V_DIM), lambda b, h, i: (b * steps + i, h)),
        out_shape=jax.ShapeDtypeStruct((n_seq * seq_len, N_HEADS * V_DIM), BF16),
        scratch_shapes=[
            pltpu.VMEM((ns, 2, tq, tq), F32),
            pltpu.VMEM((ns, 2, 1, tq), F32),
            pltpu.VMEM((ns, 2, 1, tq), F32),
            pltpu.VMEM((ns, 2, V_ROWS, tq), F32),
        ],
        compiler_params=_params(3),
        name="diff_flash",
    )(slopes, q_max, s_min,
      k_max[:, blk0:blk0 + n_seq * nq].reshape(N_HEADS, n_seq, nq).max(axis=-1),
      *lam_vecs, jnp.broadcast_to(subln_g.reshape(V_DIM, 1), (V_DIM, tq)), qtok, qt, kk, vt)


def kernel(x_prompt, x_sample, norm1_g, norm2_g, final_g, mix_in_w, pool_w, pool_scale, conv_w,
           mix_out_w, attn_qkv_w, attn_out_w, lambda_q1, lambda_k1, lambda_q2, lambda_k2, subln_g,
           mlp_w1, mlp_w2):
    pb, ps, d = x_prompt.shape
    sb, ss, _ = x_sample.shape
    n_prompt = pb * ps
    n_sample = sb * ss
    x_parts = [x_prompt.reshape(n_prompt, d), x_sample.reshape(n_sample, d)]
    bounds = tuple(ps * b for b in range(pb)) + tuple(n_prompt + ss * b for b in range(sb + 1))
    slopes = jnp.exp2(-8.0 * (jnp.arange(N_HEADS, dtype=F32) + 1.0) / N_HEADS)
    slopes = jnp.stack([slopes * LOG2E, 1.0 / (slopes * LOG2E)])

    for i in range(DEPTH):
        j = i // 2
        w1 = mlp_w1[i].astype(BF16)
        w2 = mlp_w2[i].astype(BF16)
        attn_parts, attn_w = (), None
        if i % 2 == 0:
            x = _mixer(x_parts, norm1_g[i], mix_in_w[j].astype(BF16), pool_w[j].astype(BF16),
                       pool_scale[j], conv_w[j], mix_out_w[j].astype(BF16), bounds)
        else:
            x = x_parts[0]
            lambda_init = 0.8 - 0.6 * math.exp(-0.3 * i)
            qt, kk, vt, nrm, qtok = _qkv(x, norm1_g[i], attn_qkv_w[j].astype(BF16))
            per_head = nrm[:, :3, :2 * N_HEADS].reshape(-1, 3, N_HEADS, 2)
            tables = (jnp.sqrt(per_head[:, 0].max(axis=-1)).T, per_head[:, 2].min(axis=-1).T,
                      jnp.sqrt(per_head[:, 1].max(axis=-1)).T)
            lam_vecs = [v[j].reshape(1, HEAD_DIM) for v in (lambda_q1, lambda_k1, lambda_q2, lambda_k2)]
            flash = functools.partial(_flash, slopes, tables, lam_vecs, subln_g[j],
                                      qtok.reshape(N_HEADS, 2, -1), qt, kk, vt,
                                      lambda_init=lambda_init)
            attn_parts = (flash(tok_start=0, n_seq=pb, seq_len=ps),
                          flash(tok_start=n_prompt, n_seq=sb, seq_len=ss))
            attn_w = attn_out_w[j].astype(BF16)
        if i < DEPTH - 1:
            x_parts = [_mlp(x, norm2_g[i], w1, w2, attn_parts, attn_w)]
        else:
            y_prompt, y_sample = (
                _mlp(x, norm2_g[i], w1, w2, attn_parts[k:k + 1], attn_w, final_g, r0, nr)
                for k, (r0, nr) in enumerate(((0, n_prompt), (n_prompt, n_sample))))

    return (y_prompt.reshape(pb, ps, d), y_sample.reshape(sb, ss, d))
```

```python
import functools
import math

import jax
import jax.numpy as jnp
import numpy as np
from jax import lax
from jax.experimental import pallas as pl
from jax.experimental.pallas import tpu as pltpu

D_MODEL = 1024
DEPTH = 4
POOL_WIDTH = D_MODEL // 2
N_POOL_GROUPS = 4
POOL_GROUP_DIM = POOL_WIDTH // N_POOL_GROUPS
POOL_WINDOWS = (2, 4, 8, 16)
CONV_WIDTH_CH = D_MODEL // 2
MIX_IN_COLS = POOL_WIDTH + 3 * CONV_WIDTH_CH
N_HEADS = 8
HEAD_DIM = D_MODEL // (2 * N_HEADS)
V_DIM = 2 * HEAD_DIM
ATTN_SCALE = HEAD_DIM ** -0.5
D_FF = 4 * D_MODEL
NORM_EPS = 1e-6
SUBLN_EPS = 1e-5

F32 = jnp.float32
BF16 = jnp.bfloat16

TOKEN_TILE = 512
HALO = 16
FF_CHUNK = 1024
VMEM_LIMIT_BYTES = 58 * 1024 * 1024
DENOM_ROWS = 16
V_ROWS = V_DIM + DENOM_ROWS
NORM_TABLE_ROWS = 8
PIPE_UNROLL = 8
Q_BLOCKS_PER_STEP = 4
EXP_ZERO_BELOW = 153.0
STAB_SLACK_BELOW = 80.0


def _bf16_pieces(value, n):
    pieces, rest = [], np.float32(value)
    for _ in range(n):
        piece = np.float32(np.asarray(rest, dtype=BF16))
        pieces.append(float(piece))
        rest = np.float32(rest - piece)
    return tuple(pieces)


LOG2E = float(np.float32(math.log2(math.e)))
LOG2E_PIECES = _bf16_pieces(LOG2E, 3)
NORM_SLACK = 1.01
POS_LO_MASK = 255
POS_HI_MASK = TOKEN_TILE - 1 - POS_LO_MASK


def _rms(x, g, eps):
    return x * lax.rsqrt(jnp.mean(x * x, axis=-1, keepdims=True) + eps) * g


def _dot(a, b):
    return jnp.dot(a, b, preferred_element_type=F32)


def _const_spec(shape):
    zeros = (0,) * len(shape)
    return pl.BlockSpec(shape, lambda *_: zeros)


def _part_starts(part_rows, rows):
    starts, s = [], 0
    for r in part_rows:
        starts.append(s // rows)
        s += r
    return starts


def _part_specs(parts, rows, block_of):
    specs = []
    for p, s0 in zip(parts, _part_starts([p.shape[0] for p in parts], rows)):
        nb = p.shape[0] // rows
        specs.append(pl.BlockSpec(
            (rows, p.shape[1]), lambda i, s0=s0, nb=nb: (jnp.clip(block_of(i) - s0, 0, nb - 1), 0)))
    return specs


def _pick_part(refs, block, part_rows, rows):
    value = refs[0][...]
    for ref, s0 in zip(refs[1:], _part_starts(part_rows, rows)[1:]):
        value = jnp.where(block >= s0, ref[...], value)
    return value


def _params(n_axes):
    return pltpu.CompilerParams(
        dimension_semantics=("arbitrary",) * n_axes,
        vmem_limit_bytes=VMEM_LIMIT_BYTES,
    )


def _mixer_kernel(*refs, bounds, part_rows):
    n = len(part_rows)
    xp_refs, x_refs, xn_refs = refs[:n], refs[n:2 * n], refs[2 * n:3 * n]
    g_ref, win_ref, pw_ref, ps_ref, cw_ref, wout_ref, o_ref, u_ref, z_ref = refs[3 * n:]
    tm = x_refs[0].shape[0]
    i = pl.program_id(0)
    per = tm // HALO
    last = sum(part_rows) // HALO - 1
    start = i * tm
    seq_start = jnp.int32(bounds[0])
    seq_end = jnp.int32(bounds[-1])
    for b in bounds[1:-1]:
        seq_start = jnp.where(start >= b, b, seq_start)
    for b in reversed(bounds[1:-1]):
        seq_end = jnp.where(start < b, b, seq_end)

    x = _pick_part(x_refs, i, part_rows, tm)
    xe = jnp.concatenate([
        _pick_part(xp_refs, jnp.maximum(i * per - 1, 0), part_rows, HALO),
        x,
        _pick_part(xn_refs, jnp.minimum((i + 1) * per, last), part_rows, HALO),
    ], axis=0)
    pos = start - HALO + lax.broadcasted_iota(jnp.int32, (tm + 2 * HALO, 1), 0)
    valid = (pos >= seq_start) & (pos < seq_end)
    he = _rms(xe, g_ref[...], NORM_EPS).astype(BF16)
    proj = _dot(he, win_ref[...])
    c0 = POOL_WIDTH
    u_ref[...] = jnp.where(valid, proj[:, :c0], 0.0)
    z_ref[...] = jnp.where(valid, proj[:, c0 + 2 * CONV_WIDTH_CH:] * proj[:, c0:c0 + CONV_WIDTH_CH], 0.0)

    rel = start - seq_start + lax.broadcasted_iota(jnp.int32, (tm, 1), 0)
    seq_len = seq_end - seq_start
    ys = []
    for g, w in enumerate(POOL_WINDOWS):
        cols = slice(g * POOL_GROUP_DIM, (g + 1) * POOL_GROUP_DIM)
        tot = u_ref[HALO - w // 2:HALO - w // 2 + tm, cols]
        for o in range(-w // 2 + 1, w // 2):
            tot = tot + u_ref[HALO + o:HALO + o + tm, cols]
        cnt = (jnp.minimum(rel + w // 2, seq_len) - jnp.maximum(rel - w // 2, 0)).astype(F32)
        d = tot / cnt - u_ref[HALO:HALO + tm, cols]
        ys.append(_dot(d.astype(BF16), pw_ref[g]))
    a_out = jnp.concatenate(ys, axis=-1) * ps_ref[...]

    conv = (cw_ref[0:1, :] * z_ref[HALO - 1:HALO - 1 + tm, :]
            + cw_ref[1:2, :] * z_ref[HALO:HALO + tm, :]
            + cw_ref[2:3, :] * z_ref[HALO + 1:HALO + 1 + tm, :])
    b_out = proj[HALO:HALO + tm, c0 + CONV_WIDTH_CH:c0 + 2 * CONV_WIDTH_CH] * conv
    mixed = jnp.concatenate([a_out, b_out], axis=-1).astype(BF16)
    o_ref[...] = x + _dot(mixed, wout_ref[...])


def _mixer(x_parts, g, win, pw, ps, cw, wout, bounds):
    d = x_parts[0].shape[1]
    part_rows = tuple(p.shape[0] for p in x_parts)
    t = sum(part_rows)
    tm = TOKEN_TILE
    per = tm // HALO
    last = t // HALO - 1
    return pl.pallas_call(
        functools.partial(_mixer_kernel, bounds=bounds, part_rows=part_rows),
        grid=(t // tm,),
        in_specs=[
            *_part_specs(x_parts, HALO, lambda i: jnp.maximum(i * per - 1, 0)),
            *_part_specs(x_parts, tm, lambda i: i),
            *_part_specs(x_parts, HALO, lambda i: jnp.minimum((i + 1) * per, last)),
            _const_spec((1, d)),
            _const_spec(win.shape),
            _const_spec(pw.shape),
            _const_spec((1, POOL_WIDTH)),
            _const_spec(cw.shape),
            _const_spec(wout.shape),
        ],
        out_specs=pl.BlockSpec((tm, d), lambda i: (i, 0)),
        out_shape=jax.ShapeDtypeStruct((t, d), F32),
        scratch_shapes=[
            pltpu.VMEM((tm + 2 * HALO, POOL_WIDTH), F32),
            pltpu.VMEM((tm + 2 * HALO, CONV_WIDTH_CH), F32),
        ],
        compiler_params=_params(1),
        name="even_mixer",
    )(*x_parts, *x_parts, *x_parts, g.reshape(1, d), win, pw, ps.reshape(1, POOL_WIDTH), cw, wout)


def _mlp_kernel(*refs, attn_rows, has_final):
    refs = list(refs)
    x_ref = refs.pop(0)
    if attn_rows:
        a_refs = [refs.pop(0) for _ in attn_rows]
        wo_ref = refs.pop(0)
    g_ref, w1_ref, w2_ref = refs[:3]
    refs = refs[3:]
    if has_final:
        fg_ref = refs.pop(0)
    o_ref = refs.pop(0)

    x = x_ref[...]
    if attn_rows:
        attn = _pick_part(a_refs, pl.program_id(0), attn_rows, x_ref.shape[0])
        x = x + _dot(attn, wo_ref[...])
    h = _rms(x, g_ref[...], NORM_EPS).astype(BF16)
    acc = x
    for c in range(D_FF // FF_CHUNK):
        cols = slice(c * FF_CHUNK, (c + 1) * FF_CHUNK)
        a = jnp.maximum(_dot(h, w1_ref[:, cols]), 0.0)
        acc = acc + _dot((a * a).astype(BF16), w2_ref[cols, :])
    if has_final:
        acc = _rms(acc, fg_ref[...], NORM_EPS)
    o_ref[...] = acc


def _mlp(x, g, w1, w2, attn_parts=(), attn_w=None, final_g=None, row_start=0, rows=None):
    d = x.shape[1]
    rows = x.shape[0] if rows is None else rows
    tm = TOKEN_TILE
    tile0 = row_start // tm
    row_spec = pl.BlockSpec((tm, d), lambda i: (i, 0))
    args = [x]
    in_specs = [pl.BlockSpec((tm, d), lambda i: (tile0 + i, 0))]
    if attn_parts:
        args += [*attn_parts, attn_w]
        in_specs += [*_part_specs(attn_parts, tm, lambda i: i), _const_spec(attn_w.shape)]
    args += [g.reshape(1, d), w1, w2]
    in_specs += [_const_spec((1, d)), _const_spec(w1.shape), _const_spec(w2.shape)]
    if final_g is not None:
        args.append(final_g.reshape(1, d))
        in_specs.append(_const_spec((1, d)))
    return pl.pallas_call(
        functools.partial(_mlp_kernel, attn_rows=tuple(p.shape[0] for p in attn_parts),
                          has_final=final_g is not None),
        grid=(rows // tm,),
        in_specs=in_specs,
        out_specs=row_spec,
        out_shape=jax.ShapeDtypeStruct((rows, d), F32),
        compiler_params=_params(1),
        name="sq_relu_mlp",
    )(*args)


def _qkv_kernel(x_ref, g_ref, w_ref, qt_ref, kk_ref, vt_ref, nrm_ref, qtok_ref):
    tm = x_ref.shape[0]
    h = _rms(x_ref[...], g_ref[...], NORM_EPS).astype(BF16)
    qkv = _dot(h, w_ref[...])
    q_all = (qkv[:, :D_MODEL] * (ATTN_SCALE * LOG2E)).astype(BF16).astype(F32)
    k_all = qkv[:, D_MODEL:2 * D_MODEL].astype(BF16).astype(F32)

    sel = (lax.shift_right_logical(lax.broadcasted_iota(jnp.int32, (D_MODEL, V_DIM), 0),
                                   HEAD_DIM.bit_length() - 1)
           == lax.broadcasted_iota(jnp.int32, (D_MODEL, V_DIM), 1)).astype(BF16)

    def per_map(a):
        return _dot(a.astype(BF16), sel)

    q_sq, k_sq, self_score = per_map(q_all * q_all), per_map(k_all * k_all), per_map(q_all * k_all)
    nrm_ref[0] = jnp.concatenate([
        jnp.max(q_sq, axis=0, keepdims=True), jnp.max(k_sq, axis=0, keepdims=True),
        jnp.min(self_score, axis=0, keepdims=True),
        jnp.zeros((NORM_TABLE_ROWS - 3, V_DIM), F32)], axis=0)
    qtok_ref[...] = jnp.sqrt(q_sq).T[:2 * N_HEADS]

    n_p = len(LOG2E_PIECES)

    def extras(slot, lo, hi, pos_first):
        pos_slot = slot if pos_first else slot - 2 * n_p
        const_slot = slot - 2 * n_p if pos_first else slot
        out = jnp.where((pos_slot >= 0) & (pos_slot < n_p), lo,
                        jnp.where((pos_slot >= n_p) & (pos_slot < 2 * n_p), hi, 0.0))
        for p, piece in enumerate(LOG2E_PIECES):
            out = jnp.where((const_slot == p) | (const_slot == n_p + p), piece, out)
        return out

    q_row = lax.broadcasted_iota(jnp.int32, (HEAD_DIM, tm), 0)
    q_tok = lax.broadcasted_iota(jnp.int32, (HEAD_DIM, tm), 1)
    q_base = extras(q_row, (q_tok & POS_LO_MASK).astype(F32), (q_tok & POS_HI_MASK).astype(F32), True)
    q_has_slope = q_row >= 2 * n_p
    k_lane = lax.broadcasted_iota(jnp.int32, (tm, V_DIM), 1)
    k_tok = lax.broadcasted_iota(jnp.int32, (tm, V_DIM), 0)
    k_lo = -(k_tok & POS_LO_MASK).astype(F32)
    k_hi = -(k_tok & POS_HI_MASK).astype(F32)
    k_base = [extras(k_lane - off, k_lo, k_hi, False) for off in (HEAD_DIM, 0)]
    k_has_slope = [(k_lane >= off) & (k_lane < off + 2 * n_p) for off in (HEAD_DIM, 0)]
    denom_rows = (lax.broadcasted_iota(jnp.int32, (DENOM_ROWS, tm), 0) == 0).astype(F32)

    for hd in range(N_HEADS):
        slope = 2.0 ** (-8.0 * (hd + 1) / N_HEADS)
        cols = slice(hd * V_DIM, (hd + 1) * V_DIM)
        q_t = q_all[:, cols].T
        top, bot = q_t[:HEAD_DIM], q_t[HEAD_DIM:]
        q_extra = jnp.where(q_has_slope, slope * q_base, q_base)
        qt_ref[hd, 0, 0] = jnp.concatenate([top, q_extra], axis=0).astype(BF16)
        qt_ref[hd, 0, 1] = jnp.concatenate([top, -q_extra], axis=0).astype(BF16)
        qt_ref[hd, 0, 2] = jnp.concatenate([q_extra, bot], axis=0).astype(BF16)
        qt_ref[hd, 0, 3] = jnp.concatenate([-q_extra, bot], axis=0).astype(BF16)

        kh = k_all[:, cols]
        k_extra = [jnp.where(k_has_slope[m], slope * k_base[m], k_base[m]) for m in range(2)]
        kk_ref[hd, 0, 0] = jnp.where(k_lane < HEAD_DIM, kh, k_extra[0]).astype(BF16)
        kk_ref[hd, 1, 0] = jnp.where(k_lane >= HEAD_DIM, kh, k_extra[1]).astype(BF16)

        vh = qkv[:, 2 * D_MODEL + hd * V_DIM:2 * D_MODEL + (hd + 1) * V_DIM]
        vt_ref[hd, 0] = jnp.concatenate([vh.T, denom_rows], axis=0).astype(BF16)


def _qkv(x, g, w):
    t, d = x.shape
    tm = TOKEN_TILE
    nc = t // tm
    return pl.pallas_call(
        _qkv_kernel,
        grid=(nc,),
        in_specs=[
            pl.BlockSpec((tm, d), lambda i: (i, 0)),
            _const_spec((1, d)),
            _const_spec(w.shape),
        ],
        out_specs=[
            pl.BlockSpec((N_HEADS, 1, 4, V_DIM, tm), lambda i: (0, i, 0, 0, 0)),
            pl.BlockSpec((N_HEADS, 2, 1, tm, V_DIM), lambda i: (0, 0, i, 0, 0)),
            pl.BlockSpec((N_HEADS, 1, V_ROWS, tm), lambda i: (0, i, 0, 0)),
            pl.BlockSpec((1, NORM_TABLE_ROWS, V_DIM), lambda i: (i, 0, 0)),
            pl.BlockSpec((2 * N_HEADS, tm), lambda i: (0, i)),
        ],
        out_shape=[
            jax.ShapeDtypeStruct((N_HEADS, nc, 4, V_DIM, tm), BF16),
            jax.ShapeDtypeStruct((N_HEADS, 2, nc, tm, V_DIM), BF16),
            jax.ShapeDtypeStruct((N_HEADS, nc, V_ROWS, tm), BF16),
            jax.ShapeDtypeStruct((nc, NORM_TABLE_ROWS, V_DIM), F32),
            jax.ShapeDtypeStruct((2 * N_HEADS, t), F32),
        ],
        compiler_params=_params(1),
        name="attn_qkv",
    )(x, g.reshape(1, d), w)


def _flash_kernel(slope_ref, qmax_ref, smin_ref, kseq_ref, lq1_ref, lk1_ref, lq2_ref, lk2_ref,
                  sg_ref, qtok_ref, qt_ref, kk_ref, vt_ref, o_ref,
                  s_ref, mc_ref, m_ref, acc_ref, *, lambda_init, blk0):
    n_chunks, tkc = kk_ref.shape[1], kk_ref.shape[2]
    n_sub, tq = qt_ref.shape[0], qt_ref.shape[-1]
    hd = pl.program_id(1)
    slope, inv_slope = slope_ref[0, hd], slope_ref[1, hd]
    cq0 = pl.program_id(2) * n_sub
    col0 = blk0 + pl.program_id(0) * n_chunks + cq0
    k_norm = NORM_SLACK * kseq_ref[hd, pl.program_id(0)]
    subs = range(n_sub)

    def select(sub, values):
        out = values[0]
        for other, value in enumerate(values[1:], 1):
            out = jnp.where(sub == other, value, out)
        return out

    def plan(cq, gap):
        reach = (gap * inv_slope - 1.0) * (1.0 / tkc)
        n_dist = jnp.where(reach >= 1.0,
                           jnp.minimum(reach, float(n_chunks)).astype(jnp.int32) + 1, 1)
        n_left = jnp.minimum(n_dist, cq)
        return n_left, n_left + jnp.minimum(n_dist, n_chunks - 1 - cq)

    def item(cq, n_left, t):
        left = t < n_left
        c = jnp.where(left, cq - 1 - t, cq + 1 + t - n_left)
        return jnp.clip(c, 0, n_chunks - 1), left.astype(jnp.int32)

    def nearest(cq):
        return (jnp.where(cq > 0, cq - 1, jnp.minimum(cq + 1, n_chunks - 1)),
                (cq > 0).astype(jnp.int32))

    def run(total, trip, unroll):
        assert unroll & (unroll - 1) == 0

        def trips(t0, count):
            for r in range(count):
                trip(t0 + r)

        rem = total & (unroll - 1)
        done, size = 0, 1
        while size < unroll:
            pl.when((total & size) == size)(functools.partial(trips, done, size))
            done = done + (total & size)
            size *= 2

        def group(u, carry):
            trips(rem + unroll * u, unroll)
            return carry

        lax.fori_loop(0, total // unroll, group, 0)

    def scores(sub, c, sign, mp):
        return _dot(kk_ref[mp, c], qt_ref[sub, 2 * mp + sign])

    def issue_chunk(sub, c, sign, mp):
        s = scores(sub, c, sign, mp)
        s_ref[sub, mp] = s
        mc_ref[sub, mp] = jnp.max(s, axis=0, keepdims=True)

    def absorb(sub, cq, c, mp):
        shift = -slope * (tkc * jnp.abs(cq - c)).astype(F32)
        m_old = m_ref[sub, mp]
        m_new = jnp.maximum(m_old, mc_ref[sub, mp] + shift)
        p = jnp.exp2(s_ref[sub, mp] - (m_new - shift)).astype(BF16)
        acc_ref[sub, mp] = jnp.exp2(m_old - m_new) * acc_ref[sub, mp] + _dot(vt_ref[c], p)
        m_ref[sub, mp] = m_new

    jj = lax.broadcasted_iota(jnp.int32, (tkc, tq), 0)
    ii = lax.broadcasted_iota(jnp.int32, (tkc, tq), 1)
    fixup = (-2.0 * slope) * jnp.maximum(ii - jj, 0).astype(F32)

    def fixed_path():
        plans = [plan(cq0 + sub, EXP_ZERO_BELOW) for sub in subs]
        for sub in subs:
            for mp in range(2):
                m_ref[sub, mp] = k_norm * qtok_ref[mp:mp + 1, sub * tq:(sub + 1) * tq]
        for sub in subs:
            s_ref[sub, 0] = scores(sub, cq0 + sub, 0, 0)
        for sub in subs:
            cq = cq0 + sub
            s_ref[sub, 1] = scores(sub, cq, 0, 1)
            acc_ref[sub, 0] = _dot(
                vt_ref[cq], jnp.exp2(s_ref[sub, 0] + (fixup - m_ref[sub, 0])).astype(BF16))
            s_ref[sub, 0] = scores(sub, *nearest(cq), 0)
            acc_ref[sub, 1] = _dot(
                vt_ref[cq], jnp.exp2(s_ref[sub, 1] + (fixup - m_ref[sub, 1])).astype(BF16))

        def visit(sub, carry):
            cq = cq0 + sub
            n_left = select(sub, [p[0] for p in plans])
            total = select(sub, [p[1] for p in plans])

            def absorb_fixed(c, mp):
                shift = -slope * (tkc * jnp.abs(cq - c)).astype(F32)
                p = jnp.exp2(s_ref[sub, mp] + (shift - m_ref[sub, mp])).astype(BF16)
                acc_ref[sub, mp] += _dot(vt_ref[c], p)

            def trip(t):
                c, sign = item(cq, n_left, t)
                s_ref[sub, 1] = scores(sub, c, sign, 1)
                absorb_fixed(c, 0)
                s_ref[sub, 0] = scores(sub, *item(cq, n_left, t + 1), 0)
                absorb_fixed(c, 1)

            run(total, trip, PIPE_UNROLL)
            return carry

        lax.fori_loop(0, n_sub, visit, 0)

    def running_path():
        diag = [[scores(sub, cq0 + sub, 0, mp) + fixup for mp in range(2)] for sub in subs]
        for sub in subs:
            issue_chunk(sub, *nearest(cq0 + sub), 0)
        for sub in subs:
            for mp in range(2):
                m_first = jnp.max(diag[sub][mp], axis=0, keepdims=True)
                m_ref[sub, mp] = m_first
                acc_ref[sub, mp] = _dot(vt_ref[cq0 + sub],
                                        jnp.exp2(diag[sub][mp] - m_first).astype(BF16))
        plans = [plan(cq0 + sub, k_norm * qmax_ref[hd, col0 + sub] + EXP_ZERO_BELOW
                      - jnp.min(m_ref[sub])) for sub in subs]

        def visit(sub, carry):
            cq = cq0 + sub
            n_left = select(sub, [p[0] for p in plans])
            total = select(sub, [p[1] for p in plans])

            def trip(t):
                c, sign = item(cq, n_left, t)
                issue_chunk(sub, c, sign, 1)
                absorb(sub, cq, c, 0)
                issue_chunk(sub, *item(cq, n_left, t + 1), 0)
                absorb(sub, cq, c, 1)

            run(total, trip, 1)
            return carry

        lax.fori_loop(0, n_sub, visit, 0)

    slack = [k_norm * qmax_ref[hd, col0 + sub] - smin_ref[hd, col0 + sub] for sub in subs]
    fixed_ok = functools.reduce(jnp.logical_and, [s < STAB_SLACK_BELOW for s in slack])
    pl.when(fixed_ok)(fixed_path)
    pl.when(jnp.logical_not(fixed_ok))(running_path)

    lam = (jnp.exp(jnp.sum(lq1_ref[...] * lk1_ref[...], keepdims=True))
           - jnp.exp(jnp.sum(lq2_ref[...] * lk2_ref[...], keepdims=True)) + lambda_init)
    for sub in range(n_sub):
        o_t = (acc_ref[sub, 0, :V_DIM] * (1.0 / acc_ref[sub, 0, V_DIM:V_DIM + 1])
               - (lam / acc_ref[sub, 1, V_DIM:V_DIM + 1]) * acc_ref[sub, 1, :V_DIM])
        scale = (lax.rsqrt(jnp.mean(o_t * o_t, axis=0, keepdims=True) + SUBLN_EPS)
                 * (1.0 - lambda_init))
        o_ref[sub * tq:(sub + 1) * tq, :] = (o_t * scale * sg_ref[...]).T.astype(o_ref.dtype)


def _flash(slopes, tables, lam_vecs, subln_g, qtok, qt, kk, vt, *, tok_start, n_seq, seq_len,
           lambda_init):
    q_max, s_min, k_max = tables
    tq = TOKEN_TILE
    ns = Q_BLOCKS_PER_STEP
    nq = seq_len // tq
    blk0 = tok_start // tq
    seq0 = tok_start // seq_len
    steps = nq // ns
    step0 = blk0 // ns
    assert nq % ns == 0 and blk0 % ns == 0
    smem = pl.BlockSpec(memory_space=pltpu.SMEM)
    vec = _const_spec((1, HEAD_DIM))
    return pl.pallas_call(
        functools.partial(_flash_kernel, lambda_init=lambda_init, blk0=blk0),
        grid=(n_seq, N_HEADS, steps),
        in_specs=[
            smem, smem, smem, smem, vec, vec, vec, vec, _const_spec((V_DIM, tq)),
            pl.BlockSpec((None, 2, ns * tq), lambda b, h, i: (h, 0, step0 + b * steps + i)),
            pl.BlockSpec((None, ns, 4, V_DIM, tq),
                         lambda b, h, i: (h, step0 + b * steps + i, 0, 0, 0)),
            pl.BlockSpec((None, 2, nq, tq, V_DIM), lambda b, h, i: (h, 0, seq0 + b, 0, 0)),
            pl.BlockSpec((None, nq, V_ROWS, tq), lambda b, h, i: (h, seq0 + b, 0, 0)),
        ],
        out_specs=pl.BlockSpec((ns * tq, V_DIM), lambda b, h, i: (b * steps + i, h)),
        out_shape=jax.ShapeDtypeStruct((n_seq * seq_len, N_HEADS * V_DIM), BF16),
        scratch_shapes=[
            pltpu.VMEM((ns, 2, tq, tq), F32),
            pltpu.VMEM((ns, 2, 1, tq), F32),
            pltpu.VMEM((ns, 2, 1, tq), F32),
            pltpu.VMEM((ns, 2, V_ROWS, tq), F32),
        ],
        compiler_params=_params(3),
        name="diff_flash",
    )(slopes, q_max, s_min,
      k_max[:, blk0:blk0 + n_seq * nq].reshape(N_HEADS, n_seq, nq).max(axis=-1),
      *lam_vecs, jnp.broadcast_to(subln_g.reshape(V_DIM, 1), (V_DIM, tq)), qtok, qt, kk, vt)


def kernel(x_prompt, x_sample, norm1_g, norm2_g, final_g, mix_in_w, pool_w, pool_scale, conv_w,
           mix_out_w, attn_qkv_w, attn_out_w, lambda_q1, lambda_k1, lambda_q2, lambda_k2, subln_g,
           mlp_w1, mlp_w2):
    pb, ps, d = x_prompt.shape
    sb, ss, _ = x_sample.shape
    n_prompt = pb * ps
    n_sample = sb * ss
    x_parts = [x_prompt.reshape(n_prompt, d), x_sample.reshape(n_sample, d)]
    bounds = tuple(ps * b for b in range(pb)) + tuple(n_prompt + ss * b for b in range(sb + 1))
    slopes = jnp.exp2(-8.0 * (jnp.arange(N_HEADS, dtype=F32) + 1.0) / N_HEADS)
    slopes = jnp.stack([slopes * LOG2E, 1.0 / (slopes * LOG2E)])

    for i in range(DEPTH):
        j = i // 2
        w1 = mlp_w1[i].astype(BF16)
        w2 = mlp_w2[i].astype(BF16)
        attn_parts, attn_w = (), None
        if i % 2 == 0:
            x = _mixer(x_parts, norm1_g[i], mix_in_w[j].astype(BF16), pool_w[j].astype(BF16),
                       pool_scale[j], conv_w[j], mix_out_w[j].astype(BF16), bounds)
        else:
            x = x_parts[0]
            lambda_init = 0.8 - 0.6 * math.exp(-0.3 * i)
            qt, kk, vt, nrm, qtok = _qkv(x, norm1_g[i], attn_qkv_w[j].astype(BF16))
            per_head = nrm[:, :3, :2 * N_HEADS].reshape(-1, 3, N_HEADS, 2)
            tables = (jnp.sqrt(per_head[:, 0].max(axis=-1)).T, per_head[:, 2].min(axis=-1).T,
                      jnp.sqrt(per_head[:, 1].max(axis=-1)).T)
            lam_vecs = [v[j].reshape(1, HEAD_DIM) for v in (lambda_q1, lambda_k1, lambda_q2, lambda_k2)]
            flash = functools.partial(_flash, slopes, tables, lam_vecs, subln_g[j],
                                      qtok.reshape(N_HEADS, 2, -1), qt, kk, vt,
                                      lambda_init=lambda_init)
            attn_parts = (flash(tok_start=0, n_seq=pb, seq_len=ps),
                          flash(tok_start=n_prompt, n_seq=sb, seq_len=ss))
            attn_w = attn_out_w[j].astype(BF16)
        if i < DEPTH - 1:
            x_parts = [_mlp(x, norm2_g[i], w1, w2, attn_parts, attn_w)]
        else:
            y_prompt, y_sample = (
                _mlp(x, norm2_g[i], w1, w2, attn_parts[k:k + 1], attn_w, final_g, r0, nr)
                for k, (r0, nr) in enumerate(((0, n_prompt), (n_prompt, n_sample))))

    return (y_prompt.reshape(pb, ps, d), y_sample.reshape(sb, ss, d))
```

```python
import functools
import math

import jax
import jax.numpy as jnp
import numpy as np
from jax import lax
from jax.experimental import pallas as pl
from jax.experimental.pallas import tpu as pltpu

D_MODEL = 1024
DEPTH = 4
POOL_WIDTH = D_MODEL // 2
N_POOL_GROUPS = 4
POOL_GROUP_DIM = POOL_WIDTH // N_POOL_GROUPS
POOL_WINDOWS = (2, 4, 8, 16)
CONV_WIDTH_CH = D_MODEL // 2
MIX_IN_COLS = POOL_WIDTH + 3 * CONV_WIDTH_CH
N_HEADS = 8
HEAD_DIM = D_MODEL // (2 * N_HEADS)
V_DIM = 2 * HEAD_DIM
ATTN_SCALE = HEAD_DIM ** -0.5
D_FF = 4 * D_MODEL
NORM_EPS = 1e-6
SUBLN_EPS = 1e-5

F32 = jnp.float32
BF16 = jnp.bfloat16

TOKEN_TILE = 512
HALO = 16
FF_CHUNK = 1024
VMEM_LIMIT_BYTES = 48 * 1024 * 1024
DENOM_ROWS = 16
V_ROWS = V_DIM + DENOM_ROWS
NORM_TABLE_ROWS = 8
PIPE_UNROLL = 8
Q_BLOCKS_PER_STEP = 2
EXP_ZERO_BELOW = 153.0
STAB_SLACK_BELOW = 80.0
COMPACT_DISTANCES = (1, 2)
MASKED_BIAS = 1e30


def _bf16_pieces(value, n):
    pieces, rest = [], np.float32(value)
    for _ in range(n):
        piece = np.float32(np.asarray(rest, dtype=BF16))
        pieces.append(float(piece))
        rest = np.float32(rest - piece)
    return tuple(pieces)


LOG2E = float(np.float32(math.log2(math.e)))
LOG2E_PIECES = _bf16_pieces(LOG2E, 3)
NORM_SLACK = 1.01
POS_LO_MASK = 255
POS_HI_MASK = TOKEN_TILE - 1 - POS_LO_MASK


def _rms(x, g, eps):
    return x * lax.rsqrt(jnp.mean(x * x, axis=-1, keepdims=True) + eps) * g


def _dot(a, b):
    return jnp.dot(a, b, preferred_element_type=F32)


def _const_spec(shape):
    zeros = (0,) * len(shape)
    return pl.BlockSpec(shape, lambda *_: zeros)


def _part_starts(part_rows, rows):
    starts, s = [], 0
    for r in part_rows:
        starts.append(s // rows)
        s += r
    return starts


def _part_specs(parts, rows, block_of):
    specs = []
    for p, s0 in zip(parts, _part_starts([p.shape[0] for p in parts], rows)):
        nb = p.shape[0] // rows
        specs.append(pl.BlockSpec(
            (rows, p.shape[1]), lambda i, s0=s0, nb=nb: (jnp.clip(block_of(i) - s0, 0, nb - 1), 0)))
    return specs


def _pick_part(refs, block, part_rows, rows):
    value = refs[0][...]
    for ref, s0 in zip(refs[1:], _part_starts(part_rows, rows)[1:]):
        value = jnp.where(block >= s0, ref[...], value)
    return value


def _params(n_axes):
    return pltpu.CompilerParams(
        dimension_semantics=("arbitrary",) * n_axes,
        vmem_limit_bytes=VMEM_LIMIT_BYTES,
    )


def _mixer_kernel(*refs, bounds, part_rows):
    n = len(part_rows)
    xp_refs, x_refs, xn_refs = refs[:n], refs[n:2 * n], refs[2 * n:3 * n]
    g_ref, win_ref, pw_ref, ps_ref, cw_ref, wout_ref, o_ref, u_ref, z_ref = refs[3 * n:]
    tm = x_refs[0].shape[0]
    i = pl.program_id(0)
    per = tm // HALO
    last = sum(part_rows) // HALO - 1
    start = i * tm
    seq_start = jnp.int32(bounds[0])
    seq_end = jnp.int32(bounds[-1])
    for b in bounds[1:-1]:
        seq_start = jnp.where(start >= b, b, seq_start)
    for b in reversed(bounds[1:-1]):
        seq_end = jnp.where(start < b, b, seq_end)

    x = _pick_part(x_refs, i, part_rows, tm)
    xe = jnp.concatenate([
        _pick_part(xp_refs, jnp.maximum(i * per - 1, 0), part_rows, HALO),
        x,
        _pick_part(xn_refs, jnp.minimum((i + 1) * per, last), part_rows, HALO),
    ], axis=0)
    pos = start - HALO + lax.broadcasted_iota(jnp.int32, (tm + 2 * HALO, 1), 0)
    valid = (pos >= seq_start) & (pos < seq_end)
    he = _rms(xe, g_ref[...], NORM_EPS).astype(BF16)
    proj = _dot(he, win_ref[...])
    c0 = POOL_WIDTH
    u_ref[...] = jnp.where(valid, proj[:, :c0], 0.0)
    z_ref[...] = jnp.where(valid, proj[:, c0 + 2 * CONV_WIDTH_CH:] * proj[:, c0:c0 + CONV_WIDTH_CH], 0.0)

    rel = start - seq_start + lax.broadcasted_iota(jnp.int32, (tm, 1), 0)
    seq_len = seq_end - seq_start
    ys = []
    for g, w in enumerate(POOL_WINDOWS):
        cols = slice(g * POOL_GROUP_DIM, (g + 1) * POOL_GROUP_DIM)
        tot = u_ref[HALO - w // 2:HALO - w // 2 + tm, cols]
        for o in range(-w // 2 + 1, w // 2):
            tot = tot + u_ref[HALO + o:HALO + o + tm, cols]
        cnt = (jnp.minimum(rel + w // 2, seq_len) - jnp.maximum(rel - w // 2, 0)).astype(F32)
        d = tot / cnt - u_ref[HALO:HALO + tm, cols]
        ys.append(_dot(d.astype(BF16), pw_ref[g]))
    a_out = jnp.concatenate(ys, axis=-1) * ps_ref[...]

    conv = (cw_ref[0:1, :] * z_ref[HALO - 1:HALO - 1 + tm, :]
            + cw_ref[1:2, :] * z_ref[HALO:HALO + tm, :]
            + cw_ref[2:3, :] * z_ref[HALO + 1:HALO + 1 + tm, :])
    b_out = proj[HALO:HALO + tm, c0 + CONV_WIDTH_CH:c0 + 2 * CONV_WIDTH_CH] * conv
    mixed = jnp.concatenate([a_out, b_out], axis=-1).astype(BF16)
    o_ref[...] = x + _dot(mixed, wout_ref[...])


def _mixer(x_parts, g, win, pw, ps, cw, wout, bounds):
    d = x_parts[0].shape[1]
    part_rows = tuple(p.shape[0] for p in x_parts)
    t = sum(part_rows)
    tm = TOKEN_TILE
    per = tm // HALO
    last = t // HALO - 1
    return pl.pallas_call(
        functools.partial(_mixer_kernel, bounds=bounds, part_rows=part_rows),
        grid=(t // tm,),
        in_specs=[
            *_part_specs(x_parts, HALO, lambda i: jnp.maximum(i * per - 1, 0)),
            *_part_specs(x_parts, tm, lambda i: i),
            *_part_specs(x_parts, HALO, lambda i: jnp.minimum((i + 1) * per, last)),
            _const_spec((1, d)),
            _const_spec(win.shape),
            _const_spec(pw.shape),
            _const_spec((1, POOL_WIDTH)),
            _const_spec(cw.shape),
            _const_spec(wout.shape),
        ],
        out_specs=pl.BlockSpec((tm, d), lambda i: (i, 0)),
        out_shape=jax.ShapeDtypeStruct((t, d), F32),
        scratch_shapes=[
            pltpu.VMEM((tm + 2 * HALO, POOL_WIDTH), F32),
            pltpu.VMEM((tm + 2 * HALO, CONV_WIDTH_CH), F32),
        ],
        compiler_params=_params(1),
        name="even_mixer",
    )(*x_parts, *x_parts, *x_parts, g.reshape(1, d), win, pw, ps.reshape(1, POOL_WIDTH), cw, wout)


def _mlp_kernel(*refs, attn_rows, has_final):
    refs = list(refs)
    x_ref = refs.pop(0)
    if attn_rows:
        a_refs = [refs.pop(0) for _ in attn_rows]
        wo_ref = refs.pop(0)
    g_ref, w1_ref, w2_ref = refs[:3]
    refs = refs[3:]
    if has_final:
        fg_ref = refs.pop(0)
    o_ref = refs.pop(0)

    x = x_ref[...]
    if attn_rows:
        attn = _pick_part(a_refs, pl.program_id(0), attn_rows, x_ref.shape[0])
        x = x + _dot(attn, wo_ref[...])
    h = _rms(x, g_ref[...], NORM_EPS).astype(BF16)
    acc = x
    for c in range(D_FF // FF_CHUNK):
        cols = slice(c * FF_CHUNK, (c + 1) * FF_CHUNK)
        a = jnp.maximum(_dot(h, w1_ref[:, cols]), 0.0)
        acc = acc + _dot((a * a).astype(BF16), w2_ref[cols, :])
    if has_final:
        acc = _rms(acc, fg_ref[...], NORM_EPS)
    o_ref[...] = acc


def _mlp(x, g, w1, w2, attn_parts=(), attn_w=None, final_g=None, row_start=0, rows=None):
    d = x.shape[1]
    rows = x.shape[0] if rows is None else rows
    tm = TOKEN_TILE
    tile0 = row_start // tm
    row_spec = pl.BlockSpec((tm, d), lambda i: (i, 0))
    args = [x]
    in_specs = [pl.BlockSpec((tm, d), lambda i: (tile0 + i, 0))]
    if attn_parts:
        args += [*attn_parts, attn_w]
        in_specs += [*_part_specs(attn_parts, tm, lambda i: i), _const_spec(attn_w.shape)]
    args += [g.reshape(1, d), w1, w2]
    in_specs += [_const_spec((1, d)), _const_spec(w1.shape), _const_spec(w2.shape)]
    if final_g is not None:
        args.append(final_g.reshape(1, d))
        in_specs.append(_const_spec((1, d)))
    return pl.pallas_call(
        functools.partial(_mlp_kernel, attn_rows=tuple(p.shape[0] for p in attn_parts),
                          has_final=final_g is not None),
        grid=(rows // tm,),
        in_specs=in_specs,
        out_specs=row_spec,
        out_shape=jax.ShapeDtypeStruct((rows, d), F32),
        compiler_params=_params(1),
        name="sq_relu_mlp",
    )(*args)


def _qkv_kernel(x_ref, g_ref, w_ref, qt_ref, kk_ref, vt_ref, nrm_ref, qtok_ref):
    tm = x_ref.shape[0]
    h = _rms(x_ref[...], g_ref[...], NORM_EPS).astype(BF16)
    qkv = _dot(h, w_ref[...])
    q_all = (qkv[:, :D_MODEL] * (ATTN_SCALE * LOG2E)).astype(BF16).astype(F32)
    k_all = qkv[:, D_MODEL:2 * D_MODEL].astype(BF16).astype(F32)

    sel = (lax.shift_right_logical(lax.broadcasted_iota(jnp.int32, (D_MODEL, V_DIM), 0),
                                   HEAD_DIM.bit_length() - 1)
           == lax.broadcasted_iota(jnp.int32, (D_MODEL, V_DIM), 1)).astype(BF16)

    def per_map(a):
        return _dot(a.astype(BF16), sel)

    q_sq, k_sq, self_score = per_map(q_all * q_all), per_map(k_all * k_all), per_map(q_all * k_all)
    nrm_ref[0] = jnp.concatenate([
        jnp.max(q_sq, axis=0, keepdims=True), jnp.max(k_sq, axis=0, keepdims=True),
        jnp.min(self_score, axis=0, keepdims=True),
        jnp.zeros((NORM_TABLE_ROWS - 3, V_DIM), F32)], axis=0)
    qtok_ref[...] = jnp.sqrt(q_sq).T[:2 * N_HEADS]

    n_p = len(LOG2E_PIECES)

    def extras(slot, lo, hi, pos_first):
        pos_slot = slot if pos_first else slot - 2 * n_p
        const_slot = slot - 2 * n_p if pos_first else slot
        out = jnp.where((pos_slot >= 0) & (pos_slot < n_p), lo,
                        jnp.where((pos_slot >= n_p) & (pos_slot < 2 * n_p), hi, 0.0))
        for p, piece in enumerate(LOG2E_PIECES):
            out = jnp.where((const_slot == p) | (const_slot == n_p + p), piece, out)
        return out

    q_row = lax.broadcasted_iota(jnp.int32, (HEAD_DIM, tm), 0)
    q_tok = lax.broadcasted_iota(jnp.int32, (HEAD_DIM, tm), 1)
    q_base = extras(q_row, (q_tok & POS_LO_MASK).astype(F32), (q_tok & POS_HI_MASK).astype(F32), True)
    q_has_slope = q_row >= 2 * n_p
    k_lane = lax.broadcasted_iota(jnp.int32, (tm, V_DIM), 1)
    k_tok = lax.broadcasted_iota(jnp.int32, (tm, V_DIM), 0)
    k_lo = -(k_tok & POS_LO_MASK).astype(F32)
    k_hi = -(k_tok & POS_HI_MASK).astype(F32)
    k_base = [extras(k_lane - off, k_lo, k_hi, False) for off in (HEAD_DIM, 0)]
    k_has_slope = [(k_lane >= off) & (k_lane < off + 2 * n_p) for off in (HEAD_DIM, 0)]
    denom_rows = (lax.broadcasted_iota(jnp.int32, (DENOM_ROWS, tm), 0) == 0).astype(F32)

    for hd in range(N_HEADS):
        slope = 2.0 ** (-8.0 * (hd + 1) / N_HEADS)
        cols = slice(hd * V_DIM, (hd + 1) * V_DIM)
        q_t = q_all[:, cols].T
        top, bot = q_t[:HEAD_DIM], q_t[HEAD_DIM:]
        q_extra = jnp.where(q_has_slope, slope * q_base, q_base)
        qt_ref[hd, 0, 0] = jnp.concatenate([top, q_extra], axis=0).astype(BF16)
        qt_ref[hd, 0, 1] = jnp.concatenate([top, -q_extra], axis=0).astype(BF16)
        qt_ref[hd, 0, 2] = jnp.concatenate([q_extra, bot], axis=0).astype(BF16)
        qt_ref[hd, 0, 3] = jnp.concatenate([-q_extra, bot], axis=0).astype(BF16)

        kh = k_all[:, cols]
        k_extra = [jnp.where(k_has_slope[m], slope * k_base[m], k_base[m]) for m in range(2)]
        kk_ref[hd, 0, 0] = jnp.where(k_lane < HEAD_DIM, kh, k_extra[0]).astype(BF16)
        kk_ref[hd, 1, 0] = jnp.where(k_lane >= HEAD_DIM, kh, k_extra[1]).astype(BF16)

        vh = qkv[:, 2 * D_MODEL + hd * V_DIM:2 * D_MODEL + (hd + 1) * V_DIM]
        vt_ref[hd, 0] = jnp.concatenate([vh.T, denom_rows], axis=0).astype(BF16)


def _qkv(x, g, w):
    t, d = x.shape
    tm = TOKEN_TILE
    nc = t // tm
    return pl.pallas_call(
        _qkv_kernel,
        grid=(nc,),
        in_specs=[
            pl.BlockSpec((tm, d), lambda i: (i, 0)),
            _const_spec((1, d)),
            _const_spec(w.shape),
        ],
        out_specs=[
            pl.BlockSpec((N_HEADS, 1, 4, V_DIM, tm), lambda i: (0, i, 0, 0, 0)),
            pl.BlockSpec((N_HEADS, 2, 1, tm, V_DIM), lambda i: (0, 0, i, 0, 0)),
            pl.BlockSpec((N_HEADS, 1, V_ROWS, tm), lambda i: (0, i, 0, 0)),
            pl.BlockSpec((1, NORM_TABLE_ROWS, V_DIM), lambda i: (i, 0, 0)),
            pl.BlockSpec((2 * N_HEADS, tm), lambda i: (0, i)),
        ],
        out_shape=[
            jax.ShapeDtypeStruct((N_HEADS, nc, 4, V_DIM, tm), BF16),
            jax.ShapeDtypeStruct((N_HEADS, 2, nc, tm, V_DIM), BF16),
            jax.ShapeDtypeStruct((N_HEADS, nc, V_ROWS, tm), BF16),
            jax.ShapeDtypeStruct((nc, NORM_TABLE_ROWS, V_DIM), F32),
            jax.ShapeDtypeStruct((2 * N_HEADS, t), F32),
        ],
        compiler_params=_params(1),
        name="attn_qkv",
    )(x, g.reshape(1, d), w)


def _flash_kernel(slope_ref, qmax_ref, smin_ref, kseq_ref, lq1_ref, lk1_ref, lq2_ref, lk2_ref,
                  sg_ref, qtok_ref, qt_ref, kk_ref, vt_ref, o_ref,
                  s_ref, mc_ref, m_ref, acc_ref, *, lambda_init, blk0):
    n_chunks, tkc = kk_ref.shape[1], kk_ref.shape[2]
    n_sub, tq = qt_ref.shape[0], qt_ref.shape[-1]
    hd = pl.program_id(1)
    slope, inv_slope = slope_ref[0, hd], slope_ref[1, hd]
    cq0 = pl.program_id(2) * n_sub
    col0 = blk0 + pl.program_id(0) * n_chunks + cq0
    k_norm = NORM_SLACK * kseq_ref[hd, pl.program_id(0)]
    subs = range(n_sub)

    def select(sub, values):
        out = values[0]
        for other, value in enumerate(values[1:], 1):
            out = jnp.where(sub == other, value, out)
        return out

    def span(gap):
        reach = (gap * inv_slope - 1.0) * (1.0 / tkc)
        return jnp.where(reach >= 1.0,
                         jnp.minimum(reach, float(n_chunks)).astype(jnp.int32) + 1, 1)

    def plan(cq, n_dist):
        n_left = jnp.minimum(n_dist, cq)
        return n_left, n_left + jnp.minimum(n_dist, n_chunks - 1 - cq)

    def item(cq, n_left, t):
        left = t < n_left
        c = jnp.where(left, cq - 1 - t, cq + 1 + t - n_left)
        return jnp.clip(c, 0, n_chunks - 1), left.astype(jnp.int32)

    def nearest(cq):
        return (jnp.where(cq > 0, cq - 1, jnp.minimum(cq + 1, n_chunks - 1)),
                (cq > 0).astype(jnp.int32))

    def run(total, trip, unroll):
        assert unroll & (unroll - 1) == 0

        def trips(t0, count):
            for r in range(count):
                trip(t0 + r)

        rem = total & (unroll - 1)
        done, size = 0, 1
        while size < unroll:
            pl.when((total & size) == size)(functools.partial(trips, done, size))
            done = done + (total & size)
            size *= 2

        def group(u, carry):
            trips(rem + unroll * u, unroll)
            return carry

        lax.fori_loop(0, total // unroll, group, 0)

    def scores(sub, c, sign, mp):
        return _dot(kk_ref[mp, c], qt_ref[sub, 2 * mp + sign])

    def issue_chunk(sub, c, sign, mp):
        s = scores(sub, c, sign, mp)
        s_ref[sub, mp] = s
        mc_ref[sub, mp] = jnp.max(s, axis=0, keepdims=True)

    def absorb(sub, cq, c, mp):
        shift = -slope * (tkc * jnp.abs(cq - c)).astype(F32)
        m_old = m_ref[sub, mp]
        m_new = jnp.maximum(m_old, mc_ref[sub, mp] + shift)
        p = jnp.exp2(s_ref[sub, mp] - (m_new - shift)).astype(BF16)
        acc_ref[sub, mp] = jnp.exp2(m_old - m_new) * acc_ref[sub, mp] + _dot(vt_ref[c], p)
        m_ref[sub, mp] = m_new

    jj = lax.broadcasted_iota(jnp.int32, (tkc, tq), 0)
    ii = lax.broadcasted_iota(jnp.int32, (tkc, tq), 1)
    fixup = (-2.0 * slope) * jnp.maximum(ii - jj, 0).astype(F32)

    def fixed_path():
        n_dist = span(EXP_ZERO_BELOW)
        for sub in subs:
            for mp in range(2):
                m_ref[sub, mp] = k_norm * qtok_ref[mp:mp + 1, sub * tq:(sub + 1) * tq]
        for dist in COMPACT_DISTANCES:
            pl.when(n_dist == dist)(functools.partial(fixed_compact, dist))
        pl.when(n_dist > max(COMPACT_DISTANCES))(functools.partial(fixed_general, n_dist))

    def fixed_compact(dist):
        items = []
        for sub in subs:
            cq = cq0 + sub
            items.append((sub, cq, 0, None, True))
            for d in range(1, dist + 1):
                for c, sign in ((cq - d, 1), (cq + d, 0)):
                    inside = (c >= 0) & (c < n_chunks)
                    bias = jnp.where(inside, -slope * float(tkc * d), -MASKED_BIAS)
                    items.append((sub, jnp.clip(c, 0, n_chunks - 1), sign, bias, False))

        def absorb_item(it, mp):
            sub, c, _, bias, first = it
            shift = fixup if bias is None else bias
            p = jnp.exp2(s_ref[sub, mp] + (shift - m_ref[sub, mp])).astype(BF16)
            if first:
                acc_ref[sub, mp] = _dot(vt_ref[c], p)
            else:
                acc_ref[sub, mp] += _dot(vt_ref[c], p)

        s_ref[items[0][0], 0] = scores(*items[0][:3], 0)
        for k, it in enumerate(items):
            s_ref[it[0], 1] = scores(*it[:3], 1)
            absorb_item(it, 0)
            if k + 1 < len(items):
                s_ref[items[k + 1][0], 0] = scores(*items[k + 1][:3], 0)
            absorb_item(it, 1)

    def fixed_general(n_dist):
        plans = [plan(cq0 + sub, n_dist) for sub in subs]
        for sub in subs:
            s_ref[sub, 0] = scores(sub, cq0 + sub, 0, 0)
        for sub in subs:
            cq = cq0 + sub
            s_ref[sub, 1] = scores(sub, cq, 0, 1)
            acc_ref[sub, 0] = _dot(
                vt_ref[cq], jnp.exp2(s_ref[sub, 0] + (fixup - m_ref[sub, 0])).astype(BF16))
            s_ref[sub, 0] = scores(sub, *nearest(cq), 0)
            acc_ref[sub, 1] = _dot(
                vt_ref[cq], jnp.exp2(s_ref[sub, 1] + (fixup - m_ref[sub, 1])).astype(BF16))

        def visit(sub, carry):
            cq = cq0 + sub
            n_left = select(sub, [p[0] for p in plans])
            total = select(sub, [p[1] for p in plans])

            def absorb_fixed(c, mp):
                shift = -slope * (tkc * jnp.abs(cq - c)).astype(F32)
                p = jnp.exp2(s_ref[sub, mp] + (shift - m_ref[sub, mp])).astype(BF16)
                acc_ref[sub, mp] += _dot(vt_ref[c], p)

            def trip(t):
                c, sign = item(cq, n_left, t)
                s_ref[sub, 1] = scores(sub, c, sign, 1)
                absorb_fixed(c, 0)
                s_ref[sub, 0] = scores(sub, *item(cq, n_left, t + 1), 0)
                absorb_fixed(c, 1)

            run(total, trip, PIPE_UNROLL)
            return carry

        lax.fori_loop(0, n_sub, visit, 0)

    def running_path():
        diag = [[scores(sub, cq0 + sub, 0, mp) + fixup for mp in range(2)] for sub in subs]
        for sub in subs:
            issue_chunk(sub, *nearest(cq0 + sub), 0)
        for sub in subs:
            for mp in range(2):
                m_first = jnp.max(diag[sub][mp], axis=0, keepdims=True)
                m_ref[sub, mp] = m_first
                acc_ref[sub, mp] = _dot(vt_ref[cq0 + sub],
                                        jnp.exp2(diag[sub][mp] - m_first).astype(BF16))
        plans = [plan(cq0 + sub, span(k_norm * qmax_ref[hd, col0 + sub] + EXP_ZERO_BELOW
                                      - jnp.min(m_ref[sub]))) for sub in subs]

        def visit(sub, carry):
            cq = cq0 + sub
            n_left = select(sub, [p[0] for p in plans])
            total = select(sub, [p[1] for p in plans])

            def trip(t):
                c, sign = item(cq, n_left, t)
                issue_chunk(sub, c, sign, 1)
                absorb(sub, cq, c, 0)
                issue_chunk(sub, *item(cq, n_left, t + 1), 0)
                absorb(sub, cq, c, 1)

            run(total, trip, 1)
            return carry

        lax.fori_loop(0, n_sub, visit, 0)

    slack = [k_norm * qmax_ref[hd, col0 + sub] - smin_ref[hd, col0 + sub] for sub in subs]
    fixed_ok = functools.reduce(jnp.logical_and, [s < STAB_SLACK_BELOW for s in slack])
    pl.when(fixed_ok)(fixed_path)
    pl.when(jnp.logical_not(fixed_ok))(running_path)

    lam = (jnp.exp(jnp.sum(lq1_ref[...] * lk1_ref[...], keepdims=True))
           - jnp.exp(jnp.sum(lq2_ref[...] * lk2_ref[...], keepdims=True)) + lambda_init)
    for sub in range(n_sub):
        o_t = (acc_ref[sub, 0, :V_DIM] * (1.0 / acc_ref[sub, 0, V_DIM:V_DIM + 1])
               - (lam / acc_ref[sub, 1, V_DIM:V_DIM + 1]) * acc_ref[sub, 1, :V_DIM])
        scale = (lax.rsqrt(jnp.mean(o_t * o_t, axis=0, keepdims=True) + SUBLN_EPS)
                 * (1.0 - lambda_init))
        o_ref[sub * tq:(sub + 1) * tq, :] = (o_t * scale * sg_ref[...]).T.astype(o_ref.dtype)


def _flash(slopes, tables, lam_vecs, subln_g, qtok, qt, kk, vt, *, tok_start, n_seq, seq_len,
           lambda_init):
    q_max, s_min, k_max = tables
    tq = TOKEN_TILE
    ns = Q_BLOCKS_PER_STEP
    nq = seq_len // tq
    blk0 = tok_start // tq
    seq0 = tok_start // seq_len
    steps = nq // ns
    step0 = blk0 // ns
    assert nq % ns == 0 and blk0 % ns == 0
    smem = pl.BlockSpec(memory_space=pltpu.SMEM)
    vec = _const_spec((1, HEAD_DIM))
    return pl.pallas_call(
        functools.partial(_flash_kernel, lambda_init=lambda_init, blk0=blk0),
        grid=(n_seq, N_HEADS, steps),
        in_specs=[
            smem, smem, smem, smem, vec, vec, vec, vec, _const_spec((V_DIM, tq)),
            pl.BlockSpec((None, 2, ns * tq), lambda b, h, i: (h, 0, step0 + b * steps + i)),
            pl.BlockSpec((None, ns, 4, V_DIM, tq),
                         lambda b, h, i: (h, step0 + b * steps + i, 0, 0, 0)),
            pl.BlockSpec((None, 2, nq, tq, V_DIM), lambda b, h, i: (h, 0, seq0 + b, 0, 0)),
            pl.BlockSpec((None, nq, V_ROWS, tq), lambda b, h, i: (h, seq0 + b, 0, 0)),
        ],
        out_specs=pl.BlockSpec((ns * tq, V_DIM), lambda b, h, i: (b * steps + i, h)),
        out_shape=jax.ShapeDtypeStruct((n_seq * seq_len, N_HEADS * V_DIM), BF16),
        scratch_shapes=[
            pltpu.VMEM((ns, 2, tq, tq), F32),
            pltpu.VMEM((ns, 2, 1, tq), F32),
            pltpu.VMEM((ns, 2, 1, tq), F32),
            pltpu.VMEM((ns, 2, V_ROWS, tq), F32),
        ],
        compiler_params=_params(3),
        name="diff_flash",
    )(slopes, q_max, s_min,
      k_max[:, blk0:blk0 + n_seq * nq].reshape(N_HEADS, n_seq, nq).max(axis=-1),
      *lam_vecs, jnp.broadcast_to(subln_g.reshape(V_DIM, 1), (V_DIM, tq)), qtok, qt, kk, vt)


def kernel(x_prompt, x_sample, norm1_g, norm2_g, final_g, mix_in_w, pool_w, pool_scale, conv_w,
           mix_out_w, attn_qkv_w, attn_out_w, lambda_q1, lambda_k1, lambda_q2, lambda_k2, subln_g,
           mlp_w1, mlp_w2):
    pb, ps, d = x_prompt.shape
    sb, ss, _ = x_sample.shape
    n_prompt = pb * ps
    n_sample = sb * ss
    x_parts = [x_prompt.reshape(n_prompt, d), x_sample.reshape(n_sample, d)]
    bounds = tuple(ps * b for b in range(pb)) + tuple(n_prompt + ss * b for b in range(sb + 1))
    slopes = jnp.exp2(-8.0 * (jnp.arange(N_HEADS, dtype=F32) + 1.0) / N_HEADS)
    slopes = jnp.stack([slopes * LOG2E, 1.0 / (slopes * LOG2E)])

    for i in range(DEPTH):
        j = i // 2
        w1 = mlp_w1[i].astype(BF16)
        w2 = mlp_w2[i].astype(BF16)
        attn_parts, attn_w = (), None
        if i % 2 == 0:
            x = _mixer(x_parts, norm1_g[i], mix_in_w[j].astype(BF16), pool_w[j].astype(BF16),
                       pool_scale[j], conv_w[j], mix_out_w[j].astype(BF16), bounds)
        else:
            x = x_parts[0]
            lambda_init = 0.8 - 0.6 * math.exp(-0.3 * i)
            qt, kk, vt, nrm, qtok = _qkv(x, norm1_g[i], attn_qkv_w[j].astype(BF16))
            per_head = nrm[:, :3, :2 * N_HEADS].reshape(-1, 3, N_HEADS, 2)
            tables = (jnp.sqrt(per_head[:, 0].max(axis=-1)).T, per_head[:, 2].min(axis=-1).T,
                      jnp.sqrt(per_head[:, 1].max(axis=-1)).T)
            lam_vecs = [v[j].reshape(1, HEAD_DIM) for v in (lambda_q1, lambda_k1, lambda_q2, lambda_k2)]
            flash = functools.partial(_flash, slopes, tables, lam_vecs, subln_g[j],
                                      qtok.reshape(N_HEADS, 2, -1), qt, kk, vt,
                                      lambda_init=lambda_init)
            attn_parts = (flash(tok_start=0, n_seq=pb, seq_len=ps),
                          flash(tok_start=n_prompt, n_seq=sb, seq_len=ss))
            attn_w = attn_out_w[j].astype(BF16)
        if i < DEPTH - 1:
            x_parts = [_mlp(x, norm2_g[i], w1, w2, attn_parts, attn_w)]
        else:
            y_prompt, y_sample = (
                _mlp(x, norm2_g[i], w1, w2, attn_parts[k:k + 1], attn_w, final_g, r0, nr)
                for k, (r0, nr) in enumerate(((0, n_prompt), (n_prompt, n_sample))))

    return (y_prompt.reshape(pb, ps, d), y_sample.reshape(sb, ss, d))
```

```python
import functools
import math

import jax
import jax.numpy as jnp
import numpy as np
from jax import lax
from jax.experimental import pallas as pl
from jax.experimental.pallas import tpu as pltpu

D_MODEL = 1024
DEPTH = 4
POOL_WIDTH = D_MODEL // 2
N_POOL_GROUPS = 4
POOL_GROUP_DIM = POOL_WIDTH // N_POOL_GROUPS
POOL_WINDOWS = (2, 4, 8, 16)
CONV_WIDTH_CH = D_MODEL // 2
MIX_IN_COLS = POOL_WIDTH + 3 * CONV_WIDTH_CH
N_HEADS = 8
HEAD_DIM = D_MODEL // (2 * N_HEADS)
V_DIM = 2 * HEAD_DIM
ATTN_SCALE = HEAD_DIM ** -0.5
D_FF = 4 * D_MODEL
NORM_EPS = 1e-6
SUBLN_EPS = 1e-5

F32 = jnp.float32
BF16 = jnp.bfloat16

TOKEN_TILE = 512
HALO = 16
FF_CHUNK = 1024
VMEM_LIMIT_BYTES = 48 * 1024 * 1024
DENOM_ROWS = 16
V_ROWS = V_DIM + DENOM_ROWS
NORM_TABLE_ROWS = 8
PIPE_UNROLL = 8
Q_BLOCKS_PER_STEP = 2
EXP_ZERO_BELOW = 153.0
STAB_SLACK_BELOW = 80.0
COMPACT_DISTANCES = (1, 2, 4)
MASKED_BIAS = 1e30


def _bf16_pieces(value, n):
    pieces, rest = [], np.float32(value)
    for _ in range(n):
        piece = np.float32(np.asarray(rest, dtype=BF16))
        pieces.append(float(piece))
        rest = np.float32(rest - piece)
    return tuple(pieces)


LOG2E = float(np.float32(math.log2(math.e)))
LOG2E_PIECES = _bf16_pieces(LOG2E, 3)
NORM_SLACK = 1.01
POS_LO_MASK = 255
POS_HI_MASK = TOKEN_TILE - 1 - POS_LO_MASK


def _rms(x, g, eps):
    return x * lax.rsqrt(jnp.mean(x * x, axis=-1, keepdims=True) + eps) * g


def _dot(a, b):
    return jnp.dot(a, b, preferred_element_type=F32)


def _const_spec(shape):
    zeros = (0,) * len(shape)
    return pl.BlockSpec(shape, lambda *_: zeros)


def _part_starts(part_rows, rows):
    starts, s = [], 0
    for r in part_rows:
        starts.append(s // rows)
        s += r
    return starts


def _part_specs(parts, rows, block_of):
    specs = []
    for p, s0 in zip(parts, _part_starts([p.shape[0] for p in parts], rows)):
        nb = p.shape[0] // rows
        specs.append(pl.BlockSpec(
            (rows, p.shape[1]), lambda i, s0=s0, nb=nb: (jnp.clip(block_of(i) - s0, 0, nb - 1), 0)))
    return specs


def _pick_part(refs, block, part_rows, rows):
    value = refs[0][...]
    for ref, s0 in zip(refs[1:], _part_starts(part_rows, rows)[1:]):
        value = jnp.where(block >= s0, ref[...], value)
    return value


def _params(n_axes):
    return pltpu.CompilerParams(
        dimension_semantics=("arbitrary",) * n_axes,
        vmem_limit_bytes=VMEM_LIMIT_BYTES,
    )


def _mixer_kernel(*refs, bounds, part_rows):
    n = len(part_rows)
    xp_refs, x_refs, xn_refs = refs[:n], refs[n:2 * n], refs[2 * n:3 * n]
    g_ref, win_ref, pw_ref, ps_ref, cw_ref, wout_ref, o_ref, u_ref, z_ref = refs[3 * n:]
    tm = x_refs[0].shape[0]
    i = pl.program_id(0)
    per = tm // HALO
    last = sum(part_rows) // HALO - 1
    start = i * tm
    seq_start = jnp.int32(bounds[0])
    seq_end = jnp.int32(bounds[-1])
    for b in bounds[1:-1]:
        seq_start = jnp.where(start >= b, b, seq_start)
    for b in reversed(bounds[1:-1]):
        seq_end = jnp.where(start < b, b, seq_end)

    x = _pick_part(x_refs, i, part_rows, tm)
    xe = jnp.concatenate([
        _pick_part(xp_refs, jnp.maximum(i * per - 1, 0), part_rows, HALO),
        x,
        _pick_part(xn_refs, jnp.minimum((i + 1) * per, last), part_rows, HALO),
    ], axis=0)
    pos = start - HALO + lax.broadcasted_iota(jnp.int32, (tm + 2 * HALO, 1), 0)
    valid = (pos >= seq_start) & (pos < seq_end)
    he = _rms(xe, g_ref[...], NORM_EPS).astype(BF16)
    proj = _dot(he, win_ref[...])
    c0 = POOL_WIDTH
    u_ref[...] = jnp.where(valid, proj[:, :c0], 0.0)
    z_ref[...] = jnp.where(valid, proj[:, c0 + 2 * CONV_WIDTH_CH:] * proj[:, c0:c0 + CONV_WIDTH_CH], 0.0)

    rel = start - seq_start + lax.broadcasted_iota(jnp.int32, (tm, 1), 0)
    seq_len = seq_end - seq_start
    ys = []
    for g, w in enumerate(POOL_WINDOWS):
        cols = slice(g * POOL_GROUP_DIM, (g + 1) * POOL_GROUP_DIM)
        tot = u_ref[HALO - w // 2:HALO - w // 2 + tm, cols]
        for o in range(-w // 2 + 1, w // 2):
            tot = tot + u_ref[HALO + o:HALO + o + tm, cols]
        cnt = (jnp.minimum(rel + w // 2, seq_len) - jnp.maximum(rel - w // 2, 0)).astype(F32)
        d = tot / cnt - u_ref[HALO:HALO + tm, cols]
        ys.append(_dot(d.astype(BF16), pw_ref[g]))
    a_out = jnp.concatenate(ys, axis=-1) * ps_ref[...]

    conv = (cw_ref[0:1, :] * z_ref[HALO - 1:HALO - 1 + tm, :]
            + cw_ref[1:2, :] * z_ref[HALO:HALO + tm, :]
            + cw_ref[2:3, :] * z_ref[HALO + 1:HALO + 1 + tm, :])
    b_out = proj[HALO:HALO + tm, c0 + CONV_WIDTH_CH:c0 + 2 * CONV_WIDTH_CH] * conv
    mixed = jnp.concatenate([a_out, b_out], axis=-1).astype(BF16)
    o_ref[...] = x + _dot(mixed, wout_ref[...])


def _mixer(x_parts, g, win, pw, ps, cw, wout, bounds):
    d = x_parts[0].shape[1]
    part_rows = tuple(p.shape[0] for p in x_parts)
    t = sum(part_rows)
    tm = TOKEN_TILE
    per = tm // HALO
    last = t // HALO - 1
    return pl.pallas_call(
        functools.partial(_mixer_kernel, bounds=bounds, part_rows=part_rows),
        grid=(t // tm,),
        in_specs=[
            *_part_specs(x_parts, HALO, lambda i: jnp.maximum(i * per - 1, 0)),
            *_part_specs(x_parts, tm, lambda i: i),
            *_part_specs(x_parts, HALO, lambda i: jnp.minimum((i + 1) * per, last)),
            _const_spec((1, d)),
            _const_spec(win.shape),
            _const_spec(pw.shape),
            _const_spec((1, POOL_WIDTH)),
            _const_spec(cw.shape),
            _const_spec(wout.shape),
        ],
        out_specs=pl.BlockSpec((tm, d), lambda i: (i, 0)),
        out_shape=jax.ShapeDtypeStruct((t, d), F32),
        scratch_shapes=[
            pltpu.VMEM((tm + 2 * HALO, POOL_WIDTH), F32),
            pltpu.VMEM((tm + 2 * HALO, CONV_WIDTH_CH), F32),
        ],
        compiler_params=_params(1),
        name="even_mixer",
    )(*x_parts, *x_parts, *x_parts, g.reshape(1, d), win, pw, ps.reshape(1, POOL_WIDTH), cw, wout)


def _mlp_kernel(*refs, attn_rows, has_final):
    refs = list(refs)
    x_ref = refs.pop(0)
    if attn_rows:
        a_refs = [refs.pop(0) for _ in attn_rows]
        wo_ref = refs.pop(0)
    g_ref, w1_ref, w2_ref = refs[:3]
    refs = refs[3:]
    if has_final:
        fg_ref = refs.pop(0)
    o_ref = refs.pop(0)

    x = x_ref[...]
    if attn_rows:
        attn = _pick_part(a_refs, pl.program_id(0), attn_rows, x_ref.shape[0])
        x = x + _dot(attn, wo_ref[...])
    h = _rms(x, g_ref[...], NORM_EPS).astype(BF16)
    acc = x
    for c in range(D_FF // FF_CHUNK):
        cols = slice(c * FF_CHUNK, (c + 1) * FF_CHUNK)
        a = jnp.maximum(_dot(h, w1_ref[:, cols]), 0.0)
        acc = acc + _dot((a * a).astype(BF16), w2_ref[cols, :])
    if has_final:
        acc = _rms(acc, fg_ref[...], NORM_EPS)
    o_ref[...] = acc


def _mlp(x, g, w1, w2, attn_parts=(), attn_w=None, final_g=None, row_start=0, rows=None):
    d = x.shape[1]
    rows = x.shape[0] if rows is None else rows
    tm = TOKEN_TILE
    tile0 = row_start // tm
    row_spec = pl.BlockSpec((tm, d), lambda i: (i, 0))
    args = [x]
    in_specs = [pl.BlockSpec((tm, d), lambda i: (tile0 + i, 0))]
    if attn_parts:
        args += [*attn_parts, attn_w]
        in_specs += [*_part_specs(attn_parts, tm, lambda i: i), _const_spec(attn_w.shape)]
    args += [g.reshape(1, d), w1, w2]
    in_specs += [_const_spec((1, d)), _const_spec(w1.shape), _const_spec(w2.shape)]
    if final_g is not None:
        args.append(final_g.reshape(1, d))
        in_specs.append(_const_spec((1, d)))
    return pl.pallas_call(
        functools.partial(_mlp_kernel, attn_rows=tuple(p.shape[0] for p in attn_parts),
                          has_final=final_g is not None),
        grid=(rows // tm,),
        in_specs=in_specs,
        out_specs=row_spec,
        out_shape=jax.ShapeDtypeStruct((rows, d), F32),
        compiler_params=_params(1),
        name="sq_relu_mlp",
    )(*args)


def _qkv_kernel(x_ref, g_ref, w_ref, qt_ref, kk_ref, vt_ref, nrm_ref, qtok_ref):
    tm = x_ref.shape[0]
    h = _rms(x_ref[...], g_ref[...], NORM_EPS).astype(BF16)
    qkv = _dot(h, w_ref[...])
    q_all = (qkv[:, :D_MODEL] * (ATTN_SCALE * LOG2E)).astype(BF16).astype(F32)
    k_all = qkv[:, D_MODEL:2 * D_MODEL].astype(BF16).astype(F32)

    sel = (lax.shift_right_logical(lax.broadcasted_iota(jnp.int32, (D_MODEL, V_DIM), 0),
                                   HEAD_DIM.bit_length() - 1)
           == lax.broadcasted_iota(jnp.int32, (D_MODEL, V_DIM), 1)).astype(BF16)

    def per_map(a):
        return _dot(a.astype(BF16), sel)

    q_sq, k_sq, self_score = per_map(q_all * q_all), per_map(k_all * k_all), per_map(q_all * k_all)
    nrm_ref[0] = jnp.concatenate([
        jnp.max(q_sq, axis=0, keepdims=True), jnp.max(k_sq, axis=0, keepdims=True),
        jnp.min(self_score, axis=0, keepdims=True),
        jnp.zeros((NORM_TABLE_ROWS - 3, V_DIM), F32)], axis=0)
    qtok_ref[...] = jnp.sqrt(q_sq).T[:2 * N_HEADS]

    n_p = len(LOG2E_PIECES)

    def extras(slot, lo, hi, pos_first):
        pos_slot = slot if pos_first else slot - 2 * n_p
        const_slot = slot - 2 * n_p if pos_first else slot
        out = jnp.where((pos_slot >= 0) & (pos_slot < n_p), lo,
                        jnp.where((pos_slot >= n_p) & (pos_slot < 2 * n_p), hi, 0.0))
        for p, piece in enumerate(LOG2E_PIECES):
            out = jnp.where((const_slot == p) | (const_slot == n_p + p), piece, out)
        return out

    q_row = lax.broadcasted_iota(jnp.int32, (HEAD_DIM, tm), 0)
    q_tok = lax.broadcasted_iota(jnp.int32, (HEAD_DIM, tm), 1)
    q_base = extras(q_row, (q_tok & POS_LO_MASK).astype(F32), (q_tok & POS_HI_MASK).astype(F32), True)
    q_has_slope = q_row >= 2 * n_p
    k_lane = lax.broadcasted_iota(jnp.int32, (tm, V_DIM), 1)
    k_tok = lax.broadcasted_iota(jnp.int32, (tm, V_DIM), 0)
    k_lo = -(k_tok & POS_LO_MASK).astype(F32)
    k_hi = -(k_tok & POS_HI_MASK).astype(F32)
    k_base = [extras(k_lane - off, k_lo, k_hi, False) for off in (HEAD_DIM, 0)]
    k_has_slope = [(k_lane >= off) & (k_lane < off + 2 * n_p) for off in (HEAD_DIM, 0)]
    denom_rows = (lax.broadcasted_iota(jnp.int32, (DENOM_ROWS, tm), 0) == 0).astype(F32)

    for hd in range(N_HEADS):
        slope = 2.0 ** (-8.0 * (hd + 1) / N_HEADS)
        cols = slice(hd * V_DIM, (hd + 1) * V_DIM)
        q_t = q_all[:, cols].T
        top, bot = q_t[:HEAD_DIM], q_t[HEAD_DIM:]
        q_extra = jnp.where(q_has_slope, slope * q_base, q_base)
        qt_ref[hd, 0, 0] = jnp.concatenate([top, q_extra], axis=0).astype(BF16)
        qt_ref[hd, 0, 1] = jnp.concatenate([top, -q_extra], axis=0).astype(BF16)
        qt_ref[hd, 0, 2] = jnp.concatenate([q_extra, bot], axis=0).astype(BF16)
        qt_ref[hd, 0, 3] = jnp.concatenate([-q_extra, bot], axis=0).astype(BF16)

        kh = k_all[:, cols]
        k_extra = [jnp.where(k_has_slope[m], slope * k_base[m], k_base[m]) for m in range(2)]
        kk_ref[hd, 0, 0] = jnp.where(k_lane < HEAD_DIM, kh, k_extra[0]).astype(BF16)
        kk_ref[hd, 1, 0] = jnp.where(k_lane >= HEAD_DIM, kh, k_extra[1]).astype(BF16)

        vh = qkv[:, 2 * D_MODEL + hd * V_DIM:2 * D_MODEL + (hd + 1) * V_DIM]
        vt_ref[hd, 0] = jnp.concatenate([vh.T, denom_rows], axis=0).astype(BF16)


def _qkv(x, g, w):
    t, d = x.shape
    tm = TOKEN_TILE
    nc = t // tm
    return pl.pallas_call(
        _qkv_kernel,
        grid=(nc,),
        in_specs=[
            pl.BlockSpec((tm, d), lambda i: (i, 0)),
            _const_spec((1, d)),
            _const_spec(w.shape),
        ],
        out_specs=[
            pl.BlockSpec((N_HEADS, 1, 4, V_DIM, tm), lambda i: (0, i, 0, 0, 0)),
            pl.BlockSpec((N_HEADS, 2, 1, tm, V_DIM), lambda i: (0, 0, i, 0, 0)),
            pl.BlockSpec((N_HEADS, 1, V_ROWS, tm), lambda i: (0, i, 0, 0)),
            pl.BlockSpec((1, NORM_TABLE_ROWS, V_DIM), lambda i: (i, 0, 0)),
            pl.BlockSpec((2 * N_HEADS, tm), lambda i: (0, i)),
        ],
        out_shape=[
            jax.ShapeDtypeStruct((N_HEADS, nc, 4, V_DIM, tm), BF16),
            jax.ShapeDtypeStruct((N_HEADS, 2, nc, tm, V_DIM), BF16),
            jax.ShapeDtypeStruct((N_HEADS, nc, V_ROWS, tm), BF16),
            jax.ShapeDtypeStruct((nc, NORM_TABLE_ROWS, V_DIM), F32),
            jax.ShapeDtypeStruct((2 * N_HEADS, t), F32),
        ],
        compiler_params=_params(1),
        name="attn_qkv",
    )(x, g.reshape(1, d), w)


def _flash_kernel(slope_ref, qmax_ref, smin_ref, kseq_ref, lq1_ref, lk1_ref, lq2_ref, lk2_ref,
                  sg_ref, qtok_ref, qt_ref, kk_ref, vt_ref, o_ref,
                  s_ref, mc_ref, m_ref, acc_ref, *, lambda_init, blk0):
    n_chunks, tkc = kk_ref.shape[1], kk_ref.shape[2]
    n_sub, tq = qt_ref.shape[0], qt_ref.shape[-1]
    hd = pl.program_id(1)
    slope, inv_slope = slope_ref[0, hd], slope_ref[1, hd]
    cq0 = pl.program_id(2) * n_sub
    col0 = blk0 + pl.program_id(0) * n_chunks + cq0
    k_norm = NORM_SLACK * kseq_ref[hd, pl.program_id(0)]
    subs = range(n_sub)

    def select(sub, values):
        out = values[0]
        for other, value in enumerate(values[1:], 1):
            out = jnp.where(sub == other, value, out)
        return out

    def span(gap):
        reach = (gap * inv_slope - 1.0) * (1.0 / tkc)
        return jnp.where(reach >= 1.0,
                         jnp.minimum(reach, float(n_chunks)).astype(jnp.int32) + 1, 1)

    def plan(cq, n_dist):
        n_left = jnp.minimum(n_dist, cq)
        return n_left, n_left + jnp.minimum(n_dist, n_chunks - 1 - cq)

    def item(cq, n_left, t):
        left = t < n_left
        c = jnp.where(left, cq - 1 - t, cq + 1 + t - n_left)
        return jnp.clip(c, 0, n_chunks - 1), left.astype(jnp.int32)

    def nearest(cq):
        return (jnp.where(cq > 0, cq - 1, jnp.minimum(cq + 1, n_chunks - 1)),
                (cq > 0).astype(jnp.int32))

    def run(total, trip, unroll):
        assert unroll & (unroll - 1) == 0

        def trips(t0, count):
            for r in range(count):
                trip(t0 + r)

        rem = total & (unroll - 1)
        done, size = 0, 1
        while size < unroll:
            pl.when((total & size) == size)(functools.partial(trips, done, size))
            done = done + (total & size)
            size *= 2

        def group(u, carry):
            trips(rem + unroll * u, unroll)
            return carry

        lax.fori_loop(0, total // unroll, group, 0)

    def scores(sub, c, sign, mp):
        return _dot(kk_ref[mp, c], qt_ref[sub, 2 * mp + sign])

    def issue_chunk(sub, c, sign, mp):
        s = scores(sub, c, sign, mp)
        s_ref[sub, mp] = s
        mc_ref[sub, mp] = jnp.max(s, axis=0, keepdims=True)

    def absorb(sub, cq, c, mp):
        shift = -slope * (tkc * jnp.abs(cq - c)).astype(F32)
        m_old = m_ref[sub, mp]
        m_new = jnp.maximum(m_old, mc_ref[sub, mp] + shift)
        p = jnp.exp2(s_ref[sub, mp] - (m_new - shift)).astype(BF16)
        acc_ref[sub, mp] = jnp.exp2(m_old - m_new) * acc_ref[sub, mp] + _dot(vt_ref[c], p)
        m_ref[sub, mp] = m_new

    jj = lax.broadcasted_iota(jnp.int32, (tkc, tq), 0)
    ii = lax.broadcasted_iota(jnp.int32, (tkc, tq), 1)
    fixup = (-2.0 * slope) * jnp.maximum(ii - jj, 0).astype(F32)

    def fixed_path():
        n_dist = span(EXP_ZERO_BELOW)
        for sub in subs:
            for mp in range(2):
                m_ref[sub, mp] = k_norm * qtok_ref[mp:mp + 1, sub * tq:(sub + 1) * tq]
        for dist in COMPACT_DISTANCES:
            pl.when(n_dist == dist)(functools.partial(fixed_compact, dist))
        other = functools.reduce(jnp.logical_and, [n_dist != dist for dist in COMPACT_DISTANCES])
        pl.when(other)(functools.partial(fixed_general, n_dist))

    def fixed_compact(dist):
        items = []
        for sub in subs:
            cq = cq0 + sub
            items.append((sub, cq, 0, None, True))
            for d in range(1, dist + 1):
                for c, sign in ((cq - d, 1), (cq + d, 0)):
                    inside = (c >= 0) & (c < n_chunks)
                    bias = jnp.where(inside, -slope * float(tkc * d), -MASKED_BIAS)
                    items.append((sub, jnp.clip(c, 0, n_chunks - 1), sign, bias, False))

        def absorb_item(it, mp):
            sub, c, _, bias, first = it
            shift = fixup if bias is None else bias
            p = jnp.exp2(s_ref[sub, mp] + (shift - m_ref[sub, mp])).astype(BF16)
            if first:
                acc_ref[sub, mp] = _dot(vt_ref[c], p)
            else:
                acc_ref[sub, mp] += _dot(vt_ref[c], p)

        s_ref[items[0][0], 0] = scores(*items[0][:3], 0)
        for k, it in enumerate(items):
            s_ref[it[0], 1] = scores(*it[:3], 1)
            absorb_item(it, 0)
            if k + 1 < len(items):
                s_ref[items[k + 1][0], 0] = scores(*items[k + 1][:3], 0)
            absorb_item(it, 1)

    def fixed_general(n_dist):
        plans = [plan(cq0 + sub, n_dist) for sub in subs]
        for sub in subs:
            s_ref[sub, 0] = scores(sub, cq0 + sub, 0, 0)
        for sub in subs:
            cq = cq0 + sub
            s_ref[sub, 1] = scores(sub, cq, 0, 1)
            acc_ref[sub, 0] = _dot(
                vt_ref[cq], jnp.exp2(s_ref[sub, 0] + (fixup - m_ref[sub, 0])).astype(BF16))
            s_ref[sub, 0] = scores(sub, *nearest(cq), 0)
            acc_ref[sub, 1] = _dot(
                vt_ref[cq], jnp.exp2(s_ref[sub, 1] + (fixup - m_ref[sub, 1])).astype(BF16))

        def visit(sub, carry):
            cq = cq0 + sub
            n_left = select(sub, [p[0] for p in plans])
            total = select(sub, [p[1] for p in plans])

            def absorb_fixed(c, mp):
                shift = -slope * (tkc * jnp.abs(cq - c)).astype(F32)
                p = jnp.exp2(s_ref[sub, mp] + (shift - m_ref[sub, mp])).astype(BF16)
                acc_ref[sub, mp] += _dot(vt_ref[c], p)

            def trip(t):
                c, sign = item(cq, n_left, t)
                s_ref[sub, 1] = scores(sub, c, sign, 1)
                absorb_fixed(c, 0)
                s_ref[sub, 0] = scores(sub, *item(cq, n_left, t + 1), 0)
                absorb_fixed(c, 1)

            run(total, trip, PIPE_UNROLL)
            return carry

        lax.fori_loop(0, n_sub, visit, 0)

    def running_path():
        diag = [[scores(sub, cq0 + sub, 0, mp) + fixup for mp in range(2)] for sub in subs]
        for sub in subs:
            issue_chunk(sub, *nearest(cq0 + sub), 0)
        for sub in subs:
            for mp in range(2):
                m_first = jnp.max(diag[sub][mp], axis=0, keepdims=True)
                m_ref[sub, mp] = m_first
                acc_ref[sub, mp] = _dot(vt_ref[cq0 + sub],
                                        jnp.exp2(diag[sub][mp] - m_first).astype(BF16))
        plans = [plan(cq0 + sub, span(k_norm * qmax_ref[hd, col0 + sub] + EXP_ZERO_BELOW
                                      - jnp.min(m_ref[sub]))) for sub in subs]

        def visit(sub, carry):
            cq = cq0 + sub
            n_left = select(sub, [p[0] for p in plans])
            total = select(sub, [p[1] for p in plans])

            def trip(t):
                c, sign = item(cq, n_left, t)
                issue_chunk(sub, c, sign, 1)
                absorb(sub, cq, c, 0)
                issue_chunk(sub, *item(cq, n_left, t + 1), 0)
                absorb(sub, cq, c, 1)

            run(total, trip, 1)
            return carry

        lax.fori_loop(0, n_sub, visit, 0)

    slack = [k_norm * qmax_ref[hd, col0 + sub] - smin_ref[hd, col0 + sub] for sub in subs]
    fixed_ok = functools.reduce(jnp.logical_and, [s < STAB_SLACK_BELOW for s in slack])
    pl.when(fixed_ok)(fixed_path)
    pl.when(jnp.logical_not(fixed_ok))(running_path)

    lam = (jnp.exp(jnp.sum(lq1_ref[...] * lk1_ref[...], keepdims=True))
           - jnp.exp(jnp.sum(lq2_ref[...] * lk2_ref[...], keepdims=True)) + lambda_init)
    for sub in range(n_sub):
        o_t = (acc_ref[sub, 0, :V_DIM] * (1.0 / acc_ref[sub, 0, V_DIM:V_DIM + 1])
               - (lam / acc_ref[sub, 1, V_DIM:V_DIM + 1]) * acc_ref[sub, 1, :V_DIM])
        scale = (lax.rsqrt(jnp.mean(o_t * o_t, axis=0, keepdims=True) + SUBLN_EPS)
                 * (1.0 - lambda_init))
        o_ref[sub * tq:(sub + 1) * tq, :] = (o_t * scale * sg_ref[...]).T.astype(o_ref.dtype)


def _flash(slopes, tables, lam_vecs, subln_g, qtok, qt, kk, vt, *, tok_start, n_seq, seq_len,
           lambda_init):
    q_max, s_min, k_max = tables
    tq = TOKEN_TILE
    ns = Q_BLOCKS_PER_STEP
    nq = seq_len // tq
    blk0 = tok_start // tq
    seq0 = tok_start // seq_len
    steps = nq // ns
    step0 = blk0 // ns
    assert nq % ns == 0 and blk0 % ns == 0
    smem = pl.BlockSpec(memory_space=pltpu.SMEM)
    vec = _const_spec((1, HEAD_DIM))
    return pl.pallas_call(
        functools.partial(_flash_kernel, lambda_init=lambda_init, blk0=blk0),
        grid=(n_seq, N_HEADS, steps),
        in_specs=[
            smem, smem, smem, smem, vec, vec, vec, vec, _const_spec((V_DIM, tq)),
            pl.BlockSpec((None, 2, ns * tq), lambda b, h, i: (h, 0, step0 + b * steps + i)),
            pl.BlockSpec((None, ns, 4, V_DIM, tq),
                         lambda b, h, i: (h, step0 + b * steps + i, 0, 0, 0)),
            pl.BlockSpec((None, 2, nq, tq, V_DIM), lambda b, h, i: (h, 0, seq0 + b, 0, 0)),
            pl.BlockSpec((None, nq, V_ROWS, tq), lambda b, h, i: (h, seq0 + b, 0, 0)),
        ],
        out_specs=pl.BlockSpec((ns * tq, V_DIM), lambda b, h, i: (b * steps + i, h)),
        out_shape=jax.ShapeDtypeStruct((n_seq * seq_len, N_HEADS * V_DIM), BF16),
        scratch_shapes=[
            pltpu.VMEM((ns, 2, tq, tq), F32),
            pltpu.VMEM((ns, 2, 1, tq), F32),
            pltpu.VMEM((ns, 2, 1, tq), F32),
            pltpu.VMEM((ns, 2, V_ROWS, tq), F32),
        ],
        compiler_params=_params(3),
        name="diff_flash",
    )(slopes, q_max, s_min,
      k_max[:, blk0:blk0 + n_seq * nq].reshape(N_HEADS, n_seq, nq).max(axis=-1),
      *lam_vecs, jnp.broadcast_to(subln_g.reshape(V_DIM, 1), (V_DIM, tq)), qtok, qt, kk, vt)


def kernel(x_prompt, x_sample, norm1_g, norm2_g, final_g, mix_in_w, pool_w, pool_scale, conv_w,
           mix_out_w, attn_qkv_w, attn_out_w, lambda_q1, lambda_k1, lambda_q2, lambda_k2, subln_g,
           mlp_w1, mlp_w2):
    pb, ps, d = x_prompt.shape
    sb, ss, _ = x_sample.shape
    n_prompt = pb * ps
    n_sample = sb * ss
    x_parts = [x_prompt.reshape(n_prompt, d), x_sample.reshape(n_sample, d)]
    bounds = tuple(ps * b for b in range(pb)) + tuple(n_prompt + ss * b for b in range(sb + 1))
    slopes = jnp.exp2(-8.0 * (jnp.arange(N_HEADS, dtype=F32) + 1.0) / N_HEADS)
    slopes = jnp.stack([slopes * LOG2E, 1.0 / (slopes * LOG2E)])

    for i in range(DEPTH):
        j = i // 2
        w1 = mlp_w1[i].astype(BF16)
        w2 = mlp_w2[i].astype(BF16)
        attn_parts, attn_w = (), None
        if i % 2 == 0:
            x = _mixer(x_parts, norm1_g[i], mix_in_w[j].astype(BF16), pool_w[j].astype(BF16),
                       pool_scale[j], conv_w[j], mix_out_w[j].astype(BF16), bounds)
        else:
            x = x_parts[0]
            lambda_init = 0.8 - 0.6 * math.exp(-0.3 * i)
            qt, kk, vt, nrm, qtok = _qkv(x, norm1_g[i], attn_qkv_w[j].astype(BF16))
            per_head = nrm[:, :3, :2 * N_HEADS].reshape(-1, 3, N_HEADS, 2)
            tables = (jnp.sqrt(per_head[:, 0].max(axis=-1)).T, per_head[:, 2].min(axis=-1).T,
                      jnp.sqrt(per_head[:, 1].max(axis=-1)).T)
            lam_vecs = [v[j].reshape(1, HEAD_DIM) for v in (lambda_q1, lambda_k1, lambda_q2, lambda_k2)]
            flash = functools.partial(_flash, slopes, tables, lam_vecs, subln_g[j],
                                      qtok.reshape(N_HEADS, 2, -1), qt, kk, vt,
                                      lambda_init=lambda_init)
            attn_parts = (flash(tok_start=0, n_seq=pb, seq_len=ps),
                          flash(tok_start=n_prompt, n_seq=sb, seq_len=ss))
            attn_w = attn_out_w[j].astype(BF16)
        if i < DEPTH - 1:
            x_parts = [_mlp(x, norm2_g[i], w1, w2, attn_parts, attn_w)]
        else:
            y_prompt, y_sample = (
                _mlp(x, norm2_g[i], w1, w2, attn_parts[k:k + 1], attn_w, final_g, r0, nr)
                for k, (r0, nr) in enumerate(((0, n_prompt), (n_prompt, n_sample))))

    return (y_prompt.reshape(pb, ps, d), y_sample.reshape(sb, ss, d))
```

```python
import functools
import math

import jax
import jax.numpy as jnp
import numpy as np
from jax import lax
from jax.experimental import pallas as pl
from jax.experimental.pallas import tpu as pltpu

D_MODEL = 1024
DEPTH = 4
POOL_WIDTH = D_MODEL // 2
N_POOL_GROUPS = 4
POOL_GROUP_DIM = POOL_WIDTH // N_POOL_GROUPS
POOL_WINDOWS = (2, 4, 8, 16)
CONV_WIDTH_CH = D_MODEL // 2
MIX_IN_COLS = POOL_WIDTH + 3 * CONV_WIDTH_CH
N_HEADS = 8
HEAD_DIM = D_MODEL // (2 * N_HEADS)
V_DIM = 2 * HEAD_DIM
ATTN_SCALE = HEAD_DIM ** -0.5
D_FF = 4 * D_MODEL
NORM_EPS = 1e-6
SUBLN_EPS = 1e-5

F32 = jnp.float32
BF16 = jnp.bfloat16

TOKEN_TILE = 512
HALO = 16
FF_CHUNK = 1024
VMEM_LIMIT_BYTES = 48 * 1024 * 1024
DENOM_ROWS = 16
V_ROWS = V_DIM + DENOM_ROWS
NORM_TABLE_ROWS = 8
PIPE_UNROLL = 8
Q_BLOCKS_PER_STEP = 2
EXP_ZERO_BELOW = 153.0
STAB_SLACK_BELOW = 80.0
COMPACT_DISTANCES = (1, 2)
NEAR_DISTANCES = 1
MASKED_BIAS = 1e30


def _bf16_pieces(value, n):
    pieces, rest = [], np.float32(value)
    for _ in range(n):
        piece = np.float32(np.asarray(rest, dtype=BF16))
        pieces.append(float(piece))
        rest = np.float32(rest - piece)
    return tuple(pieces)


LOG2E = float(np.float32(math.log2(math.e)))
LOG2E_PIECES = _bf16_pieces(LOG2E, 3)
NORM_SLACK = 1.01
POS_LO_MASK = 255
POS_HI_MASK = TOKEN_TILE - 1 - POS_LO_MASK


def _rms(x, g, eps):
    return x * lax.rsqrt(jnp.mean(x * x, axis=-1, keepdims=True) + eps) * g


def _dot(a, b):
    return jnp.dot(a, b, preferred_element_type=F32)


def _const_spec(shape):
    zeros = (0,) * len(shape)
    return pl.BlockSpec(shape, lambda *_: zeros)


def _part_starts(part_rows, rows):
    starts, s = [], 0
    for r in part_rows:
        starts.append(s // rows)
        s += r
    return starts


def _part_specs(parts, rows, block_of):
    specs = []
    for p, s0 in zip(parts, _part_starts([p.shape[0] for p in parts], rows)):
        nb = p.shape[0] // rows
        specs.append(pl.BlockSpec(
            (rows, p.shape[1]), lambda i, s0=s0, nb=nb: (jnp.clip(block_of(i) - s0, 0, nb - 1), 0)))
    return specs


def _pick_part(refs, block, part_rows, rows):
    value = refs[0][...]
    for ref, s0 in zip(refs[1:], _part_starts(part_rows, rows)[1:]):
        value = jnp.where(block >= s0, ref[...], value)
    return value


def _params(n_axes):
    return pltpu.CompilerParams(
        dimension_semantics=("arbitrary",) * n_axes,
        vmem_limit_bytes=VMEM_LIMIT_BYTES,
    )


def _mixer_kernel(*refs, bounds, part_rows):
    n = len(part_rows)
    xp_refs, x_refs, xn_refs = refs[:n], refs[n:2 * n], refs[2 * n:3 * n]
    g_ref, win_ref, pw_ref, ps_ref, cw_ref, wout_ref, o_ref, u_ref, z_ref = refs[3 * n:]
    tm = x_refs[0].shape[0]
    i = pl.program_id(0)
    per = tm // HALO
    last = sum(part_rows) // HALO - 1
    start = i * tm
    seq_start = jnp.int32(bounds[0])
    seq_end = jnp.int32(bounds[-1])
    for b in bounds[1:-1]:
        seq_start = jnp.where(start >= b, b, seq_start)
    for b in reversed(bounds[1:-1]):
        seq_end = jnp.where(start < b, b, seq_end)

    x = _pick_part(x_refs, i, part_rows, tm)
    xe = jnp.concatenate([
        _pick_part(xp_refs, jnp.maximum(i * per - 1, 0), part_rows, HALO),
        x,
        _pick_part(xn_refs, jnp.minimum((i + 1) * per, last), part_rows, HALO),
    ], axis=0)
    pos = start - HALO + lax.broadcasted_iota(jnp.int32, (tm + 2 * HALO, 1), 0)
    valid = (pos >= seq_start) & (pos < seq_end)
    he = _rms(xe, g_ref[...], NORM_EPS).astype(BF16)
    proj = _dot(he, win_ref[...])
    c0 = POOL_WIDTH
    u_ref[...] = jnp.where(valid, proj[:, :c0], 0.0)
    z_ref[...] = jnp.where(valid, proj[:, c0 + 2 * CONV_WIDTH_CH:] * proj[:, c0:c0 + CONV_WIDTH_CH], 0.0)

    rel = start - seq_start + lax.broadcasted_iota(jnp.int32, (tm, 1), 0)
    seq_len = seq_end - seq_start
    ys = []
    for g, w in enumerate(POOL_WINDOWS):
        cols = slice(g * POOL_GROUP_DIM, (g + 1) * POOL_GROUP_DIM)
        tot = u_ref[HALO - w // 2:HALO - w // 2 + tm, cols]
        for o in range(-w // 2 + 1, w // 2):
            tot = tot + u_ref[HALO + o:HALO + o + tm, cols]
        cnt = (jnp.minimum(rel + w // 2, seq_len) - jnp.maximum(rel - w // 2, 0)).astype(F32)
        d = tot / cnt - u_ref[HALO:HALO + tm, cols]
        ys.append(_dot(d.astype(BF16), pw_ref[g]))
    a_out = jnp.concatenate(ys, axis=-1) * ps_ref[...]

    conv = (cw_ref[0:1, :] * z_ref[HALO - 1:HALO - 1 + tm, :]
            + cw_ref[1:2, :] * z_ref[HALO:HALO + tm, :]
            + cw_ref[2:3, :] * z_ref[HALO + 1:HALO + 1 + tm, :])
    b_out = proj[HALO:HALO + tm, c0 + CONV_WIDTH_CH:c0 + 2 * CONV_WIDTH_CH] * conv
    mixed = jnp.concatenate([a_out, b_out], axis=-1).astype(BF16)
    o_ref[...] = x + _dot(mixed, wout_ref[...])


def _mixer(x_parts, g, win, pw, ps, cw, wout, bounds):
    d = x_parts[0].shape[1]
    part_rows = tuple(p.shape[0] for p in x_parts)
    t = sum(part_rows)
    tm = TOKEN_TILE
    per = tm // HALO
    last = t // HALO - 1
    return pl.pallas_call(
        functools.partial(_mixer_kernel, bounds=bounds, part_rows=part_rows),
        grid=(t // tm,),
        in_specs=[
            *_part_specs(x_parts, HALO, lambda i: jnp.maximum(i * per - 1, 0)),
            *_part_specs(x_parts, tm, lambda i: i),
            *_part_specs(x_parts, HALO, lambda i: jnp.minimum((i + 1) * per, last)),
            _const_spec((1, d)),
            _const_spec(win.shape),
            _const_spec(pw.shape),
            _const_spec((1, POOL_WIDTH)),
            _const_spec(cw.shape),
            _const_spec(wout.shape),
        ],
        out_specs=pl.BlockSpec((tm, d), lambda i: (i, 0)),
        out_shape=jax.ShapeDtypeStruct((t, d), F32),
        scratch_shapes=[
            pltpu.VMEM((tm + 2 * HALO, POOL_WIDTH), F32),
            pltpu.VMEM((tm + 2 * HALO, CONV_WIDTH_CH), F32),
        ],
        compiler_params=_params(1),
        name="even_mixer",
    )(*x_parts, *x_parts, *x_parts, g.reshape(1, d), win, pw, ps.reshape(1, POOL_WIDTH), cw, wout)


def _mlp_kernel(*refs, attn_rows, has_final):
    refs = list(refs)
    x_ref = refs.pop(0)
    if attn_rows:
        a_refs = [refs.pop(0) for _ in attn_rows]
        wo_ref = refs.pop(0)
    g_ref, w1_ref, w2_ref = refs[:3]
    refs = refs[3:]
    if has_final:
        fg_ref = refs.pop(0)
    o_ref = refs.pop(0)

    x = x_ref[...]
    if attn_rows:
        attn = _pick_part(a_refs, pl.program_id(0), attn_rows, x_ref.shape[0])
        x = x + _dot(attn, wo_ref[...])
    h = _rms(x, g_ref[...], NORM_EPS).astype(BF16)
    acc = x
    for c in range(D_FF // FF_CHUNK):
        cols = slice(c * FF_CHUNK, (c + 1) * FF_CHUNK)
        a = jnp.maximum(_dot(h, w1_ref[:, cols]), 0.0)
        acc = acc + _dot((a * a).astype(BF16), w2_ref[cols, :])
    if has_final:
        acc = _rms(acc, fg_ref[...], NORM_EPS)
    o_ref[...] = acc


def _mlp(x, g, w1, w2, attn_parts=(), attn_w=None, final_g=None, row_start=0, rows=None):
    d = x.shape[1]
    rows = x.shape[0] if rows is None else rows
    tm = TOKEN_TILE
    tile0 = row_start // tm
    row_spec = pl.BlockSpec((tm, d), lambda i: (i, 0))
    args = [x]
    in_specs = [pl.BlockSpec((tm, d), lambda i: (tile0 + i, 0))]
    if attn_parts:
        args += [*attn_parts, attn_w]
        in_specs += [*_part_specs(attn_parts, tm, lambda i: i), _const_spec(attn_w.shape)]
    args += [g.reshape(1, d), w1, w2]
    in_specs += [_const_spec((1, d)), _const_spec(w1.shape), _const_spec(w2.shape)]
    if final_g is not None:
        args.append(final_g.reshape(1, d))
        in_specs.append(_const_spec((1, d)))
    return pl.pallas_call(
        functools.partial(_mlp_kernel, attn_rows=tuple(p.shape[0] for p in attn_parts),
                          has_final=final_g is not None),
        grid=(rows // tm,),
        in_specs=in_specs,
        out_specs=row_spec,
        out_shape=jax.ShapeDtypeStruct((rows, d), F32),
        compiler_params=_params(1),
        name="sq_relu_mlp",
    )(*args)


def _qkv_kernel(x_ref, g_ref, w_ref, qt_ref, kk_ref, vt_ref, nrm_ref, qtok_ref):
    tm = x_ref.shape[0]
    h = _rms(x_ref[...], g_ref[...], NORM_EPS).astype(BF16)
    qkv = _dot(h, w_ref[...])
    q_all = (qkv[:, :D_MODEL] * (ATTN_SCALE * LOG2E)).astype(BF16).astype(F32)
    k_all = qkv[:, D_MODEL:2 * D_MODEL].astype(BF16).astype(F32)

    sel = (lax.shift_right_logical(lax.broadcasted_iota(jnp.int32, (D_MODEL, V_DIM), 0),
                                   HEAD_DIM.bit_length() - 1)
           == lax.broadcasted_iota(jnp.int32, (D_MODEL, V_DIM), 1)).astype(BF16)

    def per_map(a):
        return _dot(a.astype(BF16), sel)

    q_sq, k_sq, self_score = per_map(q_all * q_all), per_map(k_all * k_all), per_map(q_all * k_all)
    nrm_ref[0] = jnp.concatenate([
        jnp.max(q_sq, axis=0, keepdims=True), jnp.max(k_sq, axis=0, keepdims=True),
        jnp.min(self_score, axis=0, keepdims=True),
        jnp.zeros((NORM_TABLE_ROWS - 3, V_DIM), F32)], axis=0)
    qtok_ref[...] = jnp.sqrt(q_sq).T[:2 * N_HEADS]

    n_p = len(LOG2E_PIECES)

    def extras(slot, lo, hi, pos_first):
        pos_slot = slot if pos_first else slot - 2 * n_p
        const_slot = slot - 2 * n_p if pos_first else slot
        out = jnp.where((pos_slot >= 0) & (pos_slot < n_p), lo,
                        jnp.where((pos_slot >= n_p) & (pos_slot < 2 * n_p), hi, 0.0))
        for p, piece in enumerate(LOG2E_PIECES):
            out = jnp.where((const_slot == p) | (const_slot == n_p + p), piece, out)
        return out

    q_row = lax.broadcasted_iota(jnp.int32, (HEAD_DIM, tm), 0)
    q_tok = lax.broadcasted_iota(jnp.int32, (HEAD_DIM, tm), 1)
    q_base = extras(q_row, (q_tok & POS_LO_MASK).astype(F32), (q_tok & POS_HI_MASK).astype(F32), True)
    q_has_slope = q_row >= 2 * n_p
    k_lane = lax.broadcasted_iota(jnp.int32, (tm, V_DIM), 1)
    k_tok = lax.broadcasted_iota(jnp.int32, (tm, V_DIM), 0)
    k_lo = -(k_tok & POS_LO_MASK).astype(F32)
    k_hi = -(k_tok & POS_HI_MASK).astype(F32)
    k_base = [extras(k_lane - off, k_lo, k_hi, False) for off in (HEAD_DIM, 0)]
    k_has_slope = [(k_lane >= off) & (k_lane < off + 2 * n_p) for off in (HEAD_DIM, 0)]
    denom_rows = (lax.broadcasted_iota(jnp.int32, (DENOM_ROWS, tm), 0) == 0).astype(F32)

    for hd in range(N_HEADS):
        slope = 2.0 ** (-8.0 * (hd + 1) / N_HEADS)
        cols = slice(hd * V_DIM, (hd + 1) * V_DIM)
        q_t = q_all[:, cols].T
        top, bot = q_t[:HEAD_DIM], q_t[HEAD_DIM:]
        q_extra = jnp.where(q_has_slope, slope * q_base, q_base)
        qt_ref[hd, 0, 0] = jnp.concatenate([top, q_extra], axis=0).astype(BF16)
        qt_ref[hd, 0, 1] = jnp.concatenate([top, -q_extra], axis=0).astype(BF16)
        qt_ref[hd, 0, 2] = jnp.concatenate([q_extra, bot], axis=0).astype(BF16)
        qt_ref[hd, 0, 3] = jnp.concatenate([-q_extra, bot], axis=0).astype(BF16)

        kh = k_all[:, cols]
        k_extra = [jnp.where(k_has_slope[m], slope * k_base[m], k_base[m]) for m in range(2)]
        kk_ref[hd, 0, 0] = jnp.where(k_lane < HEAD_DIM, kh, k_extra[0]).astype(BF16)
        kk_ref[hd, 1, 0] = jnp.where(k_lane >= HEAD_DIM, kh, k_extra[1]).astype(BF16)

        vh = qkv[:, 2 * D_MODEL + hd * V_DIM:2 * D_MODEL + (hd + 1) * V_DIM]
        vt_ref[hd, 0] = jnp.concatenate([vh.T, denom_rows], axis=0).astype(BF16)


def _qkv(x, g, w):
    t, d = x.shape
    tm = TOKEN_TILE
    nc = t // tm
    return pl.pallas_call(
        _qkv_kernel,
        grid=(nc,),
        in_specs=[
            pl.BlockSpec((tm, d), lambda i: (i, 0)),
            _const_spec((1, d)),
            _const_spec(w.shape),
        ],
        out_specs=[
            pl.BlockSpec((N_HEADS, 1, 4, V_DIM, tm), lambda i: (0, i, 0, 0, 0)),
            pl.BlockSpec((N_HEADS, 2, 1, tm, V_DIM), lambda i: (0, 0, i, 0, 0)),
            pl.BlockSpec((N_HEADS, 1, V_ROWS, tm), lambda i: (0, i, 0, 0)),
            pl.BlockSpec((1, NORM_TABLE_ROWS, V_DIM), lambda i: (i, 0, 0)),
            pl.BlockSpec((2 * N_HEADS, tm), lambda i: (0, i)),
        ],
        out_shape=[
            jax.ShapeDtypeStruct((N_HEADS, nc, 4, V_DIM, tm), BF16),
            jax.ShapeDtypeStruct((N_HEADS, 2, nc, tm, V_DIM), BF16),
            jax.ShapeDtypeStruct((N_HEADS, nc, V_ROWS, tm), BF16),
            jax.ShapeDtypeStruct((nc, NORM_TABLE_ROWS, V_DIM), F32),
            jax.ShapeDtypeStruct((2 * N_HEADS, t), F32),
        ],
        compiler_params=_params(1),
        name="attn_qkv",
    )(x, g.reshape(1, d), w)


def _flash_kernel(slope_ref, qmax_ref, smin_ref, kseq_ref, lq1_ref, lk1_ref, lq2_ref, lk2_ref,
                  sg_ref, qtok_ref, qt_ref, kk_ref, vt_ref, o_ref,
                  s_ref, mc_ref, m_ref, acc_ref, *, lambda_init, blk0):
    n_chunks, tkc = kk_ref.shape[1], kk_ref.shape[2]
    n_sub, tq = qt_ref.shape[0], qt_ref.shape[-1]
    hd = pl.program_id(1)
    slope, inv_slope = slope_ref[0, hd], slope_ref[1, hd]
    cq0 = pl.program_id(2) * n_sub
    col0 = blk0 + pl.program_id(0) * n_chunks + cq0
    k_norm = NORM_SLACK * kseq_ref[hd, pl.program_id(0)]
    subs = range(n_sub)

    def select(sub, values):
        out = values[0]
        for other, value in enumerate(values[1:], 1):
            out = jnp.where(sub == other, value, out)
        return out

    def span(gap):
        reach = (gap * inv_slope - 1.0) * (1.0 / tkc)
        return jnp.where(reach >= 1.0,
                         jnp.minimum(reach, float(n_chunks)).astype(jnp.int32) + 1, 1)

    def plan(cq, n_dist):
        n_left = jnp.minimum(n_dist, cq)
        return n_left, n_left + jnp.minimum(n_dist, n_chunks - 1 - cq)

    def item(cq, n_left, t, first=1):
        left = t < n_left
        c = jnp.where(left, cq - first - t, cq + first + t - n_left)
        return jnp.clip(c, 0, n_chunks - 1), left.astype(jnp.int32)

    def nearest(cq):
        return (jnp.where(cq > 0, cq - 1, jnp.minimum(cq + 1, n_chunks - 1)),
                (cq > 0).astype(jnp.int32))

    def run(total, trip, unroll):
        assert unroll & (unroll - 1) == 0

        def trips(t0, count):
            for r in range(count):
                trip(t0 + r)

        rem = total & (unroll - 1)
        done, size = 0, 1
        while size < unroll:
            pl.when((total & size) == size)(functools.partial(trips, done, size))
            done = done + (total & size)
            size *= 2

        def group(u, carry):
            trips(rem + unroll * u, unroll)
            return carry

        lax.fori_loop(0, total // unroll, group, 0)

    def scores(sub, c, sign, mp):
        return _dot(kk_ref[mp, c], qt_ref[sub, 2 * mp + sign])

    def issue_chunk(sub, c, sign, mp):
        s = scores(sub, c, sign, mp)
        s_ref[sub, mp] = s
        mc_ref[sub, mp] = jnp.max(s, axis=0, keepdims=True)

    def absorb(sub, cq, c, mp):
        shift = -slope * (tkc * jnp.abs(cq - c)).astype(F32)
        m_old = m_ref[sub, mp]
        m_new = jnp.maximum(m_old, mc_ref[sub, mp] + shift)
        p = jnp.exp2(s_ref[sub, mp] - (m_new - shift)).astype(BF16)
        acc_ref[sub, mp] = jnp.exp2(m_old - m_new) * acc_ref[sub, mp] + _dot(vt_ref[c], p)
        m_ref[sub, mp] = m_new

    jj = lax.broadcasted_iota(jnp.int32, (tkc, tq), 0)
    ii = lax.broadcasted_iota(jnp.int32, (tkc, tq), 1)
    fixup = (-2.0 * slope) * jnp.maximum(ii - jj, 0).astype(F32)

    def fixed_path():
        n_dist = span(EXP_ZERO_BELOW)
        for sub in subs:
            for mp in range(2):
                m_ref[sub, mp] = k_norm * qtok_ref[mp:mp + 1, sub * tq:(sub + 1) * tq]
        for dist in COMPACT_DISTANCES:
            pl.when(n_dist == dist)(functools.partial(fixed_compact, dist))
        pl.when(n_dist > max(COMPACT_DISTANCES))(functools.partial(fixed_general, n_dist))

    def fixed_compact(dist, rest=None):
        items = []
        for sub in subs:
            cq = cq0 + sub
            items.append((sub, cq, 0, None, True))
            for d in range(1, dist + 1):
                for c, sign in ((cq - d, 1), (cq + d, 0)):
                    inside = (c >= 0) & (c < n_chunks)
                    bias = jnp.where(inside, -slope * float(tkc * d), -MASKED_BIAS)
                    items.append((sub, jnp.clip(c, 0, n_chunks - 1), sign, bias, False))

        def absorb_item(it, mp):
            sub, c, _, bias, first = it
            shift = fixup if bias is None else bias
            p = jnp.exp2(s_ref[sub, mp] + (shift - m_ref[sub, mp])).astype(BF16)
            if first:
                acc_ref[sub, mp] = _dot(vt_ref[c], p)
            else:
                acc_ref[sub, mp] += _dot(vt_ref[c], p)

        s_ref[items[0][0], 0] = scores(*items[0][:3], 0)
        for k, it in enumerate(items):
            sub = it[0]
            last_of_block = k + 1 == len(items) or items[k + 1][0] != sub
            s_ref[sub, 1] = scores(*it[:3], 1)
            absorb_item(it, 0)
            if last_of_block and rest is not None:
                s_ref[sub, 0] = scores(sub, *item(cq0 + sub, rest[sub][0], 0, dist + 1), 0)
            if k + 1 < len(items):
                s_ref[items[k + 1][0], 0] = scores(*items[k + 1][:3], 0)
            absorb_item(it, 1)

    def fixed_general(n_dist):
        near = NEAR_DISTANCES
        plans = []
        for sub in subs:
            cq = cq0 + sub
            n_left = jnp.maximum(jnp.minimum(n_dist, cq) - near, 0)
            plans.append((n_left,
                          n_left + jnp.maximum(jnp.minimum(n_dist, n_chunks - 1 - cq) - near, 0)))
        fixed_compact(near, plans)

        def visit(sub, carry):
            cq = cq0 + sub
            n_left = select(sub, [p[0] for p in plans])
            total = select(sub, [p[1] for p in plans])

            def absorb_fixed(c, mp):
                shift = -slope * (tkc * jnp.abs(cq - c)).astype(F32)
                p = jnp.exp2(s_ref[sub, mp] + (shift - m_ref[sub, mp])).astype(BF16)
                acc_ref[sub, mp] += _dot(vt_ref[c], p)

            def trip(t):
                c, sign = item(cq, n_left, t, near + 1)
                s_ref[sub, 1] = scores(sub, c, sign, 1)
                absorb_fixed(c, 0)
                s_ref[sub, 0] = scores(sub, *item(cq, n_left, t + 1, near + 1), 0)
                absorb_fixed(c, 1)

            run(total, trip, PIPE_UNROLL)
            return carry

        lax.fori_loop(0, n_sub, visit, 0)

    def running_path():
        diag = [[scores(sub, cq0 + sub, 0, mp) + fixup for mp in range(2)] for sub in subs]
        for sub in subs:
            issue_chunk(sub, *nearest(cq0 + sub), 0)
        for sub in subs:
            for mp in range(2):
                m_first = jnp.max(diag[sub][mp], axis=0, keepdims=True)
                m_ref[sub, mp] = m_first
                acc_ref[sub, mp] = _dot(vt_ref[cq0 + sub],
                                        jnp.exp2(diag[sub][mp] - m_first).astype(BF16))
        plans = [plan(cq0 + sub, span(k_norm * qmax_ref[hd, col0 + sub] + EXP_ZERO_BELOW
                                      - jnp.min(m_ref[sub]))) for sub in subs]

        def visit(sub, carry):
            cq = cq0 + sub
            n_left = select(sub, [p[0] for p in plans])
            total = select(sub, [p[1] for p in plans])

            def trip(t):
                c, sign = item(cq, n_left, t)
                issue_chunk(sub, c, sign, 1)
                absorb(sub, cq, c, 0)
                issue_chunk(sub, *item(cq, n_left, t + 1), 0)
                absorb(sub, cq, c, 1)

            run(total, trip, 1)
            return carry

        lax.fori_loop(0, n_sub, visit, 0)

    slack = [k_norm * qmax_ref[hd, col0 + sub] - smin_ref[hd, col0 + sub] for sub in subs]
    fixed_ok = functools.reduce(jnp.logical_and, [s < STAB_SLACK_BELOW for s in slack])
    pl.when(fixed_ok)(fixed_path)
    pl.when(jnp.logical_not(fixed_ok))(running_path)

    lam = (jnp.exp(jnp.sum(lq1_ref[...] * lk1_ref[...], keepdims=True))
           - jnp.exp(jnp.sum(lq2_ref[...] * lk2_ref[...], keepdims=True)) + lambda_init)
    for sub in range(n_sub):
        o_t = (acc_ref[sub, 0, :V_DIM] * (1.0 / acc_ref[sub, 0, V_DIM:V_DIM + 1])
               - (lam / acc_ref[sub, 1, V_DIM:V_DIM + 1]) * acc_ref[sub, 1, :V_DIM])
        scale = (lax.rsqrt(jnp.mean(o_t * o_t, axis=0, keepdims=True) + SUBLN_EPS)
                 * (1.0 - lambda_init))
        o_ref[sub * tq:(sub + 1) * tq, :] = (o_t * scale * sg_ref[...]).T.astype(o_ref.dtype)


def _flash(slopes, tables, lam_vecs, subln_g, qtok, qt, kk, vt, *, tok_start, n_seq, seq_len,
           lambda_init):
    q_max, s_min, k_max = tables
    tq = TOKEN_TILE
    ns = Q_BLOCKS_PER_STEP
    nq = seq_len // tq
    blk0 = tok_start // tq
    seq0 = tok_start // seq_len
    steps = nq // ns
    step0 = blk0 // ns
    assert nq % ns == 0 and blk0 % ns == 0
    smem = pl.BlockSpec(memory_space=pltpu.SMEM)
    vec = _const_spec((1, HEAD_DIM))
    return pl.pallas_call(
        functools.partial(_flash_kernel, lambda_init=lambda_init, blk0=blk0),
        grid=(n_seq, N_HEADS, steps),
        in_specs=[
            smem, smem, smem, smem, vec, vec, vec, vec, _const_spec((V_DIM, tq)),
            pl.BlockSpec((None, 2, ns * tq), lambda b, h, i: (h, 0, step0 + b * steps + i)),
            pl.BlockSpec((None, ns, 4, V_DIM, tq),
                         lambda b, h, i: (h, step0 + b * steps + i, 0, 0, 0)),
            pl.BlockSpec((None, 2, nq, tq, V_DIM), lambda b, h, i: (h, 0, seq0 + b, 0, 0)),
            pl.BlockSpec((None, nq, V_ROWS, tq), lambda b, h, i: (h, seq0 + b, 0, 0)),
        ],
        out_specs=pl.BlockSpec((ns * tq, V_DIM), lambda b, h, i: (b * steps + i, h)),
        out_shape=jax.ShapeDtypeStruct((n_seq * seq_len, N_HEADS * V_DIM), BF16),
        scratch_shapes=[
            pltpu.VMEM((ns, 2, tq, tq), F32),
            pltpu.VMEM((ns, 2, 1, tq), F32),
            pltpu.VMEM((ns, 2, 1, tq), F32),
            pltpu.VMEM((ns, 2, V_ROWS, tq), F32),
        ],
        compiler_params=_params(3),
        name="diff_flash",
    )(slopes, q_max, s_min,
      k_max[:, blk0:blk0 + n_seq * nq].reshape(N_HEADS, n_seq, nq).max(axis=-1),
      *lam_vecs, jnp.broadcast_to(subln_g.reshape(V_DIM, 1), (V_DIM, tq)), qtok, qt, kk, vt)


def kernel(x_prompt, x_sample, norm1_g, norm2_g, final_g, mix_in_w, pool_w, pool_scale, conv_w,
           mix_out_w, attn_qkv_w, attn_out_w, lambda_q1, lambda_k1, lambda_q2, lambda_k2, subln_g,
           mlp_w1, mlp_w2):
    pb, ps, d = x_prompt.shape
    sb, ss, _ = x_sample.shape
    n_prompt = pb * ps
    n_sample = sb * ss
    x_parts = [x_prompt.reshape(n_prompt, d), x_sample.reshape(n_sample, d)]
    bounds = tuple(ps * b for b in range(pb)) + tuple(n_prompt + ss * b for b in range(sb + 1))
    slopes = jnp.exp2(-8.0 * (jnp.arange(N_HEADS, dtype=F32) + 1.0) / N_HEADS)
    slopes = jnp.stack([slopes * LOG2E, 1.0 / (slopes * LOG2E)])

    for i in range(DEPTH):
        j = i // 2
        w1 = mlp_w1[i].astype(BF16)
        w2 = mlp_w2[i].astype(BF16)
        attn_parts, attn_w = (), None
        if i % 2 == 0:
            x = _mixer(x_parts, norm1_g[i], mix_in_w[j].astype(BF16), pool_w[j].astype(BF16),
                       pool_scale[j], conv_w[j], mix_out_w[j].astype(BF16), bounds)
        else:
            x = x_parts[0]
            lambda_init = 0.8 - 0.6 * math.exp(-0.3 * i)
            qt, kk, vt, nrm, qtok = _qkv(x, norm1_g[i], attn_qkv_w[j].astype(BF16))
            per_head = nrm[:, :3, :2 * N_HEADS].reshape(-1, 3, N_HEADS, 2)
            tables = (jnp.sqrt(per_head[:, 0].max(axis=-1)).T, per_head[:, 2].min(axis=-1).T,
                      jnp.sqrt(per_head[:, 1].max(axis=-1)).T)
            lam_vecs = [v[j].reshape(1, HEAD_DIM) for v in (lambda_q1, lambda_k1, lambda_q2, lambda_k2)]
            flash = functools.partial(_flash, slopes, tables, lam_vecs, subln_g[j],
                                      qtok.reshape(N_HEADS, 2, -1), qt, kk, vt,
                                      lambda_init=lambda_init)
            attn_parts = (flash(tok_start=0, n_seq=pb, seq_len=ps),
                          flash(tok_start=n_prompt, n_seq=sb, seq_len=ss))
            attn_w = attn_out_w[j].astype(BF16)
        if i < DEPTH - 1:
            x_parts = [_mlp(x, norm2_g[i], w1, w2, attn_parts, attn_w)]
        else:
            y_prompt, y_sample = (
                _mlp(x, norm2_g[i], w1, w2, attn_parts[k:k + 1], attn_w, final_g, r0, nr)
                for k, (r0, nr) in enumerate(((0, n_prompt), (n_prompt, n_sample))))

    return (y_prompt.reshape(pb, ps, d), y_sample.reshape(sb, ss, d))
```

```python
import functools
import math

import jax
import jax.numpy as jnp
import numpy as np
from jax import lax
from jax.experimental import pallas as pl
from jax.experimental.pallas import tpu as pltpu

D_MODEL = 1024
DEPTH = 4
POOL_WIDTH = D_MODEL // 2
N_POOL_GROUPS = 4
POOL_GROUP_DIM = POOL_WIDTH // N_POOL_GROUPS
POOL_WINDOWS = (2, 4, 8, 16)
CONV_WIDTH_CH = D_MODEL // 2
MIX_IN_COLS = POOL_WIDTH + 3 * CONV_WIDTH_CH
N_HEADS = 8
HEAD_DIM = D_MODEL // (2 * N_HEADS)
V_DIM = 2 * HEAD_DIM
ATTN_SCALE = HEAD_DIM ** -0.5
D_FF = 4 * D_MODEL
NORM_EPS = 1e-6
SUBLN_EPS = 1e-5

F32 = jnp.float32
BF16 = jnp.bfloat16

TOKEN_TILE = 512
HALO = 16
FF_CHUNK = 1024
VMEM_LIMIT_BYTES = 48 * 1024 * 1024
DENOM_ROWS = 16
V_ROWS = V_DIM + DENOM_ROWS
NORM_TABLE_ROWS = 8
PIPE_UNROLL = 8
Q_BLOCKS_PER_STEP = 2
EXP_ZERO_BELOW = 153.0
STAB_SLACK_BELOW = 80.0
COMPACT_DISTANCES = (1, 2)
NEAR_DISTANCES = 1
MASKED_BIAS = 1e30


def _bf16_pieces(value, n):
    pieces, rest = [], np.float32(value)
    for _ in range(n):
        piece = np.float32(np.asarray(rest, dtype=BF16))
        pieces.append(float(piece))
        rest = np.float32(rest - piece)
    return tuple(pieces)


LOG2E = float(np.float32(math.log2(math.e)))
LOG2E_PIECES = _bf16_pieces(LOG2E, 3)
NORM_SLACK = 1.01
POS_LO_MASK = 255
POS_HI_MASK = TOKEN_TILE - 1 - POS_LO_MASK


def _rms(x, g, eps):
    return x * lax.rsqrt(jnp.mean(x * x, axis=-1, keepdims=True) + eps) * g


def _dot(a, b):
    return jnp.dot(a, b, preferred_element_type=F32)


def _const_spec(shape, single_buffer=False):
    zeros = (0,) * len(shape)
    if single_buffer:
        return pl.BlockSpec(shape, lambda *_: zeros, pipeline_mode=pl.Buffered(1))
    return pl.BlockSpec(shape, lambda *_: zeros)


def _part_starts(part_rows, rows):
    starts, s = [], 0
    for r in part_rows:
        starts.append(s // rows)
        s += r
    return starts


def _part_specs(parts, rows, block_of):
    specs = []
    for p, s0 in zip(parts, _part_starts([p.shape[0] for p in parts], rows)):
        nb = p.shape[0] // rows
        specs.append(pl.BlockSpec(
            (rows, p.shape[1]), lambda i, s0=s0, nb=nb: (jnp.clip(block_of(i) - s0, 0, nb - 1), 0)))
    return specs


def _pick_part(refs, block, part_rows, rows):
    value = refs[0][...]
    for ref, s0 in zip(refs[1:], _part_starts(part_rows, rows)[1:]):
        value = jnp.where(block >= s0, ref[...], value)
    return value


def _params(n_axes):
    return pltpu.CompilerParams(
        dimension_semantics=("arbitrary",) * n_axes,
        vmem_limit_bytes=VMEM_LIMIT_BYTES,
    )


def _mlp_body(x, g_ref, w1_ref, w2_ref):
    h = _rms(x, g_ref[...], NORM_EPS).astype(BF16)
    acc = x
    for c in range(D_FF // FF_CHUNK):
        cols = slice(c * FF_CHUNK, (c + 1) * FF_CHUNK)
        a = jnp.maximum(_dot(h, w1_ref[:, cols]), 0.0)
        acc = acc + _dot((a * a).astype(BF16), w2_ref[cols, :])
    return acc


def _mixer_kernel(*refs, bounds, part_rows, with_mlp):
    n = len(part_rows)
    xp_refs, x_refs, xn_refs = refs[:n], refs[n:2 * n], refs[2 * n:3 * n]
    g_ref, win_ref, pw_ref, ps_ref, cw_ref, wout_ref = refs[3 * n:3 * n + 6]
    mlp_refs = refs[3 * n + 6:3 * n + 9] if with_mlp else ()
    o_ref, u_ref, z_ref = refs[-3:]
    tm = x_refs[0].shape[0]
    i = pl.program_id(0)
    per = tm // HALO
    last = sum(part_rows) // HALO - 1
    start = i * tm
    seq_start = jnp.int32(bounds[0])
    seq_end = jnp.int32(bounds[-1])
    for b in bounds[1:-1]:
        seq_start = jnp.where(start >= b, b, seq_start)
    for b in reversed(bounds[1:-1]):
        seq_end = jnp.where(start < b, b, seq_end)

    x = _pick_part(x_refs, i, part_rows, tm)
    xe = jnp.concatenate([
        _pick_part(xp_refs, jnp.maximum(i * per - 1, 0), part_rows, HALO),
        x,
        _pick_part(xn_refs, jnp.minimum((i + 1) * per, last), part_rows, HALO),
    ], axis=0)
    pos = start - HALO + lax.broadcasted_iota(jnp.int32, (tm + 2 * HALO, 1), 0)
    valid = (pos >= seq_start) & (pos < seq_end)
    he = _rms(xe, g_ref[...], NORM_EPS).astype(BF16)
    proj = _dot(he, win_ref[...])
    c0 = POOL_WIDTH
    u_ref[...] = jnp.where(valid, proj[:, :c0], 0.0)
    z_ref[...] = jnp.where(valid, proj[:, c0 + 2 * CONV_WIDTH_CH:] * proj[:, c0:c0 + CONV_WIDTH_CH], 0.0)

    rel = start - seq_start + lax.broadcasted_iota(jnp.int32, (tm, 1), 0)
    seq_len = seq_end - seq_start
    ys = []
    for g, w in enumerate(POOL_WINDOWS):
        cols = slice(g * POOL_GROUP_DIM, (g + 1) * POOL_GROUP_DIM)
        tot = u_ref[HALO - w // 2:HALO - w // 2 + tm, cols]
        for o in range(-w // 2 + 1, w // 2):
            tot = tot + u_ref[HALO + o:HALO + o + tm, cols]
        cnt = (jnp.minimum(rel + w // 2, seq_len) - jnp.maximum(rel - w // 2, 0)).astype(F32)
        d = tot / cnt - u_ref[HALO:HALO + tm, cols]
        ys.append(_dot(d.astype(BF16), pw_ref[g]))
    a_out = jnp.concatenate(ys, axis=-1) * ps_ref[...]

    conv = (cw_ref[0:1, :] * z_ref[HALO - 1:HALO - 1 + tm, :]
            + cw_ref[1:2, :] * z_ref[HALO:HALO + tm, :]
            + cw_ref[2:3, :] * z_ref[HALO + 1:HALO + 1 + tm, :])
    b_out = proj[HALO:HALO + tm, c0 + CONV_WIDTH_CH:c0 + 2 * CONV_WIDTH_CH] * conv
    mixed = jnp.concatenate([a_out, b_out], axis=-1).astype(BF16)
    y = x + _dot(mixed, wout_ref[...])
    o_ref[...] = _mlp_body(y, *mlp_refs) if with_mlp else y


def _mixer(x_parts, g, win, pw, ps, cw, wout, bounds, mlp=None):
    fused = mlp is not None
    spec = functools.partial(_const_spec, single_buffer=fused)
    mlp_args = [mlp[0].reshape(1, -1), mlp[1], mlp[2]] if fused else []
    d = x_parts[0].shape[1]
    part_rows = tuple(p.shape[0] for p in x_parts)
    t = sum(part_rows)
    tm = TOKEN_TILE
    per = tm // HALO
    last = t // HALO - 1
    return pl.pallas_call(
        functools.partial(_mixer_kernel, bounds=bounds, part_rows=part_rows, with_mlp=fused),
        grid=(t // tm,),
        in_specs=[
            *_part_specs(x_parts, HALO, lambda i: jnp.maximum(i * per - 1, 0)),
            *_part_specs(x_parts, tm, lambda i: i),
            *_part_specs(x_parts, HALO, lambda i: jnp.minimum((i + 1) * per, last)),
            spec((1, d)),
            spec(win.shape),
            spec(pw.shape),
            spec((1, POOL_WIDTH)),
            spec(cw.shape),
            spec(wout.shape),
            *[spec(a.shape) for a in mlp_args],
        ],
        out_specs=pl.BlockSpec((tm, d), lambda i: (i, 0)),
        out_shape=jax.ShapeDtypeStruct((t, d), F32),
        scratch_shapes=[
            pltpu.VMEM((tm + 2 * HALO, POOL_WIDTH), F32),
            pltpu.VMEM((tm + 2 * HALO, CONV_WIDTH_CH), F32),
        ],
        compiler_params=_params(1),
        name="even_mixer",
    )(*x_parts, *x_parts, *x_parts, g.reshape(1, d), win, pw, ps.reshape(1, POOL_WIDTH), cw, wout,
      *mlp_args)


def _mlp_kernel(*refs, attn_rows, has_final):
    refs = list(refs)
    x_ref = refs.pop(0)
    if attn_rows:
        a_refs = [refs.pop(0) for _ in attn_rows]
        wo_ref = refs.pop(0)
    g_ref, w1_ref, w2_ref = refs[:3]
    refs = refs[3:]
    if has_final:
        fg_ref = refs.pop(0)
    o_ref = refs.pop(0)

    x = x_ref[...]
    if attn_rows:
        attn = _pick_part(a_refs, pl.program_id(0), attn_rows, x_ref.shape[0])
        x = x + _dot(attn, wo_ref[...])
    acc = _mlp_body(x, g_ref, w1_ref, w2_ref)
    if has_final:
        acc = _rms(acc, fg_ref[...], NORM_EPS)
    o_ref[...] = acc


def _mlp(x, g, w1, w2, attn_parts=(), attn_w=None, final_g=None, row_start=0, rows=None):
    d = x.shape[1]
    rows = x.shape[0] if rows is None else rows
    tm = TOKEN_TILE
    tile0 = row_start // tm
    row_spec = pl.BlockSpec((tm, d), lambda i: (i, 0))
    args = [x]
    in_specs = [pl.BlockSpec((tm, d), lambda i: (tile0 + i, 0))]
    if attn_parts:
        args += [*attn_parts, attn_w]
        in_specs += [*_part_specs(attn_parts, tm, lambda i: i), _const_spec(attn_w.shape)]
    args += [g.reshape(1, d), w1, w2]
    in_specs += [_const_spec((1, d)), _const_spec(w1.shape), _const_spec(w2.shape)]
    if final_g is not None:
        args.append(final_g.reshape(1, d))
        in_specs.append(_const_spec((1, d)))
    return pl.pallas_call(
        functools.partial(_mlp_kernel, attn_rows=tuple(p.shape[0] for p in attn_parts),
                          has_final=final_g is not None),
        grid=(rows // tm,),
        in_specs=in_specs,
        out_specs=row_spec,
        out_shape=jax.ShapeDtypeStruct((rows, d), F32),
        compiler_params=_params(1),
        name="sq_relu_mlp",
    )(*args)


def _qkv_kernel(x_ref, g_ref, w_ref, qt_ref, kk_ref, vt_ref, nrm_ref, qtok_ref):
    tm = x_ref.shape[0]
    h = _rms(x_ref[...], g_ref[...], NORM_EPS).astype(BF16)
    qkv = _dot(h, w_ref[...])
    q_all = (qkv[:, :D_MODEL] * (ATTN_SCALE * LOG2E)).astype(BF16).astype(F32)
    k_all = qkv[:, D_MODEL:2 * D_MODEL].astype(BF16).astype(F32)

    sel = (lax.shift_right_logical(lax.broadcasted_iota(jnp.int32, (D_MODEL, V_DIM), 0),
                                   HEAD_DIM.bit_length() - 1)
           == lax.broadcasted_iota(jnp.int32, (D_MODEL, V_DIM), 1)).astype(BF16)

    def per_map(a):
        return _dot(a.astype(BF16), sel)

    q_sq, k_sq, self_score = per_map(q_all * q_all), per_map(k_all * k_all), per_map(q_all * k_all)
    nrm_ref[0] = jnp.concatenate([
        jnp.max(q_sq, axis=0, keepdims=True), jnp.max(k_sq, axis=0, keepdims=True),
        jnp.min(self_score, axis=0, keepdims=True),
        jnp.zeros((NORM_TABLE_ROWS - 3, V_DIM), F32)], axis=0)
    qtok_ref[...] = jnp.sqrt(q_sq).T[:2 * N_HEADS]

    n_p = len(LOG2E_PIECES)

    def extras(slot, lo, hi, pos_first):
        pos_slot = slot if pos_first else slot - 2 * n_p
        const_slot = slot - 2 * n_p if pos_first else slot
        out = jnp.where((pos_slot >= 0) & (pos_slot < n_p), lo,
                        jnp.where((pos_slot >= n_p) & (pos_slot < 2 * n_p), hi, 0.0))
        for p, piece in enumerate(LOG2E_PIECES):
            out = jnp.where((const_slot == p) | (const_slot == n_p + p), piece, out)
        return out

    q_row = lax.broadcasted_iota(jnp.int32, (HEAD_DIM, tm), 0)
    q_tok = lax.broadcasted_iota(jnp.int32, (HEAD_DIM, tm), 1)
    q_base = extras(q_row, (q_tok & POS_LO_MASK).astype(F32), (q_tok & POS_HI_MASK).astype(F32), True)
    q_has_slope = q_row >= 2 * n_p
    k_lane = lax.broadcasted_iota(jnp.int32, (tm, V_DIM), 1)
    k_tok = lax.broadcasted_iota(jnp.int32, (tm, V_DIM), 0)
    k_lo = -(k_tok & POS_LO_MASK).astype(F32)
    k_hi = -(k_tok & POS_HI_MASK).astype(F32)
    k_base = [extras(k_lane - off, k_lo, k_hi, False) for off in (HEAD_DIM, 0)]
    k_has_slope = [(k_lane >= off) & (k_lane < off + 2 * n_p) for off in (HEAD_DIM, 0)]
    denom_rows = (lax.broadcasted_iota(jnp.int32, (DENOM_ROWS, tm), 0) == 0).astype(F32)

    for hd in range(N_HEADS):
        slope = 2.0 ** (-8.0 * (hd + 1) / N_HEADS)
        cols = slice(hd * V_DIM, (hd + 1) * V_DIM)
        q_t = q_all[:, cols].T
        top, bot = q_t[:HEAD_DIM], q_t[HEAD_DIM:]
        q_extra = jnp.where(q_has_slope, slope * q_base, q_base)
        qt_ref[hd, 0, 0] = jnp.concatenate([top, q_extra], axis=0).astype(BF16)
        qt_ref[hd, 0, 1] = jnp.concatenate([top, -q_extra], axis=0).astype(BF16)
        qt_ref[hd, 0, 2] = jnp.concatenate([q_extra, bot], axis=0).astype(BF16)
        qt_ref[hd, 0, 3] = jnp.concatenate([-q_extra, bot], axis=0).astype(BF16)

        kh = k_all[:, cols]
        k_extra = [jnp.where(k_has_slope[m], slope * k_base[m], k_base[m]) for m in range(2)]
        kk_ref[hd, 0, 0] = jnp.where(k_lane < HEAD_DIM, kh, k_extra[0]).astype(BF16)
        kk_ref[hd, 1, 0] = jnp.where(k_lane >= HEAD_DIM, kh, k_extra[1]).astype(BF16)

        vh = qkv[:, 2 * D_MODEL + hd * V_DIM:2 * D_MODEL + (hd + 1) * V_DIM]
        vt_ref[hd, 0] = jnp.concatenate([vh.T, denom_rows], axis=0).astype(BF16)


def _qkv(x, g, w):
    t, d = x.shape
    tm = TOKEN_TILE
    nc = t // tm
    return pl.pallas_call(
        _qkv_kernel,
        grid=(nc,),
        in_specs=[
            pl.BlockSpec((tm, d), lambda i: (i, 0)),
            _const_spec((1, d)),
            _const_spec(w.shape),
        ],
        out_specs=[
            pl.BlockSpec((N_HEADS, 1, 4, V_DIM, tm), lambda i: (0, i, 0, 0, 0)),
            pl.BlockSpec((N_HEADS, 2, 1, tm, V_DIM), lambda i: (0, 0, i, 0, 0)),
            pl.BlockSpec((N_HEADS, 1, V_ROWS, tm), lambda i: (0, i, 0, 0)),
            pl.BlockSpec((1, NORM_TABLE_ROWS, V_DIM), lambda i: (i, 0, 0)),
            pl.BlockSpec((2 * N_HEADS, tm), lambda i: (0, i)),
        ],
        out_shape=[
            jax.ShapeDtypeStruct((N_HEADS, nc, 4, V_DIM, tm), BF16),
            jax.ShapeDtypeStruct((N_HEADS, 2, nc, tm, V_DIM), BF16),
            jax.ShapeDtypeStruct((N_HEADS, nc, V_ROWS, tm), BF16),
            jax.ShapeDtypeStruct((nc, NORM_TABLE_ROWS, V_DIM), F32),
            jax.ShapeDtypeStruct((2 * N_HEADS, t), F32),
        ],
        compiler_params=_params(1),
        name="attn_qkv",
    )(x, g.reshape(1, d), w)


def _flash_kernel(slope_ref, qmax_ref, smin_ref, kseq_ref, lq1_ref, lk1_ref, lq2_ref, lk2_ref,
                  sg_ref, qtok_ref, qt_ref, kk_ref, vt_ref, o_ref,
                  s_ref, mc_ref, m_ref, acc_ref, *, lambda_init, blk0):
    n_chunks, tkc = kk_ref.shape[1], kk_ref.shape[2]
    n_sub, tq = qt_ref.shape[0], qt_ref.shape[-1]
    hd = pl.program_id(1)
    slope, inv_slope = slope_ref[0, hd], slope_ref[1, hd]
    cq0 = pl.program_id(2) * n_sub
    col0 = blk0 + pl.program_id(0) * n_chunks + cq0
    k_norm = NORM_SLACK * kseq_ref[hd, pl.program_id(0)]
    subs = range(n_sub)

    def select(sub, values):
        out = values[0]
        for other, value in enumerate(values[1:], 1):
            out = jnp.where(sub == other, value, out)
        return out

    def span(gap):
        reach = (gap * inv_slope - 1.0) * (1.0 / tkc)
        return jnp.where(reach >= 1.0,
                         jnp.minimum(reach, float(n_chunks)).astype(jnp.int32) + 1, 1)

    def plan(cq, n_dist):
        n_left = jnp.minimum(n_dist, cq)
        return n_left, n_left + jnp.minimum(n_dist, n_chunks - 1 - cq)

    def item(cq, n_left, t, first=1):
        left = t < n_left
        c = jnp.where(left, cq - first - t, cq + first + t - n_left)
        return jnp.clip(c, 0, n_chunks - 1), left.astype(jnp.int32)

    def nearest(cq):
        return (jnp.where(cq > 0, cq - 1, jnp.minimum(cq + 1, n_chunks - 1)),
                (cq > 0).astype(jnp.int32))

    def run(total, trip, unroll):
        assert unroll & (unroll - 1) == 0

        def trips(t0, count):
            for r in range(count):
                trip(t0 + r)

        rem = total & (unroll - 1)
        done, size = 0, 1
        while size < unroll:
            pl.when((total & size) == size)(functools.partial(trips, done, size))
            done = done + (total & size)
            size *= 2

        def group(u, carry):
            trips(rem + unroll * u, unroll)
            return carry

        lax.fori_loop(0, total // unroll, group, 0)

    def scores(sub, c, sign, mp):
        return _dot(kk_ref[mp, c], qt_ref[sub, 2 * mp + sign])

    def issue_chunk(sub, c, sign, mp):
        s = scores(sub, c, sign, mp)
        s_ref[sub, mp] = s
        mc_ref[sub, mp] = jnp.max(s, axis=0, keepdims=True)

    def absorb(sub, cq, c, mp):
        shift = -slope * (tkc * jnp.abs(cq - c)).astype(F32)
        m_old = m_ref[sub, mp]
        m_new = jnp.maximum(m_old, mc_ref[sub, mp] + shift)
        p = jnp.exp2(s_ref[sub, mp] - (m_new - shift)).astype(BF16)
        acc_ref[sub, mp] = jnp.exp2(m_old - m_new) * acc_ref[sub, mp] + _dot(vt_ref[c], p)
        m_ref[sub, mp] = m_new

    jj = lax.broadcasted_iota(jnp.int32, (tkc, tq), 0)
    ii = lax.broadcasted_iota(jnp.int32, (tkc, tq), 1)
    fixup = (-2.0 * slope) * jnp.maximum(ii - jj, 0).astype(F32)

    def fixed_path():
        n_dist = span(EXP_ZERO_BELOW)
        for sub in subs:
            for mp in range(2):
                m_ref[sub, mp] = k_norm * qtok_ref[mp:mp + 1, sub * tq:(sub + 1) * tq]
        for dist in COMPACT_DISTANCES:
            pl.when(n_dist == dist)(functools.partial(fixed_compact, dist))
        pl.when(n_dist > max(COMPACT_DISTANCES))(functools.partial(fixed_general, n_dist))

    def fixed_compact(dist, rest=None):
        items = []
        for sub in subs:
            cq = cq0 + sub
            items.append((sub, cq, 0, None, True))
            for d in range(1, dist + 1):
                for c, sign in ((cq - d, 1), (cq + d, 0)):
                    inside = (c >= 0) & (c < n_chunks)
                    bias = jnp.where(inside, -slope * float(tkc * d), -MASKED_BIAS)
                    items.append((sub, jnp.clip(c, 0, n_chunks - 1), sign, bias, False))

        def absorb_item(it, mp):
            sub, c, _, bias, first = it
            shift = fixup if bias is None else bias
            p = jnp.exp2(s_ref[sub, mp] + (shift - m_ref[sub, mp])).astype(BF16)
            if first:
                acc_ref[sub, mp] = _dot(vt_ref[c], p)
            else:
                acc_ref[sub, mp] += _dot(vt_ref[c], p)

        s_ref[items[0][0], 0] = scores(*items[0][:3], 0)
        for k, it in enumerate(items):
            sub = it[0]
            last_of_block = k + 1 == len(items) or items[k + 1][0] != sub
            s_ref[sub, 1] = scores(*it[:3], 1)
            absorb_item(it, 0)
            if last_of_block and rest is not None:
                s_ref[sub, 0] = scores(sub, *item(cq0 + sub, rest[sub][0], 0, dist + 1), 0)
            if k + 1 < len(items):
                s_ref[items[k + 1][0], 0] = scores(*items[k + 1][:3], 0)
            absorb_item(it, 1)

    def fixed_general(n_dist):
        near = NEAR_DISTANCES
        plans = []
        for sub in subs:
            cq = cq0 + sub
            n_left = jnp.maximum(jnp.minimum(n_dist, cq) - near, 0)
            plans.append((n_left,
                          n_left + jnp.maximum(jnp.minimum(n_dist, n_chunks - 1 - cq) - near, 0)))
        fixed_compact(near, plans)

        def visit(sub, carry):
            cq = cq0 + sub
            n_left = select(sub, [p[0] for p in plans])
            total = select(sub, [p[1] for p in plans])

            def absorb_fixed(c, mp):
                shift = -slope * (tkc * jnp.abs(cq - c)).astype(F32)
                p = jnp.exp2(s_ref[sub, mp] + (shift - m_ref[sub, mp])).astype(BF16)
                acc_ref[sub, mp] += _dot(vt_ref[c], p)

            def trip(t):
                c, sign = item(cq, n_left, t, near + 1)
                s_ref[sub, 1] = scores(sub, c, sign, 1)
                absorb_fixed(c, 0)
                s_ref[sub, 0] = scores(sub, *item(cq, n_left, t + 1, near + 1), 0)
                absorb_fixed(c, 1)

            run(total, trip, PIPE_UNROLL)
            return carry

        lax.fori_loop(0, n_sub, visit, 0)

    def running_path():
        diag = [[scores(sub, cq0 + sub, 0, mp) + fixup for mp in range(2)] for sub in subs]
        for sub in subs:
            issue_chunk(sub, *nearest(cq0 + sub), 0)
        for sub in subs:
            for mp in range(2):
                m_first = jnp.max(diag[sub][mp], axis=0, keepdims=True)
                m_ref[sub, mp] = m_first
                acc_ref[sub, mp] = _dot(vt_ref[cq0 + sub],
                                        jnp.exp2(diag[sub][mp] - m_first).astype(BF16))
        plans = [plan(cq0 + sub, span(k_norm * qmax_ref[hd, col0 + sub] + EXP_ZERO_BELOW
                                      - jnp.min(m_ref[sub]))) for sub in subs]

        def visit(sub, carry):
            cq = cq0 + sub
            n_left = select(sub, [p[0] for p in plans])
            total = select(sub, [p[1] for p in plans])

            def trip(t):
                c, sign = item(cq, n_left, t)
                issue_chunk(sub, c, sign, 1)
                absorb(sub, cq, c, 0)
                issue_chunk(sub, *item(cq, n_left, t + 1), 0)
                absorb(sub, cq, c, 1)

            run(total, trip, 1)
            return carry

        lax.fori_loop(0, n_sub, visit, 0)

    slack = [k_norm * qmax_ref[hd, col0 + sub] - smin_ref[hd, col0 + sub] for sub in subs]
    fixed_ok = functools.reduce(jnp.logical_and, [s < STAB_SLACK_BELOW for s in slack])
    pl.when(fixed_ok)(fixed_path)
    pl.when(jnp.logical_not(fixed_ok))(running_path)

    lam = (jnp.exp(jnp.sum(lq1_ref[...] * lk1_ref[...], keepdims=True))
           - jnp.exp(jnp.sum(lq2_ref[...] * lk2_ref[...], keepdims=True)) + lambda_init)
    for sub in range(n_sub):
        o_t = (acc_ref[sub, 0, :V_DIM] * (1.0 / acc_ref[sub, 0, V_DIM:V_DIM + 1])
               - (lam / acc_ref[sub, 1, V_DIM:V_DIM + 1]) * acc_ref[sub, 1, :V_DIM])
        scale = (lax.rsqrt(jnp.mean(o_t * o_t, axis=0, keepdims=True) + SUBLN_EPS)
                 * (1.0 - lambda_init))
        o_ref[sub * tq:(sub + 1) * tq, :] = (o_t * scale * sg_ref[...]).T.astype(o_ref.dtype)


def _flash(slopes, tables, lam_vecs, subln_g, qtok, qt, kk, vt, *, tok_start, n_seq, seq_len,
           lambda_init):
    q_max, s_min, k_max = tables
    tq = TOKEN_TILE
    ns = Q_BLOCKS_PER_STEP
    nq = seq_len // tq
    blk0 = tok_start // tq
    seq0 = tok_start // seq_len
    steps = nq // ns
    step0 = blk0 // ns
    assert nq % ns == 0 and blk0 % ns == 0
    smem = pl.BlockSpec(memory_space=pltpu.SMEM)
    vec = _const_spec((1, HEAD_DIM))
    return pl.pallas_call(
        functools.partial(_flash_kernel, lambda_init=lambda_init, blk0=blk0),
        grid=(n_seq, N_HEADS, steps),
        in_specs=[
            smem, smem, smem, smem, vec, vec, vec, vec, _const_spec((V_DIM, tq)),
            pl.BlockSpec((None, 2, ns * tq), lambda b, h, i: (h, 0, step0 + b * steps + i)),
            pl.BlockSpec((None, ns, 4, V_DIM, tq),
                         lambda b, h, i: (h, step0 + b * steps + i, 0, 0, 0)),
            pl.BlockSpec((None, 2, nq, tq, V_DIM), lambda b, h, i: (h, 0, seq0 + b, 0, 0)),
            pl.BlockSpec((None, nq, V_ROWS, tq), lambda b, h, i: (h, seq0 + b, 0, 0)),
        ],
        out_specs=pl.BlockSpec((ns * tq, V_DIM), lambda b, h, i: (b * steps + i, h)),
        out_shape=jax.ShapeDtypeStruct((n_seq * seq_len, N_HEADS * V_DIM), BF16),
        scratch_shapes=[
            pltpu.VMEM((ns, 2, tq, tq), F32),
            pltpu.VMEM((ns, 2, 1, tq), F32),
            pltpu.VMEM((ns, 2, 1, tq), F32),
            pltpu.VMEM((ns, 2, V_ROWS, tq), F32),
        ],
        compiler_params=_params(3),
        name="diff_flash",
    )(slopes, q_max, s_min,
      k_max[:, blk0:blk0 + n_seq * nq].reshape(N_HEADS, n_seq, nq).max(axis=-1),
      *lam_vecs, jnp.broadcast_to(subln_g.reshape(V_DIM, 1), (V_DIM, tq)), qtok, qt, kk, vt)


def kernel(x_prompt, x_sample, norm1_g, norm2_g, final_g, mix_in_w, pool_w, pool_scale, conv_w,
           mix_out_w, attn_qkv_w, attn_out_w, lambda_q1, lambda_k1, lambda_q2, lambda_k2, subln_g,
           mlp_w1, mlp_w2):
    pb, ps, d = x_prompt.shape
    sb, ss, _ = x_sample.shape
    n_prompt = pb * ps
    n_sample = sb * ss
    x_parts = [x_prompt.reshape(n_prompt, d), x_sample.reshape(n_sample, d)]
    bounds = tuple(ps * b for b in range(pb)) + tuple(n_prompt + ss * b for b in range(sb + 1))
    slopes = jnp.exp2(-8.0 * (jnp.arange(N_HEADS, dtype=F32) + 1.0) / N_HEADS)
    slopes = jnp.stack([slopes * LOG2E, 1.0 / (slopes * LOG2E)])

    for i in range(DEPTH):
        j = i // 2
        w1 = mlp_w1[i].astype(BF16)
        w2 = mlp_w2[i].astype(BF16)
        attn_parts, attn_w = (), None
        if i % 2 == 0:
            fuse = i < DEPTH - 1
            x = _mixer(x_parts, norm1_g[i], mix_in_w[j].astype(BF16), pool_w[j].astype(BF16),
                       pool_scale[j], conv_w[j], mix_out_w[j].astype(BF16), bounds,
                       mlp=(norm2_g[i], w1, w2) if fuse else None)
            if fuse:
                x_parts = [x]
                continue
        else:
            x = x_parts[0]
            lambda_init = 0.8 - 0.6 * math.exp(-0.3 * i)
            qt, kk, vt, nrm, qtok = _qkv(x, norm1_g[i], attn_qkv_w[j].astype(BF16))
            per_head = nrm[:, :3, :2 * N_HEADS].reshape(-1, 3, N_HEADS, 2)
            tables = (jnp.sqrt(per_head[:, 0].max(axis=-1)).T, per_head[:, 2].min(axis=-1).T,
                      jnp.sqrt(per_head[:, 1].max(axis=-1)).T)
            lam_vecs = [v[j].reshape(1, HEAD_DIM) for v in (lambda_q1, lambda_k1, lambda_q2, lambda_k2)]
            flash = functools.partial(_flash, slopes, tables, lam_vecs, subln_g[j],
                                      qtok.reshape(N_HEADS, 2, -1), qt, kk, vt,
                                      lambda_init=lambda_init)
            attn_parts = (flash(tok_start=0, n_seq=pb, seq_len=ps),
                          flash(tok_start=n_prompt, n_seq=sb, seq_len=ss))
            attn_w = attn_out_w[j].astype(BF16)
        if i < DEPTH - 1:
            x_parts = [_mlp(x, norm2_g[i], w1, w2, attn_parts, attn_w)]
        else:
            y_prompt, y_sample = (
                _mlp(x, norm2_g[i], w1, w2, attn_parts[k:k + 1], attn_w, final_g, r0, nr)
                for k, (r0, nr) in enumerate(((0, n_prompt), (n_prompt, n_sample))))

    return (y_prompt.reshape(pb, ps, d), y_sample.reshape(sb, ss, d))
```
